```python
import jax, jax.numpy as jnp
from jax import lax
import numpy as np


D_MODEL = 2048
BATCH = 8
SEQ = 4096
DEPTH = 2

RMS_EPS = 1e-6
LN_EPS = 1e-5
D_CONV = 1024
CONV_WIDTH = 3
D_SGU = 1024
SGU_GROUPS = 8
SGU_HD = D_SGU // SGU_GROUPS
CHUNK = 128
HYB_IN = 3 * D_CONV + 2 * D_SGU
HEAD_DIM = 64
N_Q_HEADS = D_MODEL // HEAD_DIM
N_KV_HEADS = N_Q_HEADS // 8
GQA_GROUP = N_Q_HEADS // N_KV_HEADS
WINDOW = 128
ATT_BLOCK = 128
QKV_DIM = (N_Q_HEADS + 2 * N_KV_HEADS) * HEAD_DIM
REL_BUCKETS = 32
REL_MAX_DIST = 128
N_EXPERTS = 64
TOP_K = 8
N_GROUPS = 8
TOPK_GROUPS = 4
D_EXPERT = 512
D_SHARED = 512
ROUTED_SCALE = 2.5
MOE_BLOCK = 256

kernel_name = 'hybrid_conv_sgu_swa_moe_adaln'


def rmsnorm(x, g):
    x32 = x.astype(jnp.float32)
    y = x32 * lax.rsqrt(jnp.mean(x32 * x32, axis=-1, keepdims=True) + RMS_EPS)
    return (y * g.astype(jnp.float32)).astype(x.dtype)


def layernorm(x, g, b):
    x32 = x.astype(jnp.float32)
    mu = jnp.mean(x32, axis=-1, keepdims=True)
    var = jnp.mean(jnp.square(x32 - mu), axis=-1, keepdims=True)
    y = (x32 - mu) * lax.rsqrt(var + LN_EPS)
    return (y * g.astype(jnp.float32) + b.astype(jnp.float32)).astype(x.dtype)


def modulate(xn, shift, scale):
    return xn * (1 + scale[:, None, :]) + shift[:, None, :]


def t5_bucket(dist):
    n = np.maximum(dist, 0)
    max_exact = REL_BUCKETS // 2
    large = max_exact + (np.log(np.maximum(n, 1) / max_exact) / np.log(REL_MAX_DIST / max_exact)
                         * (REL_BUCKETS - max_exact)).astype(np.int32)
    large = np.minimum(large, REL_BUCKETS - 1)
    return np.where(n < max_exact, n, large).astype(np.int32)


def conv_sgu_mixer(xn, w_in, conv_w, ln_g, ln_b, sgu_w, sgu_b, w_out):
    B, S, _ = xn.shape
    proj = xn @ w_in
    bg, cg, hc, u, v = jnp.split(
        proj, [D_CONV, 2 * D_CONV, 3 * D_CONV, 3 * D_CONV + D_SGU], axis=-1)
    z = cg * hc
    zp = jnp.pad(z, ((0, 0), (CONV_WIDTH - 1, 0), (0, 0)))
    conv = sum(conv_w[k] * zp[:, k:k + S] for k in range(CONV_WIDTH))
    y_a = bg * conv
    u = jax.nn.gelu(u, approximate=False)
    v = layernorm(jax.nn.gelu(v, approximate=False), ln_g, ln_b)
    vc = v.reshape(B, S // CHUNK, CHUNK, SGU_GROUPS, SGU_HD)
    tril = jnp.asarray(np.tril(np.ones((CHUNK, CHUNK), dtype=np.float32)), dtype=sgu_w.dtype)
    w_s = sgu_w * tril
    s = jnp.einsum('gts,bnsgc->bntgc', w_s, vc) + sgu_b.T[None, None, :, :, None]
    y_b = u * s.reshape(B, S, D_SGU)
    return jnp.concatenate([y_a, y_b], axis=-1) @ w_out


def swa_attention(xn, w_qkv, sinks, w_o, rel_bias):
    B, S, _ = xn.shape
    nb = S // ATT_BLOCK
    qkv = xn @ w_qkv
    q, k, v = jnp.split(qkv, [N_Q_HEADS * HEAD_DIM, (N_Q_HEADS + N_KV_HEADS) * HEAD_DIM], axis=-1)
    q = (q * (HEAD_DIM ** -0.5)).reshape(B, nb, ATT_BLOCK, N_KV_HEADS, GQA_GROUP, HEAD_DIM)
    k = k.reshape(B, nb, ATT_BLOCK, N_KV_HEADS, HEAD_DIM)
    v = v.reshape(B, nb, ATT_BLOCK, N_KV_HEADS, HEAD_DIM)
    pad = ((0, 0), (1, 0), (0, 0), (0, 0), (0, 0))
    kb = jnp.concatenate([jnp.pad(k, pad)[:, :-1], k], axis=2)
    vb = jnp.concatenate([jnp.pad(v, pad)[:, :-1], v], axis=2)
    t_loc = np.arange(ATT_BLOCK)[:, None]
    j_loc = np.arange(2 * ATT_BLOCK)[None, :]
    dist = ATT_BLOCK + t_loc - j_loc
    band = jnp.asarray((dist >= 0) & (dist < WINDOW))
    bias = rel_bias[t5_bucket(dist)].astype(jnp.float32)
    bias = bias.transpose(2, 0, 1).reshape(N_KV_HEADS, GQA_GROUP, ATT_BLOCK, 2 * ATT_BLOCK)
    sink = sinks.astype(jnp.float32).reshape(N_KV_HEADS, GQA_GROUP, 1, 1)
    key_off = jnp.arange(2 * ATT_BLOCK) - ATT_BLOCK

    def block(args):
        i, qi, ki, vi = args
        s = jnp.einsum('btkgd,bskd->bkgts', qi, ki).astype(jnp.float32) + bias
        valid = band & ((i * ATT_BLOCK + key_off) >= 0)[None, :]
        s = jnp.where(valid, s, -jnp.inf)
        m = jnp.maximum(jnp.max(s, axis=-1, keepdims=True), sink)
        p = jnp.exp(s - m)
        denom = jnp.sum(p, axis=-1, keepdims=True) + jnp.exp(sink - m)
        return jnp.einsum('bkgts,bskd->btkgd', (p / denom).astype(vi.dtype), vi)

    o = lax.map(block, (jnp.arange(nb), jnp.moveaxis(q, 1, 0),
                        jnp.moveaxis(kb, 1, 0), jnp.moveaxis(vb, 1, 0)))
    o = jnp.moveaxis(o, 0, 1).reshape(B, S, N_Q_HEADS * HEAD_DIM)
    return o @ w_o


def moe_ffn(h, router_w, router_bias, w_gate, w_up, w_down, sh_gate, sh_up, sh_down):
    T = h.shape[0]
    e_per_g = N_EXPERTS // N_GROUPS
    scores = jax.nn.sigmoid(h.astype(jnp.float32) @ router_w.astype(jnp.float32))
    sel = scores + router_bias.astype(jnp.float32)
    grp_score = lax.top_k(sel.reshape(T, N_GROUPS, e_per_g), 2)[0].sum(-1)
    _, gidx = lax.top_k(grp_score, TOPK_GROUPS)
    gmask = jax.nn.one_hot(gidx, N_GROUPS, dtype=jnp.float32).sum(1) > 0
    emask = jnp.repeat(gmask, e_per_g, axis=1)
    _, eidx = lax.top_k(jnp.where(emask, sel, -jnp.inf), TOP_K)
    gw = jnp.take_along_axis(scores, eidx, axis=1)
    gw = gw / jnp.sum(gw, axis=-1, keepdims=True) * ROUTED_SCALE

    A = T * TOP_K
    nblk = -(-A // MOE_BLOCK) + N_EXPERTS
    flat_e = eidx.reshape(-1)
    flat_w = gw.reshape(-1).astype(h.dtype)
    flat_tok = jnp.arange(A, dtype=jnp.int32) // TOP_K
    order = jnp.argsort(flat_e)
    sorted_e = flat_e[order]
    counts = jnp.bincount(flat_e, length=N_EXPERTS)
    padded = -(-counts // MOE_BLOCK) * MOE_BLOCK
    cum_pad = jnp.cumsum(padded)
    pad_start = cum_pad - padded
    start = jnp.cumsum(counts) - counts
    dest = pad_start[sorted_e] + (jnp.arange(A) - start[sorted_e])
    pad_tok = jnp.zeros((nblk * MOE_BLOCK,), jnp.int32).at[dest].set(flat_tok[order])
    pad_w = jnp.zeros((nblk * MOE_BLOCK,), h.dtype).at[dest].set(flat_w[order])
    blk_e = jnp.minimum(jnp.searchsorted(cum_pad, jnp.arange(nblk) * MOE_BLOCK, side='right'),
                        N_EXPERTS - 1)

    def step(out, blk):
        idx, wts, e = blk
        xb = h[idx]
        a = jax.nn.silu(xb @ w_gate[e]) * (xb @ w_up[e])
        y = (a @ w_down[e]) * wts[:, None]
        return out.at[idx].add(y), None

    routed, _ = lax.scan(step, jnp.zeros_like(h),
                         (pad_tok.reshape(nblk, MOE_BLOCK), pad_w.reshape(nblk, MOE_BLOCK), blk_e))
    shared = (jax.nn.silu(h @ sh_gate) * (h @ sh_up)) @ sh_down
    return routed + shared


def setup_inputs(seed: int = 0) -> dict:
    key = jax.random.key(seed)
    ks = jax.random.split(key, 32)
    f32 = jnp.float32
    D = D_MODEL
    NE = (DEPTH + 1) // 2
    NO = DEPTH // 2

    def nrm(k, shape, scale):
        return jax.random.normal(k, shape, f32) * scale

    return {
        'x': nrm(ks[0], (BATCH, SEQ, D), 1.0),
        'c': nrm(ks[1], (BATCH, D), 1.0),
        'ada_w': nrm(ks[2], (DEPTH, D, 6 * D), 0.5 * D ** -0.5),
        'ada_b': nrm(ks[3], (DEPTH, 6 * D), 0.02),
        'norm_mix_g': 1.0 + nrm(ks[4], (DEPTH, D), 0.1),
        'norm_ffn_g': 1.0 + nrm(ks[5], (DEPTH, D), 0.1),
        'hyb_w_in': nrm(ks[6], (NE, D, HYB_IN), D ** -0.5),
        'conv_w': nrm(ks[7], (NE, CONV_WIDTH, D_CONV), CONV_WIDTH ** -0.5),
        'sgu_ln_g': 1.0 + nrm(ks[8], (NE, D_SGU), 0.1),
        'sgu_ln_b': nrm(ks[9], (NE, D_SGU), 0.02),
        'sgu_w': nrm(ks[10], (NE, SGU_GROUPS, CHUNK, CHUNK), CHUNK ** -0.5),
        'sgu_b': 1.0 + nrm(ks[11], (NE, SGU_GROUPS, CHUNK), 0.1),
        'hyb_w_out': nrm(ks[12], (NE, D_CONV + D_SGU, D), (D_CONV + D_SGU) ** -0.5),
        'attn_w_qkv': nrm(ks[13], (NO, D, QKV_DIM), D ** -0.5),
        'attn_sinks': nrm(ks[14], (NO, N_Q_HEADS), 1.0),
        'attn_w_o': nrm(ks[15], (NO, N_Q_HEADS * HEAD_DIM, D), (N_Q_HEADS * HEAD_DIM) ** -0.5),
        'rel_bias': nrm(ks[16], (REL_BUCKETS, N_Q_HEADS), 0.5),
        'router_w': nrm(ks[17], (DEPTH, D, N_EXPERTS), D ** -0.5),
        'router_bias': nrm(ks[18], (DEPTH, N_EXPERTS), 0.01),
        'exp_w_gate': nrm(ks[19], (DEPTH, N_EXPERTS, D, D_EXPERT), D ** -0.5),
        'exp_w_up': nrm(ks[20], (DEPTH, N_EXPERTS, D, D_EXPERT), D ** -0.5),
        'exp_w_down': nrm(ks[21], (DEPTH, N_EXPERTS, D_EXPERT, D), D_EXPERT ** -0.5),
        'sh_w_gate': nrm(ks[22], (DEPTH, D, D_SHARED), D ** -0.5),
        'sh_w_up': nrm(ks[23], (DEPTH, D, D_SHARED), D ** -0.5),
        'sh_w_down': nrm(ks[24], (DEPTH, D_SHARED, D), D_SHARED ** -0.5),
        'final_g': 1.0 + nrm(ks[25], (D,), 0.1),
    }


def reference(x, c, ada_w, ada_b, norm_mix_g, norm_ffn_g, hyb_w_in, conv_w, sgu_ln_g, sgu_ln_b,
              sgu_w, sgu_b, hyb_w_out, attn_w_qkv, attn_sinks, attn_w_o, rel_bias, router_w,
              router_bias, exp_w_gate, exp_w_up, exp_w_down, sh_w_gate, sh_w_up, sh_w_down, final_g):
    B, S, D = x.shape
    c_act = jax.nn.silu(c)
    for l in range(DEPTH):
        mod = c_act @ ada_w[l] + ada_b[l]
        sh_m, sc_m, g_m, sh_f, sc_f, g_f = jnp.split(mod, 6, axis=-1)
        xn = modulate(rmsnorm(x, norm_mix_g[l]), sh_m, sc_m)
        if l % 2 == 0:
            i = l // 2
            y = conv_sgu_mixer(xn, hyb_w_in[i], conv_w[i], sgu_ln_g[i], sgu_ln_b[i],
                               sgu_w[i], sgu_b[i], hyb_w_out[i])
        else:
            i = l // 2
            y = swa_attention(xn, attn_w_qkv[i], attn_sinks[i], attn_w_o[i], rel_bias)
        x = x + g_m[:, None, :] * y
        xn = modulate(rmsnorm(x, norm_ffn_g[l]), sh_f, sc_f)
        y = moe_ffn(xn.reshape(B * S, D), router_w[l], router_bias[l], exp_w_gate[l], exp_w_up[l],
                    exp_w_down[l], sh_w_gate[l], sh_w_up[l], sh_w_down[l]).reshape(B, S, D)
        x = x + g_f[:, None, :] * y
    return rmsnorm(x, final_g)
```

```python
import functools

import numpy as np
import jax
import jax.numpy as jnp
from jax import lax
from jax.experimental import pallas as pl
from jax.experimental.pallas import tpu as pltpu

f32 = jnp.float32
bf16 = jnp.bfloat16
i32 = jnp.int32
u32 = jnp.uint32

D_MODEL = 2048
DEPTH = 2
RMS_EPS = 1e-6
LN_EPS = 1e-5
D_CONV = 1024
CONV_WIDTH = 3
D_SGU = 1024
SGU_GROUPS = 8
SGU_HD = D_SGU // SGU_GROUPS
CHUNK = 128
HYB_IN = 3 * D_CONV + 2 * D_SGU
HEAD_DIM = 64
N_Q_HEADS = D_MODEL // HEAD_DIM
N_KV_HEADS = N_Q_HEADS // 8
GQA_GROUP = N_Q_HEADS // N_KV_HEADS
WINDOW = 128
ATT_BLOCK = 128
REL_BUCKETS = 32
REL_MAX_DIST = 128
N_EXPERTS = 64
TOP_K = 8
N_GROUPS = 8
TOPK_GROUPS = 4
E_PER_G = N_EXPERTS // N_GROUPS
D_EXPERT = 512
D_SHARED = 512
ROUTED_SCALE = 2.5

LANES = 128
HALF = D_MODEL // 2
VMEM_LIMIT = 56 * 1024 * 1024
TM = 512
TN_IN = 1024
TR = 512
TD = 256
BM = 512
ADA_TN = 1024


def _cparams(sem, vmem=VMEM_LIMIT):
    return pltpu.CompilerParams(dimension_semantics=sem, vmem_limit_bytes=vmem)


def _norm_mod(x, g, sh, sc):
    ms = jnp.mean(x * x, axis=-1, keepdims=True)
    y = x * lax.rsqrt(ms + RMS_EPS)
    return (y * g) * (1.0 + sc) + sh


def _pack_halves(y):
    h = y.shape[1] // 2
    lo = lax.bitcast_convert_type(y[:, :h].astype(bf16).astype(f32), u32)
    hi = lax.bitcast_convert_type(y[:, h:].astype(bf16).astype(f32), u32)
    return (lo >> 16) | (hi & jnp.uint32(0xFFFF0000))


def _unpack_halves(w):
    lo = lax.bitcast_convert_type(w << 16, f32)
    hi = lax.bitcast_convert_type(w & jnp.uint32(0xFFFF0000), f32)
    return lo, hi


def _gelu(x):
    return 0.5 * x * (1.0 + lax.erf(x * np.float32(np.sqrt(0.5))))


def _dot(a, b):
    return jnp.dot(a, b, preferred_element_type=f32)


def _dot_nt(a, b):
    return lax.dot_general(a, b, (((1,), (1,)), ((), ())), preferred_element_type=f32)


def _ada_kernel(c_ref, w_ref, b_ref, o_ref):
    ca = jax.nn.silu(c_ref[...]).astype(bf16)
    o_ref[...] = _dot(ca, w_ref[...].astype(bf16)) + b_ref[...]


def ada_mod(c, ada_w, ada_b):
    depth, d, n = ada_w.shape
    b = c.shape[0]
    return pl.pallas_call(
        _ada_kernel,
        grid=(depth, n // ADA_TN),
        in_specs=[
            pl.BlockSpec((b, d), lambda l, j: (0, 0)),
            pl.BlockSpec((None, d, ADA_TN), lambda l, j: (l, 0, j)),
            pl.BlockSpec((None, 1, ADA_TN), lambda l, j: (l, 0, j)),
        ],
        out_specs=pl.BlockSpec((None, b, ADA_TN), lambda l, j: (l, 0, j)),
        out_shape=jax.ShapeDtypeStruct((depth, b, n), f32),
        compiler_params=_cparams(("arbitrary", "arbitrary")),
        name="ada_mod",
    )(c, ada_w, ada_b.reshape(depth, 1, n))


def _hyb_in_kernel(x_ref, g_ref, sh_ref, sc_ref, w_ref, cw_ref, lng_ref, lnb_ref, sw_ref, sb_ref,
                   o_ref, xn_scr, a_scr, b_scr, carry_scr, *, tiles_per_seq):
    i = pl.program_id(0)
    j = pl.program_id(1)
    tm = x_ref.shape[0]

    @pl.when(j == 0)
    def _():
        xn_scr[...] = _norm_mod(x_ref[...], g_ref[...], sh_ref[...], sc_ref[...]).astype(bf16)

    @pl.when(jnp.logical_and(i == 0, j == 0))
    def _():
        carry_scr[...] = jnp.zeros_like(carry_scr)

    p = _dot(xn_scr[...], w_ref[...])

    @pl.when(j == 0)
    def _():
        a_scr[...] = p

    @pl.when(j == 1)
    def _():
        b_scr[...] = p

    @pl.when(j == 2)
    def _():
        z = b_scr[...] * p
        row = lax.broadcasted_iota(i32, z.shape, 0)
        first = (i % tiles_per_seq) == 0
        prev = jnp.where(first, 0.0, carry_scr[...])
        p1 = prev[7:8, :]
        p2 = prev[6:7, :]
        z1 = jnp.where(row == 0, p1, pltpu.roll(z, 1, 0))
        z2 = jnp.where(row == 0, p2, jnp.where(row == 1, p1, pltpu.roll(z, 2, 0)))
        cw = cw_ref[...]
        conv = cw[0:1, :] * z2 + cw[1:2, :] * z1 + cw[2:3, :] * z
        carry_scr[...] = z[tm - 8:, :]
        o_ref[:, :D_CONV] = (a_scr[...] * conv).astype(o_ref.dtype)

    @pl.when(j == 3)
    def _():
        a_scr[...] = _gelu(p)

    @pl.when(j == 4)
    def _():
        v = _gelu(p)
        mu = jnp.mean(v, axis=-1, keepdims=True)
        vc = v - mu
        var = jnp.mean(vc * vc, axis=-1, keepdims=True)
        v = (vc * lax.rsqrt(var + LN_EPS)) * lng_ref[...] + lnb_ref[...]
        vb = v.astype(bf16)
        nch = tm // CHUNK
        r = lax.broadcasted_iota(i32, (CHUNK, CHUNK), 0)
        c = lax.broadcasted_iota(i32, (CHUNK, CHUNK), 1)
        tril = r >= c
        for g in range(SGU_GROUPS):
            ws = jnp.where(tril, sw_ref[g], 0.0).astype(bf16)
            vg = jnp.concatenate(
                [vb[n * CHUNK:(n + 1) * CHUNK, g * SGU_HD:(g + 1) * SGU_HD] for n in range(nch)], axis=1)
            sg = _dot(ws, vg)
            bb = sb_ref[g]
            for n in range(nch):
                s = sg[:, n * SGU_HD:(n + 1) * SGU_HD] + bb
                u = a_scr[n * CHUNK:(n + 1) * CHUNK, g * SGU_HD:(g + 1) * SGU_HD]
                o_ref[n * CHUNK:(n + 1) * CHUNK, D_CONV + g * SGU_HD:D_CONV + (g + 1) * SGU_HD] = (
                    (u * s).astype(o_ref.dtype))


def hyb_in(x2, g, sh, sc, w_in_bf, conv_w, ln_g, ln_b, sgu_w, sgu_b, seq):
    t, d = x2.shape
    tps = seq // TM
    nj = HYB_IN // TN_IN
    sbb = jnp.broadcast_to(sgu_b[:, :, None], (SGU_GROUPS, CHUNK, SGU_HD))
    kern = functools.partial(_hyb_in_kernel, tiles_per_seq=tps)
    return pl.pallas_call(
        kern,
        grid=(t // TM, nj),
        in_specs=[
            pl.BlockSpec((TM, d), lambda i, j: (i, 0)),
            pl.BlockSpec((1, d), lambda i, j: (0, 0)),
            pl.BlockSpec((None, 1, d), lambda i, j: (i // tps, 0, 0)),
            pl.BlockSpec((None, 1, d), lambda i, j: (i // tps, 0, 0)),
            pl.BlockSpec((d, TN_IN), lambda i, j: (0, j)),
            pl.BlockSpec((CONV_WIDTH, D_CONV), lambda i, j: (0, 0)),
            pl.BlockSpec((1, D_SGU), lambda i, j: (0, 0)),
            pl.BlockSpec((1, D_SGU), lambda i, j: (0, 0)),
            pl.BlockSpec((SGU_GROUPS, CHUNK, CHUNK), lambda i, j: (0, 0, 0)),
            pl.BlockSpec((SGU_GROUPS, CHUNK, SGU_HD), lambda i, j: (0, 0, 0)),
        ],
        out_specs=pl.BlockSpec((TM, D_CONV + D_SGU), lambda i, j: (i, 0)),
        out_shape=jax.ShapeDtypeStruct((t, D_CONV + D_SGU), bf16),
        scratch_shapes=[
            pltpu.VMEM((TM, d), bf16),
            pltpu.VMEM((TM, TN_IN), f32),
            pltpu.VMEM((TM, TN_IN), f32),
            pltpu.VMEM((8, D_CONV), f32),
        ],
        compiler_params=_cparams(("arbitrary", "arbitrary")),
        name="hyb_in",
    )(x2, g.reshape(1, d), sh, sc, w_in_bf, conv_w, ln_g.reshape(1, -1), ln_b.reshape(1, -1), sgu_w, sbb)


def _qkv_kernel(x_ref, g_ref, sh_ref, sc_ref, w_ref, o_ref, xn_scr):
    @pl.when(pl.program_id(1) == 0)
    def _():
        xn_scr[...] = _norm_mod(x_ref[...], g_ref[...], sh_ref[...], sc_ref[...]).astype(bf16)

    o_ref[...] = _dot(xn_scr[...], w_ref[...]).astype(o_ref.dtype)


def qkv_proj(x2, g, sh, sc, w_bf, seq, tn):
    t, d = x2.shape
    n = w_bf.shape[1]
    tps = seq // TM
    return pl.pallas_call(
        _qkv_kernel,
        grid=(t // TM, n // tn),
        in_specs=[
            pl.BlockSpec((TM, d), lambda i, j: (i, 0)),
            pl.BlockSpec((1, d), lambda i, j: (0, 0)),
            pl.BlockSpec((None, 1, d), lambda i, j: (i // tps, 0, 0)),
            pl.BlockSpec((None, 1, d), lambda i, j: (i // tps, 0, 0)),
            pl.BlockSpec((d, tn), lambda i, j: (0, j)),
        ],
        out_specs=pl.BlockSpec((TM, tn), lambda i, j: (i, j)),
        out_shape=jax.ShapeDtypeStruct((t, n), bf16),
        scratch_shapes=[pltpu.VMEM((TM, d), bf16)],
        compiler_params=_cparams(("arbitrary", "arbitrary")),
        name="qkv_proj",
    )(x2, g.reshape(1, d), sh, sc, w_bf)


QK_COLS = N_KV_HEADS * 2 * HEAD_DIM


def _attn_kernel(q_ref, kp_ref, kc_ref, vp_ref, vc_ref, bias_ref, sink_ref, o_ref):
    blk = pl.program_id(1)
    lane = lax.broadcasted_iota(i32, (2 * ATT_BLOCK, 2 * HEAD_DIM), 1)
    low = lane < HEAD_DIM
    col = lax.broadcasted_iota(i32, (ATT_BLOCK, 2 * ATT_BLOCK), 1)
    no_prev = jnp.logical_and(blk == 0, col < ATT_BLOCK)
    zero = jnp.zeros((), bf16)
    for kh in range(N_KV_HEADS):
        cs = slice(kh * 2 * HEAD_DIM, (kh + 1) * 2 * HEAD_DIM)
        kk = jnp.concatenate([kp_ref[:, cs], kc_ref[:, cs]], axis=0)
        vv = jnp.concatenate([vp_ref[:, cs], vc_ref[:, cs]], axis=0)
        k0 = jnp.where(low, kk, zero)
        k1 = jnp.where(low, zero, kk)
        v0 = jnp.where(low, vv, zero)
        v1 = jnp.where(low, zero, vv)
        for pr in range(GQA_GROUP // 2):
            h0 = kh * GQA_GROUP + 2 * pr
            qp = q_ref[:, h0 * HEAD_DIM:(h0 + 2) * HEAD_DIM] * jnp.asarray(HEAD_DIM ** -0.5, bf16)
            acc = None
            for hh, (kz, vz) in enumerate(((k0, v0), (k1, v1))):
                h = h0 + hh
                s = _dot_nt(qp, kz) + bias_ref[h]
                s = jnp.where(no_prev, -jnp.inf, s)
                sk = sink_ref[h]
                m = jnp.maximum(jnp.max(s, axis=-1, keepdims=True), sk)
                p = jnp.exp(s - m)
                den = jnp.sum(p, axis=-1, keepdims=True) + jnp.exp(sk - m)
                o = _dot((p / den).astype(bf16), vz)
                acc = o if acc is None else acc + o
            o_ref[:, h0 * HEAD_DIM:(h0 + 2) * HEAD_DIM] = acc.astype(o_ref.dtype)


def swa_attn(qkv, bias_m, sinks, batch, seq):
    t = qkv.shape[0]
    nb = seq // ATT_BLOCK
    dq = N_Q_HEADS * HEAD_DIM
    kcol = dq // QK_COLS
    vcol = kcol + 1

    def prev(b, i):
        return b * nb + jnp.maximum(i - 1, 0)

    return pl.pallas_call(
        _attn_kernel,
        grid=(batch, nb),
        in_specs=[
            pl.BlockSpec((ATT_BLOCK, dq), lambda b, i: (b * nb + i, 0)),
            pl.BlockSpec((ATT_BLOCK, QK_COLS), lambda b, i: (prev(b, i), kcol)),
            pl.BlockSpec((ATT_BLOCK, QK_COLS), lambda b, i: (b * nb + i, kcol)),
            pl.BlockSpec((ATT_BLOCK, QK_COLS), lambda b, i: (prev(b, i), vcol)),
            pl.BlockSpec((ATT_BLOCK, QK_COLS), lambda b, i: (b * nb + i, vcol)),
            pl.BlockSpec((N_Q_HEADS, ATT_BLOCK, 2 * ATT_BLOCK), lambda b, i: (0, 0, 0)),
            pl.BlockSpec(memory_space=pltpu.SMEM),
        ],
        out_specs=pl.BlockSpec((ATT_BLOCK, dq), lambda b, i: (b * nb + i, 0)),
        out_shape=jax.ShapeDtypeStruct((t, dq), bf16),
        compiler_params=_cparams(("arbitrary", "arbitrary")),
        name="swa_attn",
    )(qkv, qkv, qkv, qkv, qkv, bias_m, sinks)


def _t5_bucket(dist):
    n = np.maximum(dist, 0)
    max_exact = REL_BUCKETS // 2
    large = max_exact + (np.log(np.maximum(n, 1) / max_exact) / np.log(REL_MAX_DIST / max_exact)
                         * (REL_BUCKETS - max_exact)).astype(np.int32)
    large = np.minimum(large, REL_BUCKETS - 1)
    return np.where(n < max_exact, n, large).astype(np.int32)


def _attn_bias_table(rel_bias):
    t_loc = np.arange(ATT_BLOCK)[:, None]
    j_loc = np.arange(2 * ATT_BLOCK)[None, :]
    dist = ATT_BLOCK + t_loc - j_loc
    band = (dist >= 0) & (dist < WINDOW)
    bias = rel_bias[_t5_bucket(dist)].astype(f32).transpose(2, 0, 1)
    return jnp.where(jnp.asarray(band)[None], bias, -jnp.inf)


def _proj_res_kernel(a_ref, w_ref, x_ref, gate_ref, o_ref):
    o_ref[...] = x_ref[...] + gate_ref[...] * _dot(a_ref[...], w_ref[...])


def proj_residual(a, w_bf, x2, gate, seq):
    t, k = a.shape
    d = w_bf.shape[1]
    tps = seq // TM
    return pl.pallas_call(
        _proj_res_kernel,
        grid=(t // TM,),
        in_specs=[
            pl.BlockSpec((TM, k), lambda i: (i, 0)),
            pl.BlockSpec((k, d), lambda i: (0, 0)),
            pl.BlockSpec((TM, d), lambda i: (i, 0)),
            pl.BlockSpec((None, 1, d), lambda i: (i // tps, 0, 0)),
        ],
        out_specs=pl.BlockSpec((TM, d), lambda i: (i, 0)),
        out_shape=jax.ShapeDtypeStruct((t, d), f32),
        compiler_params=_cparams(("arbitrary",)),
        name="proj_residual",
    )(a, w_bf, x2, gate)


def _route_kernel(x_ref, g_ref, sh_ref, sc_ref, rw_ref, rb_ref,
                  xp_ref, eidx_ref, rk_ref, gw_ref, cnt_ref, sel_scr, grp_scr, carry_scr):
    tm = x_ref.shape[0]
    ne = N_EXPERTS

    @pl.when(pl.program_id(0) == 0)
    def _():
        carry_scr[...] = jnp.zeros_like(carry_scr)

    xn = _norm_mod(x_ref[...], g_ref[...], sh_ref[...], sc_ref[...])
    xp_ref[...] = _pack_halves(xn)
    x_hi = xn.astype(bf16)
    x_lo = (xn - x_hi.astype(f32)).astype(bf16)
    rw = rw_ref[...]
    a = _dot_nt(rw, x_hi)
    b = _dot_nt(rw[:ne], x_lo)
    logits = a[:ne] + a[ne:] + b
    scores = jax.nn.sigmoid(logits)
    sel = scores + rb_ref[...]

    for g in range(N_GROUPS):
        tile = sel[g * E_PER_G:(g + 1) * E_PER_G, :]
        t1 = jnp.max(tile, axis=0, keepdims=True)
        dup = jnp.sum(jnp.where(tile == t1, 1.0, 0.0), axis=0, keepdims=True) >= 2.0
        t2 = jnp.max(jnp.where(tile < t1, tile, -jnp.inf), axis=0, keepdims=True)
        grp_scr[g:g + 1, :] = t1 + jnp.where(dup, t1, t2)
    gs = grp_scr[...]

    gi = lax.broadcasted_iota(i32, (N_GROUPS, tm), 0)
    grank = jnp.zeros((N_GROUPS, tm), i32)
    for g in range(N_GROUPS):
        r = grp_scr[g:g + 1, :]
        ge = jnp.where(r >= gs, 1, 0)
        gt = jnp.where(r > gs, 1, 0)
        grank = grank + jnp.where(gi > g, ge, gt)
    grp_scr[...] = jnp.where(grank < TOPK_GROUPS, 1.0, 0.0)

    masked = jnp.concatenate(
        [jnp.where(grp_scr[g:g + 1, :] > 0.5, sel[g * E_PER_G:(g + 1) * E_PER_G, :], -jnp.inf)
         for g in range(N_GROUPS)], axis=0)

    sel_scr[...] = masked
    ei = lax.broadcasted_iota(i32, (ne, tm), 0)
    rank = jnp.zeros((ne, tm), i32)
    for e in range(ne):
        r = sel_scr[e:e + 1, :]
        ge = jnp.where(r >= masked, 1, 0)
        gt = jnp.where(r > masked, 1, 0)
        rank = rank + jnp.where(ei > e, ge, gt)
    chosen = rank < TOP_K
    chf = jnp.where(chosen, 1.0, 0.0)

    wun = jnp.where(chosen, scores, 0.0)
    gwt = wun / jnp.sum(wun, axis=0, keepdims=True) * ROUTED_SCALE

    rr = lax.broadcasted_iota(i32, (tm, tm), 0)
    cc = lax.broadcasted_iota(i32, (tm, tm), 1)
    upper = jnp.where(rr < cc, 1.0, 0.0).astype(bf16)
    chb = chf.astype(bf16)
    pos = carry_scr[:, 0:1] + _dot(chb, upper)
    carry_scr[...] = carry_scr[...] + jnp.sum(chf, axis=1, keepdims=True)
    cnt_ref[...] = carry_scr[...].astype(i32)

    er = lax.broadcasted_iota(i32, (ne, ne), 0)
    ec = lax.broadcasted_iota(i32, (ne, ne), 1)
    lower = jnp.where(ec < er, 1.0, 0.0).astype(bf16)
    below = _dot(lower, chb)
    slot = jnp.where(chosen, below, -1.0)
    eif = ei.astype(f32)
    for k in range(TOP_K):
        mk = slot == float(k)
        eidx_ref[k:k + 1, :] = jnp.sum(jnp.where(mk, eif, 0.0), axis=0, keepdims=True).astype(i32)
        rk_ref[k:k + 1, :] = jnp.sum(jnp.where(mk, pos, 0.0), axis=0, keepdims=True).astype(i32)
        gw_ref[k:k + 1, :] = jnp.sum(jnp.where(mk, gwt, 0.0), axis=0, keepdims=True)


def moe_route(x2, g, sh, sc, rw_cat, rbias, seq):
    t, d = x2.shape
    tps = seq // TR
    return pl.pallas_call(
        _route_kernel,
        grid=(t // TR,),
        in_specs=[
            pl.BlockSpec((TR, d), lambda i: (i, 0)),
            pl.BlockSpec((1, d), lambda i: (0, 0)),
            pl.BlockSpec((None, 1, d), lambda i: (i // tps, 0, 0)),
            pl.BlockSpec((None, 1, d), lambda i: (i // tps, 0, 0)),
            pl.BlockSpec((2 * N_EXPERTS, d), lambda i: (0, 0)),
            pl.BlockSpec((N_EXPERTS, 1), lambda i: (0, 0)),
        ],
        out_specs=[
            pl.BlockSpec((TR, HALF), lambda i: (i, 0)),
            pl.BlockSpec((TOP_K, TR), lambda i: (0, i)),
            pl.BlockSpec((TOP_K, TR), lambda i: (0, i)),
            pl.BlockSpec((TOP_K, TR), lambda i: (0, i)),
            pl.BlockSpec((N_EXPERTS, LANES), lambda i: (0, 0)),
        ],
        out_shape=[
            jax.ShapeDtypeStruct((t, HALF), u32),
            jax.ShapeDtypeStruct((TOP_K, t), i32),
            jax.ShapeDtypeStruct((TOP_K, t), i32),
            jax.ShapeDtypeStruct((TOP_K, t), f32),
            jax.ShapeDtypeStruct((N_EXPERTS, LANES), i32),
        ],
        scratch_shapes=[
            pltpu.VMEM((N_EXPERTS, TR), f32),
            pltpu.VMEM((N_GROUPS, TR), f32),
            pltpu.VMEM((N_EXPERTS, LANES), f32),
        ],
        compiler_params=_cparams(("arbitrary",)),
        name="moe_route",
    )(x2, g.reshape(1, d), sh, sc, rw_cat, rbias.reshape(N_EXPERTS, 1))


def _row_copy(src, s, dst, d, sem):
    return pltpu.make_async_copy(src.at[pl.ds(s, 1), :], dst.at[pl.ds(d, 1), :], sem)


def _dispatch_kernel(ztail_ref, dest_ref, xp_ref, xs_ref, zeros_scr, sem, zsem):
    td = xp_ref.shape[0]

    @pl.when(pl.program_id(0) == 0)
    def _():
        zeros_scr[...] = jnp.zeros_like(zeros_scr)

        def zcopy(e):
            z = pl.multiple_of(ztail_ref[e], BM)
            return pltpu.make_async_copy(zeros_scr, xs_ref.at[pl.ds(z, BM), :], zsem)

        def zstart(e, c):
            zcopy(e).start()
            return c

        def zwait(e, c):
            zcopy(e).wait()
            return c

        lax.fori_loop(0, N_EXPERTS, zstart, 0)
        lax.fori_loop(0, N_EXPERTS, zwait, 0)

    def start(t, c):
        for k in range(TOP_K):
            _row_copy(xp_ref, t, xs_ref, dest_ref[k, t], sem).start()
        return c

    def wait(t, c):
        for k in range(TOP_K):
            _row_copy(xp_ref, t, xs_ref, dest_ref[k, t], sem).wait()
        return c

    lax.fori_loop(0, td, start, 0)
    lax.fori_loop(0, td, wait, 0)


def moe_dispatch(xp, dest, ztail, nrows):
    t = xp.shape[0]
    grid_spec = pltpu.PrefetchScalarGridSpec(
        num_scalar_prefetch=1,
        grid=(t // TD,),
        in_specs=[
            pl.BlockSpec((TOP_K, TD), lambda i, z: (0, i), memory_space=pltpu.SMEM),
            pl.BlockSpec((TD, HALF), lambda i, z: (i, 0)),
        ],
        out_specs=pl.BlockSpec(memory_space=pl.ANY),
        scratch_shapes=[
            pltpu.VMEM((BM, HALF), u32),
            pltpu.SemaphoreType.DMA(()),
            pltpu.SemaphoreType.DMA(()),
        ],
    )
    return pl.pallas_call(
        _dispatch_kernel,
        grid_spec=grid_spec,
        out_shape=jax.ShapeDtypeStruct((nrows, HALF), u32),
        compiler_params=_cparams(("arbitrary",)),
        name="moe_dispatch",
    )(ztail, dest, xp)


def _ffn_packed(xw, wg, wu, wd):
    lo, hi = _unpack_halves(xw)
    xa = lo.astype(bf16)
    xb = hi.astype(bf16)
    hg = _dot(xa, wg[:HALF, :]) + _dot(xb, wg[HALF:, :])
    hu = _dot(xa, wu[:HALF, :]) + _dot(xb, wu[HALF:, :])
    a = (jax.nn.silu(hg) * hu).astype(bf16)
    return _dot(a, wd[...])


def _gmm_kernel(be_ref, bf_ref, nu_ref, xs_ref, wg_ref, wu_ref, wd_ref, ys_ref, wg_s, wu_s, wd_s):
    i = pl.program_id(0)

    @pl.when(bf_ref[i] == 1)
    def _():
        wg_s[...] = wg_ref[...].astype(bf16)
        wu_s[...] = wu_ref[...].astype(bf16)
        wd_s[...] = wd_ref[...].astype(bf16)

    @pl.when(i < nu_ref[0])
    def _():
        ys_ref[...] = _pack_halves(_ffn_packed(xs_ref[...], wg_s, wu_s, wd_s))

    @pl.when(i >= nu_ref[0])
    def _():
        ys_ref[...] = jnp.zeros_like(ys_ref)


def moe_gmm(xs, blk_e, blk_first, n_used, w_gate, w_up, w_down, nblk):
    d, de = w_gate.shape[1], w_gate.shape[2]
    grid_spec = pltpu.PrefetchScalarGridSpec(
        num_scalar_prefetch=3,
        grid=(nblk,),
        in_specs=[
            pl.BlockSpec((BM, HALF), lambda i, be, bf, nu: (jnp.minimum(i, nu[0] - 1), 0)),
            pl.BlockSpec((None, d, de), lambda i, be, bf, nu: (be[i], 0, 0)),
            pl.BlockSpec((None, d, de), lambda i, be, bf, nu: (be[i], 0, 0)),
            pl.BlockSpec((None, de, d), lambda i, be, bf, nu: (be[i], 0, 0)),
        ],
        out_specs=pl.BlockSpec((BM, HALF), lambda i, be, bf, nu: (i, 0)),
        scratch_shapes=[
            pltpu.VMEM((d, de), bf16),
            pltpu.VMEM((d, de), bf16),
            pltpu.VMEM((de, d), bf16),
        ],
    )
    return pl.pallas_call(
        _gmm_kernel,
        grid_spec=grid_spec,
        out_shape=jax.ShapeDtypeStruct((nblk * BM, HALF), u32),
        compiler_params=_cparams(("arbitrary",)),
        name="moe_gmm",
    )(blk_e, blk_first, n_used, xs, w_gate, w_up, w_down)


def _combine_kernel(dest_ref, x_ref, xp_ref, gw_ref, gate_ref, sg_ref, su_ref, sd_ref, fg_ref, ys_ref,
                    o_ref, gath, sem, *, final_norm):
    td = x_ref.shape[0]

    def copy(t, k):
        return pltpu.make_async_copy(ys_ref.at[pl.ds(dest_ref[k, t], 1), :], gath.at[k, pl.ds(t, 1), :], sem)

    def start(t, c):
        for k in range(TOP_K):
            copy(t, k).start()
        return c

    def wait(t, c):
        for k in range(TOP_K):
            copy(t, k).wait()
        return c

    lax.fori_loop(0, td, start, 0)
    ysh = _ffn_packed(xp_ref[...], sg_ref, su_ref, sd_ref)
    lax.fori_loop(0, td, wait, 0)

    acc_lo = ysh[:, :HALF]
    acc_hi = ysh[:, HALF:]
    gw = gw_ref[...]
    for k in range(TOP_K):
        lo, hi = _unpack_halves(gath[k])
        wk = gw[:, k:k + 1]
        acc_lo = acc_lo + wk * lo
        acc_hi = acc_hi + wk * hi
    y = jnp.concatenate([acc_lo, acc_hi], axis=1)
    out = x_ref[...] + gate_ref[...] * y
    if final_norm:
        ms = jnp.mean(out * out, axis=-1, keepdims=True)
        out = (out * lax.rsqrt(ms + RMS_EPS)) * fg_ref[...]
    o_ref[...] = out


def moe_combine(x2, xp, ys, dest, gw_t, gate, sg_bf, su_bf, sd_bf, final_g, seq, final_norm):
    t, d = x2.shape
    tps = seq // TD
    kern = functools.partial(_combine_kernel, final_norm=final_norm)
    return pl.pallas_call(
        kern,
        grid=(t // TD,),
        in_specs=[
            pl.BlockSpec((TOP_K, TD), lambda i: (0, i), memory_space=pltpu.SMEM),
            pl.BlockSpec((TD, d), lambda i: (i, 0)),
            pl.BlockSpec((TD, HALF), lambda i: (i, 0)),
            pl.BlockSpec((TD, TOP_K), lambda i: (i, 0)),
            pl.BlockSpec((None, 1, d), lambda i: (i // tps, 0, 0)),
            pl.BlockSpec((d, D_SHARED), lambda i: (0, 0)),
            pl.BlockSpec((d, D_SHARED), lambda i: (0, 0)),
            pl.BlockSpec((D_SHARED, d), lambda i: (0, 0)),
            pl.BlockSpec((1, d), lambda i: (0, 0)),
            pl.BlockSpec(memory_space=pl.ANY),
        ],
        out_specs=pl.BlockSpec((TD, d), lambda i: (i, 0)),
        out_shape=jax.ShapeDtypeStruct((t, d), f32),
        scratch_shapes=[
            pltpu.VMEM((TOP_K, TD, HALF), u32),
            pltpu.SemaphoreType.DMA(()),
        ],
        compiler_params=_cparams(("arbitrary",)),
        name="moe_combine",
    )(dest, x2, xp, gw_t, gate, sg_bf, su_bf, sd_bf, final_g.reshape(1, d), ys)


def moe_layer(x2, g, sh, sc, gate, router_w, router_bias, w_gate, w_up, w_down, sh_gate, sh_up, sh_down,
              final_g, seq, final_norm):
    t = x2.shape[0]
    a = t * TOP_K
    nblk = a // BM + N_EXPERTS
    rw_t = router_w.T
    rw_hi = rw_t.astype(bf16)
    rw_lo = (rw_t - rw_hi.astype(f32)).astype(bf16)
    rw_cat = jnp.concatenate([rw_hi, rw_lo], axis=0)

    xp, eidx, rk, gw, cnt = moe_route(x2, g, sh, sc, rw_cat, router_bias, seq)

    counts = cnt[:, 0]
    padded = ((counts + BM - 1) // BM) * BM
    cum_pad = jnp.cumsum(padded)
    pad_start = cum_pad - padded
    n_used = (cum_pad[-1] // BM).astype(i32).reshape(1)
    blk_e = jnp.minimum(jnp.searchsorted(cum_pad, jnp.arange(nblk, dtype=i32) * BM, side='right'),
                        N_EXPERTS - 1).astype(i32)
    blk_first = jnp.concatenate([jnp.ones((1,), i32), (blk_e[1:] != blk_e[:-1]).astype(i32)])
    dest = (jnp.take(pad_start, eidx, axis=0) + rk).astype(i32)
    ztail = jnp.maximum(cum_pad - BM, 0).astype(i32)

    xs = moe_dispatch(xp, dest, ztail, nblk * BM)
    ys = moe_gmm(xs, blk_e, blk_first, n_used, w_gate, w_up, w_down, nblk)
    return moe_combine(x2, xp, ys, dest, gw.T, gate, sh_gate.astype(bf16), sh_up.astype(bf16),
                       sh_down.astype(bf16), final_g, seq, final_norm)


def _qkv_weight(w_qkv):
    dq = N_Q_HEADS * HEAD_DIM
    dkv = N_KV_HEADS * HEAD_DIM
    d = w_qkv.shape[0]
    wq = w_qkv[:, :dq]
    wk = w_qkv[:, dq:dq + dkv].reshape(d, N_KV_HEADS, 1, HEAD_DIM)
    wv = w_qkv[:, dq + dkv:].reshape(d, N_KV_HEADS, 1, HEAD_DIM)
    wk2 = jnp.broadcast_to(wk, (d, N_KV_HEADS, 2, HEAD_DIM)).reshape(d, QK_COLS)
    wv2 = jnp.broadcast_to(wv, (d, N_KV_HEADS, 2, HEAD_DIM)).reshape(d, QK_COLS)
    return jnp.concatenate([wq, wk2, wv2], axis=1).astype(bf16)


def kernel(x, c, ada_w, ada_b, norm_mix_g, norm_ffn_g, hyb_w_in, conv_w, sgu_ln_g, sgu_ln_b, sgu_w, sgu_b,
           hyb_w_out, attn_w_qkv, attn_sinks, attn_w_o, rel_bias, router_w, router_bias, exp_w_gate,
           exp_w_up, exp_w_down, sh_w_gate, sh_w_up, sh_w_down, final_g):
    batch, seq, d = x.shape
    t = batch * seq
    x2 = x.reshape(t, d)
    mod = ada_mod(c, ada_w, ada_b)
    bias_m = _attn_bias_table(rel_bias)
    for l in range(DEPTH):
        parts = [mod[l, :, k * d:(k + 1) * d].reshape(batch, 1, d) for k in range(6)]
        sh_m, sc_m, g_m, sh_f, sc_f, g_f = parts
        i = l // 2
        if l % 2 == 0:
            ycat = hyb_in(x2, norm_mix_g[l], sh_m, sc_m, hyb_w_in[i].astype(bf16), conv_w[i], sgu_ln_g[i],
                          sgu_ln_b[i], sgu_w[i], sgu_b[i], seq)
            x2 = proj_residual(ycat, hyb_w_out[i].astype(bf16), x2, g_m, seq)
        else:
            qkv = qkv_proj(x2, norm_mix_g[l], sh_m, sc_m, _qkv_weight(attn_w_qkv[i]), seq, 1024)
            o = swa_attn(qkv, bias_m, attn_sinks[i], batch, seq)
            x2 = proj_residual(o, attn_w_o[i].astype(bf16), x2, g_m, seq)
        x2 = moe_layer(x2, norm_ffn_g[l], sh_f, sc_f, g_f, router_w[l], router_bias[l], exp_w_gate[l],
                       exp_w_up[l], exp_w_down[l], sh_w_gate[l], sh_w_up[l], sh_w_down[l], final_g, seq,
                       final_norm=(l == DEPTH - 1))
    return x2.reshape(batch, seq, d)
```

```python
import functools

import numpy as np
import jax
import jax.numpy as jnp
from jax import lax
from jax.experimental import pallas as pl
from jax.experimental.pallas import tpu as pltpu

f32 = jnp.float32
bf16 = jnp.bfloat16
i32 = jnp.int32
u32 = jnp.uint32

D_MODEL = 2048
DEPTH = 2
RMS_EPS = 1e-6
LN_EPS = 1e-5
D_CONV = 1024
CONV_WIDTH = 3
D_SGU = 1024
SGU_GROUPS = 8
SGU_HD = D_SGU // SGU_GROUPS
CHUNK = 128
HYB_IN = 3 * D_CONV + 2 * D_SGU
HEAD_DIM = 64
N_Q_HEADS = D_MODEL // HEAD_DIM
N_KV_HEADS = N_Q_HEADS // 8
GQA_GROUP = N_Q_HEADS // N_KV_HEADS
WINDOW = 128
ATT_BLOCK = 128
REL_BUCKETS = 32
REL_MAX_DIST = 128
N_EXPERTS = 64
TOP_K = 8
N_GROUPS = 8
TOPK_GROUPS = 4
E_PER_G = N_EXPERTS // N_GROUPS
D_EXPERT = 512
D_SHARED = 512
ROUTED_SCALE = 2.5

LANES = 128
HALF = D_MODEL // 2
VMEM_LIMIT = 56 * 1024 * 1024
TM = 512
TN_IN = 1024
TR = 512
TD = 256
BM = 512
ADA_TN = 1024


def _cparams(sem, vmem=VMEM_LIMIT):
    return pltpu.CompilerParams(dimension_semantics=sem, vmem_limit_bytes=vmem)


def _norm_mod(x, g, sh, sc):
    ms = jnp.mean(x * x, axis=-1, keepdims=True)
    y = x * lax.rsqrt(ms + RMS_EPS)
    return (y * g) * (1.0 + sc) + sh


def _pack_halves(y):
    h = y.shape[1] // 2
    lo = lax.bitcast_convert_type(y[:, :h].astype(bf16).astype(f32), u32)
    hi = lax.bitcast_convert_type(y[:, h:].astype(bf16).astype(f32), u32)
    return (lo >> 16) | (hi & jnp.uint32(0xFFFF0000))


def _unpack_halves(w):
    lo = lax.bitcast_convert_type(w << 16, f32)
    hi = lax.bitcast_convert_type(w & jnp.uint32(0xFFFF0000), f32)
    return lo, hi


def _gelu(x):
    return 0.5 * x * (1.0 + lax.erf(x * np.float32(np.sqrt(0.5))))


def _dot(a, b):
    return jnp.dot(a, b, preferred_element_type=f32)


def _dot_nt(a, b):
    return lax.dot_general(a, b, (((1,), (1,)), ((), ())), preferred_element_type=f32)


def _ada_kernel(c_ref, w_ref, b_ref, o_ref):
    ca = jax.nn.silu(c_ref[...]).astype(bf16)
    o_ref[...] = _dot(ca, w_ref[...].astype(bf16)) + b_ref[...]


def ada_mod(c, ada_w, ada_b):
    depth, d, n = ada_w.shape
    b = c.shape[0]
    return pl.pallas_call(
        _ada_kernel,
        grid=(depth, n // ADA_TN),
        in_specs=[
            pl.BlockSpec((b, d), lambda l, j: (0, 0)),
            pl.BlockSpec((None, d, ADA_TN), lambda l, j: (l, 0, j)),
            pl.BlockSpec((None, 1, ADA_TN), lambda l, j: (l, 0, j)),
        ],
        out_specs=pl.BlockSpec((None, b, ADA_TN), lambda l, j: (l, 0, j)),
        out_shape=jax.ShapeDtypeStruct((depth, b, n), f32),
        compiler_params=_cparams(("arbitrary", "arbitrary")),
        name="ada_mod",
    )(c, ada_w, ada_b.reshape(depth, 1, n))


def _hyb_in_kernel(x_ref, g_ref, sh_ref, sc_ref, w_ref, cw_ref, lng_ref, lnb_ref, sw_ref, sb_ref,
                   o_ref, xn_scr, a_scr, b_scr, carry_scr, *, tiles_per_seq):
    i = pl.program_id(0)
    j = pl.program_id(1)
    tm = x_ref.shape[0]

    @pl.when(j == 0)
    def _():
        xn_scr[...] = _norm_mod(x_ref[...], g_ref[...], sh_ref[...], sc_ref[...]).astype(bf16)

    @pl.when(jnp.logical_and(i == 0, j == 0))
    def _():
        carry_scr[...] = jnp.zeros_like(carry_scr)

    p = _dot(xn_scr[...], w_ref[...])

    @pl.when(j == 0)
    def _():
        a_scr[...] = p

    @pl.when(j == 1)
    def _():
        b_scr[...] = p

    @pl.when(j == 2)
    def _():
        z = b_scr[...] * p
        row = lax.broadcasted_iota(i32, z.shape, 0)
        first = (i % tiles_per_seq) == 0
        prev = jnp.where(first, 0.0, carry_scr[...])
        p1 = prev[7:8, :]
        p2 = prev[6:7, :]
        z1 = jnp.where(row == 0, p1, pltpu.roll(z, 1, 0))
        z2 = jnp.where(row == 0, p2, jnp.where(row == 1, p1, pltpu.roll(z, 2, 0)))
        cw = cw_ref[...]
        conv = cw[0:1, :] * z2 + cw[1:2, :] * z1 + cw[2:3, :] * z
        carry_scr[...] = z[tm - 8:, :]
        o_ref[:, :D_CONV] = (a_scr[...] * conv).astype(o_ref.dtype)

    @pl.when(j == 3)
    def _():
        a_scr[...] = _gelu(p)

    @pl.when(j == 4)
    def _():
        v = _gelu(p)
        mu = jnp.mean(v, axis=-1, keepdims=True)
        vc = v - mu
        var = jnp.mean(vc * vc, axis=-1, keepdims=True)
        v = (vc * lax.rsqrt(var + LN_EPS)) * lng_ref[...] + lnb_ref[...]
        vb = v.astype(bf16)
        nch = tm // CHUNK
        r = lax.broadcasted_iota(i32, (CHUNK, CHUNK), 0)
        c = lax.broadcasted_iota(i32, (CHUNK, CHUNK), 1)
        tril = r >= c
        for g in range(SGU_GROUPS):
            ws = jnp.where(tril, sw_ref[g], 0.0).astype(bf16)
            vg = jnp.concatenate(
                [vb[n * CHUNK:(n + 1) * CHUNK, g * SGU_HD:(g + 1) * SGU_HD] for n in range(nch)], axis=1)
            sg = _dot(ws, vg)
            bb = sb_ref[g]
            for n in range(nch):
                s = sg[:, n * SGU_HD:(n + 1) * SGU_HD] + bb
                u = a_scr[n * CHUNK:(n + 1) * CHUNK, g * SGU_HD:(g + 1) * SGU_HD]
                o_ref[n * CHUNK:(n + 1) * CHUNK, D_CONV + g * SGU_HD:D_CONV + (g + 1) * SGU_HD] = (
                    (u * s).astype(o_ref.dtype))


def hyb_in(x2, g, sh, sc, w_in_bf, conv_w, ln_g, ln_b, sgu_w, sgu_b, seq):
    t, d = x2.shape
    tps = seq // TM
    nj = HYB_IN // TN_IN
    sbb = jnp.broadcast_to(sgu_b[:, :, None], (SGU_GROUPS, CHUNK, SGU_HD))
    kern = functools.partial(_hyb_in_kernel, tiles_per_seq=tps)
    return pl.pallas_call(
        kern,
        grid=(t // TM, nj),
        in_specs=[
            pl.BlockSpec((TM, d), lambda i, j: (i, 0)),
            pl.BlockSpec((1, d), lambda i, j: (0, 0)),
            pl.BlockSpec((None, 1, d), lambda i, j: (i // tps, 0, 0)),
            pl.BlockSpec((None, 1, d), lambda i, j: (i // tps, 0, 0)),
            pl.BlockSpec((d, TN_IN), lambda i, j: (0, j)),
            pl.BlockSpec((CONV_WIDTH, D_CONV), lambda i, j: (0, 0)),
            pl.BlockSpec((1, D_SGU), lambda i, j: (0, 0)),
            pl.BlockSpec((1, D_SGU), lambda i, j: (0, 0)),
            pl.BlockSpec((SGU_GROUPS, CHUNK, CHUNK), lambda i, j: (0, 0, 0)),
            pl.BlockSpec((SGU_GROUPS, CHUNK, SGU_HD), lambda i, j: (0, 0, 0)),
        ],
        out_specs=pl.BlockSpec((TM, D_CONV + D_SGU), lambda i, j: (i, 0)),
        out_shape=jax.ShapeDtypeStruct((t, D_CONV + D_SGU), bf16),
        scratch_shapes=[
            pltpu.VMEM((TM, d), bf16),
            pltpu.VMEM((TM, TN_IN), f32),
            pltpu.VMEM((TM, TN_IN), f32),
            pltpu.VMEM((8, D_CONV), f32),
        ],
        compiler_params=_cparams(("arbitrary", "arbitrary")),
        name="hyb_in",
    )(x2, g.reshape(1, d), sh, sc, w_in_bf, conv_w, ln_g.reshape(1, -1), ln_b.reshape(1, -1), sgu_w, sbb)


def _qkv_kernel(x_ref, g_ref, sh_ref, sc_ref, w_ref, o_ref, xn_scr):
    @pl.when(pl.program_id(1) == 0)
    def _():
        xn_scr[...] = _norm_mod(x_ref[...], g_ref[...], sh_ref[...], sc_ref[...]).astype(bf16)

    o_ref[...] = _dot(xn_scr[...], w_ref[...]).astype(o_ref.dtype)


def qkv_proj(x2, g, sh, sc, w_bf, seq, tn):
    t, d = x2.shape
    n = w_bf.shape[1]
    tps = seq // TM
    return pl.pallas_call(
        _qkv_kernel,
        grid=(t // TM, n // tn),
        in_specs=[
            pl.BlockSpec((TM, d), lambda i, j: (i, 0)),
            pl.BlockSpec((1, d), lambda i, j: (0, 0)),
            pl.BlockSpec((None, 1, d), lambda i, j: (i // tps, 0, 0)),
            pl.BlockSpec((None, 1, d), lambda i, j: (i // tps, 0, 0)),
            pl.BlockSpec((d, tn), lambda i, j: (0, j)),
        ],
        out_specs=pl.BlockSpec((TM, tn), lambda i, j: (i, j)),
        out_shape=jax.ShapeDtypeStruct((t, n), bf16),
        scratch_shapes=[pltpu.VMEM((TM, d), bf16)],
        compiler_params=_cparams(("arbitrary", "arbitrary")),
        name="qkv_proj",
    )(x2, g.reshape(1, d), sh, sc, w_bf)


QK_COLS = N_KV_HEADS * 2 * HEAD_DIM


def _attn_kernel(q_ref, kp_ref, kc_ref, vp_ref, vc_ref, bias_ref, sink_ref, o_ref):
    blk = pl.program_id(1)
    lane = lax.broadcasted_iota(i32, (2 * ATT_BLOCK, 2 * HEAD_DIM), 1)
    low = lane < HEAD_DIM
    col = lax.broadcasted_iota(i32, (ATT_BLOCK, 2 * ATT_BLOCK), 1)
    no_prev = jnp.logical_and(blk == 0, col < ATT_BLOCK)
    zero = jnp.zeros((), bf16)
    for kh in range(N_KV_HEADS):
        cs = slice(kh * 2 * HEAD_DIM, (kh + 1) * 2 * HEAD_DIM)
        kk = jnp.concatenate([kp_ref[:, cs], kc_ref[:, cs]], axis=0)
        vv = jnp.concatenate([vp_ref[:, cs], vc_ref[:, cs]], axis=0)
        k0 = jnp.where(low, kk, zero)
        k1 = jnp.where(low, zero, kk)
        v0 = jnp.where(low, vv, zero)
        v1 = jnp.where(low, zero, vv)
        for pr in range(GQA_GROUP // 2):
            h0 = kh * GQA_GROUP + 2 * pr
            qp = q_ref[:, h0 * HEAD_DIM:(h0 + 2) * HEAD_DIM] * jnp.asarray(HEAD_DIM ** -0.5, bf16)
            acc = None
            for hh, (kz, vz) in enumerate(((k0, v0), (k1, v1))):
                h = h0 + hh
                s = _dot_nt(qp, kz) + bias_ref[h]
                s = jnp.where(no_prev, -jnp.inf, s)
                sk = sink_ref[h]
                m = jnp.maximum(jnp.max(s, axis=-1, keepdims=True), sk)
                p = jnp.exp(s - m)
                den = jnp.sum(p, axis=-1, keepdims=True) + jnp.exp(sk - m)
                o = _dot((p / den).astype(bf16), vz)
                acc = o if acc is None else acc + o
            o_ref[:, h0 * HEAD_DIM:(h0 + 2) * HEAD_DIM] = acc.astype(o_ref.dtype)


def swa_attn(qkv, bias_m, sinks, batch, seq):
    t = qkv.shape[0]
    nb = seq // ATT_BLOCK
    dq = N_Q_HEADS * HEAD_DIM
    kcol = dq // QK_COLS
    vcol = kcol + 1

    def prev(b, i):
        return b * nb + jnp.maximum(i - 1, 0)

    return pl.pallas_call(
        _attn_kernel,
        grid=(batch, nb),
        in_specs=[
            pl.BlockSpec((ATT_BLOCK, dq), lambda b, i: (b * nb + i, 0)),
            pl.BlockSpec((ATT_BLOCK, QK_COLS), lambda b, i: (prev(b, i), kcol)),
            pl.BlockSpec((ATT_BLOCK, QK_COLS), lambda b, i: (b * nb + i, kcol)),
            pl.BlockSpec((ATT_BLOCK, QK_COLS), lambda b, i: (prev(b, i), vcol)),
            pl.BlockSpec((ATT_BLOCK, QK_COLS), lambda b, i: (b * nb + i, vcol)),
            pl.BlockSpec((N_Q_HEADS, ATT_BLOCK, 2 * ATT_BLOCK), lambda b, i: (0, 0, 0)),
            pl.BlockSpec(memory_space=pltpu.SMEM),
        ],
        out_specs=pl.BlockSpec((ATT_BLOCK, dq), lambda b, i: (b * nb + i, 0)),
        out_shape=jax.ShapeDtypeStruct((t, dq), bf16),
        compiler_params=_cparams(("arbitrary", "arbitrary")),
        name="swa_attn",
    )(qkv, qkv, qkv, qkv, qkv, bias_m, sinks)


def _t5_bucket(dist):
    n = np.maximum(dist, 0)
    max_exact = REL_BUCKETS // 2
    large = max_exact + (np.log(np.maximum(n, 1) / max_exact) / np.log(REL_MAX_DIST / max_exact)
                         * (REL_BUCKETS - max_exact)).astype(np.int32)
    large = np.minimum(large, REL_BUCKETS - 1)
    return np.where(n < max_exact, n, large).astype(np.int32)


def _attn_bias_table(rel_bias):
    t_loc = np.arange(ATT_BLOCK)[:, None]
    j_loc = np.arange(2 * ATT_BLOCK)[None, :]
    dist = ATT_BLOCK + t_loc - j_loc
    band = (dist >= 0) & (dist < WINDOW)
    bias = rel_bias[_t5_bucket(dist)].astype(f32).transpose(2, 0, 1)
    return jnp.where(jnp.asarray(band)[None], bias, -jnp.inf)


def _proj_res_kernel(a_ref, w_ref, x_ref, gate_ref, o_ref):
    o_ref[...] = x_ref[...] + gate_ref[...] * _dot(a_ref[...], w_ref[...])


def proj_residual(a, w_bf, x2, gate, seq):
    t, k = a.shape
    d = w_bf.shape[1]
    tps = seq // TM
    return pl.pallas_call(
        _proj_res_kernel,
        grid=(t // TM,),
        in_specs=[
            pl.BlockSpec((TM, k), lambda i: (i, 0)),
            pl.BlockSpec((k, d), lambda i: (0, 0)),
            pl.BlockSpec((TM, d), lambda i: (i, 0)),
            pl.BlockSpec((None, 1, d), lambda i: (i // tps, 0, 0)),
        ],
        out_specs=pl.BlockSpec((TM, d), lambda i: (i, 0)),
        out_shape=jax.ShapeDtypeStruct((t, d), f32),
        compiler_params=_cparams(("arbitrary",)),
        name="proj_residual",
    )(a, w_bf, x2, gate)


def _route_kernel(x_ref, g_ref, sh_ref, sc_ref, rw_ref, rb_ref,
                  xp_ref, eidx_ref, rk_ref, gw_ref, cnt_ref, sel_scr, grp_scr, carry_scr):
    tm = x_ref.shape[0]
    ne = N_EXPERTS

    @pl.when(pl.program_id(0) == 0)
    def _():
        carry_scr[...] = jnp.zeros_like(carry_scr)

    xn = _norm_mod(x_ref[...], g_ref[...], sh_ref[...], sc_ref[...])
    xp_ref[...] = _pack_halves(xn)
    x_hi = xn.astype(bf16)
    x_lo = (xn - x_hi.astype(f32)).astype(bf16)
    rw = rw_ref[...]
    a = _dot_nt(rw, x_hi)
    b = _dot_nt(rw[:ne], x_lo)
    logits = a[:ne] + a[ne:] + b
    scores = jax.nn.sigmoid(logits)
    sel = scores + rb_ref[...]

    for g in range(N_GROUPS):
        tile = sel[g * E_PER_G:(g + 1) * E_PER_G, :]
        t1 = jnp.max(tile, axis=0, keepdims=True)
        dup = jnp.sum(jnp.where(tile == t1, 1.0, 0.0), axis=0, keepdims=True) >= 2.0
        t2 = jnp.max(jnp.where(tile < t1, tile, -jnp.inf), axis=0, keepdims=True)
        grp_scr[g:g + 1, :] = t1 + jnp.where(dup, t1, t2)
    gs = grp_scr[...]

    gi = lax.broadcasted_iota(i32, (N_GROUPS, tm), 0)
    grank = jnp.zeros((N_GROUPS, tm), i32)
    for g in range(N_GROUPS):
        r = grp_scr[g:g + 1, :]
        ge = jnp.where(r >= gs, 1, 0)
        gt = jnp.where(r > gs, 1, 0)
        grank = grank + jnp.where(gi > g, ge, gt)
    grp_scr[...] = jnp.where(grank < TOPK_GROUPS, 1.0, 0.0)

    masked = jnp.concatenate(
        [jnp.where(grp_scr[g:g + 1, :] > 0.5, sel[g * E_PER_G:(g + 1) * E_PER_G, :], -jnp.inf)
         for g in range(N_GROUPS)], axis=0)

    sel_scr[...] = masked
    ei = lax.broadcasted_iota(i32, (ne, tm), 0)
    rank = jnp.zeros((ne, tm), i32)
    for e in range(ne):
        r = sel_scr[e:e + 1, :]
        ge = jnp.where(r >= masked, 1, 0)
        gt = jnp.where(r > masked, 1, 0)
        rank = rank + jnp.where(ei > e, ge, gt)
    chosen = rank < TOP_K
    chf = jnp.where(chosen, 1.0, 0.0)

    wun = jnp.where(chosen, scores, 0.0)
    gwt = wun / jnp.sum(wun, axis=0, keepdims=True) * ROUTED_SCALE

    rr = lax.broadcasted_iota(i32, (tm, tm), 0)
    cc = lax.broadcasted_iota(i32, (tm, tm), 1)
    upper = jnp.where(rr < cc, 1.0, 0.0).astype(bf16)
    chb = chf.astype(bf16)
    pos = carry_scr[:, 0:1] + _dot(chb, upper)
    carry_scr[...] = carry_scr[...] + jnp.sum(chf, axis=1, keepdims=True)
    cnt_ref[...] = carry_scr[...].astype(i32)

    er = lax.broadcasted_iota(i32, (ne, ne), 0)
    ec = lax.broadcasted_iota(i32, (ne, ne), 1)
    lower = jnp.where(ec < er, 1.0, 0.0).astype(bf16)
    below = _dot(lower, chb)
    slot = jnp.where(chosen, below, -1.0)
    eif = ei.astype(f32)
    for k in range(TOP_K):
        mk = slot == float(k)
        eidx_ref[k:k + 1, :] = jnp.sum(jnp.where(mk, eif, 0.0), axis=0, keepdims=True).astype(i32)
        rk_ref[k:k + 1, :] = jnp.sum(jnp.where(mk, pos, 0.0), axis=0, keepdims=True).astype(i32)
        gw_ref[k:k + 1, :] = jnp.sum(jnp.where(mk, gwt, 0.0), axis=0, keepdims=True)


def moe_route(x2, g, sh, sc, rw_cat, rbias, seq):
    t, d = x2.shape
    tps = seq // TR
    return pl.pallas_call(
        _route_kernel,
        grid=(t // TR,),
        in_specs=[
            pl.BlockSpec((TR, d), lambda i: (i, 0)),
            pl.BlockSpec((1, d), lambda i: (0, 0)),
            pl.BlockSpec((None, 1, d), lambda i: (i // tps, 0, 0)),
            pl.BlockSpec((None, 1, d), lambda i: (i // tps, 0, 0)),
            pl.BlockSpec((2 * N_EXPERTS, d), lambda i: (0, 0)),
            pl.BlockSpec((N_EXPERTS, 1), lambda i: (0, 0)),
        ],
        out_specs=[
            pl.BlockSpec((TR, HALF), lambda i: (i, 0)),
            pl.BlockSpec((TOP_K, TR), lambda i: (0, i)),
            pl.BlockSpec((TOP_K, TR), lambda i: (0, i)),
            pl.BlockSpec((TOP_K, TR), lambda i: (0, i)),
            pl.BlockSpec((N_EXPERTS, LANES), lambda i: (0, 0)),
        ],
        out_shape=[
            jax.ShapeDtypeStruct((t, HALF), u32),
            jax.ShapeDtypeStruct((TOP_K, t), i32),
            jax.ShapeDtypeStruct((TOP_K, t), i32),
            jax.ShapeDtypeStruct((TOP_K, t), f32),
            jax.ShapeDtypeStruct((N_EXPERTS, LANES), i32),
        ],
        scratch_shapes=[
            pltpu.VMEM((N_EXPERTS, TR), f32),
            pltpu.VMEM((N_GROUPS, TR), f32),
            pltpu.VMEM((N_EXPERTS, LANES), f32),
        ],
        compiler_params=_cparams(("arbitrary",)),
        name="moe_route",
    )(x2, g.reshape(1, d), sh, sc, rw_cat, rbias.reshape(N_EXPERTS, 1))


def _row_copy(src, s, dst, d, sem):
    return pltpu.make_async_copy(src.at[pl.ds(s, 1), :], dst.at[pl.ds(d, 1), :], sem)


def _dispatch_kernel(ztail_ref, dest_ref, xp_ref, xs_ref, zeros_scr, sem, zsem):
    td = xp_ref.shape[0]

    @pl.when(pl.program_id(0) == 0)
    def _():
        zeros_scr[...] = jnp.zeros_like(zeros_scr)

        def zcopy(e):
            z = pl.multiple_of(ztail_ref[e], BM)
            return pltpu.make_async_copy(zeros_scr, xs_ref.at[pl.ds(z, BM), :], zsem)

        def zstart(e, c):
            zcopy(e).start()
            return c

        def zwait(e, c):
            zcopy(e).wait()
            return c

        lax.fori_loop(0, N_EXPERTS, zstart, 0)
        lax.fori_loop(0, N_EXPERTS, zwait, 0)

    def start(t, c):
        for k in range(TOP_K):
            _row_copy(xp_ref, t, xs_ref, dest_ref[k, t], sem).start()
        return c

    def wait(t, c):
        for k in range(TOP_K):
            _row_copy(xp_ref, t, xs_ref, dest_ref[k, t], sem).wait()
        return c

    lax.fori_loop(0, td, start, 0)
    lax.fori_loop(0, td, wait, 0)


def moe_dispatch(xp, dest, ztail, nrows):
    t = xp.shape[0]
    grid_spec = pltpu.PrefetchScalarGridSpec(
        num_scalar_prefetch=1,
        grid=(t // TD,),
        in_specs=[
            pl.BlockSpec((TOP_K, TD), lambda i, z: (0, i), memory_space=pltpu.SMEM),
            pl.BlockSpec((TD, HALF), lambda i, z: (i, 0)),
        ],
        out_specs=pl.BlockSpec(memory_space=pl.ANY),
        scratch_shapes=[
            pltpu.VMEM((BM, HALF), u32),
            pltpu.SemaphoreType.DMA(()),
            pltpu.SemaphoreType.DMA(()),
        ],
    )
    return pl.pallas_call(
        _dispatch_kernel,
        grid_spec=grid_spec,
        out_shape=jax.ShapeDtypeStruct((nrows, HALF), u32),
        compiler_params=_cparams(("arbitrary",)),
        name="moe_dispatch",
    )(ztail, dest, xp)


def _ffn_packed(xw, wg, wu, wd):
    lo, hi = _unpack_halves(xw)
    xa = lo.astype(bf16)
    xb = hi.astype(bf16)
    hg = _dot(xa, wg[:HALF, :]) + _dot(xb, wg[HALF:, :])
    hu = _dot(xa, wu[:HALF, :]) + _dot(xb, wu[HALF:, :])
    a = (jax.nn.silu(hg) * hu).astype(bf16)
    return _dot(a, wd[...])


def _gmm_kernel(be_ref, bf_ref, nu_ref, xs_ref, wg_ref, wu_ref, wd_ref, ys_ref, wg_s, wu_s, wd_s):
    i = pl.program_id(0)

    @pl.when(bf_ref[i] == 1)
    def _():
        wg_s[...] = wg_ref[...].astype(bf16)
        wu_s[...] = wu_ref[...].astype(bf16)
        wd_s[...] = wd_ref[...].astype(bf16)

    @pl.when(i < nu_ref[0])
    def _():
        ys_ref[...] = _pack_halves(_ffn_packed(xs_ref[...], wg_s, wu_s, wd_s))

    @pl.when(i >= nu_ref[0])
    def _():
        ys_ref[...] = jnp.zeros_like(ys_ref)


def moe_gmm(xs, blk_e, blk_first, n_used, w_gate, w_up, w_down, layer, nblk):
    d, de = w_gate.shape[2], w_gate.shape[3]
    grid_spec = pltpu.PrefetchScalarGridSpec(
        num_scalar_prefetch=3,
        grid=(nblk,),
        in_specs=[
            pl.BlockSpec((BM, HALF), lambda i, be, bf, nu: (jnp.minimum(i, nu[0] - 1), 0)),
            pl.BlockSpec((None, None, d, de), lambda i, be, bf, nu: (layer, be[i], 0, 0)),
            pl.BlockSpec((None, None, d, de), lambda i, be, bf, nu: (layer, be[i], 0, 0)),
            pl.BlockSpec((None, None, de, d), lambda i, be, bf, nu: (layer, be[i], 0, 0)),
        ],
        out_specs=pl.BlockSpec((BM, HALF), lambda i, be, bf, nu: (i, 0)),
        scratch_shapes=[
            pltpu.VMEM((d, de), bf16),
            pltpu.VMEM((d, de), bf16),
            pltpu.VMEM((de, d), bf16),
        ],
    )
    return pl.pallas_call(
        _gmm_kernel,
        grid_spec=grid_spec,
        out_shape=jax.ShapeDtypeStruct((nblk * BM, HALF), u32),
        compiler_params=_cparams(("arbitrary",)),
        name="moe_gmm",
    )(blk_e, blk_first, n_used, xs, w_gate, w_up, w_down)


def _combine_kernel(dest_ref, x_ref, xp_ref, gw_ref, gate_ref, sg_ref, su_ref, sd_ref, fg_ref, ys_ref,
                    o_ref, gath, sem, *, final_norm):
    td = x_ref.shape[0]

    def copy(t, k):
        return pltpu.make_async_copy(ys_ref.at[pl.ds(dest_ref[k, t], 1), :], gath.at[k, pl.ds(t, 1), :], sem)

    def start(t, c):
        for k in range(TOP_K):
            copy(t, k).start()
        return c

    def wait(t, c):
        for k in range(TOP_K):
            copy(t, k).wait()
        return c

    lax.fori_loop(0, td, start, 0)
    ysh = _ffn_packed(xp_ref[...], sg_ref, su_ref, sd_ref)
    lax.fori_loop(0, td, wait, 0)

    acc_lo = ysh[:, :HALF]
    acc_hi = ysh[:, HALF:]
    gw = gw_ref[...]
    for k in range(TOP_K):
        lo, hi = _unpack_halves(gath[k])
        wk = gw[:, k:k + 1]
        acc_lo = acc_lo + wk * lo
        acc_hi = acc_hi + wk * hi
    y = jnp.concatenate([acc_lo, acc_hi], axis=1)
    out = x_ref[...] + gate_ref[...] * y
    if final_norm:
        ms = jnp.mean(out * out, axis=-1, keepdims=True)
        out = (out * lax.rsqrt(ms + RMS_EPS)) * fg_ref[...]
    o_ref[...] = out


def moe_combine(x2, xp, ys, dest, gw_t, gate, sg_bf, su_bf, sd_bf, final_g, seq, final_norm):
    t, d = x2.shape
    tps = seq // TD
    kern = functools.partial(_combine_kernel, final_norm=final_norm)
    return pl.pallas_call(
        kern,
        grid=(t // TD,),
        in_specs=[
            pl.BlockSpec((TOP_K, TD), lambda i: (0, i), memory_space=pltpu.SMEM),
            pl.BlockSpec((TD, d), lambda i: (i, 0)),
            pl.BlockSpec((TD, HALF), lambda i: (i, 0)),
            pl.BlockSpec((TD, TOP_K), lambda i: (i, 0)),
            pl.BlockSpec((None, 1, d), lambda i: (i // tps, 0, 0)),
            pl.BlockSpec((d, D_SHARED), lambda i: (0, 0)),
            pl.BlockSpec((d, D_SHARED), lambda i: (0, 0)),
            pl.BlockSpec((D_SHARED, d), lambda i: (0, 0)),
            pl.BlockSpec((1, d), lambda i: (0, 0)),
            pl.BlockSpec(memory_space=pl.ANY),
        ],
        out_specs=pl.BlockSpec((TD, d), lambda i: (i, 0)),
        out_shape=jax.ShapeDtypeStruct((t, d), f32),
        scratch_shapes=[
            pltpu.VMEM((TOP_K, TD, HALF), u32),
            pltpu.SemaphoreType.DMA(()),
        ],
        compiler_params=_cparams(("arbitrary",)),
        name="moe_combine",
    )(dest, x2, xp, gw_t, gate, sg_bf, su_bf, sd_bf, final_g.reshape(1, d), ys)


SLOT_TN = 4096


def _slots_kernel(ps_ref, eidx_ref, rk_ref, o_ref):
    e = eidx_ref[...]
    start = jnp.zeros_like(e)
    for j in range(N_EXPERTS):
        start = jnp.where(e == j, ps_ref[j], start)
    o_ref[...] = start + rk_ref[...]


def moe_slots(pad_start, eidx, rk):
    t = eidx.shape[1]
    tn = min(t, SLOT_TN)
    spec = pl.BlockSpec((TOP_K, tn), lambda i, ps: (0, i))
    grid_spec = pltpu.PrefetchScalarGridSpec(
        num_scalar_prefetch=1, grid=(t // tn,), in_specs=[spec, spec], out_specs=spec)
    return pl.pallas_call(
        _slots_kernel,
        grid_spec=grid_spec,
        out_shape=jax.ShapeDtypeStruct((TOP_K, t), i32),
        compiler_params=_cparams(("arbitrary",)),
        name="moe_slots",
    )(pad_start, eidx, rk)


def moe_layer(x2, g, sh, sc, gate, router_w, router_bias, w_gate, w_up, w_down, layer, sh_gate, sh_up, sh_down,
              final_g, seq, final_norm):
    t = x2.shape[0]
    a = t * TOP_K
    nblk = a // BM + N_EXPERTS
    rw_t = router_w.T
    rw_hi = rw_t.astype(bf16)
    rw_lo = (rw_t - rw_hi.astype(f32)).astype(bf16)
    rw_cat = jnp.concatenate([rw_hi, rw_lo], axis=0)

    xp, eidx, rk, gw, cnt = moe_route(x2, g, sh, sc, rw_cat, router_bias, seq)

    counts = cnt[:, 0]
    padded = ((counts + BM - 1) // BM) * BM
    cum_pad = jnp.cumsum(padded)
    pad_start = cum_pad - padded
    n_used = (cum_pad[-1] // BM).astype(i32).reshape(1)
    blk_row = jnp.arange(nblk, dtype=i32) * BM
    blk_e = jnp.minimum(jnp.sum((cum_pad[None, :] <= blk_row[:, None]).astype(i32), axis=1), N_EXPERTS - 1)
    blk_first = jnp.concatenate([jnp.ones((1,), i32), (blk_e[1:] != blk_e[:-1]).astype(i32)])
    dest = moe_slots(pad_start.astype(i32), eidx, rk)
    ztail = jnp.maximum(cum_pad - BM, 0).astype(i32)

    xs = moe_dispatch(xp, dest, ztail, nblk * BM)
    ys = moe_gmm(xs, blk_e, blk_first, n_used, w_gate, w_up, w_down, layer, nblk)
    return moe_combine(x2, xp, ys, dest, gw.T, gate, sh_gate.astype(bf16), sh_up.astype(bf16),
                       sh_down.astype(bf16), final_g, seq, final_norm)


def _qkv_weight(w_qkv):
    dq = N_Q_HEADS * HEAD_DIM
    dkv = N_KV_HEADS * HEAD_DIM
    d = w_qkv.shape[0]
    wq = w_qkv[:, :dq]
    wk = w_qkv[:, dq:dq + dkv].reshape(d, N_KV_HEADS, 1, HEAD_DIM)
    wv = w_qkv[:, dq + dkv:].reshape(d, N_KV_HEADS, 1, HEAD_DIM)
    wk2 = jnp.broadcast_to(wk, (d, N_KV_HEADS, 2, HEAD_DIM)).reshape(d, QK_COLS)
    wv2 = jnp.broadcast_to(wv, (d, N_KV_HEADS, 2, HEAD_DIM)).reshape(d, QK_COLS)
    return jnp.concatenate([wq, wk2, wv2], axis=1).astype(bf16)


def kernel(x, c, ada_w, ada_b, norm_mix_g, norm_ffn_g, hyb_w_in, conv_w, sgu_ln_g, sgu_ln_b, sgu_w, sgu_b,
           hyb_w_out, attn_w_qkv, attn_sinks, attn_w_o, rel_bias, router_w, router_bias, exp_w_gate,
           exp_w_up, exp_w_down, sh_w_gate, sh_w_up, sh_w_down, final_g):
    batch, seq, d = x.shape
    t = batch * seq
    x2 = x.reshape(t, d)
    mod = ada_mod(c, ada_w, ada_b)
    bias_m = _attn_bias_table(rel_bias)
    for l in range(DEPTH):
        parts = [mod[l, :, k * d:(k + 1) * d].reshape(batch, 1, d) for k in range(6)]
        sh_m, sc_m, g_m, sh_f, sc_f, g_f = parts
        i = l // 2
        if l % 2 == 0:
            ycat = hyb_in(x2, norm_mix_g[l], sh_m, sc_m, hyb_w_in[i].astype(bf16), conv_w[i], sgu_ln_g[i],
                          sgu_ln_b[i], sgu_w[i], sgu_b[i], seq)
            x2 = proj_residual(ycat, hyb_w_out[i].astype(bf16), x2, g_m, seq)
        else:
            qkv = qkv_proj(x2, norm_mix_g[l], sh_m, sc_m, _qkv_weight(attn_w_qkv[i]), seq, 1024)
            o = swa_attn(qkv, bias_m, attn_sinks[i], batch, seq)
            x2 = proj_residual(o, attn_w_o[i].astype(bf16), x2, g_m, seq)
        x2 = moe_layer(x2, norm_ffn_g[l], sh_f, sc_f, g_f, router_w[l], router_bias[l], exp_w_gate,
                       exp_w_up, exp_w_down, l, sh_w_gate[l], sh_w_up[l], sh_w_down[l], final_g, seq,
                       final_norm=(l == DEPTH - 1))
    return x2.reshape(batch, seq, d)
```

```python
import functools

import numpy as np
import jax
import jax.numpy as jnp
from jax import lax
from jax.experimental import pallas as pl
from jax.experimental.pallas import tpu as pltpu

f32 = jnp.float32
bf16 = jnp.bfloat16
i32 = jnp.int32
u32 = jnp.uint32

D_MODEL = 2048
DEPTH = 2
RMS_EPS = 1e-6
LN_EPS = 1e-5
D_CONV = 1024
CONV_WIDTH = 3
D_SGU = 1024
SGU_GROUPS = 8
SGU_HD = D_SGU // SGU_GROUPS
CHUNK = 128
HYB_IN = 3 * D_CONV + 2 * D_SGU
HEAD_DIM = 64
N_Q_HEADS = D_MODEL // HEAD_DIM
N_KV_HEADS = N_Q_HEADS // 8
GQA_GROUP = N_Q_HEADS // N_KV_HEADS
WINDOW = 128
ATT_BLOCK = 128
REL_BUCKETS = 32
REL_MAX_DIST = 128
N_EXPERTS = 64
TOP_K = 8
N_GROUPS = 8
TOPK_GROUPS = 4
E_PER_G = N_EXPERTS // N_GROUPS
D_EXPERT = 512
D_SHARED = 512
ROUTED_SCALE = 2.5

LANES = 128
HALF = D_MODEL // 2
VMEM_LIMIT = 56 * 1024 * 1024
TM = 512
TN_IN = 1024
TR = 512
TD = 256
BM = 512
ADA_TN = 1024


def _cparams(sem, vmem=VMEM_LIMIT):
    return pltpu.CompilerParams(dimension_semantics=sem, vmem_limit_bytes=vmem)


def _norm_mod(x, g, sh, sc):
    ms = jnp.mean(x * x, axis=-1, keepdims=True)
    y = x * lax.rsqrt(ms + RMS_EPS)
    return (y * g) * (1.0 + sc) + sh


def _pack_halves(y):
    h = y.shape[1] // 2
    lo = lax.bitcast_convert_type(y[:, :h].astype(bf16).astype(f32), u32)
    hi = lax.bitcast_convert_type(y[:, h:].astype(bf16).astype(f32), u32)
    return (lo >> 16) | (hi & jnp.uint32(0xFFFF0000))


def _unpack_halves(w):
    lo = lax.bitcast_convert_type(w << 16, f32)
    hi = lax.bitcast_convert_type(w & jnp.uint32(0xFFFF0000), f32)
    return lo, hi


ROW_SUB = HALF // LANES


def _store_row_tiles(ref, packed):
    n = packed.shape[0]
    for c in range(ROW_SUB):
        ref[pl.ds(c, n, stride=ROW_SUB), :] = packed[:, c * LANES:(c + 1) * LANES]


def _load_row_tiles(ref, n):
    los, his = [], []
    for c in range(ROW_SUB):
        lo, hi = _unpack_halves(ref[pl.ds(c, n, stride=ROW_SUB), :])
        los.append(lo)
        his.append(hi)
    return los, his


def _gelu(x):
    return 0.5 * x * (1.0 + lax.erf(x * np.float32(np.sqrt(0.5))))


def _dot(a, b):
    return jnp.dot(a, b, preferred_element_type=f32)


def _dot_nt(a, b):
    return lax.dot_general(a, b, (((1,), (1,)), ((), ())), preferred_element_type=f32)


def _ada_kernel(c_ref, w_ref, b_ref, o_ref):
    ca = jax.nn.silu(c_ref[...]).astype(bf16)
    o_ref[...] = _dot(ca, w_ref[...].astype(bf16)) + b_ref[...]


def ada_mod(c, ada_w, ada_b):
    depth, d, n = ada_w.shape
    b = c.shape[0]
    return pl.pallas_call(
        _ada_kernel,
        grid=(depth, n // ADA_TN),
        in_specs=[
            pl.BlockSpec((b, d), lambda l, j: (0, 0)),
            pl.BlockSpec((None, d, ADA_TN), lambda l, j: (l, 0, j)),
            pl.BlockSpec((None, 1, ADA_TN), lambda l, j: (l, 0, j)),
        ],
        out_specs=pl.BlockSpec((None, b, ADA_TN), lambda l, j: (l, 0, j)),
        out_shape=jax.ShapeDtypeStruct((depth, b, n), f32),
        compiler_params=_cparams(("arbitrary", "arbitrary")),
        name="ada_mod",
    )(c, ada_w, ada_b.reshape(depth, 1, n))


def _hyb_in_kernel(x_ref, g_ref, sh_ref, sc_ref, w_ref, cw_ref, lng_ref, lnb_ref, sw_ref, sb_ref,
                   o_ref, xn_scr, a_scr, b_scr, carry_scr, *, tiles_per_seq):
    i = pl.program_id(0)
    j = pl.program_id(1)
    tm = x_ref.shape[0]

    @pl.when(j == 0)
    def _():
        xn_scr[...] = _norm_mod(x_ref[...], g_ref[...], sh_ref[...], sc_ref[...]).astype(bf16)

    @pl.when(jnp.logical_and(i == 0, j == 0))
    def _():
        carry_scr[...] = jnp.zeros_like(carry_scr)

    p = _dot(xn_scr[...], w_ref[...])

    @pl.when(j == 0)
    def _():
        a_scr[...] = p

    @pl.when(j == 1)
    def _():
        b_scr[...] = p

    @pl.when(j == 2)
    def _():
        z = b_scr[...] * p
        row = lax.broadcasted_iota(i32, z.shape, 0)
        first = (i % tiles_per_seq) == 0
        prev = jnp.where(first, 0.0, carry_scr[...])
        p1 = prev[7:8, :]
        p2 = prev[6:7, :]
        z1 = jnp.where(row == 0, p1, pltpu.roll(z, 1, 0))
        z2 = jnp.where(row == 0, p2, jnp.where(row == 1, p1, pltpu.roll(z, 2, 0)))
        cw = cw_ref[...]
        conv = cw[0:1, :] * z2 + cw[1:2, :] * z1 + cw[2:3, :] * z
        carry_scr[...] = z[tm - 8:, :]
        o_ref[:, :D_CONV] = (a_scr[...] * conv).astype(o_ref.dtype)

    @pl.when(j == 3)
    def _():
        a_scr[...] = _gelu(p)

    @pl.when(j == 4)
    def _():
        v = _gelu(p)
        mu = jnp.mean(v, axis=-1, keepdims=True)
        vc = v - mu
        var = jnp.mean(vc * vc, axis=-1, keepdims=True)
        v = (vc * lax.rsqrt(var + LN_EPS)) * lng_ref[...] + lnb_ref[...]
        vb = v.astype(bf16)
        nch = tm // CHUNK
        r = lax.broadcasted_iota(i32, (CHUNK, CHUNK), 0)
        c = lax.broadcasted_iota(i32, (CHUNK, CHUNK), 1)
        tril = r >= c
        for g in range(SGU_GROUPS):
            ws = jnp.where(tril, sw_ref[g], 0.0).astype(bf16)
            vg = jnp.concatenate(
                [vb[n * CHUNK:(n + 1) * CHUNK, g * SGU_HD:(g + 1) * SGU_HD] for n in range(nch)], axis=1)
            sg = _dot(ws, vg)
            bb = sb_ref[g]
            for n in range(nch):
                s = sg[:, n * SGU_HD:(n + 1) * SGU_HD] + bb
                u = a_scr[n * CHUNK:(n + 1) * CHUNK, g * SGU_HD:(g + 1) * SGU_HD]
                o_ref[n * CHUNK:(n + 1) * CHUNK, D_CONV + g * SGU_HD:D_CONV + (g + 1) * SGU_HD] = (
                    (u * s).astype(o_ref.dtype))


def hyb_in(x2, g, sh, sc, w_in_bf, conv_w, ln_g, ln_b, sgu_w, sgu_b, seq):
    t, d = x2.shape
    tps = seq // TM
    nj = HYB_IN // TN_IN
    sbb = jnp.broadcast_to(sgu_b[:, :, None], (SGU_GROUPS, CHUNK, SGU_HD))
    kern = functools.partial(_hyb_in_kernel, tiles_per_seq=tps)
    return pl.pallas_call(
        kern,
        grid=(t // TM, nj),
        in_specs=[
            pl.BlockSpec((TM, d), lambda i, j: (i, 0)),
            pl.BlockSpec((1, d), lambda i, j: (0, 0)),
            pl.BlockSpec((None, 1, d), lambda i, j: (i // tps, 0, 0)),
            pl.BlockSpec((None, 1, d), lambda i, j: (i // tps, 0, 0)),
            pl.BlockSpec((d, TN_IN), lambda i, j: (0, j)),
            pl.BlockSpec((CONV_WIDTH, D_CONV), lambda i, j: (0, 0)),
            pl.BlockSpec((1, D_SGU), lambda i, j: (0, 0)),
            pl.BlockSpec((1, D_SGU), lambda i, j: (0, 0)),
            pl.BlockSpec((SGU_GROUPS, CHUNK, CHUNK), lambda i, j: (0, 0, 0)),
            pl.BlockSpec((SGU_GROUPS, CHUNK, SGU_HD), lambda i, j: (0, 0, 0)),
        ],
        out_specs=pl.BlockSpec((TM, D_CONV + D_SGU), lambda i, j: (i, 0)),
        out_shape=jax.ShapeDtypeStruct((t, D_CONV + D_SGU), bf16),
        scratch_shapes=[
            pltpu.VMEM((TM, d), bf16),
            pltpu.VMEM((TM, TN_IN), f32),
            pltpu.VMEM((TM, TN_IN), f32),
            pltpu.VMEM((8, D_CONV), f32),
        ],
        compiler_params=_cparams(("arbitrary", "arbitrary")),
        name="hyb_in",
    )(x2, g.reshape(1, d), sh, sc, w_in_bf, conv_w, ln_g.reshape(1, -1), ln_b.reshape(1, -1), sgu_w, sbb)


def _qkv_kernel(x_ref, g_ref, sh_ref, sc_ref, w_ref, o_ref, xn_scr):
    @pl.when(pl.program_id(1) == 0)
    def _():
        xn_scr[...] = _norm_mod(x_ref[...], g_ref[...], sh_ref[...], sc_ref[...]).astype(bf16)

    o_ref[...] = _dot(xn_scr[...], w_ref[...]).astype(o_ref.dtype)


def qkv_proj(x2, g, sh, sc, w_bf, seq, tn):
    t, d = x2.shape
    n = w_bf.shape[1]
    tps = seq // TM
    return pl.pallas_call(
        _qkv_kernel,
        grid=(t // TM, n // tn),
        in_specs=[
            pl.BlockSpec((TM, d), lambda i, j: (i, 0)),
            pl.BlockSpec((1, d), lambda i, j: (0, 0)),
            pl.BlockSpec((None, 1, d), lambda i, j: (i // tps, 0, 0)),
            pl.BlockSpec((None, 1, d), lambda i, j: (i // tps, 0, 0)),
            pl.BlockSpec((d, tn), lambda i, j: (0, j)),
        ],
        out_specs=pl.BlockSpec((TM, tn), lambda i, j: (i, j)),
        out_shape=jax.ShapeDtypeStruct((t, n), bf16),
        scratch_shapes=[pltpu.VMEM((TM, d), bf16)],
        compiler_params=_cparams(("arbitrary", "arbitrary")),
        name="qkv_proj",
    )(x2, g.reshape(1, d), sh, sc, w_bf)


QK_COLS = N_KV_HEADS * 2 * HEAD_DIM


def _attn_kernel(q_ref, kp_ref, kc_ref, vp_ref, vc_ref, bias_ref, sink_ref, o_ref):
    blk = pl.program_id(1)
    lane = lax.broadcasted_iota(i32, (2 * ATT_BLOCK, 2 * HEAD_DIM), 1)
    low = lane < HEAD_DIM
    col = lax.broadcasted_iota(i32, (ATT_BLOCK, 2 * ATT_BLOCK), 1)
    no_prev = jnp.logical_and(blk == 0, col < ATT_BLOCK)
    zero = jnp.zeros((), bf16)
    for kh in range(N_KV_HEADS):
        cs = slice(kh * 2 * HEAD_DIM, (kh + 1) * 2 * HEAD_DIM)
        kk = jnp.concatenate([kp_ref[:, cs], kc_ref[:, cs]], axis=0)
        vv = jnp.concatenate([vp_ref[:, cs], vc_ref[:, cs]], axis=0)
        k0 = jnp.where(low, kk, zero)
        k1 = jnp.where(low, zero, kk)
        v0 = jnp.where(low, vv, zero)
        v1 = jnp.where(low, zero, vv)
        for pr in range(GQA_GROUP // 2):
            h0 = kh * GQA_GROUP + 2 * pr
            qp = q_ref[:, h0 * HEAD_DIM:(h0 + 2) * HEAD_DIM] * jnp.asarray(HEAD_DIM ** -0.5, bf16)
            acc = None
            for hh, (kz, vz) in enumerate(((k0, v0), (k1, v1))):
                h = h0 + hh
                s = _dot_nt(qp, kz) + bias_ref[h]
                s = jnp.where(no_prev, -jnp.inf, s)
                sk = sink_ref[h]
                m = jnp.maximum(jnp.max(s, axis=-1, keepdims=True), sk)
                p = jnp.exp(s - m)
                den = jnp.sum(p, axis=-1, keepdims=True) + jnp.exp(sk - m)
                o = _dot((p / den).astype(bf16), vz)
                acc = o if acc is None else acc + o
            o_ref[:, h0 * HEAD_DIM:(h0 + 2) * HEAD_DIM] = acc.astype(o_ref.dtype)


def swa_attn(qkv, bias_m, sinks, batch, seq):
    t = qkv.shape[0]
    nb = seq // ATT_BLOCK
    dq = N_Q_HEADS * HEAD_DIM
    kcol = dq // QK_COLS
    vcol = kcol + 1

    def prev(b, i):
        return b * nb + jnp.maximum(i - 1, 0)

    return pl.pallas_call(
        _attn_kernel,
        grid=(batch, nb),
        in_specs=[
            pl.BlockSpec((ATT_BLOCK, dq), lambda b, i: (b * nb + i, 0)),
            pl.BlockSpec((ATT_BLOCK, QK_COLS), lambda b, i: (prev(b, i), kcol)),
            pl.BlockSpec((ATT_BLOCK, QK_COLS), lambda b, i: (b * nb + i, kcol)),
            pl.BlockSpec((ATT_BLOCK, QK_COLS), lambda b, i: (prev(b, i), vcol)),
            pl.BlockSpec((ATT_BLOCK, QK_COLS), lambda b, i: (b * nb + i, vcol)),
            pl.BlockSpec((N_Q_HEADS, ATT_BLOCK, 2 * ATT_BLOCK), lambda b, i: (0, 0, 0)),
            pl.BlockSpec(memory_space=pltpu.SMEM),
        ],
        out_specs=pl.BlockSpec((ATT_BLOCK, dq), lambda b, i: (b * nb + i, 0)),
        out_shape=jax.ShapeDtypeStruct((t, dq), bf16),
        compiler_params=_cparams(("arbitrary", "arbitrary")),
        name="swa_attn",
    )(qkv, qkv, qkv, qkv, qkv, bias_m, sinks)


def _t5_bucket(dist):
    n = np.maximum(dist, 0)
    max_exact = REL_BUCKETS // 2
    large = max_exact + (np.log(np.maximum(n, 1) / max_exact) / np.log(REL_MAX_DIST / max_exact)
                         * (REL_BUCKETS - max_exact)).astype(np.int32)
    large = np.minimum(large, REL_BUCKETS - 1)
    return np.where(n < max_exact, n, large).astype(np.int32)


def _attn_bias_table(rel_bias):
    t_loc = np.arange(ATT_BLOCK)[:, None]
    j_loc = np.arange(2 * ATT_BLOCK)[None, :]
    dist = ATT_BLOCK + t_loc - j_loc
    band = (dist >= 0) & (dist < WINDOW)
    bias = rel_bias[_t5_bucket(dist)].astype(f32).transpose(2, 0, 1)
    return jnp.where(jnp.asarray(band)[None], bias, -jnp.inf)


def _proj_res_kernel(a_ref, w_ref, x_ref, gate_ref, o_ref):
    o_ref[...] = x_ref[...] + gate_ref[...] * _dot(a_ref[...], w_ref[...])


def proj_residual(a, w_bf, x2, gate, seq):
    t, k = a.shape
    d = w_bf.shape[1]
    tps = seq // TM
    return pl.pallas_call(
        _proj_res_kernel,
        grid=(t // TM,),
        in_specs=[
            pl.BlockSpec((TM, k), lambda i: (i, 0)),
            pl.BlockSpec((k, d), lambda i: (0, 0)),
            pl.BlockSpec((TM, d), lambda i: (i, 0)),
            pl.BlockSpec((None, 1, d), lambda i: (i // tps, 0, 0)),
        ],
        out_specs=pl.BlockSpec((TM, d), lambda i: (i, 0)),
        out_shape=jax.ShapeDtypeStruct((t, d), f32),
        compiler_params=_cparams(("arbitrary",)),
        name="proj_residual",
    )(a, w_bf, x2, gate)


def _route_kernel(x_ref, g_ref, sh_ref, sc_ref, rw_ref, rb_ref,
                  xp_ref, eidx_ref, rk_ref, gw_ref, cnt_ref, sel_scr, grp_scr, carry_scr):
    tm = x_ref.shape[0]
    ne = N_EXPERTS

    @pl.when(pl.program_id(0) == 0)
    def _():
        carry_scr[...] = jnp.zeros_like(carry_scr)

    xn = _norm_mod(x_ref[...], g_ref[...], sh_ref[...], sc_ref[...])
    _store_row_tiles(xp_ref, _pack_halves(xn))
    x_hi = xn.astype(bf16)
    x_lo = (xn - x_hi.astype(f32)).astype(bf16)
    rw = rw_ref[...]
    a = _dot_nt(rw, x_hi)
    b = _dot_nt(rw[:ne], x_lo)
    logits = a[:ne] + a[ne:] + b
    scores = jax.nn.sigmoid(logits)
    sel = scores + rb_ref[...]

    for g in range(N_GROUPS):
        tile = sel[g * E_PER_G:(g + 1) * E_PER_G, :]
        t1 = jnp.max(tile, axis=0, keepdims=True)
        dup = jnp.sum(jnp.where(tile == t1, 1.0, 0.0), axis=0, keepdims=True) >= 2.0
        t2 = jnp.max(jnp.where(tile < t1, tile, -jnp.inf), axis=0, keepdims=True)
        grp_scr[g:g + 1, :] = t1 + jnp.where(dup, t1, t2)
    gs = grp_scr[...]

    gi = lax.broadcasted_iota(i32, (N_GROUPS, tm), 0)
    grank = jnp.zeros((N_GROUPS, tm), i32)
    for g in range(N_GROUPS):
        r = grp_scr[g:g + 1, :]
        ge = jnp.where(r >= gs, 1, 0)
        gt = jnp.where(r > gs, 1, 0)
        grank = grank + jnp.where(gi > g, ge, gt)
    grp_scr[...] = jnp.where(grank < TOPK_GROUPS, 1.0, 0.0)

    masked = jnp.concatenate(
        [jnp.where(grp_scr[g:g + 1, :] > 0.5, sel[g * E_PER_G:(g + 1) * E_PER_G, :], -jnp.inf)
         for g in range(N_GROUPS)], axis=0)

    sel_scr[...] = masked
    ei = lax.broadcasted_iota(i32, (ne, tm), 0)
    rank = jnp.zeros((ne, tm), i32)
    for e in range(ne):
        r = sel_scr[e:e + 1, :]
        ge = jnp.where(r >= masked, 1, 0)
        gt = jnp.where(r > masked, 1, 0)
        rank = rank + jnp.where(ei > e, ge, gt)
    chosen = rank < TOP_K
    chf = jnp.where(chosen, 1.0, 0.0)

    wun = jnp.where(chosen, scores, 0.0)
    gwt = wun / jnp.sum(wun, axis=0, keepdims=True) * ROUTED_SCALE

    rr = lax.broadcasted_iota(i32, (tm, tm), 0)
    cc = lax.broadcasted_iota(i32, (tm, tm), 1)
    upper = jnp.where(rr < cc, 1.0, 0.0).astype(bf16)
    chb = chf.astype(bf16)
    pos = carry_scr[:, 0:1] + _dot(chb, upper)
    carry_scr[...] = carry_scr[...] + jnp.sum(chf, axis=1, keepdims=True)
    cnt_ref[...] = carry_scr[...].astype(i32)

    er = lax.broadcasted_iota(i32, (ne, ne), 0)
    ec = lax.broadcasted_iota(i32, (ne, ne), 1)
    lower = jnp.where(ec < er, 1.0, 0.0).astype(bf16)
    below = _dot(lower, chb)
    slot = jnp.where(chosen, below, -1.0)
    eif = ei.astype(f32)
    for k in range(TOP_K):
        mk = slot == float(k)
        eidx_ref[k:k + 1, :] = jnp.sum(jnp.where(mk, eif, 0.0), axis=0, keepdims=True).astype(i32)
        rk_ref[k:k + 1, :] = jnp.sum(jnp.where(mk, pos, 0.0), axis=0, keepdims=True).astype(i32)
        gw_ref[k:k + 1, :] = jnp.sum(jnp.where(mk, gwt, 0.0), axis=0, keepdims=True)


def moe_route(x2, g, sh, sc, rw_cat, rbias, seq):
    t, d = x2.shape
    tps = seq // TR
    return pl.pallas_call(
        _route_kernel,
        grid=(t // TR,),
        in_specs=[
            pl.BlockSpec((TR, d), lambda i: (i, 0)),
            pl.BlockSpec((1, d), lambda i: (0, 0)),
            pl.BlockSpec((None, 1, d), lambda i: (i // tps, 0, 0)),
            pl.BlockSpec((None, 1, d), lambda i: (i // tps, 0, 0)),
            pl.BlockSpec((2 * N_EXPERTS, d), lambda i: (0, 0)),
            pl.BlockSpec((N_EXPERTS, 1), lambda i: (0, 0)),
        ],
        out_specs=[
            pl.BlockSpec((TR * ROW_SUB, LANES), lambda i: (i, 0)),
            pl.BlockSpec((TOP_K, TR), lambda i: (0, i)),
            pl.BlockSpec((TOP_K, TR), lambda i: (0, i)),
            pl.BlockSpec((TOP_K, TR), lambda i: (0, i)),
            pl.BlockSpec((N_EXPERTS, LANES), lambda i: (0, 0)),
        ],
        out_shape=[
            jax.ShapeDtypeStruct((t * ROW_SUB, LANES), u32),
            jax.ShapeDtypeStruct((TOP_K, t), i32),
            jax.ShapeDtypeStruct((TOP_K, t), i32),
            jax.ShapeDtypeStruct((TOP_K, t), f32),
            jax.ShapeDtypeStruct((N_EXPERTS, LANES), i32),
        ],
        scratch_shapes=[
            pltpu.VMEM((N_EXPERTS, TR), f32),
            pltpu.VMEM((N_GROUPS, TR), f32),
            pltpu.VMEM((N_EXPERTS, LANES), f32),
        ],
        compiler_params=_cparams(("arbitrary",)),
        name="moe_route",
    )(x2, g.reshape(1, d), sh, sc, rw_cat, rbias.reshape(N_EXPERTS, 1))


def _tile_rows(ref, r, n=1):
    return ref.at[pl.ds(pl.multiple_of(r * ROW_SUB, ROW_SUB), n * ROW_SUB), :]


def _row_copy(src, s, dst, d, sem):
    return pltpu.make_async_copy(_tile_rows(src, s), _tile_rows(dst, d), sem)


def _dispatch_kernel(ztail_ref, dest_ref, xp_ref, xs_ref, zeros_scr, sem, zsem):
    td = xp_ref.shape[0] // ROW_SUB

    @pl.when(pl.program_id(0) == 0)
    def _():
        zeros_scr[...] = jnp.zeros_like(zeros_scr)

        def zcopy(e):
            return pltpu.make_async_copy(zeros_scr, _tile_rows(xs_ref, ztail_ref[e], BM), zsem)

        def zstart(e, c):
            zcopy(e).start()
            return c

        def zwait(e, c):
            zcopy(e).wait()
            return c

        lax.fori_loop(0, N_EXPERTS, zstart, 0)
        lax.fori_loop(0, N_EXPERTS, zwait, 0)

    def start(t, c):
        for k in range(TOP_K):
            _row_copy(xp_ref, t, xs_ref, dest_ref[k, t], sem).start()
        return c

    def wait(t, c):
        for k in range(TOP_K):
            _row_copy(xp_ref, t, xs_ref, dest_ref[k, t], sem).wait()
        return c

    lax.fori_loop(0, td, start, 0)
    lax.fori_loop(0, td, wait, 0)


def moe_dispatch(xp, dest, ztail, nrows):
    t = xp.shape[0] // ROW_SUB
    grid_spec = pltpu.PrefetchScalarGridSpec(
        num_scalar_prefetch=1,
        grid=(t // TD,),
        in_specs=[
            pl.BlockSpec((TOP_K, TD), lambda i, z: (0, i), memory_space=pltpu.SMEM),
            pl.BlockSpec((TD * ROW_SUB, LANES), lambda i, z: (i, 0)),
        ],
        out_specs=pl.BlockSpec(memory_space=pl.ANY),
        scratch_shapes=[
            pltpu.VMEM((BM * ROW_SUB, LANES), u32),
            pltpu.SemaphoreType.DMA(()),
            pltpu.SemaphoreType.DMA(()),
        ],
    )
    return pl.pallas_call(
        _dispatch_kernel,
        grid_spec=grid_spec,
        out_shape=jax.ShapeDtypeStruct((nrows * ROW_SUB, LANES), u32),
        compiler_params=_cparams(("arbitrary",)),
        name="moe_dispatch",
    )(ztail, dest, xp)


def _ffn_packed(x_ref, m, wg, wu, wd):
    los, his = _load_row_tiles(x_ref, m)
    xa = jnp.concatenate([p.astype(bf16) for p in los], axis=1)
    xb = jnp.concatenate([p.astype(bf16) for p in his], axis=1)
    hg = _dot(xa, wg[:HALF, :]) + _dot(xb, wg[HALF:, :])
    hu = _dot(xa, wu[:HALF, :]) + _dot(xb, wu[HALF:, :])
    a = (jax.nn.silu(hg) * hu).astype(bf16)
    return _dot(a, wd[...])


def _gmm_kernel(be_ref, bf_ref, nu_ref, xs_ref, wg_ref, wu_ref, wd_ref, ys_ref, wg_s, wu_s, wd_s):
    i = pl.program_id(0)

    @pl.when(bf_ref[i] == 1)
    def _():
        wg_s[...] = wg_ref[...].astype(bf16)
        wu_s[...] = wu_ref[...].astype(bf16)
        wd_s[...] = wd_ref[...].astype(bf16)

    @pl.when(i < nu_ref[0])
    def _():
        _store_row_tiles(ys_ref, _pack_halves(_ffn_packed(xs_ref, BM, wg_s, wu_s, wd_s)))

    @pl.when(i >= nu_ref[0])
    def _():
        ys_ref[...] = jnp.zeros_like(ys_ref)


def moe_gmm(xs, blk_e, blk_first, n_used, w_gate, w_up, w_down, layer, nblk):
    d, de = w_gate.shape[2], w_gate.shape[3]
    grid_spec = pltpu.PrefetchScalarGridSpec(
        num_scalar_prefetch=3,
        grid=(nblk,),
        in_specs=[
            pl.BlockSpec((BM * ROW_SUB, LANES), lambda i, be, bf, nu: (jnp.minimum(i, nu[0] - 1), 0)),
            pl.BlockSpec((None, None, d, de), lambda i, be, bf, nu: (layer, be[i], 0, 0)),
            pl.BlockSpec((None, None, d, de), lambda i, be, bf, nu: (layer, be[i], 0, 0)),
            pl.BlockSpec((None, None, de, d), lambda i, be, bf, nu: (layer, be[i], 0, 0)),
        ],
        out_specs=pl.BlockSpec((BM * ROW_SUB, LANES), lambda i, be, bf, nu: (i, 0)),
        scratch_shapes=[
            pltpu.VMEM((d, de), bf16),
            pltpu.VMEM((d, de), bf16),
            pltpu.VMEM((de, d), bf16),
        ],
    )
    return pl.pallas_call(
        _gmm_kernel,
        grid_spec=grid_spec,
        out_shape=jax.ShapeDtypeStruct((nblk * BM * ROW_SUB, LANES), u32),
        compiler_params=_cparams(("arbitrary",)),
        name="moe_gmm",
    )(blk_e, blk_first, n_used, xs, w_gate, w_up, w_down)


def _combine_kernel(dest_ref, x_ref, xp_ref, gw_ref, gate_ref, sg_ref, su_ref, sd_ref, fg_ref, ys_ref,
                    o_ref, gath, sem, *, final_norm):
    td = x_ref.shape[0]

    def copy(t, k):
        return _row_copy(ys_ref, dest_ref[k, t], gath.at[k], t, sem)

    def start(t, c):
        for k in range(TOP_K):
            copy(t, k).start()
        return c

    def wait(t, c):
        for k in range(TOP_K):
            copy(t, k).wait()
        return c

    lax.fori_loop(0, td, start, 0)
    ysh = _ffn_packed(xp_ref, td, sg_ref, su_ref, sd_ref)
    lax.fori_loop(0, td, wait, 0)

    acc_lo = [ysh[:, c * LANES:(c + 1) * LANES] for c in range(ROW_SUB)]
    acc_hi = [ysh[:, HALF + c * LANES:HALF + (c + 1) * LANES] for c in range(ROW_SUB)]
    gw = gw_ref[...]
    for k in range(TOP_K):
        los, his = _load_row_tiles(gath.at[k], td)
        wk = jnp.broadcast_to(gw[:, k:k + 1], (td, LANES))
        for c in range(ROW_SUB):
            acc_lo[c] = acc_lo[c] + wk * los[c]
            acc_hi[c] = acc_hi[c] + wk * his[c]
    y = jnp.concatenate(acc_lo + acc_hi, axis=1)
    out = x_ref[...] + gate_ref[...] * y
    if final_norm:
        ms = jnp.mean(out * out, axis=-1, keepdims=True)
        out = (out * lax.rsqrt(ms + RMS_EPS)) * fg_ref[...]
    o_ref[...] = out


def moe_combine(x2, xp, ys, dest, gw_t, gate, sg_bf, su_bf, sd_bf, final_g, seq, final_norm):
    t, d = x2.shape
    tps = seq // TD
    kern = functools.partial(_combine_kernel, final_norm=final_norm)
    return pl.pallas_call(
        kern,
        grid=(t // TD,),
        in_specs=[
            pl.BlockSpec((TOP_K, TD), lambda i: (0, i), memory_space=pltpu.SMEM),
            pl.BlockSpec((TD, d), lambda i: (i, 0)),
            pl.BlockSpec((TD * ROW_SUB, LANES), lambda i: (i, 0)),
            pl.BlockSpec((TD, TOP_K), lambda i: (i, 0)),
            pl.BlockSpec((None, 1, d), lambda i: (i // tps, 0, 0)),
            pl.BlockSpec((d, D_SHARED), lambda i: (0, 0)),
            pl.BlockSpec((d, D_SHARED), lambda i: (0, 0)),
            pl.BlockSpec((D_SHARED, d), lambda i: (0, 0)),
            pl.BlockSpec((1, d), lambda i: (0, 0)),
            pl.BlockSpec(memory_space=pl.ANY),
        ],
        out_specs=pl.BlockSpec((TD, d), lambda i: (i, 0)),
        out_shape=jax.ShapeDtypeStruct((t, d), f32),
        scratch_shapes=[
            pltpu.VMEM((TOP_K, TD * ROW_SUB, LANES), u32),
            pltpu.SemaphoreType.DMA(()),
        ],
        compiler_params=_cparams(("arbitrary",)),
        name="moe_combine",
    )(dest, x2, xp, gw_t, gate, sg_bf, su_bf, sd_bf, final_g.reshape(1, d), ys)


SLOT_TN = 4096


def _slots_kernel(ps_ref, eidx_ref, rk_ref, o_ref):
    e = eidx_ref[...]
    start = jnp.zeros_like(e)
    for j in range(N_EXPERTS):
        start = jnp.where(e == j, ps_ref[j], start)
    o_ref[...] = start + rk_ref[...]


def moe_slots(pad_start, eidx, rk):
    t = eidx.shape[1]
    tn = min(t, SLOT_TN)
    spec = pl.BlockSpec((TOP_K, tn), lambda i, ps: (0, i))
    grid_spec = pltpu.PrefetchScalarGridSpec(
        num_scalar_prefetch=1, grid=(t // tn,), in_specs=[spec, spec], out_specs=spec)
    return pl.pallas_call(
        _slots_kernel,
        grid_spec=grid_spec,
        out_shape=jax.ShapeDtypeStruct((TOP_K, t), i32),
        compiler_params=_cparams(("arbitrary",)),
        name="moe_slots",
    )(pad_start, eidx, rk)


def moe_layer(x2, g, sh, sc, gate, router_w, router_bias, w_gate, w_up, w_down, layer, sh_gate, sh_up, sh_down,
              final_g, seq, final_norm):
    t = x2.shape[0]
    a = t * TOP_K
    nblk = a // BM + N_EXPERTS
    rw_t = router_w.T
    rw_hi = rw_t.astype(bf16)
    rw_lo = (rw_t - rw_hi.astype(f32)).astype(bf16)
    rw_cat = jnp.concatenate([rw_hi, rw_lo], axis=0)

    xp, eidx, rk, gw, cnt = moe_route(x2, g, sh, sc, rw_cat, router_bias, seq)

    counts = cnt[:, 0]
    padded = ((counts + BM - 1) // BM) * BM
    cum_pad = jnp.cumsum(padded)
    pad_start = cum_pad - padded
    n_used = (cum_pad[-1] // BM).astype(i32).reshape(1)
    blk_row = jnp.arange(nblk, dtype=i32) * BM
    blk_e = jnp.minimum(jnp.sum((cum_pad[None, :] <= blk_row[:, None]).astype(i32), axis=1), N_EXPERTS - 1)
    blk_first = jnp.concatenate([jnp.ones((1,), i32), (blk_e[1:] != blk_e[:-1]).astype(i32)])
    dest = moe_slots(pad_start.astype(i32), eidx, rk)
    ztail = jnp.maximum(cum_pad - BM, 0).astype(i32)

    xs = moe_dispatch(xp, dest, ztail, nblk * BM)
    ys = moe_gmm(xs, blk_e, blk_first, n_used, w_gate, w_up, w_down, layer, nblk)
    return moe_combine(x2, xp, ys, dest, gw.T, gate, sh_gate.astype(bf16), sh_up.astype(bf16),
                       sh_down.astype(bf16), final_g, seq, final_norm)


def _qkv_weight(w_qkv):
    dq = N_Q_HEADS * HEAD_DIM
    dkv = N_KV_HEADS * HEAD_DIM
    d = w_qkv.shape[0]
    wq = w_qkv[:, :dq]
    wk = w_qkv[:, dq:dq + dkv].reshape(d, N_KV_HEADS, 1, HEAD_DIM)
    wv = w_qkv[:, dq + dkv:].reshape(d, N_KV_HEADS, 1, HEAD_DIM)
    wk2 = jnp.broadcast_to(wk, (d, N_KV_HEADS, 2, HEAD_DIM)).reshape(d, QK_COLS)
    wv2 = jnp.broadcast_to(wv, (d, N_KV_HEADS, 2, HEAD_DIM)).reshape(d, QK_COLS)
    return jnp.concatenate([wq, wk2, wv2], axis=1).astype(bf16)


def kernel(x, c, ada_w, ada_b, norm_mix_g, norm_ffn_g, hyb_w_in, conv_w, sgu_ln_g, sgu_ln_b, sgu_w, sgu_b,
           hyb_w_out, attn_w_qkv, attn_sinks, attn_w_o, rel_bias, router_w, router_bias, exp_w_gate,
           exp_w_up, exp_w_down, sh_w_gate, sh_w_up, sh_w_down, final_g):
    batch, seq, d = x.shape
    t = batch * seq
    x2 = x.reshape(t, d)
    mod = ada_mod(c, ada_w, ada_b)
    bias_m = _attn_bias_table(rel_bias)
    for l in range(DEPTH):
        parts = [mod[l, :, k * d:(k + 1) * d].reshape(batch, 1, d) for k in range(6)]
        sh_m, sc_m, g_m, sh_f, sc_f, g_f = parts
        i = l // 2
        if l % 2 == 0:
            ycat = hyb_in(x2, norm_mix_g[l], sh_m, sc_m, hyb_w_in[i].astype(bf16), conv_w[i], sgu_ln_g[i],
                          sgu_ln_b[i], sgu_w[i], sgu_b[i], seq)
            x2 = proj_residual(ycat, hyb_w_out[i].astype(bf16), x2, g_m, seq)
        else:
            qkv = qkv_proj(x2, norm_mix_g[l], sh_m, sc_m, _qkv_weight(attn_w_qkv[i]), seq, 1024)
            o = swa_attn(qkv, bias_m, attn_sinks[i], batch, seq)
            x2 = proj_residual(o, attn_w_o[i].astype(bf16), x2, g_m, seq)
        x2 = moe_layer(x2, norm_ffn_g[l], sh_f, sc_f, g_f, router_w[l], router_bias[l], exp_w_gate,
                       exp_w_up, exp_w_down, l, sh_w_gate[l], sh_w_up[l], sh_w_down[l], final_g, seq,
                       final_norm=(l == DEPTH - 1))
    return x2.reshape(batch, seq, d)
```

```python
import functools

import numpy as np
import jax
import jax.numpy as jnp
from jax import lax
from jax.experimental import pallas as pl
from jax.experimental.pallas import tpu as pltpu

f32 = jnp.float32
bf16 = jnp.bfloat16
i32 = jnp.int32
u32 = jnp.uint32

D_MODEL = 2048
DEPTH = 2
RMS_EPS = 1e-6
LN_EPS = 1e-5
D_CONV = 1024
CONV_WIDTH = 3
D_SGU = 1024
SGU_GROUPS = 8
SGU_HD = D_SGU // SGU_GROUPS
CHUNK = 128
HYB_IN = 3 * D_CONV + 2 * D_SGU
HEAD_DIM = 64
N_Q_HEADS = D_MODEL // HEAD_DIM
N_KV_HEADS = N_Q_HEADS // 8
GQA_GROUP = N_Q_HEADS // N_KV_HEADS
WINDOW = 128
ATT_BLOCK = 128
REL_BUCKETS = 32
REL_MAX_DIST = 128
N_EXPERTS = 64
TOP_K = 8
N_GROUPS = 8
TOPK_GROUPS = 4
E_PER_G = N_EXPERTS // N_GROUPS
D_EXPERT = 512
D_SHARED = 512
ROUTED_SCALE = 2.5

LANES = 128
HALF = D_MODEL // 2
VMEM_LIMIT = 56 * 1024 * 1024
TM = 512
TN_IN = 1024
TR = 512
TD = 256
BM = 512
ADA_TN = 1024
DMA_PRIORITIES = 2


def _cparams(sem, vmem=VMEM_LIMIT):
    return pltpu.CompilerParams(dimension_semantics=sem, vmem_limit_bytes=vmem)


def _norm_mod(x, g, sh, sc):
    ms = jnp.mean(x * x, axis=-1, keepdims=True)
    y = x * lax.rsqrt(ms + RMS_EPS)
    return (y * g) * (1.0 + sc) + sh


def _pack_halves(y):
    h = y.shape[1] // 2
    lo = lax.bitcast_convert_type(y[:, :h].astype(bf16).astype(f32), u32)
    hi = lax.bitcast_convert_type(y[:, h:].astype(bf16).astype(f32), u32)
    return (lo >> 16) | (hi & jnp.uint32(0xFFFF0000))


def _unpack_halves(w):
    lo = lax.bitcast_convert_type(w << 16, f32)
    hi = lax.bitcast_convert_type(w & jnp.uint32(0xFFFF0000), f32)
    return lo, hi


ROW_SUB = HALF // LANES


def _store_row_tiles(ref, packed):
    n = packed.shape[0]
    for c in range(ROW_SUB):
        ref[pl.ds(c, n, stride=ROW_SUB), :] = packed[:, c * LANES:(c + 1) * LANES]


def _load_row_tiles(ref, n):
    los, his = [], []
    for c in range(ROW_SUB):
        lo, hi = _unpack_halves(ref[pl.ds(c, n, stride=ROW_SUB), :])
        los.append(lo)
        his.append(hi)
    return los, his


def _gelu(x):
    return 0.5 * x * (1.0 + lax.erf(x * np.float32(np.sqrt(0.5))))


def _dot(a, b):
    return jnp.dot(a, b, preferred_element_type=f32)


def _dot_nt(a, b):
    return lax.dot_general(a, b, (((1,), (1,)), ((), ())), preferred_element_type=f32)


def _ada_kernel(c_ref, w_ref, b_ref, o_ref):
    ca = jax.nn.silu(c_ref[...]).astype(bf16)
    o_ref[...] = _dot(ca, w_ref[...].astype(bf16)) + b_ref[...]


def ada_mod(c, ada_w, ada_b):
    depth, d, n = ada_w.shape
    b = c.shape[0]
    return pl.pallas_call(
        _ada_kernel,
        grid=(depth, n // ADA_TN),
        in_specs=[
            pl.BlockSpec((b, d), lambda l, j: (0, 0)),
            pl.BlockSpec((None, d, ADA_TN), lambda l, j: (l, 0, j)),
            pl.BlockSpec((None, 1, ADA_TN), lambda l, j: (l, 0, j)),
        ],
        out_specs=pl.BlockSpec((None, b, ADA_TN), lambda l, j: (l, 0, j)),
        out_shape=jax.ShapeDtypeStruct((depth, b, n), f32),
        compiler_params=_cparams(("arbitrary", "arbitrary")),
        name="ada_mod",
    )(c, ada_w, ada_b.reshape(depth, 1, n))


def _hyb_in_kernel(x_ref, g_ref, sh_ref, sc_ref, w_ref, cw_ref, lng_ref, lnb_ref, sw_ref, sb_ref,
                   o_ref, xn_scr, a_scr, b_scr, carry_scr, *, tiles_per_seq):
    i = pl.program_id(0)
    j = pl.program_id(1)
    tm = x_ref.shape[0]

    @pl.when(j == 0)
    def _():
        xn_scr[...] = _norm_mod(x_ref[...], g_ref[...], sh_ref[...], sc_ref[...]).astype(bf16)

    @pl.when(jnp.logical_and(i == 0, j == 0))
    def _():
        carry_scr[...] = jnp.zeros_like(carry_scr)

    p = _dot(xn_scr[...], w_ref[...])

    @pl.when(j == 0)
    def _():
        a_scr[...] = p

    @pl.when(j == 1)
    def _():
        b_scr[...] = p

    @pl.when(j == 2)
    def _():
        z = b_scr[...] * p
        row = lax.broadcasted_iota(i32, z.shape, 0)
        first = (i % tiles_per_seq) == 0
        prev = jnp.where(first, 0.0, carry_scr[...])
        p1 = prev[7:8, :]
        p2 = prev[6:7, :]
        z1 = jnp.where(row == 0, p1, pltpu.roll(z, 1, 0))
        z2 = jnp.where(row == 0, p2, jnp.where(row == 1, p1, pltpu.roll(z, 2, 0)))
        cw = cw_ref[...]
        conv = cw[0:1, :] * z2 + cw[1:2, :] * z1 + cw[2:3, :] * z
        carry_scr[...] = z[tm - 8:, :]
        o_ref[:, :D_CONV] = (a_scr[...] * conv).astype(o_ref.dtype)

    @pl.when(j == 3)
    def _():
        a_scr[...] = _gelu(p)

    @pl.when(j == 4)
    def _():
        v = _gelu(p)
        mu = jnp.mean(v, axis=-1, keepdims=True)
        vc = v - mu
        var = jnp.mean(vc * vc, axis=-1, keepdims=True)
        v = (vc * lax.rsqrt(var + LN_EPS)) * lng_ref[...] + lnb_ref[...]
        vb = v.astype(bf16)
        nch = tm // CHUNK
        r = lax.broadcasted_iota(i32, (CHUNK, CHUNK), 0)
        c = lax.broadcasted_iota(i32, (CHUNK, CHUNK), 1)
        tril = r >= c
        for g in range(SGU_GROUPS):
            ws = jnp.where(tril, sw_ref[g], 0.0).astype(bf16)
            vg = jnp.concatenate(
                [vb[n * CHUNK:(n + 1) * CHUNK, g * SGU_HD:(g + 1) * SGU_HD] for n in range(nch)], axis=1)
            sg = _dot(ws, vg)
            bb = sb_ref[g]
            for n in range(nch):
                s = sg[:, n * SGU_HD:(n + 1) * SGU_HD] + bb
                u = a_scr[n * CHUNK:(n + 1) * CHUNK, g * SGU_HD:(g + 1) * SGU_HD]
                o_ref[n * CHUNK:(n + 1) * CHUNK, D_CONV + g * SGU_HD:D_CONV + (g + 1) * SGU_HD] = (
                    (u * s).astype(o_ref.dtype))


def hyb_in(x2, g, sh, sc, w_in_bf, conv_w, ln_g, ln_b, sgu_w, sgu_b, seq):
    t, d = x2.shape
    tps = seq // TM
    nj = HYB_IN // TN_IN
    sbb = jnp.broadcast_to(sgu_b[:, :, None], (SGU_GROUPS, CHUNK, SGU_HD))
    kern = functools.partial(_hyb_in_kernel, tiles_per_seq=tps)
    return pl.pallas_call(
        kern,
        grid=(t // TM, nj),
        in_specs=[
            pl.BlockSpec((TM, d), lambda i, j: (i, 0)),
            pl.BlockSpec((1, d), lambda i, j: (0, 0)),
            pl.BlockSpec((None, 1, d), lambda i, j: (i // tps, 0, 0)),
            pl.BlockSpec((None, 1, d), lambda i, j: (i // tps, 0, 0)),
            pl.BlockSpec((d, TN_IN), lambda i, j: (0, j)),
            pl.BlockSpec((CONV_WIDTH, D_CONV), lambda i, j: (0, 0)),
            pl.BlockSpec((1, D_SGU), lambda i, j: (0, 0)),
            pl.BlockSpec((1, D_SGU), lambda i, j: (0, 0)),
            pl.BlockSpec((SGU_GROUPS, CHUNK, CHUNK), lambda i, j: (0, 0, 0)),
            pl.BlockSpec((SGU_GROUPS, CHUNK, SGU_HD), lambda i, j: (0, 0, 0)),
        ],
        out_specs=pl.BlockSpec((TM, D_CONV + D_SGU), lambda i, j: (i, 0)),
        out_shape=jax.ShapeDtypeStruct((t, D_CONV + D_SGU), bf16),
        scratch_shapes=[
            pltpu.VMEM((TM, d), bf16),
            pltpu.VMEM((TM, TN_IN), f32),
            pltpu.VMEM((TM, TN_IN), f32),
            pltpu.VMEM((8, D_CONV), f32),
        ],
        compiler_params=_cparams(("arbitrary", "arbitrary")),
        name="hyb_in",
    )(x2, g.reshape(1, d), sh, sc, w_in_bf, conv_w, ln_g.reshape(1, -1), ln_b.reshape(1, -1), sgu_w, sbb)


def _qkv_kernel(x_ref, g_ref, sh_ref, sc_ref, w_ref, o_ref, xn_scr):
    @pl.when(pl.program_id(1) == 0)
    def _():
        xn_scr[...] = _norm_mod(x_ref[...], g_ref[...], sh_ref[...], sc_ref[...]).astype(bf16)

    o_ref[...] = _dot(xn_scr[...], w_ref[...]).astype(o_ref.dtype)


def qkv_proj(x2, g, sh, sc, w_bf, seq, tn):
    t, d = x2.shape
    n = w_bf.shape[1]
    tps = seq // TM
    return pl.pallas_call(
        _qkv_kernel,
        grid=(t // TM, n // tn),
        in_specs=[
            pl.BlockSpec((TM, d), lambda i, j: (i, 0)),
            pl.BlockSpec((1, d), lambda i, j: (0, 0)),
            pl.BlockSpec((None, 1, d), lambda i, j: (i // tps, 0, 0)),
            pl.BlockSpec((None, 1, d), lambda i, j: (i // tps, 0, 0)),
            pl.BlockSpec((d, tn), lambda i, j: (0, j)),
        ],
        out_specs=pl.BlockSpec((TM, tn), lambda i, j: (i, j)),
        out_shape=jax.ShapeDtypeStruct((t, n), bf16),
        scratch_shapes=[pltpu.VMEM((TM, d), bf16)],
        compiler_params=_cparams(("arbitrary", "arbitrary")),
        name="qkv_proj",
    )(x2, g.reshape(1, d), sh, sc, w_bf)


QK_COLS = N_KV_HEADS * 2 * HEAD_DIM


def _attn_kernel(q_ref, kp_ref, kc_ref, vp_ref, vc_ref, bias_ref, sink_ref, o_ref):
    blk = pl.program_id(1)
    lane = lax.broadcasted_iota(i32, (2 * ATT_BLOCK, 2 * HEAD_DIM), 1)
    low = lane < HEAD_DIM
    col = lax.broadcasted_iota(i32, (ATT_BLOCK, 2 * ATT_BLOCK), 1)
    no_prev = jnp.logical_and(blk == 0, col < ATT_BLOCK)
    zero = jnp.zeros((), bf16)
    for kh in range(N_KV_HEADS):
        cs = slice(kh * 2 * HEAD_DIM, (kh + 1) * 2 * HEAD_DIM)
        kk = jnp.concatenate([kp_ref[:, cs], kc_ref[:, cs]], axis=0)
        vv = jnp.concatenate([vp_ref[:, cs], vc_ref[:, cs]], axis=0)
        k0 = jnp.where(low, kk, zero)
        k1 = jnp.where(low, zero, kk)
        v0 = jnp.where(low, vv, zero)
        v1 = jnp.where(low, zero, vv)
        for pr in range(GQA_GROUP // 2):
            h0 = kh * GQA_GROUP + 2 * pr
            qp = q_ref[:, h0 * HEAD_DIM:(h0 + 2) * HEAD_DIM] * jnp.asarray(HEAD_DIM ** -0.5, bf16)
            acc = None
            for hh, (kz, vz) in enumerate(((k0, v0), (k1, v1))):
                h = h0 + hh
                s = _dot_nt(qp, kz) + bias_ref[h]
                s = jnp.where(no_prev, -jnp.inf, s)
                sk = sink_ref[h]
                m = jnp.maximum(jnp.max(s, axis=-1, keepdims=True), sk)
                p = jnp.exp(s - m)
                den = jnp.sum(p, axis=-1, keepdims=True) + jnp.exp(sk - m)
                o = _dot((p / den).astype(bf16), vz)
                acc = o if acc is None else acc + o
            o_ref[:, h0 * HEAD_DIM:(h0 + 2) * HEAD_DIM] = acc.astype(o_ref.dtype)


def swa_attn(qkv, bias_m, sinks, batch, seq):
    t = qkv.shape[0]
    nb = seq // ATT_BLOCK
    dq = N_Q_HEADS * HEAD_DIM
    kcol = dq // QK_COLS
    vcol = kcol + 1

    def prev(b, i):
        return b * nb + jnp.maximum(i - 1, 0)

    return pl.pallas_call(
        _attn_kernel,
        grid=(batch, nb),
        in_specs=[
            pl.BlockSpec((ATT_BLOCK, dq), lambda b, i: (b * nb + i, 0)),
            pl.BlockSpec((ATT_BLOCK, QK_COLS), lambda b, i: (prev(b, i), kcol)),
            pl.BlockSpec((ATT_BLOCK, QK_COLS), lambda b, i: (b * nb + i, kcol)),
            pl.BlockSpec((ATT_BLOCK, QK_COLS), lambda b, i: (prev(b, i), vcol)),
            pl.BlockSpec((ATT_BLOCK, QK_COLS), lambda b, i: (b * nb + i, vcol)),
            pl.BlockSpec((N_Q_HEADS, ATT_BLOCK, 2 * ATT_BLOCK), lambda b, i: (0, 0, 0)),
            pl.BlockSpec(memory_space=pltpu.SMEM),
        ],
        out_specs=pl.BlockSpec((ATT_BLOCK, dq), lambda b, i: (b * nb + i, 0)),
        out_shape=jax.ShapeDtypeStruct((t, dq), bf16),
        compiler_params=_cparams(("arbitrary", "arbitrary")),
        name="swa_attn",
    )(qkv, qkv, qkv, qkv, qkv, bias_m, sinks)


def _t5_bucket(dist):
    n = np.maximum(dist, 0)
    max_exact = REL_BUCKETS // 2
    large = max_exact + (np.log(np.maximum(n, 1) / max_exact) / np.log(REL_MAX_DIST / max_exact)
                         * (REL_BUCKETS - max_exact)).astype(np.int32)
    large = np.minimum(large, REL_BUCKETS - 1)
    return np.where(n < max_exact, n, large).astype(np.int32)


def _attn_bias_table(rel_bias):
    t_loc = np.arange(ATT_BLOCK)[:, None]
    j_loc = np.arange(2 * ATT_BLOCK)[None, :]
    dist = ATT_BLOCK + t_loc - j_loc
    band = (dist >= 0) & (dist < WINDOW)
    bias = rel_bias[_t5_bucket(dist)].astype(f32).transpose(2, 0, 1)
    return jnp.where(jnp.asarray(band)[None], bias, -jnp.inf)


def _proj_res_kernel(a_ref, w_ref, x_ref, gate_ref, o_ref):
    o_ref[...] = x_ref[...] + gate_ref[...] * _dot(a_ref[...], w_ref[...])


def proj_residual(a, w_bf, x2, gate, seq):
    t, k = a.shape
    d = w_bf.shape[1]
    tps = seq // TM
    return pl.pallas_call(
        _proj_res_kernel,
        grid=(t // TM,),
        in_specs=[
            pl.BlockSpec((TM, k), lambda i: (i, 0)),
            pl.BlockSpec((k, d), lambda i: (0, 0)),
            pl.BlockSpec((TM, d), lambda i: (i, 0)),
            pl.BlockSpec((None, 1, d), lambda i: (i // tps, 0, 0)),
        ],
        out_specs=pl.BlockSpec((TM, d), lambda i: (i, 0)),
        out_shape=jax.ShapeDtypeStruct((t, d), f32),
        compiler_params=_cparams(("arbitrary",)),
        name="proj_residual",
    )(a, w_bf, x2, gate)


def _route_kernel(x_ref, g_ref, sh_ref, sc_ref, rw_ref, rb_ref,
                  xp_ref, eidx_ref, rk_ref, gw_ref, cnt_ref, sel_scr, grp_scr, carry_scr):
    tm = x_ref.shape[0]
    ne = N_EXPERTS

    @pl.when(pl.program_id(0) == 0)
    def _():
        carry_scr[...] = jnp.zeros_like(carry_scr)

    xn = _norm_mod(x_ref[...], g_ref[...], sh_ref[...], sc_ref[...])
    _store_row_tiles(xp_ref, _pack_halves(xn))
    x_hi = xn.astype(bf16)
    x_lo = (xn - x_hi.astype(f32)).astype(bf16)
    rw = rw_ref[...]
    a = _dot_nt(rw, x_hi)
    b = _dot_nt(rw[:ne], x_lo)
    logits = a[:ne] + a[ne:] + b
    scores = jax.nn.sigmoid(logits)
    sel = scores + rb_ref[...]

    for g in range(N_GROUPS):
        tile = sel[g * E_PER_G:(g + 1) * E_PER_G, :]
        t1 = jnp.max(tile, axis=0, keepdims=True)
        dup = jnp.sum(jnp.where(tile == t1, 1.0, 0.0), axis=0, keepdims=True) >= 2.0
        t2 = jnp.max(jnp.where(tile < t1, tile, -jnp.inf), axis=0, keepdims=True)
        grp_scr[g:g + 1, :] = t1 + jnp.where(dup, t1, t2)
    gs = grp_scr[...]

    gi = lax.broadcasted_iota(i32, (N_GROUPS, tm), 0)
    grank = jnp.zeros((N_GROUPS, tm), i32)
    for g in range(N_GROUPS):
        r = grp_scr[g:g + 1, :]
        ge = jnp.where(r >= gs, 1, 0)
        gt = jnp.where(r > gs, 1, 0)
        grank = grank + jnp.where(gi > g, ge, gt)
    grp_scr[...] = jnp.where(grank < TOPK_GROUPS, 1.0, 0.0)

    masked = jnp.concatenate(
        [jnp.where(grp_scr[g:g + 1, :] > 0.5, sel[g * E_PER_G:(g + 1) * E_PER_G, :], -jnp.inf)
         for g in range(N_GROUPS)], axis=0)

    sel_scr[...] = masked
    ei = lax.broadcasted_iota(i32, (ne, tm), 0)
    rank = jnp.zeros((ne, tm), i32)
    for e in range(ne):
        r = sel_scr[e:e + 1, :]
        ge = jnp.where(r >= masked, 1, 0)
        gt = jnp.where(r > masked, 1, 0)
        rank = rank + jnp.where(ei > e, ge, gt)
    chosen = rank < TOP_K
    chf = jnp.where(chosen, 1.0, 0.0)

    wun = jnp.where(chosen, scores, 0.0)
    gwt = wun / jnp.sum(wun, axis=0, keepdims=True) * ROUTED_SCALE

    rr = lax.broadcasted_iota(i32, (tm, tm), 0)
    cc = lax.broadcasted_iota(i32, (tm, tm), 1)
    upper = jnp.where(rr < cc, 1.0, 0.0).astype(bf16)
    chb = chf.astype(bf16)
    pos = carry_scr[:, 0:1] + _dot(chb, upper)
    carry_scr[...] = carry_scr[...] + jnp.sum(chf, axis=1, keepdims=True)
    cnt_ref[...] = carry_scr[...].astype(i32)

    er = lax.broadcasted_iota(i32, (ne, ne), 0)
    ec = lax.broadcasted_iota(i32, (ne, ne), 1)
    lower = jnp.where(ec < er, 1.0, 0.0).astype(bf16)
    below = _dot(lower, chb)
    slot = jnp.where(chosen, below, -1.0)
    eif = ei.astype(f32)
    for k in range(TOP_K):
        mk = slot == float(k)
        eidx_ref[k:k + 1, :] = jnp.sum(jnp.where(mk, eif, 0.0), axis=0, keepdims=True).astype(i32)
        rk_ref[k:k + 1, :] = jnp.sum(jnp.where(mk, pos, 0.0), axis=0, keepdims=True).astype(i32)
        gw_ref[k:k + 1, :] = jnp.sum(jnp.where(mk, gwt, 0.0), axis=0, keepdims=True)


def moe_route(x2, g, sh, sc, rw_cat, rbias, seq):
    t, d = x2.shape
    tps = seq // TR
    return pl.pallas_call(
        _route_kernel,
        grid=(t // TR,),
        in_specs=[
            pl.BlockSpec((TR, d), lambda i: (i, 0)),
            pl.BlockSpec((1, d), lambda i: (0, 0)),
            pl.BlockSpec((None, 1, d), lambda i: (i // tps, 0, 0)),
            pl.BlockSpec((None, 1, d), lambda i: (i // tps, 0, 0)),
            pl.BlockSpec((2 * N_EXPERTS, d), lambda i: (0, 0)),
            pl.BlockSpec((N_EXPERTS, 1), lambda i: (0, 0)),
        ],
        out_specs=[
            pl.BlockSpec((TR * ROW_SUB, LANES), lambda i: (i, 0)),
            pl.BlockSpec((TOP_K, TR), lambda i: (0, i)),
            pl.BlockSpec((TOP_K, TR), lambda i: (0, i)),
            pl.BlockSpec((TOP_K, TR), lambda i: (0, i)),
            pl.BlockSpec((N_EXPERTS, LANES), lambda i: (0, 0)),
        ],
        out_shape=[
            jax.ShapeDtypeStruct((t * ROW_SUB, LANES), u32),
            jax.ShapeDtypeStruct((TOP_K, t), i32),
            jax.ShapeDtypeStruct((TOP_K, t), i32),
            jax.ShapeDtypeStruct((TOP_K, t), f32),
            jax.ShapeDtypeStruct((N_EXPERTS, LANES), i32),
        ],
        scratch_shapes=[
            pltpu.VMEM((N_EXPERTS, TR), f32),
            pltpu.VMEM((N_GROUPS, TR), f32),
            pltpu.VMEM((N_EXPERTS, LANES), f32),
        ],
        compiler_params=_cparams(("arbitrary",)),
        name="moe_route",
    )(x2, g.reshape(1, d), sh, sc, rw_cat, rbias.reshape(N_EXPERTS, 1))


def _tile_rows(ref, r, n=1):
    return ref.at[pl.ds(pl.multiple_of(r * ROW_SUB, ROW_SUB), n * ROW_SUB), :]


def _row_copy(src, s, dst, d, sem):
    return pltpu.make_async_copy(_tile_rows(src, s), _tile_rows(dst, d), sem)


def _dispatch_kernel(ztail_ref, dest_ref, xp_ref, xs_ref, zeros_scr, sem, zsem):
    td = xp_ref.shape[0] // ROW_SUB

    @pl.when(pl.program_id(0) == 0)
    def _():
        zeros_scr[...] = jnp.zeros_like(zeros_scr)

        def zcopy(e):
            return pltpu.make_async_copy(zeros_scr, _tile_rows(xs_ref, ztail_ref[e], BM), zsem)

        def zstart(e, c):
            zcopy(e).start()
            return c

        def zwait(e, c):
            zcopy(e).wait()
            return c

        lax.fori_loop(0, N_EXPERTS, zstart, 0)
        lax.fori_loop(0, N_EXPERTS, zwait, 0)

    def start(t, c):
        for k in range(TOP_K):
            _row_copy(xp_ref, t, xs_ref, dest_ref[k, t], sem).start(priority=k % DMA_PRIORITIES)
        return c

    def wait(t, c):
        for k in range(TOP_K):
            _row_copy(xp_ref, t, xs_ref, dest_ref[k, t], sem).wait()
        return c

    lax.fori_loop(0, td, start, 0)
    lax.fori_loop(0, td, wait, 0)


def moe_dispatch(xp, dest, ztail, nrows):
    t = xp.shape[0] // ROW_SUB
    grid_spec = pltpu.PrefetchScalarGridSpec(
        num_scalar_prefetch=1,
        grid=(t // TD,),
        in_specs=[
            pl.BlockSpec((TOP_K, TD), lambda i, z: (0, i), memory_space=pltpu.SMEM),
            pl.BlockSpec((TD * ROW_SUB, LANES), lambda i, z: (i, 0)),
        ],
        out_specs=pl.BlockSpec(memory_space=pl.ANY),
        scratch_shapes=[
            pltpu.VMEM((BM * ROW_SUB, LANES), u32),
            pltpu.SemaphoreType.DMA(()),
            pltpu.SemaphoreType.DMA(()),
        ],
    )
    return pl.pallas_call(
        _dispatch_kernel,
        grid_spec=grid_spec,
        out_shape=jax.ShapeDtypeStruct((nrows * ROW_SUB, LANES), u32),
        compiler_params=_cparams(("arbitrary",)),
        name="moe_dispatch",
    )(ztail, dest, xp)


def _ffn_packed(x_ref, m, wg, wu, wd):
    los, his = _load_row_tiles(x_ref, m)
    xa = jnp.concatenate([p.astype(bf16) for p in los], axis=1)
    xb = jnp.concatenate([p.astype(bf16) for p in his], axis=1)
    hg = _dot(xa, wg[:HALF, :]) + _dot(xb, wg[HALF:, :])
    hu = _dot(xa, wu[:HALF, :]) + _dot(xb, wu[HALF:, :])
    a = (jax.nn.silu(hg) * hu).astype(bf16)
    return _dot(a, wd[...])


def _gmm_kernel(be_ref, bf_ref, nu_ref, xs_ref, wg_ref, wu_ref, wd_ref, ys_ref, wg_s, wu_s, wd_s):
    i = pl.program_id(0)

    @pl.when(bf_ref[i] == 1)
    def _():
        wg_s[...] = wg_ref[...].astype(bf16)
        wu_s[...] = wu_ref[...].astype(bf16)
        wd_s[...] = wd_ref[...].astype(bf16)

    @pl.when(i < nu_ref[0])
    def _():
        _store_row_tiles(ys_ref, _pack_halves(_ffn_packed(xs_ref, BM, wg_s, wu_s, wd_s)))

    @pl.when(i >= nu_ref[0])
    def _():
        ys_ref[...] = jnp.zeros_like(ys_ref)


def moe_gmm(xs, blk_e, blk_first, n_used, w_gate, w_up, w_down, layer, nblk):
    d, de = w_gate.shape[2], w_gate.shape[3]
    grid_spec = pltpu.PrefetchScalarGridSpec(
        num_scalar_prefetch=3,
        grid=(nblk,),
        in_specs=[
            pl.BlockSpec((BM * ROW_SUB, LANES), lambda i, be, bf, nu: (jnp.minimum(i, nu[0] - 1), 0)),
            pl.BlockSpec((None, None, d, de), lambda i, be, bf, nu: (layer, be[i], 0, 0)),
            pl.BlockSpec((None, None, d, de), lambda i, be, bf, nu: (layer, be[i], 0, 0)),
            pl.BlockSpec((None, None, de, d), lambda i, be, bf, nu: (layer, be[i], 0, 0)),
        ],
        out_specs=pl.BlockSpec((BM * ROW_SUB, LANES), lambda i, be, bf, nu: (i, 0)),
        scratch_shapes=[
            pltpu.VMEM((d, de), bf16),
            pltpu.VMEM((d, de), bf16),
            pltpu.VMEM((de, d), bf16),
        ],
    )
    return pl.pallas_call(
        _gmm_kernel,
        grid_spec=grid_spec,
        out_shape=jax.ShapeDtypeStruct((nblk * BM * ROW_SUB, LANES), u32),
        compiler_params=_cparams(("arbitrary",)),
        name="moe_gmm",
    )(blk_e, blk_first, n_used, xs, w_gate, w_up, w_down)


def _combine_kernel(dest_ref, x_ref, xp_ref, gw_ref, gate_ref, sg_ref, su_ref, sd_ref, fg_ref, ys_ref,
                    o_ref, gath, sem, *, final_norm):
    td = x_ref.shape[0]

    def copy(t, k):
        return _row_copy(ys_ref, dest_ref[k, t], gath.at[k], t, sem)

    def start(t, c):
        for k in range(TOP_K):
            copy(t, k).start(priority=k % DMA_PRIORITIES)
        return c

    def wait(t, c):
        for k in range(TOP_K):
            copy(t, k).wait()
        return c

    lax.fori_loop(0, td, start, 0)
    ysh = _ffn_packed(xp_ref, td, sg_ref, su_ref, sd_ref)
    lax.fori_loop(0, td, wait, 0)

    acc_lo = [ysh[:, c * LANES:(c + 1) * LANES] for c in range(ROW_SUB)]
    acc_hi = [ysh[:, HALF + c * LANES:HALF + (c + 1) * LANES] for c in range(ROW_SUB)]
    gw = gw_ref[...]
    for k in range(TOP_K):
        los, his = _load_row_tiles(gath.at[k], td)
        wk = jnp.broadcast_to(gw[:, k:k + 1], (td, LANES))
        for c in range(ROW_SUB):
            acc_lo[c] = acc_lo[c] + wk * los[c]
            acc_hi[c] = acc_hi[c] + wk * his[c]
    y = jnp.concatenate(acc_lo + acc_hi, axis=1)
    out = x_ref[...] + gate_ref[...] * y
    if final_norm:
        ms = jnp.mean(out * out, axis=-1, keepdims=True)
        out = (out * lax.rsqrt(ms + RMS_EPS)) * fg_ref[...]
    o_ref[...] = out


def moe_combine(x2, xp, ys, dest, gw_t, gate, sg_bf, su_bf, sd_bf, final_g, seq, final_norm):
    t, d = x2.shape
    tps = seq // TD
    kern = functools.partial(_combine_kernel, final_norm=final_norm)
    return pl.pallas_call(
        kern,
        grid=(t // TD,),
        in_specs=[
            pl.BlockSpec((TOP_K, TD), lambda i: (0, i), memory_space=pltpu.SMEM),
            pl.BlockSpec((TD, d), lambda i: (i, 0)),
            pl.BlockSpec((TD * ROW_SUB, LANES), lambda i: (i, 0)),
            pl.BlockSpec((TD, TOP_K), lambda i: (i, 0)),
            pl.BlockSpec((None, 1, d), lambda i: (i // tps, 0, 0)),
            pl.BlockSpec((d, D_SHARED), lambda i: (0, 0)),
            pl.BlockSpec((d, D_SHARED), lambda i: (0, 0)),
            pl.BlockSpec((D_SHARED, d), lambda i: (0, 0)),
            pl.BlockSpec((1, d), lambda i: (0, 0)),
            pl.BlockSpec(memory_space=pl.ANY),
        ],
        out_specs=pl.BlockSpec((TD, d), lambda i: (i, 0)),
        out_shape=jax.ShapeDtypeStruct((t, d), f32),
        scratch_shapes=[
            pltpu.VMEM((TOP_K, TD * ROW_SUB, LANES), u32),
            pltpu.SemaphoreType.DMA(()),
        ],
        compiler_params=_cparams(("arbitrary",)),
        name="moe_combine",
    )(dest, x2, xp, gw_t, gate, sg_bf, su_bf, sd_bf, final_g.reshape(1, d), ys)


SLOT_TN = 4096


def _slots_kernel(ps_ref, eidx_ref, rk_ref, o_ref):
    e = eidx_ref[...]
    start = jnp.zeros_like(e)
    for j in range(N_EXPERTS):
        start = jnp.where(e == j, ps_ref[j], start)
    o_ref[...] = start + rk_ref[...]


def moe_slots(pad_start, eidx, rk):
    t = eidx.shape[1]
    tn = min(t, SLOT_TN)
    spec = pl.BlockSpec((TOP_K, tn), lambda i, ps: (0, i))
    grid_spec = pltpu.PrefetchScalarGridSpec(
        num_scalar_prefetch=1, grid=(t // tn,), in_specs=[spec, spec], out_specs=spec)
    return pl.pallas_call(
        _slots_kernel,
        grid_spec=grid_spec,
        out_shape=jax.ShapeDtypeStruct((TOP_K, t), i32),
        compiler_params=_cparams(("arbitrary",)),
        name="moe_slots",
    )(pad_start, eidx, rk)


def moe_layer(x2, g, sh, sc, gate, router_w, router_bias, w_gate, w_up, w_down, layer, sh_gate, sh_up, sh_down,
              final_g, seq, final_norm):
    t = x2.shape[0]
    a = t * TOP_K
    nblk = a // BM + N_EXPERTS
    rw_t = router_w.T
    rw_hi = rw_t.astype(bf16)
    rw_lo = (rw_t - rw_hi.astype(f32)).astype(bf16)
    rw_cat = jnp.concatenate([rw_hi, rw_lo], axis=0)

    xp, eidx, rk, gw, cnt = moe_route(x2, g, sh, sc, rw_cat, router_bias, seq)

    counts = cnt[:, 0]
    padded = ((counts + BM - 1) // BM) * BM
    cum_pad = jnp.cumsum(padded)
    pad_start = cum_pad - padded
    n_used = (cum_pad[-1] // BM).astype(i32).reshape(1)
    blk_row = jnp.arange(nblk, dtype=i32) * BM
    blk_e = jnp.minimum(jnp.sum((cum_pad[None, :] <= blk_row[:, None]).astype(i32), axis=1), N_EXPERTS - 1)
    blk_first = jnp.concatenate([jnp.ones((1,), i32), (blk_e[1:] != blk_e[:-1]).astype(i32)])
    dest = moe_slots(pad_start.astype(i32), eidx, rk)
    ztail = jnp.maximum(cum_pad - BM, 0).astype(i32)

    xs = moe_dispatch(xp, dest, ztail, nblk * BM)
    ys = moe_gmm(xs, blk_e, blk_first, n_used, w_gate, w_up, w_down, layer, nblk)
    return moe_combine(x2, xp, ys, dest, gw.T, gate, sh_gate.astype(bf16), sh_up.astype(bf16),
                       sh_down.astype(bf16), final_g, seq, final_norm)


def _qkv_weight(w_qkv):
    dq = N_Q_HEADS * HEAD_DIM
    dkv = N_KV_HEADS * HEAD_DIM
    d = w_qkv.shape[0]
    wq = w_qkv[:, :dq]
    wk = w_qkv[:, dq:dq + dkv].reshape(d, N_KV_HEADS, 1, HEAD_DIM)
    wv = w_qkv[:, dq + dkv:].reshape(d, N_KV_HEADS, 1, HEAD_DIM)
    wk2 = jnp.broadcast_to(wk, (d, N_KV_HEADS, 2, HEAD_DIM)).reshape(d, QK_COLS)
    wv2 = jnp.broadcast_to(wv, (d, N_KV_HEADS, 2, HEAD_DIM)).reshape(d, QK_COLS)
    return jnp.concatenate([wq, wk2, wv2], axis=1).astype(bf16)


def kernel(x, c, ada_w, ada_b, norm_mix_g, norm_ffn_g, hyb_w_in, conv_w, sgu_ln_g, sgu_ln_b, sgu_w, sgu_b,
           hyb_w_out, attn_w_qkv, attn_sinks, attn_w_o, rel_bias, router_w, router_bias, exp_w_gate,
           exp_w_up, exp_w_down, sh_w_gate, sh_w_up, sh_w_down, final_g):
    batch, seq, d = x.shape
    t = batch * seq
    x2 = x.reshape(t, d)
    mod = ada_mod(c, ada_w, ada_b)
    bias_m = _attn_bias_table(rel_bias)
    for l in range(DEPTH):
        parts = [mod[l, :, k * d:(k + 1) * d].reshape(batch, 1, d) for k in range(6)]
        sh_m, sc_m, g_m, sh_f, sc_f, g_f = parts
        i = l // 2
        if l % 2 == 0:
            ycat = hyb_in(x2, norm_mix_g[l], sh_m, sc_m, hyb_w_in[i].astype(bf16), conv_w[i], sgu_ln_g[i],
                          sgu_ln_b[i], sgu_w[i], sgu_b[i], seq)
            x2 = proj_residual(ycat, hyb_w_out[i].astype(bf16), x2, g_m, seq)
        else:
            qkv = qkv_proj(x2, norm_mix_g[l], sh_m, sc_m, _qkv_weight(attn_w_qkv[i]), seq, 1024)
            o = swa_attn(qkv, bias_m, attn_sinks[i], batch, seq)
            x2 = proj_residual(o, attn_w_o[i].astype(bf16), x2, g_m, seq)
        x2 = moe_layer(x2, norm_ffn_g[l], sh_f, sc_f, g_f, router_w[l], router_bias[l], exp_w_gate,
                       exp_w_up, exp_w_down, l, sh_w_gate[l], sh_w_up[l], sh_w_down[l], final_g, seq,
                       final_norm=(l == DEPTH - 1))
    return x2.reshape(batch, seq, d)
```

```python
import functools

import numpy as np
import jax
import jax.numpy as jnp
from jax import lax
from jax.experimental import pallas as pl
from jax.experimental.pallas import tpu as pltpu

f32 = jnp.float32
bf16 = jnp.bfloat16
i32 = jnp.int32
u32 = jnp.uint32

D_MODEL = 2048
DEPTH = 2
RMS_EPS = 1e-6
LN_EPS = 1e-5
D_CONV = 1024
CONV_WIDTH = 3
D_SGU = 1024
SGU_GROUPS = 8
SGU_HD = D_SGU // SGU_GROUPS
CHUNK = 128
HYB_IN = 3 * D_CONV + 2 * D_SGU
HEAD_DIM = 64
N_Q_HEADS = D_MODEL // HEAD_DIM
N_KV_HEADS = N_Q_HEADS // 8
GQA_GROUP = N_Q_HEADS // N_KV_HEADS
WINDOW = 128
ATT_BLOCK = 128
REL_BUCKETS = 32
REL_MAX_DIST = 128
N_EXPERTS = 64
TOP_K = 8
N_GROUPS = 8
TOPK_GROUPS = 4
E_PER_G = N_EXPERTS // N_GROUPS
D_EXPERT = 512
D_SHARED = 512
ROUTED_SCALE = 2.5

LANES = 128
HALF = D_MODEL // 2
VMEM_LIMIT = 56 * 1024 * 1024
TM = 512
TN_IN = 1024
TR = 512
TD = 256
BM = 512
ADA_TN = 1024
DMA_PRIORITIES = 2


def _cparams(sem, vmem=VMEM_LIMIT):
    return pltpu.CompilerParams(dimension_semantics=sem, vmem_limit_bytes=vmem)


def _norm_mod(x, g, sh, sc):
    ms = jnp.mean(x * x, axis=-1, keepdims=True)
    y = x * lax.rsqrt(ms + RMS_EPS)
    return (y * g) * (1.0 + sc) + sh


def _pack_halves(y):
    h = y.shape[1] // 2
    lo = lax.bitcast_convert_type(y[:, :h].astype(bf16).astype(f32), u32)
    hi = lax.bitcast_convert_type(y[:, h:].astype(bf16).astype(f32), u32)
    return (lo >> 16) | (hi & jnp.uint32(0xFFFF0000))


def _unpack_halves(w):
    lo = lax.bitcast_convert_type(w << 16, f32)
    hi = lax.bitcast_convert_type(w & jnp.uint32(0xFFFF0000), f32)
    return lo, hi


ROW_SUB = HALF // LANES


def _store_row_tiles(ref, packed):
    n = packed.shape[0]
    for c in range(ROW_SUB):
        ref[pl.ds(c, n, stride=ROW_SUB), :] = packed[:, c * LANES:(c + 1) * LANES]


def _load_row_tiles(ref, n):
    los, his = [], []
    for c in range(ROW_SUB):
        lo, hi = _unpack_halves(ref[pl.ds(c, n, stride=ROW_SUB), :])
        los.append(lo)
        his.append(hi)
    return los, his


def _gelu(x):
    return 0.5 * x * (1.0 + lax.erf(x * np.float32(np.sqrt(0.5))))


def _dot(a, b):
    return jnp.dot(a, b, preferred_element_type=f32)


def _dot_nt(a, b):
    return lax.dot_general(a, b, (((1,), (1,)), ((), ())), preferred_element_type=f32)


def _ada_kernel(c_ref, w_ref, b_ref, o_ref):
    ca = jax.nn.silu(c_ref[...]).astype(bf16)
    o_ref[...] = _dot(ca, w_ref[...].astype(bf16)) + b_ref[...]


def ada_mod(c, ada_w, ada_b):
    depth, d, n = ada_w.shape
    b = c.shape[0]
    return pl.pallas_call(
        _ada_kernel,
        grid=(depth, n // ADA_TN),
        in_specs=[
            pl.BlockSpec((b, d), lambda l, j: (0, 0)),
            pl.BlockSpec((None, d, ADA_TN), lambda l, j: (l, 0, j)),
            pl.BlockSpec((None, 1, ADA_TN), lambda l, j: (l, 0, j)),
        ],
        out_specs=pl.BlockSpec((None, b, ADA_TN), lambda l, j: (l, 0, j)),
        out_shape=jax.ShapeDtypeStruct((depth, b, n), f32),
        compiler_params=_cparams(("arbitrary", "arbitrary")),
        name="ada_mod",
    )(c, ada_w, ada_b.reshape(depth, 1, n))


def _hyb_in_kernel(x_ref, g_ref, sh_ref, sc_ref, w_ref, cw_ref, lng_ref, lnb_ref, sw_ref, sb_ref,
                   o_ref, xn_scr, a_scr, b_scr, carry_scr, *, tiles_per_seq):
    i = pl.program_id(0)
    j = pl.program_id(1)
    tm = x_ref.shape[0]

    @pl.when(j == 0)
    def _():
        xn_scr[...] = _norm_mod(x_ref[...], g_ref[...], sh_ref[...], sc_ref[...]).astype(bf16)

    @pl.when(jnp.logical_and(i == 0, j == 0))
    def _():
        carry_scr[...] = jnp.zeros_like(carry_scr)

    p = _dot(xn_scr[...], w_ref[...])

    @pl.when(j == 0)
    def _():
        a_scr[...] = p

    @pl.when(j == 1)
    def _():
        b_scr[...] = p

    @pl.when(j == 2)
    def _():
        z = b_scr[...] * p
        row = lax.broadcasted_iota(i32, z.shape, 0)
        first = (i % tiles_per_seq) == 0
        prev = jnp.where(first, 0.0, carry_scr[...])
        p1 = prev[7:8, :]
        p2 = prev[6:7, :]
        z1 = jnp.where(row == 0, p1, pltpu.roll(z, 1, 0))
        z2 = jnp.where(row == 0, p2, jnp.where(row == 1, p1, pltpu.roll(z, 2, 0)))
        cw = cw_ref[...]
        conv = cw[0:1, :] * z2 + cw[1:2, :] * z1 + cw[2:3, :] * z
        carry_scr[...] = z[tm - 8:, :]
        o_ref[:, :D_CONV] = (a_scr[...] * conv).astype(o_ref.dtype)

    @pl.when(j == 3)
    def _():
        a_scr[...] = _gelu(p)

    @pl.when(j == 4)
    def _():
        v = _gelu(p)
        mu = jnp.mean(v, axis=-1, keepdims=True)
        vc = v - mu
        var = jnp.mean(vc * vc, axis=-1, keepdims=True)
        v = (vc * lax.rsqrt(var + LN_EPS)) * lng_ref[...] + lnb_ref[...]
        vb = v.astype(bf16)
        nch = tm // CHUNK
        r = lax.broadcasted_iota(i32, (CHUNK, CHUNK), 0)
        c = lax.broadcasted_iota(i32, (CHUNK, CHUNK), 1)
        tril = r >= c
        for g in range(SGU_GROUPS):
            ws = jnp.where(tril, sw_ref[g], 0.0).astype(bf16)
            vg = jnp.concatenate(
                [vb[n * CHUNK:(n + 1) * CHUNK, g * SGU_HD:(g + 1) * SGU_HD] for n in range(nch)], axis=1)
            sg = _dot(ws, vg)
            bb = sb_ref[g]
            for n in range(nch):
                s = sg[:, n * SGU_HD:(n + 1) * SGU_HD] + bb
                u = a_scr[n * CHUNK:(n + 1) * CHUNK, g * SGU_HD:(g + 1) * SGU_HD]
                o_ref[n * CHUNK:(n + 1) * CHUNK, D_CONV + g * SGU_HD:D_CONV + (g + 1) * SGU_HD] = (
                    (u * s).astype(o_ref.dtype))


def hyb_in(x2, g, sh, sc, w_in_bf, conv_w, ln_g, ln_b, sgu_w, sgu_b, seq):
    t, d = x2.shape
    tps = seq // TM
    nj = HYB_IN // TN_IN
    sbb = jnp.broadcast_to(sgu_b[:, :, None], (SGU_GROUPS, CHUNK, SGU_HD))
    kern = functools.partial(_hyb_in_kernel, tiles_per_seq=tps)
    return pl.pallas_call(
        kern,
        grid=(t // TM, nj),
        in_specs=[
            pl.BlockSpec((TM, d), lambda i, j: (i, 0)),
            pl.BlockSpec((1, d), lambda i, j: (0, 0)),
            pl.BlockSpec((None, 1, d), lambda i, j: (i // tps, 0, 0)),
            pl.BlockSpec((None, 1, d), lambda i, j: (i // tps, 0, 0)),
            pl.BlockSpec((d, TN_IN), lambda i, j: (0, j)),
            pl.BlockSpec((CONV_WIDTH, D_CONV), lambda i, j: (0, 0)),
            pl.BlockSpec((1, D_SGU), lambda i, j: (0, 0)),
            pl.BlockSpec((1, D_SGU), lambda i, j: (0, 0)),
            pl.BlockSpec((SGU_GROUPS, CHUNK, CHUNK), lambda i, j: (0, 0, 0)),
            pl.BlockSpec((SGU_GROUPS, CHUNK, SGU_HD), lambda i, j: (0, 0, 0)),
        ],
        out_specs=pl.BlockSpec((TM, D_CONV + D_SGU), lambda i, j: (i, 0)),
        out_shape=jax.ShapeDtypeStruct((t, D_CONV + D_SGU), bf16),
        scratch_shapes=[
            pltpu.VMEM((TM, d), bf16),
            pltpu.VMEM((TM, TN_IN), f32),
            pltpu.VMEM((TM, TN_IN), f32),
            pltpu.VMEM((8, D_CONV), f32),
        ],
        compiler_params=_cparams(("arbitrary", "arbitrary")),
        name="hyb_in",
    )(x2, g.reshape(1, d), sh, sc, w_in_bf, conv_w, ln_g.reshape(1, -1), ln_b.reshape(1, -1), sgu_w, sbb)


def _qkv_kernel(x_ref, g_ref, sh_ref, sc_ref, w_ref, o_ref, xn_scr):
    @pl.when(pl.program_id(1) == 0)
    def _():
        xn_scr[...] = _norm_mod(x_ref[...], g_ref[...], sh_ref[...], sc_ref[...]).astype(bf16)

    o_ref[...] = _dot(xn_scr[...], w_ref[...]).astype(o_ref.dtype)


def qkv_proj(x2, g, sh, sc, w_bf, seq, tn):
    t, d = x2.shape
    n = w_bf.shape[1]
    tps = seq // TM
    return pl.pallas_call(
        _qkv_kernel,
        grid=(t // TM, n // tn),
        in_specs=[
            pl.BlockSpec((TM, d), lambda i, j: (i, 0)),
            pl.BlockSpec((1, d), lambda i, j: (0, 0)),
            pl.BlockSpec((None, 1, d), lambda i, j: (i // tps, 0, 0)),
            pl.BlockSpec((None, 1, d), lambda i, j: (i // tps, 0, 0)),
            pl.BlockSpec((d, tn), lambda i, j: (0, j)),
        ],
        out_specs=pl.BlockSpec((TM, tn), lambda i, j: (i, j)),
        out_shape=jax.ShapeDtypeStruct((t, n), bf16),
        scratch_shapes=[pltpu.VMEM((TM, d), bf16)],
        compiler_params=_cparams(("arbitrary", "arbitrary")),
        name="qkv_proj",
    )(x2, g.reshape(1, d), sh, sc, w_bf)


QK_COLS = N_KV_HEADS * 2 * HEAD_DIM


def _attn_kernel(q_ref, kp_ref, kc_ref, vp_ref, vc_ref, bias_ref, sink_ref, o_ref):
    lane = lax.broadcasted_iota(i32, (2 * ATT_BLOCK, 2 * HEAD_DIM), 1)
    low = lane < HEAD_DIM
    zero = jnp.zeros((), bf16)
    scale = jnp.asarray(HEAD_DIM ** -0.5, bf16)
    ones = jnp.ones((2 * ATT_BLOCK, 2 * HEAD_DIM), bf16)
    for kh in range(N_KV_HEADS):
        cs = slice(kh * 2 * HEAD_DIM, (kh + 1) * 2 * HEAD_DIM)
        kk = jnp.concatenate([kp_ref[:, cs], kc_ref[:, cs]], axis=0)
        vv = jnp.concatenate([vp_ref[:, cs], vc_ref[:, cs]], axis=0)
        kz = (jnp.where(low, kk, zero), jnp.where(low, zero, kk))
        vz = (jnp.where(low, vv, zero), jnp.where(low, zero, vv))
        q0 = kh * GQA_GROUP * HEAD_DIM
        qs = jnp.concatenate(
            [q_ref[:, q0 + pr * 2 * HEAD_DIM:q0 + (pr + 1) * 2 * HEAD_DIM] for pr in range(ATT_PAIRS)],
            axis=0) * scale
        acc = None
        for par in range(2):
            s = _dot_nt(qs, kz[par]) + bias_ref[kh, par]
            sk = jnp.concatenate(
                [jnp.full((ATT_BLOCK, ATT_BLOCK), sink_ref[kh * GQA_GROUP + 2 * pr + par], f32)
                 for pr in range(ATT_PAIRS)], axis=0)
            rm = jnp.max(s, axis=-1, keepdims=True)
            mb = jnp.maximum(jnp.broadcast_to(rm, sk.shape), sk)
            p = jnp.concatenate([jnp.exp(s[:, :ATT_BLOCK] - mb), jnp.exp(s[:, ATT_BLOCK:] - mb)], axis=1)
            ov = _dot(p.astype(bf16), jnp.concatenate([vz[par], ones], axis=1))
            den = ov[:, 2 * HEAD_DIM:] + jnp.exp(sk - mb)
            o = ov[:, :2 * HEAD_DIM] / den
            acc = o if acc is None else acc + o
        for pr in range(ATT_PAIRS):
            o_ref[:, q0 + pr * 2 * HEAD_DIM:q0 + (pr + 1) * 2 * HEAD_DIM] = (
                acc[pr * ATT_BLOCK:(pr + 1) * ATT_BLOCK, :].astype(o_ref.dtype))


ATT_PAIRS = GQA_GROUP // 2


def swa_attn(qkv, bias_m, sinks, batch, seq):
    t = qkv.shape[0]
    nb = seq // ATT_BLOCK
    dq = N_Q_HEADS * HEAD_DIM
    kcol = dq // QK_COLS
    vcol = kcol + 1

    def prev(b, i):
        return b * nb + jnp.maximum(i - 1, 0)

    return pl.pallas_call(
        _attn_kernel,
        grid=(batch, nb),
        in_specs=[
            pl.BlockSpec((ATT_BLOCK, dq), lambda b, i: (b * nb + i, 0)),
            pl.BlockSpec((ATT_BLOCK, QK_COLS), lambda b, i: (prev(b, i), kcol)),
            pl.BlockSpec((ATT_BLOCK, QK_COLS), lambda b, i: (b * nb + i, kcol)),
            pl.BlockSpec((ATT_BLOCK, QK_COLS), lambda b, i: (prev(b, i), vcol)),
            pl.BlockSpec((ATT_BLOCK, QK_COLS), lambda b, i: (b * nb + i, vcol)),
            pl.BlockSpec((None, N_KV_HEADS, 2, ATT_PAIRS * ATT_BLOCK, 2 * ATT_BLOCK),
                         lambda b, i: (jnp.where(i == 0, 1, 0), 0, 0, 0, 0)),
            pl.BlockSpec(memory_space=pltpu.SMEM),
        ],
        out_specs=pl.BlockSpec((ATT_BLOCK, dq), lambda b, i: (b * nb + i, 0)),
        out_shape=jax.ShapeDtypeStruct((t, dq), bf16),
        compiler_params=_cparams(("arbitrary", "arbitrary")),
        name="swa_attn",
    )(qkv, qkv, qkv, qkv, qkv, bias_m, sinks)


def _t5_bucket(dist):
    n = np.maximum(dist, 0)
    max_exact = REL_BUCKETS // 2
    large = max_exact + (np.log(np.maximum(n, 1) / max_exact) / np.log(REL_MAX_DIST / max_exact)
                         * (REL_BUCKETS - max_exact)).astype(np.int32)
    large = np.minimum(large, REL_BUCKETS - 1)
    return np.where(n < max_exact, n, large).astype(np.int32)


def _by_kv_parity(a):
    rest = a.shape[2:]
    a = a.reshape((N_KV_HEADS, ATT_PAIRS, 2, ATT_BLOCK) + rest)
    a = jnp.moveaxis(a, 2, 1)
    return a.reshape((N_KV_HEADS, 2, ATT_PAIRS * ATT_BLOCK) + rest)


def _attn_tables(rel_bias):
    t_loc = np.arange(ATT_BLOCK)[:, None]
    j_loc = np.arange(2 * ATT_BLOCK)[None, :]
    dist = ATT_BLOCK + t_loc - j_loc
    band = (dist >= 0) & (dist < WINDOW)
    onehot = jnp.asarray(np.eye(REL_BUCKETS, dtype=np.float32)[_t5_bucket(dist)])
    bias = jnp.einsum('tjb,bh->htj', onehot, rel_bias.astype(f32), precision=lax.Precision.HIGHEST)
    first = band & (j_loc >= ATT_BLOCK)
    tabs = [_by_kv_parity(jnp.where(jnp.asarray(mask)[None], bias, -jnp.inf)) for mask in (band, first)]
    return jnp.stack(tabs)


def _proj_res_kernel(a_ref, w_ref, x_ref, gate_ref, o_ref):
    o_ref[...] = x_ref[...] + gate_ref[...] * _dot(a_ref[...], w_ref[...])


def proj_residual(a, w_bf, x2, gate, seq):
    t, k = a.shape
    d = w_bf.shape[1]
    tps = seq // TM
    return pl.pallas_call(
        _proj_res_kernel,
        grid=(t // TM,),
        in_specs=[
            pl.BlockSpec((TM, k), lambda i: (i, 0)),
            pl.BlockSpec((k, d), lambda i: (0, 0)),
            pl.BlockSpec((TM, d), lambda i: (i, 0)),
            pl.BlockSpec((None, 1, d), lambda i: (i // tps, 0, 0)),
        ],
        out_specs=pl.BlockSpec((TM, d), lambda i: (i, 0)),
        out_shape=jax.ShapeDtypeStruct((t, d), f32),
        compiler_params=_cparams(("arbitrary",)),
        name="proj_residual",
    )(a, w_bf, x2, gate)


def _route_kernel(x_ref, g_ref, sh_ref, sc_ref, rw_ref, rb_ref,
                  xp_ref, eidx_ref, rk_ref, gw_ref, cnt_ref, sel_scr, grp_scr, carry_scr):
    tm = x_ref.shape[0]
    ne = N_EXPERTS

    @pl.when(pl.program_id(0) == 0)
    def _():
        carry_scr[...] = jnp.zeros_like(carry_scr)

    xn = _norm_mod(x_ref[...], g_ref[...], sh_ref[...], sc_ref[...])
    _store_row_tiles(xp_ref, _pack_halves(xn))
    x_hi = xn.astype(bf16)
    x_lo = (xn - x_hi.astype(f32)).astype(bf16)
    rw = rw_ref[...]
    a = _dot_nt(rw, x_hi)
    b = _dot_nt(rw[:ne], x_lo)
    logits = a[:ne] + a[ne:] + b
    scores = jax.nn.sigmoid(logits)
    sel = scores + rb_ref[...]

    for g in range(N_GROUPS):
        tile = sel[g * E_PER_G:(g + 1) * E_PER_G, :]
        t1 = jnp.max(tile, axis=0, keepdims=True)
        dup = jnp.sum(jnp.where(tile == t1, 1.0, 0.0), axis=0, keepdims=True) >= 2.0
        t2 = jnp.max(jnp.where(tile < t1, tile, -jnp.inf), axis=0, keepdims=True)
        grp_scr[g:g + 1, :] = t1 + jnp.where(dup, t1, t2)
    gs = grp_scr[...]

    gi = lax.broadcasted_iota(i32, (N_GROUPS, tm), 0)
    grank = jnp.zeros((N_GROUPS, tm), i32)
    for g in range(N_GROUPS):
        r = grp_scr[g:g + 1, :]
        ge = jnp.where(r >= gs, 1, 0)
        gt = jnp.where(r > gs, 1, 0)
        grank = grank + jnp.where(gi > g, ge, gt)
    grp_scr[...] = jnp.where(grank < TOPK_GROUPS, 1.0, 0.0)

    masked = jnp.concatenate(
        [jnp.where(grp_scr[g:g + 1, :] > 0.5, sel[g * E_PER_G:(g + 1) * E_PER_G, :], -jnp.inf)
         for g in range(N_GROUPS)], axis=0)

    sel_scr[...] = masked
    ei = lax.broadcasted_iota(i32, (ne, tm), 0)
    rank = jnp.zeros((ne, tm), i32)
    for e in range(ne):
        r = sel_scr[e:e + 1, :]
        ge = jnp.where(r >= masked, 1, 0)
        gt = jnp.where(r > masked, 1, 0)
        rank = rank + jnp.where(ei > e, ge, gt)
    chosen = rank < TOP_K
    chf = jnp.where(chosen, 1.0, 0.0)

    wun = jnp.where(chosen, scores, 0.0)
    gwt = wun / jnp.sum(wun, axis=0, keepdims=True) * ROUTED_SCALE

    rr = lax.broadcasted_iota(i32, (tm, tm), 0)
    cc = lax.broadcasted_iota(i32, (tm, tm), 1)
    upper = jnp.where(rr < cc, 1.0, 0.0).astype(bf16)
    chb = chf.astype(bf16)
    pos = carry_scr[:, 0:1] + _dot(chb, upper)
    carry_scr[...] = carry_scr[...] + jnp.sum(chf, axis=1, keepdims=True)
    cnt_ref[...] = carry_scr[...].astype(i32)

    er = lax.broadcasted_iota(i32, (ne, ne), 0)
    ec = lax.broadcasted_iota(i32, (ne, ne), 1)
    lower = jnp.where(ec < er, 1.0, 0.0).astype(bf16)
    below = _dot(lower, chb)
    slot = jnp.where(chosen, below, -1.0)
    eif = ei.astype(f32)
    for k in range(TOP_K):
        mk = slot == float(k)
        eidx_ref[k:k + 1, :] = jnp.sum(jnp.where(mk, eif, 0.0), axis=0, keepdims=True).astype(i32)
        rk_ref[k:k + 1, :] = jnp.sum(jnp.where(mk, pos, 0.0), axis=0, keepdims=True).astype(i32)
        gw_ref[k:k + 1, :] = jnp.sum(jnp.where(mk, gwt, 0.0), axis=0, keepdims=True)


def moe_route(x2, g, sh, sc, rw_cat, rbias, seq):
    t, d = x2.shape
    tps = seq // TR
    return pl.pallas_call(
        _route_kernel,
        grid=(t // TR,),
        in_specs=[
            pl.BlockSpec((TR, d), lambda i: (i, 0)),
            pl.BlockSpec((1, d), lambda i: (0, 0)),
            pl.BlockSpec((None, 1, d), lambda i: (i // tps, 0, 0)),
            pl.BlockSpec((None, 1, d), lambda i: (i // tps, 0, 0)),
            pl.BlockSpec((2 * N_EXPERTS, d), lambda i: (0, 0)),
            pl.BlockSpec((N_EXPERTS, 1), lambda i: (0, 0)),
        ],
        out_specs=[
            pl.BlockSpec((TR * ROW_SUB, LANES), lambda i: (i, 0)),
            pl.BlockSpec((TOP_K, TR), lambda i: (0, i)),
            pl.BlockSpec((TOP_K, TR), lambda i: (0, i)),
            pl.BlockSpec((TOP_K, TR), lambda i: (0, i)),
            pl.BlockSpec((N_EXPERTS, LANES), lambda i: (0, 0)),
        ],
        out_shape=[
            jax.ShapeDtypeStruct((t * ROW_SUB, LANES), u32),
            jax.ShapeDtypeStruct((TOP_K, t), i32),
            jax.ShapeDtypeStruct((TOP_K, t), i32),
            jax.ShapeDtypeStruct((TOP_K, t), f32),
            jax.ShapeDtypeStruct((N_EXPERTS, LANES), i32),
        ],
        scratch_shapes=[
            pltpu.VMEM((N_EXPERTS, TR), f32),
            pltpu.VMEM((N_GROUPS, TR), f32),
            pltpu.VMEM((N_EXPERTS, LANES), f32),
        ],
        compiler_params=_cparams(("arbitrary",)),
        name="moe_route",
    )(x2, g.reshape(1, d), sh, sc, rw_cat, rbias.reshape(N_EXPERTS, 1))


def _tile_rows(ref, r, n=1):
    return ref.at[pl.ds(pl.multiple_of(r * ROW_SUB, ROW_SUB), n * ROW_SUB), :]


def _row_copy(src, s, dst, d, sem):
    return pltpu.make_async_copy(_tile_rows(src, s), _tile_rows(dst, d), sem)


def _dispatch_kernel(ztail_ref, dest_ref, xp_ref, xs_ref, zeros_scr, sem, zsem):
    td = xp_ref.shape[0] // ROW_SUB

    @pl.when(pl.program_id(0) == 0)
    def _():
        zeros_scr[...] = jnp.zeros_like(zeros_scr)

        def zcopy(e):
            return pltpu.make_async_copy(zeros_scr, _tile_rows(xs_ref, ztail_ref[e], BM), zsem)

        def zstart(e, c):
            zcopy(e).start()
            return c

        def zwait(e, c):
            zcopy(e).wait()
            return c

        lax.fori_loop(0, N_EXPERTS, zstart, 0)
        lax.fori_loop(0, N_EXPERTS, zwait, 0)

    def start(t, c):
        for k in range(TOP_K):
            _row_copy(xp_ref, t, xs_ref, dest_ref[k, t], sem).start(priority=k % DMA_PRIORITIES)
        return c

    def wait(t, c):
        for k in range(TOP_K):
            _row_copy(xp_ref, t, xs_ref, dest_ref[k, t], sem).wait()
        return c

    lax.fori_loop(0, td, start, 0)
    lax.fori_loop(0, td, wait, 0)


def moe_dispatch(xp, dest, ztail, nrows):
    t = xp.shape[0] // ROW_SUB
    grid_spec = pltpu.PrefetchScalarGridSpec(
        num_scalar_prefetch=1,
        grid=(t // TD,),
        in_specs=[
            pl.BlockSpec((TOP_K, TD), lambda i, z: (0, i), memory_space=pltpu.SMEM),
            pl.BlockSpec((TD * ROW_SUB, LANES), lambda i, z: (i, 0)),
        ],
        out_specs=pl.BlockSpec(memory_space=pl.ANY),
        scratch_shapes=[
            pltpu.VMEM((BM * ROW_SUB, LANES), u32),
            pltpu.SemaphoreType.DMA(()),
            pltpu.SemaphoreType.DMA(()),
        ],
    )
    return pl.pallas_call(
        _dispatch_kernel,
        grid_spec=grid_spec,
        out_shape=jax.ShapeDtypeStruct((nrows * ROW_SUB, LANES), u32),
        compiler_params=_cparams(("arbitrary",)),
        name="moe_dispatch",
    )(ztail, dest, xp)


def _ffn_packed(x_ref, m, wg, wu, wd):
    los, his = _load_row_tiles(x_ref, m)
    xa = jnp.concatenate([p.astype(bf16) for p in los], axis=1)
    xb = jnp.concatenate([p.astype(bf16) for p in his], axis=1)
    hg = _dot(xa, wg[:HALF, :]) + _dot(xb, wg[HALF:, :])
    hu = _dot(xa, wu[:HALF, :]) + _dot(xb, wu[HALF:, :])
    a = (jax.nn.silu(hg) * hu).astype(bf16)
    return _dot(a, wd[...])


def _gmm_kernel(be_ref, bf_ref, nu_ref, xs_ref, wg_ref, wu_ref, wd_ref, ys_ref, wg_s, wu_s, wd_s):
    i = pl.program_id(0)

    @pl.when(bf_ref[i] == 1)
    def _():
        wg_s[...] = wg_ref[...].astype(bf16)
        wu_s[...] = wu_ref[...].astype(bf16)
        wd_s[...] = wd_ref[...].astype(bf16)

    @pl.when(i < nu_ref[0])
    def _():
        _store_row_tiles(ys_ref, _pack_halves(_ffn_packed(xs_ref, BM, wg_s, wu_s, wd_s)))

    @pl.when(i >= nu_ref[0])
    def _():
        ys_ref[...] = jnp.zeros_like(ys_ref)


def moe_gmm(xs, blk_e, blk_first, n_used, w_gate, w_up, w_down, layer, nblk):
    d, de = w_gate.shape[2], w_gate.shape[3]
    grid_spec = pltpu.PrefetchScalarGridSpec(
        num_scalar_prefetch=3,
        grid=(nblk,),
        in_specs=[
            pl.BlockSpec((BM * ROW_SUB, LANES), lambda i, be, bf, nu: (jnp.minimum(i, nu[0] - 1), 0)),
            pl.BlockSpec((None, None, d, de), lambda i, be, bf, nu: (layer, be[i], 0, 0)),
            pl.BlockSpec((None, None, d, de), lambda i, be, bf, nu: (layer, be[i], 0, 0)),
            pl.BlockSpec((None, None, de, d), lambda i, be, bf, nu: (layer, be[i], 0, 0)),
        ],
        out_specs=pl.BlockSpec((BM * ROW_SUB, LANES), lambda i, be, bf, nu: (i, 0)),
        scratch_shapes=[
            pltpu.VMEM((d, de), bf16),
            pltpu.VMEM((d, de), bf16),
            pltpu.VMEM((de, d), bf16),
        ],
    )
    return pl.pallas_call(
        _gmm_kernel,
        grid_spec=grid_spec,
        out_shape=jax.ShapeDtypeStruct((nblk * BM * ROW_SUB, LANES), u32),
        compiler_params=_cparams(("arbitrary",)),
        name="moe_gmm",
    )(blk_e, blk_first, n_used, xs, w_gate, w_up, w_down)


def _combine_kernel(dest_ref, x_ref, xp_ref, gw_ref, gate_ref, sg_ref, su_ref, sd_ref, fg_ref, ys_ref,
                    o_ref, gath, sem, *, final_norm):
    td = x_ref.shape[0]

    def copy(t, k):
        return _row_copy(ys_ref, dest_ref[k, t], gath.at[k], t, sem)

    def start(t, c):
        for k in range(TOP_K):
            copy(t, k).start(priority=k % DMA_PRIORITIES)
        return c

    def wait(t, c):
        for k in range(TOP_K):
            copy(t, k).wait()
        return c

    lax.fori_loop(0, td, start, 0)
    ysh = _ffn_packed(xp_ref, td, sg_ref, su_ref, sd_ref)
    lax.fori_loop(0, td, wait, 0)

    acc_lo = [ysh[:, c * LANES:(c + 1) * LANES] for c in range(ROW_SUB)]
    acc_hi = [ysh[:, HALF + c * LANES:HALF + (c + 1) * LANES] for c in range(ROW_SUB)]
    gw = gw_ref[...]
    for k in range(TOP_K):
        los, his = _load_row_tiles(gath.at[k], td)
        wk = jnp.broadcast_to(gw[:, k:k + 1], (td, LANES))
        for c in range(ROW_SUB):
            acc_lo[c] = acc_lo[c] + wk * los[c]
            acc_hi[c] = acc_hi[c] + wk * his[c]
    y = jnp.concatenate(acc_lo + acc_hi, axis=1)
    out = x_ref[...] + gate_ref[...] * y
    if final_norm:
        ms = jnp.mean(out * out, axis=-1, keepdims=True)
        out = (out * lax.rsqrt(ms + RMS_EPS)) * fg_ref[...]
    o_ref[...] = out


def moe_combine(x2, xp, ys, dest, gw_t, gate, sg_bf, su_bf, sd_bf, final_g, seq, final_norm):
    t, d = x2.shape
    tps = seq // TD
    kern = functools.partial(_combine_kernel, final_norm=final_norm)
    return pl.pallas_call(
        kern,
        grid=(t // TD,),
        in_specs=[
            pl.BlockSpec((TOP_K, TD), lambda i: (0, i), memory_space=pltpu.SMEM),
            pl.BlockSpec((TD, d), lambda i: (i, 0)),
            pl.BlockSpec((TD * ROW_SUB, LANES), lambda i: (i, 0)),
            pl.BlockSpec((TD, TOP_K), lambda i: (i, 0)),
            pl.BlockSpec((None, 1, d), lambda i: (i // tps, 0, 0)),
            pl.BlockSpec((d, D_SHARED), lambda i: (0, 0)),
            pl.BlockSpec((d, D_SHARED), lambda i: (0, 0)),
            pl.BlockSpec((D_SHARED, d), lambda i: (0, 0)),
            pl.BlockSpec((1, d), lambda i: (0, 0)),
            pl.BlockSpec(memory_space=pl.ANY),
        ],
        out_specs=pl.BlockSpec((TD, d), lambda i: (i, 0)),
        out_shape=jax.ShapeDtypeStruct((t, d), f32),
        scratch_shapes=[
            pltpu.VMEM((TOP_K, TD * ROW_SUB, LANES), u32),
            pltpu.SemaphoreType.DMA(()),
        ],
        compiler_params=_cparams(("arbitrary",)),
        name="moe_combine",
    )(dest, x2, xp, gw_t, gate, sg_bf, su_bf, sd_bf, final_g.reshape(1, d), ys)


SLOT_TN = 4096


def _slots_kernel(ps_ref, eidx_ref, rk_ref, o_ref):
    e = eidx_ref[...]
    start = jnp.zeros_like(e)
    for j in range(N_EXPERTS):
        start = jnp.where(e == j, ps_ref[j], start)
    o_ref[...] = start + rk_ref[...]


def moe_slots(pad_start, eidx, rk):
    t = eidx.shape[1]
    tn = min(t, SLOT_TN)
    spec = pl.BlockSpec((TOP_K, tn), lambda i, ps: (0, i))
    grid_spec = pltpu.PrefetchScalarGridSpec(
        num_scalar_prefetch=1, grid=(t // tn,), in_specs=[spec, spec], out_specs=spec)
    return pl.pallas_call(
        _slots_kernel,
        grid_spec=grid_spec,
        out_shape=jax.ShapeDtypeStruct((TOP_K, t), i32),
        compiler_params=_cparams(("arbitrary",)),
        name="moe_slots",
    )(pad_start, eidx, rk)


def moe_layer(x2, g, sh, sc, gate, router_w, router_bias, w_gate, w_up, w_down, layer, sh_gate, sh_up, sh_down,
              final_g, seq, final_norm):
    t = x2.shape[0]
    a = t * TOP_K
    nblk = a // BM + N_EXPERTS
    rw_t = router_w.T
    rw_hi = rw_t.astype(bf16)
    rw_lo = (rw_t - rw_hi.astype(f32)).astype(bf16)
    rw_cat = jnp.concatenate([rw_hi, rw_lo], axis=0)

    xp, eidx, rk, gw, cnt = moe_route(x2, g, sh, sc, rw_cat, router_bias, seq)

    counts = cnt[:, 0]
    padded = ((counts + BM - 1) // BM) * BM
    cum_pad = jnp.cumsum(padded)
    pad_start = cum_pad - padded
    n_used = (cum_pad[-1] // BM).astype(i32).reshape(1)
    blk_row = jnp.arange(nblk, dtype=i32) * BM
    blk_e = jnp.minimum(jnp.sum((cum_pad[None, :] <= blk_row[:, None]).astype(i32), axis=1), N_EXPERTS - 1)
    blk_first = jnp.concatenate([jnp.ones((1,), i32), (blk_e[1:] != blk_e[:-1]).astype(i32)])
    dest = moe_slots(pad_start.astype(i32), eidx, rk)
    ztail = jnp.maximum(cum_pad - BM, 0).astype(i32)

    xs = moe_dispatch(xp, dest, ztail, nblk * BM)
    ys = moe_gmm(xs, blk_e, blk_first, n_used, w_gate, w_up, w_down, layer, nblk)
    return moe_combine(x2, xp, ys, dest, gw.T, gate, sh_gate.astype(bf16), sh_up.astype(bf16),
                       sh_down.astype(bf16), final_g, seq, final_norm)


def _qkv_weight(w_qkv):
    dq = N_Q_HEADS * HEAD_DIM
    dkv = N_KV_HEADS * HEAD_DIM
    d = w_qkv.shape[0]
    wq = w_qkv[:, :dq]
    wk = w_qkv[:, dq:dq + dkv].reshape(d, N_KV_HEADS, 1, HEAD_DIM)
    wv = w_qkv[:, dq + dkv:].reshape(d, N_KV_HEADS, 1, HEAD_DIM)
    wk2 = jnp.broadcast_to(wk, (d, N_KV_HEADS, 2, HEAD_DIM)).reshape(d, QK_COLS)
    wv2 = jnp.broadcast_to(wv, (d, N_KV_HEADS, 2, HEAD_DIM)).reshape(d, QK_COLS)
    return jnp.concatenate([wq, wk2, wv2], axis=1).astype(bf16)


def kernel(x, c, ada_w, ada_b, norm_mix_g, norm_ffn_g, hyb_w_in, conv_w, sgu_ln_g, sgu_ln_b, sgu_w, sgu_b,
           hyb_w_out, attn_w_qkv, attn_sinks, attn_w_o, rel_bias, router_w, router_bias, exp_w_gate,
           exp_w_up, exp_w_down, sh_w_gate, sh_w_up, sh_w_down, final_g):
    batch, seq, d = x.shape
    t = batch * seq
    x2 = x.reshape(t, d)
    mod = ada_mod(c, ada_w, ada_b)
    for l in range(DEPTH):
        parts = [mod[l, :, k * d:(k + 1) * d].reshape(batch, 1, d) for k in range(6)]
        sh_m, sc_m, g_m, sh_f, sc_f, g_f = parts
        i = l // 2
        if l % 2 == 0:
            ycat = hyb_in(x2, norm_mix_g[l], sh_m, sc_m, hyb_w_in[i].astype(bf16), conv_w[i], sgu_ln_g[i],
                          sgu_ln_b[i], sgu_w[i], sgu_b[i], seq)
            x2 = proj_residual(ycat, hyb_w_out[i].astype(bf16), x2, g_m, seq)
        else:
            qkv = qkv_proj(x2, norm_mix_g[l], sh_m, sc_m, _qkv_weight(attn_w_qkv[i]), seq, 1024)
            o = swa_attn(qkv, _attn_tables(rel_bias), attn_sinks[i], batch, seq)
            x2 = proj_residual(o, attn_w_o[i].astype(bf16), x2, g_m, seq)
        x2 = moe_layer(x2, norm_ffn_g[l], sh_f, sc_f, g_f, router_w[l], router_bias[l], exp_w_gate,
                       exp_w_up, exp_w_down, l, sh_w_gate[l], sh_w_up[l], sh_w_down[l], final_g, seq,
                       final_norm=(l == DEPTH - 1))
    return x2.reshape(batch, seq, d)
```

```python
import functools

import numpy as np
import jax
import jax.numpy as jnp
from jax import lax
from jax.experimental import pallas as pl
from jax.experimental.pallas import tpu as pltpu

f32 = jnp.float32
bf16 = jnp.bfloat16
i32 = jnp.int32
u32 = jnp.uint32

D_MODEL = 2048
DEPTH = 2
RMS_EPS = 1e-6
LN_EPS = 1e-5
D_CONV = 1024
CONV_WIDTH = 3
D_SGU = 1024
SGU_GROUPS = 8
SGU_HD = D_SGU // SGU_GROUPS
CHUNK = 128
HYB_IN = 3 * D_CONV + 2 * D_SGU
HEAD_DIM = 64
N_Q_HEADS = D_MODEL // HEAD_DIM
N_KV_HEADS = N_Q_HEADS // 8
GQA_GROUP = N_Q_HEADS // N_KV_HEADS
WINDOW = 128
ATT_BLOCK = 128
REL_BUCKETS = 32
REL_MAX_DIST = 128
N_EXPERTS = 64
TOP_K = 8
N_GROUPS = 8
TOPK_GROUPS = 4
E_PER_G = N_EXPERTS // N_GROUPS
D_EXPERT = 512
D_SHARED = 512
ROUTED_SCALE = 2.5

LANES = 128
HALF = D_MODEL // 2
VMEM_LIMIT = 56 * 1024 * 1024
TM = 512
TN_IN = 1024
TR = 512
TD = 256
BM = 512
ADA_TN = 1024
DMA_PRIORITIES = 2
COMB_RB = 16


def _cparams(sem, vmem=VMEM_LIMIT):
    return pltpu.CompilerParams(dimension_semantics=sem, vmem_limit_bytes=vmem)


def _norm_mod(x, g, sh, sc):
    ms = jnp.mean(x * x, axis=-1, keepdims=True)
    y = x * lax.rsqrt(ms + RMS_EPS)
    return (y * g) * (1.0 + sc) + sh


def _pack_halves(y):
    h = y.shape[1] // 2
    lo = lax.bitcast_convert_type(y[:, :h].astype(bf16).astype(f32), u32)
    hi = lax.bitcast_convert_type(y[:, h:].astype(bf16).astype(f32), u32)
    return (lo >> 16) | (hi & jnp.uint32(0xFFFF0000))


def _unpack_halves(w):
    lo = lax.bitcast_convert_type(w << 16, f32)
    hi = lax.bitcast_convert_type(w & jnp.uint32(0xFFFF0000), f32)
    return lo, hi


ROW_SUB = HALF // LANES


def _store_row_tiles(ref, packed):
    n = packed.shape[0]
    for c in range(ROW_SUB):
        ref[pl.ds(c, n, stride=ROW_SUB), :] = packed[:, c * LANES:(c + 1) * LANES]


def _load_row_tiles(ref, n):
    los, his = [], []
    for c in range(ROW_SUB):
        lo, hi = _unpack_halves(ref[pl.ds(c, n, stride=ROW_SUB), :])
        los.append(lo)
        his.append(hi)
    return los, his


def _gelu(x):
    return 0.5 * x * (1.0 + lax.erf(x * np.float32(np.sqrt(0.5))))


def _dot(a, b):
    return jnp.dot(a, b, preferred_element_type=f32)


def _dot_nt(a, b):
    return lax.dot_general(a, b, (((1,), (1,)), ((), ())), preferred_element_type=f32)


def _ada_kernel(c_ref, w_ref, b_ref, o_ref):
    ca = jax.nn.silu(c_ref[...]).astype(bf16)
    o_ref[...] = _dot(ca, w_ref[...].astype(bf16)) + b_ref[...]


def ada_mod(c, ada_w, ada_b):
    depth, d, n = ada_w.shape
    b = c.shape[0]
    return pl.pallas_call(
        _ada_kernel,
        grid=(depth, n // ADA_TN),
        in_specs=[
            pl.BlockSpec((b, d), lambda l, j: (0, 0)),
            pl.BlockSpec((None, d, ADA_TN), lambda l, j: (l, 0, j)),
            pl.BlockSpec((None, 1, ADA_TN), lambda l, j: (l, 0, j)),
        ],
        out_specs=pl.BlockSpec((None, b, ADA_TN), lambda l, j: (l, 0, j)),
        out_shape=jax.ShapeDtypeStruct((depth, b, n), f32),
        compiler_params=_cparams(("arbitrary", "arbitrary")),
        name="ada_mod",
    )(c, ada_w, ada_b.reshape(depth, 1, n))


def _hyb_in_kernel(x_ref, g_ref, sh_ref, sc_ref, w_ref, cw_ref, lng_ref, lnb_ref, sw_ref, sb_ref,
                   o_ref, xn_scr, a_scr, b_scr, carry_scr, *, tiles_per_seq):
    i = pl.program_id(0)
    j = pl.program_id(1)
    tm = x_ref.shape[0]

    @pl.when(j == 0)
    def _():
        xn_scr[...] = _norm_mod(x_ref[...], g_ref[...], sh_ref[...], sc_ref[...]).astype(bf16)

    @pl.when(jnp.logical_and(i == 0, j == 0))
    def _():
        carry_scr[...] = jnp.zeros_like(carry_scr)

    p = _dot(xn_scr[...], w_ref[...])

    @pl.when(j == 0)
    def _():
        a_scr[...] = p

    @pl.when(j == 1)
    def _():
        b_scr[...] = p

    @pl.when(j == 2)
    def _():
        z = b_scr[...] * p
        row = lax.broadcasted_iota(i32, z.shape, 0)
        first = (i % tiles_per_seq) == 0
        prev = jnp.where(first, 0.0, carry_scr[...])
        p1 = prev[7:8, :]
        p2 = prev[6:7, :]
        z1 = jnp.where(row == 0, p1, pltpu.roll(z, 1, 0))
        z2 = jnp.where(row == 0, p2, jnp.where(row == 1, p1, pltpu.roll(z, 2, 0)))
        cw = cw_ref[...]
        conv = cw[0:1, :] * z2 + cw[1:2, :] * z1 + cw[2:3, :] * z
        carry_scr[...] = z[tm - 8:, :]
        o_ref[:, :D_CONV] = (a_scr[...] * conv).astype(o_ref.dtype)

    @pl.when(j == 3)
    def _():
        a_scr[...] = _gelu(p)

    @pl.when(j == 4)
    def _():
        v = _gelu(p)
        mu = jnp.mean(v, axis=-1, keepdims=True)
        vc = v - mu
        var = jnp.mean(vc * vc, axis=-1, keepdims=True)
        v = (vc * lax.rsqrt(var + LN_EPS)) * lng_ref[...] + lnb_ref[...]
        vb = v.astype(bf16)
        nch = tm // CHUNK
        r = lax.broadcasted_iota(i32, (CHUNK, CHUNK), 0)
        c = lax.broadcasted_iota(i32, (CHUNK, CHUNK), 1)
        tril = r >= c
        for g in range(SGU_GROUPS):
            ws = jnp.where(tril, sw_ref[g], 0.0).astype(bf16)
            vg = jnp.concatenate(
                [vb[n * CHUNK:(n + 1) * CHUNK, g * SGU_HD:(g + 1) * SGU_HD] for n in range(nch)], axis=1)
            sg = _dot(ws, vg)
            bb = sb_ref[g]
            for n in range(nch):
                s = sg[:, n * SGU_HD:(n + 1) * SGU_HD] + bb
                u = a_scr[n * CHUNK:(n + 1) * CHUNK, g * SGU_HD:(g + 1) * SGU_HD]
                o_ref[n * CHUNK:(n + 1) * CHUNK, D_CONV + g * SGU_HD:D_CONV + (g + 1) * SGU_HD] = (
                    (u * s).astype(o_ref.dtype))


def hyb_in(x2, g, sh, sc, w_in_bf, conv_w, ln_g, ln_b, sgu_w, sgu_b, seq):
    t, d = x2.shape
    tps = seq // TM
    nj = HYB_IN // TN_IN
    sbb = jnp.broadcast_to(sgu_b[:, :, None], (SGU_GROUPS, CHUNK, SGU_HD))
    kern = functools.partial(_hyb_in_kernel, tiles_per_seq=tps)
    return pl.pallas_call(
        kern,
        grid=(t // TM, nj),
        in_specs=[
            pl.BlockSpec((TM, d), lambda i, j: (i, 0)),
            pl.BlockSpec((1, d), lambda i, j: (0, 0)),
            pl.BlockSpec((None, 1, d), lambda i, j: (i // tps, 0, 0)),
            pl.BlockSpec((None, 1, d), lambda i, j: (i // tps, 0, 0)),
            pl.BlockSpec((d, TN_IN), lambda i, j: (0, j)),
            pl.BlockSpec((CONV_WIDTH, D_CONV), lambda i, j: (0, 0)),
            pl.BlockSpec((1, D_SGU), lambda i, j: (0, 0)),
            pl.BlockSpec((1, D_SGU), lambda i, j: (0, 0)),
            pl.BlockSpec((SGU_GROUPS, CHUNK, CHUNK), lambda i, j: (0, 0, 0)),
            pl.BlockSpec((SGU_GROUPS, CHUNK, SGU_HD), lambda i, j: (0, 0, 0)),
        ],
        out_specs=pl.BlockSpec((TM, D_CONV + D_SGU), lambda i, j: (i, 0)),
        out_shape=jax.ShapeDtypeStruct((t, D_CONV + D_SGU), bf16),
        scratch_shapes=[
            pltpu.VMEM((TM, d), bf16),
            pltpu.VMEM((TM, TN_IN), f32),
            pltpu.VMEM((TM, TN_IN), f32),
            pltpu.VMEM((8, D_CONV), f32),
        ],
        compiler_params=_cparams(("arbitrary", "arbitrary")),
        name="hyb_in",
    )(x2, g.reshape(1, d), sh, sc, w_in_bf, conv_w, ln_g.reshape(1, -1), ln_b.reshape(1, -1), sgu_w, sbb)


def _qkv_kernel(x_ref, g_ref, sh_ref, sc_ref, w_ref, o_ref, xn_scr):
    @pl.when(pl.program_id(1) == 0)
    def _():
        xn_scr[...] = _norm_mod(x_ref[...], g_ref[...], sh_ref[...], sc_ref[...]).astype(bf16)

    o_ref[...] = _dot(xn_scr[...], w_ref[...]).astype(o_ref.dtype)


def qkv_proj(x2, g, sh, sc, w_bf, seq, tn):
    t, d = x2.shape
    n = w_bf.shape[1]
    tps = seq // TM
    return pl.pallas_call(
        _qkv_kernel,
        grid=(t // TM, n // tn),
        in_specs=[
            pl.BlockSpec((TM, d), lambda i, j: (i, 0)),
            pl.BlockSpec((1, d), lambda i, j: (0, 0)),
            pl.BlockSpec((None, 1, d), lambda i, j: (i // tps, 0, 0)),
            pl.BlockSpec((None, 1, d), lambda i, j: (i // tps, 0, 0)),
            pl.BlockSpec((d, tn), lambda i, j: (0, j)),
        ],
        out_specs=pl.BlockSpec((TM, tn), lambda i, j: (i, j)),
        out_shape=jax.ShapeDtypeStruct((t, n), bf16),
        scratch_shapes=[pltpu.VMEM((TM, d), bf16)],
        compiler_params=_cparams(("arbitrary", "arbitrary")),
        name="qkv_proj",
    )(x2, g.reshape(1, d), sh, sc, w_bf)


QK_COLS = N_KV_HEADS * 2 * HEAD_DIM


def _attn_kernel(q_ref, kp_ref, kc_ref, vp_ref, vc_ref, bias_ref, sink_ref, o_ref):
    lane = lax.broadcasted_iota(i32, (2 * ATT_BLOCK, 2 * HEAD_DIM), 1)
    low = lane < HEAD_DIM
    zero = jnp.zeros((), bf16)
    scale = jnp.asarray(HEAD_DIM ** -0.5, bf16)
    ones = jnp.ones((2 * ATT_BLOCK, 2 * HEAD_DIM), bf16)
    for kh in range(N_KV_HEADS):
        cs = slice(kh * 2 * HEAD_DIM, (kh + 1) * 2 * HEAD_DIM)
        kk = jnp.concatenate([kp_ref[:, cs], kc_ref[:, cs]], axis=0)
        vv = jnp.concatenate([vp_ref[:, cs], vc_ref[:, cs]], axis=0)
        kz = (jnp.where(low, kk, zero), jnp.where(low, zero, kk))
        vz = (jnp.where(low, vv, zero), jnp.where(low, zero, vv))
        q0 = kh * GQA_GROUP * HEAD_DIM
        qs = jnp.concatenate(
            [q_ref[:, q0 + pr * 2 * HEAD_DIM:q0 + (pr + 1) * 2 * HEAD_DIM] for pr in range(ATT_PAIRS)],
            axis=0) * scale
        acc = None
        for par in range(2):
            s = _dot_nt(qs, kz[par]) + bias_ref[kh, par]
            sk = jnp.concatenate(
                [jnp.full((ATT_BLOCK, ATT_BLOCK), sink_ref[kh * GQA_GROUP + 2 * pr + par], f32)
                 for pr in range(ATT_PAIRS)], axis=0)
            rm = jnp.max(s, axis=-1, keepdims=True)
            mb = jnp.maximum(jnp.broadcast_to(rm, sk.shape), sk)
            p = jnp.concatenate([jnp.exp(s[:, :ATT_BLOCK] - mb), jnp.exp(s[:, ATT_BLOCK:] - mb)], axis=1)
            ov = _dot(p.astype(bf16), jnp.concatenate([vz[par], ones], axis=1))
            den = ov[:, 2 * HEAD_DIM:] + jnp.exp(sk - mb)
            o = ov[:, :2 * HEAD_DIM] / den
            acc = o if acc is None else acc + o
        for pr in range(ATT_PAIRS):
            o_ref[:, q0 + pr * 2 * HEAD_DIM:q0 + (pr + 1) * 2 * HEAD_DIM] = (
                acc[pr * ATT_BLOCK:(pr + 1) * ATT_BLOCK, :].astype(o_ref.dtype))


ATT_PAIRS = GQA_GROUP // 2


def swa_attn(qkv, bias_m, sinks, batch, seq):
    t = qkv.shape[0]
    nb = seq // ATT_BLOCK
    dq = N_Q_HEADS * HEAD_DIM
    kcol = dq // QK_COLS
    vcol = kcol + 1

    def prev(b, i):
        return b * nb + jnp.maximum(i - 1, 0)

    return pl.pallas_call(
        _attn_kernel,
        grid=(batch, nb),
        in_specs=[
            pl.BlockSpec((ATT_BLOCK, dq), lambda b, i: (b * nb + i, 0)),
            pl.BlockSpec((ATT_BLOCK, QK_COLS), lambda b, i: (prev(b, i), kcol)),
            pl.BlockSpec((ATT_BLOCK, QK_COLS), lambda b, i: (b * nb + i, kcol)),
            pl.BlockSpec((ATT_BLOCK, QK_COLS), lambda b, i: (prev(b, i), vcol)),
            pl.BlockSpec((ATT_BLOCK, QK_COLS), lambda b, i: (b * nb + i, vcol)),
            pl.BlockSpec((None, N_KV_HEADS, 2, ATT_PAIRS * ATT_BLOCK, 2 * ATT_BLOCK),
                         lambda b, i: (jnp.where(i == 0, 1, 0), 0, 0, 0, 0)),
            pl.BlockSpec(memory_space=pltpu.SMEM),
        ],
        out_specs=pl.BlockSpec((ATT_BLOCK, dq), lambda b, i: (b * nb + i, 0)),
        out_shape=jax.ShapeDtypeStruct((t, dq), bf16),
        compiler_params=_cparams(("arbitrary", "arbitrary")),
        name="swa_attn",
    )(qkv, qkv, qkv, qkv, qkv, bias_m, sinks)


def _t5_bucket(dist):
    n = np.maximum(dist, 0)
    max_exact = REL_BUCKETS // 2
    large = max_exact + (np.log(np.maximum(n, 1) / max_exact) / np.log(REL_MAX_DIST / max_exact)
                         * (REL_BUCKETS - max_exact)).astype(np.int32)
    large = np.minimum(large, REL_BUCKETS - 1)
    return np.where(n < max_exact, n, large).astype(np.int32)


def _by_kv_parity(a):
    rest = a.shape[2:]
    a = a.reshape((N_KV_HEADS, ATT_PAIRS, 2, ATT_BLOCK) + rest)
    a = jnp.moveaxis(a, 2, 1)
    return a.reshape((N_KV_HEADS, 2, ATT_PAIRS * ATT_BLOCK) + rest)


def _attn_tables(rel_bias):
    t_loc = np.arange(ATT_BLOCK)[:, None]
    j_loc = np.arange(2 * ATT_BLOCK)[None, :]
    dist = ATT_BLOCK + t_loc - j_loc
    band = (dist >= 0) & (dist < WINDOW)
    onehot = jnp.asarray(np.eye(REL_BUCKETS, dtype=np.float32)[_t5_bucket(dist)])
    bias = jnp.einsum('tjb,bh->htj', onehot, rel_bias.astype(f32), precision=lax.Precision.HIGHEST)
    first = band & (j_loc >= ATT_BLOCK)
    tabs = [_by_kv_parity(jnp.where(jnp.asarray(mask)[None], bias, -jnp.inf)) for mask in (band, first)]
    return jnp.stack(tabs)


def _proj_res_kernel(a_ref, w_ref, x_ref, gate_ref, o_ref):
    o_ref[...] = x_ref[...] + gate_ref[...] * _dot(a_ref[...], w_ref[...])


def proj_residual(a, w_bf, x2, gate, seq):
    t, k = a.shape
    d = w_bf.shape[1]
    tps = seq // TM
    return pl.pallas_call(
        _proj_res_kernel,
        grid=(t // TM,),
        in_specs=[
            pl.BlockSpec((TM, k), lambda i: (i, 0)),
            pl.BlockSpec((k, d), lambda i: (0, 0)),
            pl.BlockSpec((TM, d), lambda i: (i, 0)),
            pl.BlockSpec((None, 1, d), lambda i: (i // tps, 0, 0)),
        ],
        out_specs=pl.BlockSpec((TM, d), lambda i: (i, 0)),
        out_shape=jax.ShapeDtypeStruct((t, d), f32),
        compiler_params=_cparams(("arbitrary",)),
        name="proj_residual",
    )(a, w_bf, x2, gate)


def _route_kernel(x_ref, g_ref, sh_ref, sc_ref, rw_ref, rb_ref,
                  xp_ref, eidx_ref, rk_ref, gw_ref, cnt_ref, sel_scr, grp_scr, carry_scr):
    tm = x_ref.shape[0]
    ne = N_EXPERTS

    @pl.when(pl.program_id(0) == 0)
    def _():
        carry_scr[...] = jnp.zeros_like(carry_scr)

    xn = _norm_mod(x_ref[...], g_ref[...], sh_ref[...], sc_ref[...])
    _store_row_tiles(xp_ref, _pack_halves(xn))
    x_hi = xn.astype(bf16)
    x_lo = (xn - x_hi.astype(f32)).astype(bf16)
    rw = rw_ref[...]
    a = _dot_nt(rw, x_hi)
    b = _dot_nt(rw[:ne], x_lo)
    logits = a[:ne] + a[ne:] + b
    scores = jax.nn.sigmoid(logits)
    sel = scores + rb_ref[...]

    for g in range(N_GROUPS):
        tile = sel[g * E_PER_G:(g + 1) * E_PER_G, :]
        t1 = jnp.max(tile, axis=0, keepdims=True)
        dup = jnp.sum(jnp.where(tile == t1, 1.0, 0.0), axis=0, keepdims=True) >= 2.0
        t2 = jnp.max(jnp.where(tile < t1, tile, -jnp.inf), axis=0, keepdims=True)
        grp_scr[g:g + 1, :] = t1 + jnp.where(dup, t1, t2)
    gs = grp_scr[...]

    gi = lax.broadcasted_iota(i32, (N_GROUPS, tm), 0)
    grank = jnp.zeros((N_GROUPS, tm), i32)
    for g in range(N_GROUPS):
        r = grp_scr[g:g + 1, :]
        ge = jnp.where(r >= gs, 1, 0)
        gt = jnp.where(r > gs, 1, 0)
        grank = grank + jnp.where(gi > g, ge, gt)
    grp_scr[...] = jnp.where(grank < TOPK_GROUPS, 1.0, 0.0)

    masked = jnp.concatenate(
        [jnp.where(grp_scr[g:g + 1, :] > 0.5, sel[g * E_PER_G:(g + 1) * E_PER_G, :], -jnp.inf)
         for g in range(N_GROUPS)], axis=0)

    sel_scr[...] = masked
    ei = lax.broadcasted_iota(i32, (ne, tm), 0)
    rank = jnp.zeros((ne, tm), i32)
    for e in range(ne):
        r = sel_scr[e:e + 1, :]
        ge = jnp.where(r >= masked, 1, 0)
        gt = jnp.where(r > masked, 1, 0)
        rank = rank + jnp.where(ei > e, ge, gt)
    chosen = rank < TOP_K
    chf = jnp.where(chosen, 1.0, 0.0)

    wun = jnp.where(chosen, scores, 0.0)
    gwt = wun / jnp.sum(wun, axis=0, keepdims=True) * ROUTED_SCALE

    rr = lax.broadcasted_iota(i32, (tm, tm), 0)
    cc = lax.broadcasted_iota(i32, (tm, tm), 1)
    upper = jnp.where(rr < cc, 1.0, 0.0).astype(bf16)
    chb = chf.astype(bf16)
    pos = carry_scr[:, 0:1] + _dot(chb, upper)
    carry_scr[...] = carry_scr[...] + jnp.sum(chf, axis=1, keepdims=True)
    cnt_ref[...] = carry_scr[...].astype(i32)

    er = lax.broadcasted_iota(i32, (ne, ne), 0)
    ec = lax.broadcasted_iota(i32, (ne, ne), 1)
    lower = jnp.where(ec < er, 1.0, 0.0).astype(bf16)
    below = _dot(lower, chb)
    slot = jnp.where(chosen, below, -1.0)
    eif = ei.astype(f32)
    for k in range(TOP_K):
        mk = slot == float(k)
        eidx_ref[k:k + 1, :] = jnp.sum(jnp.where(mk, eif, 0.0), axis=0, keepdims=True).astype(i32)
        rk_ref[k:k + 1, :] = jnp.sum(jnp.where(mk, pos, 0.0), axis=0, keepdims=True).astype(i32)
        gw_ref[k:k + 1, :] = jnp.sum(jnp.where(mk, gwt, 0.0), axis=0, keepdims=True)


def moe_route(x2, g, sh, sc, rw_cat, rbias, seq):
    t, d = x2.shape
    tps = seq // TR
    return pl.pallas_call(
        _route_kernel,
        grid=(t // TR,),
        in_specs=[
            pl.BlockSpec((TR, d), lambda i: (i, 0)),
            pl.BlockSpec((1, d), lambda i: (0, 0)),
            pl.BlockSpec((None, 1, d), lambda i: (i // tps, 0, 0)),
            pl.BlockSpec((None, 1, d), lambda i: (i // tps, 0, 0)),
            pl.BlockSpec((2 * N_EXPERTS, d), lambda i: (0, 0)),
            pl.BlockSpec((N_EXPERTS, 1), lambda i: (0, 0)),
        ],
        out_specs=[
            pl.BlockSpec((TR * ROW_SUB, LANES), lambda i: (i, 0)),
            pl.BlockSpec((TOP_K, TR), lambda i: (0, i)),
            pl.BlockSpec((TOP_K, TR), lambda i: (0, i)),
            pl.BlockSpec((TOP_K, TR), lambda i: (0, i)),
            pl.BlockSpec((N_EXPERTS, LANES), lambda i: (0, 0)),
        ],
        out_shape=[
            jax.ShapeDtypeStruct((t * ROW_SUB, LANES), u32),
            jax.ShapeDtypeStruct((TOP_K, t), i32),
            jax.ShapeDtypeStruct((TOP_K, t), i32),
            jax.ShapeDtypeStruct((TOP_K, t), f32),
            jax.ShapeDtypeStruct((N_EXPERTS, LANES), i32),
        ],
        scratch_shapes=[
            pltpu.VMEM((N_EXPERTS, TR), f32),
            pltpu.VMEM((N_GROUPS, TR), f32),
            pltpu.VMEM((N_EXPERTS, LANES), f32),
        ],
        compiler_params=_cparams(("arbitrary",)),
        name="moe_route",
    )(x2, g.reshape(1, d), sh, sc, rw_cat, rbias.reshape(N_EXPERTS, 1))


def _tile_rows(ref, r, n=1):
    return ref.at[pl.ds(pl.multiple_of(r * ROW_SUB, ROW_SUB), n * ROW_SUB), :]


def _row_copy(src, s, dst, d, sem):
    return pltpu.make_async_copy(_tile_rows(src, s), _tile_rows(dst, d), sem)


def _dispatch_kernel(ztail_ref, dest_ref, xp_ref, sg_ref, su_ref, sd_ref, xs_ref, ysh_ref, zeros_scr, sem, zsem):
    td = xp_ref.shape[0] // ROW_SUB

    @pl.when(pl.program_id(0) == 0)
    def _():
        zeros_scr[...] = jnp.zeros_like(zeros_scr)

        def zcopy(e):
            return pltpu.make_async_copy(zeros_scr, _tile_rows(xs_ref, ztail_ref[e], BM), zsem)

        def zstart(e, c):
            zcopy(e).start()
            return c

        def zwait(e, c):
            zcopy(e).wait()
            return c

        lax.fori_loop(0, N_EXPERTS, zstart, 0)
        lax.fori_loop(0, N_EXPERTS, zwait, 0)

    for t in range(td):
        for k in range(TOP_K):
            _row_copy(xp_ref, t, xs_ref, dest_ref[k, t], sem).start(priority=k % DMA_PRIORITIES)
    ysh_ref[...] = _ffn_packed(xp_ref, td, sg_ref, su_ref, sd_ref)
    for t in range(td):
        for k in range(TOP_K):
            _row_copy(xp_ref, t, xs_ref, dest_ref[k, t], sem).wait()


def moe_dispatch(xp, dest, ztail, nrows, sg_bf, su_bf, sd_bf):
    t = xp.shape[0] // ROW_SUB
    d = sg_bf.shape[0]
    grid_spec = pltpu.PrefetchScalarGridSpec(
        num_scalar_prefetch=1,
        grid=(t // TD,),
        in_specs=[
            pl.BlockSpec((TOP_K, TD), lambda i, z: (0, i), memory_space=pltpu.SMEM),
            pl.BlockSpec((TD * ROW_SUB, LANES), lambda i, z: (i, 0)),
            pl.BlockSpec((d, D_SHARED), lambda i, z: (0, 0)),
            pl.BlockSpec((d, D_SHARED), lambda i, z: (0, 0)),
            pl.BlockSpec((D_SHARED, d), lambda i, z: (0, 0)),
        ],
        out_specs=[pl.BlockSpec(memory_space=pl.ANY), pl.BlockSpec((TD, d), lambda i, z: (i, 0))],
        scratch_shapes=[
            pltpu.VMEM((BM * ROW_SUB, LANES), u32),
            pltpu.SemaphoreType.DMA(()),
            pltpu.SemaphoreType.DMA(()),
        ],
    )
    return pl.pallas_call(
        _dispatch_kernel,
        grid_spec=grid_spec,
        out_shape=[jax.ShapeDtypeStruct((nrows * ROW_SUB, LANES), u32), jax.ShapeDtypeStruct((t, d), f32)],
        compiler_params=_cparams(("arbitrary",)),
        name="moe_dispatch",
    )(ztail, dest, xp, sg_bf, su_bf, sd_bf)


def _ffn_packed(x_ref, m, wg, wu, wd):
    los, his = _load_row_tiles(x_ref, m)
    xa = jnp.concatenate([p.astype(bf16) for p in los], axis=1)
    xb = jnp.concatenate([p.astype(bf16) for p in his], axis=1)
    hg = _dot(xa, wg[:HALF, :]) + _dot(xb, wg[HALF:, :])
    hu = _dot(xa, wu[:HALF, :]) + _dot(xb, wu[HALF:, :])
    a = (jax.nn.silu(hg) * hu).astype(bf16)
    return _dot(a, wd[...])


def _gmm_kernel(be_ref, bf_ref, nu_ref, xs_ref, wg_ref, wu_ref, wd_ref, ys_ref, wg_s, wu_s, wd_s):
    i = pl.program_id(0)

    @pl.when(bf_ref[i] == 1)
    def _():
        wg_s[...] = wg_ref[...].astype(bf16)
        wu_s[...] = wu_ref[...].astype(bf16)
        wd_s[...] = wd_ref[...].astype(bf16)

    @pl.when(i < nu_ref[0])
    def _():
        _store_row_tiles(ys_ref, _pack_halves(_ffn_packed(xs_ref, BM, wg_s, wu_s, wd_s)))

    @pl.when(i >= nu_ref[0])
    def _():
        ys_ref[...] = jnp.zeros_like(ys_ref)


def moe_gmm(xs, blk_e, blk_first, n_used, w_gate, w_up, w_down, layer, nblk):
    d, de = w_gate.shape[2], w_gate.shape[3]
    grid_spec = pltpu.PrefetchScalarGridSpec(
        num_scalar_prefetch=3,
        grid=(nblk,),
        in_specs=[
            pl.BlockSpec((BM * ROW_SUB, LANES), lambda i, be, bf, nu: (jnp.minimum(i, nu[0] - 1), 0)),
            pl.BlockSpec((None, None, d, de), lambda i, be, bf, nu: (layer, be[i], 0, 0)),
            pl.BlockSpec((None, None, d, de), lambda i, be, bf, nu: (layer, be[i], 0, 0)),
            pl.BlockSpec((None, None, de, d), lambda i, be, bf, nu: (layer, be[i], 0, 0)),
        ],
        out_specs=pl.BlockSpec((BM * ROW_SUB, LANES), lambda i, be, bf, nu: (i, 0)),
        scratch_shapes=[
            pltpu.VMEM((d, de), bf16),
            pltpu.VMEM((d, de), bf16),
            pltpu.VMEM((de, d), bf16),
        ],
    )
    return pl.pallas_call(
        _gmm_kernel,
        grid_spec=grid_spec,
        out_shape=jax.ShapeDtypeStruct((nblk * BM * ROW_SUB, LANES), u32),
        compiler_params=_cparams(("arbitrary",)),
        name="moe_gmm",
    )(blk_e, blk_first, n_used, xs, w_gate, w_up, w_down)


def _combine_kernel(dest_ref, dnext_ref, x_ref, ysh_ref, gw_ref, gate_ref, fg_ref, ys_ref,
                    o_ref, gath_a, gath_b, sem, *, final_norm):
    td = x_ref.shape[0]
    i = pl.program_id(0)

    def copy(dref, buf, s, t, k):
        return _row_copy(ys_ref, dref[k, t], buf.at[k], t, sem.at[s])

    def wait_tile(dref, buf, s):
        for t in range(td):
            for k in range(TOP_K):
                copy(dref, buf, s, t, k).wait()

    @pl.when(i == 0)
    def _():
        def body(t, c):
            for k in range(TOP_K):
                copy(dest_ref, gath_a, 0, t, k).start(priority=k % DMA_PRIORITIES)
            return c
        lax.fori_loop(0, td, body, 0)

    def step(cur_buf, cur_s, nxt_buf, nxt_s):
        wait_tile(dest_ref, cur_buf, cur_s)
        for t in range(td):
            for k in range(TOP_K):
                copy(dnext_ref, nxt_buf, nxt_s, t, k).start(priority=k % DMA_PRIORITIES)
        for rb in range(td // COMB_RB):
            rows = slice(rb * COMB_RB, (rb + 1) * COMB_RB)
            ysh = ysh_ref[rows, :]
            acc_lo = [ysh[:, c * LANES:(c + 1) * LANES] for c in range(ROW_SUB)]
            acc_hi = [ysh[:, HALF + c * LANES:HALF + (c + 1) * LANES] for c in range(ROW_SUB)]
            gw = gw_ref[rows, :]
            for k in range(TOP_K):
                wk = jnp.broadcast_to(gw[:, k:k + 1], (COMB_RB, LANES))
                for c in range(ROW_SUB):
                    word = cur_buf[k, pl.ds(rb * COMB_RB * ROW_SUB + c, COMB_RB, stride=ROW_SUB), :]
                    lo, hi = _unpack_halves(word)
                    acc_lo[c] = acc_lo[c] + wk * lo
                    acc_hi[c] = acc_hi[c] + wk * hi
            y = jnp.concatenate(acc_lo + acc_hi, axis=1)
            out = x_ref[rows, :] + gate_ref[...] * y
            if final_norm:
                ms = jnp.mean(out * out, axis=-1, keepdims=True)
                out = (out * lax.rsqrt(ms + RMS_EPS)) * fg_ref[...]
            o_ref[rows, :] = out

        @pl.when(i == pl.num_programs(0) - 1)
        def _():
            wait_tile(dnext_ref, nxt_buf, nxt_s)

    @pl.when(i % 2 == 0)
    def _():
        step(gath_a, 0, gath_b, 1)

    @pl.when(i % 2 == 1)
    def _():
        step(gath_b, 1, gath_a, 0)


def moe_combine(x2, ysh, ys, dest, gw_t, gate, final_g, seq, final_norm):
    t, d = x2.shape
    tps = seq // TD
    last = t // TD - 1
    kern = functools.partial(_combine_kernel, final_norm=final_norm)
    return pl.pallas_call(
        kern,
        grid=(t // TD,),
        in_specs=[
            pl.BlockSpec((TOP_K, TD), lambda i: (0, i), memory_space=pltpu.SMEM),
            pl.BlockSpec((TOP_K, TD), lambda i: (0, jnp.minimum(i + 1, last)), memory_space=pltpu.SMEM),
            pl.BlockSpec((TD, d), lambda i: (i, 0)),
            pl.BlockSpec((TD, d), lambda i: (i, 0)),
            pl.BlockSpec((TD, TOP_K), lambda i: (i, 0)),
            pl.BlockSpec((None, 1, d), lambda i: (i // tps, 0, 0)),
            pl.BlockSpec((1, d), lambda i: (0, 0)),
            pl.BlockSpec(memory_space=pl.ANY),
        ],
        out_specs=pl.BlockSpec((TD, d), lambda i: (i, 0)),
        out_shape=jax.ShapeDtypeStruct((t, d), f32),
        scratch_shapes=[
            pltpu.VMEM((TOP_K, TD * ROW_SUB, LANES), u32),
            pltpu.VMEM((TOP_K, TD * ROW_SUB, LANES), u32),
            pltpu.SemaphoreType.DMA((2,)),
        ],
        compiler_params=_cparams(("arbitrary",)),
        name="moe_combine",
    )(dest, dest, x2, ysh, gw_t, gate, final_g.reshape(1, d), ys)


SLOT_TN = 4096


def _slots_kernel(ps_ref, eidx_ref, rk_ref, o_ref):
    e = eidx_ref[...]
    start = jnp.zeros_like(e)
    for j in range(N_EXPERTS):
        start = jnp.where(e == j, ps_ref[j], start)
    o_ref[...] = start + rk_ref[...]


def moe_slots(pad_start, eidx, rk):
    t = eidx.shape[1]
    tn = min(t, SLOT_TN)
    spec = pl.BlockSpec((TOP_K, tn), lambda i, ps: (0, i))
    grid_spec = pltpu.PrefetchScalarGridSpec(
        num_scalar_prefetch=1, grid=(t // tn,), in_specs=[spec, spec], out_specs=spec)
    return pl.pallas_call(
        _slots_kernel,
        grid_spec=grid_spec,
        out_shape=jax.ShapeDtypeStruct((TOP_K, t), i32),
        compiler_params=_cparams(("arbitrary",)),
        name="moe_slots",
    )(pad_start, eidx, rk)


def moe_layer(x2, g, sh, sc, gate, router_w, router_bias, w_gate, w_up, w_down, layer, sh_gate, sh_up, sh_down,
              final_g, seq, final_norm):
    t = x2.shape[0]
    a = t * TOP_K
    nblk = a // BM + N_EXPERTS
    rw_t = router_w.T
    rw_hi = rw_t.astype(bf16)
    rw_lo = (rw_t - rw_hi.astype(f32)).astype(bf16)
    rw_cat = jnp.concatenate([rw_hi, rw_lo], axis=0)

    xp, eidx, rk, gw, cnt = moe_route(x2, g, sh, sc, rw_cat, router_bias, seq)

    counts = cnt[:, 0]
    padded = ((counts + BM - 1) // BM) * BM
    cum_pad = jnp.cumsum(padded)
    pad_start = cum_pad - padded
    n_used = (cum_pad[-1] // BM).astype(i32).reshape(1)
    blk_row = jnp.arange(nblk, dtype=i32) * BM
    blk_e = jnp.minimum(jnp.sum((cum_pad[None, :] <= blk_row[:, None]).astype(i32), axis=1), N_EXPERTS - 1)
    blk_first = jnp.concatenate([jnp.ones((1,), i32), (blk_e[1:] != blk_e[:-1]).astype(i32)])
    dest = moe_slots(pad_start.astype(i32), eidx, rk)
    ztail = jnp.maximum(cum_pad - BM, 0).astype(i32)

    xs, ysh = moe_dispatch(xp, dest, ztail, nblk * BM, sh_gate.astype(bf16), sh_up.astype(bf16),
                           sh_down.astype(bf16))
    ys = moe_gmm(xs, blk_e, blk_first, n_used, w_gate, w_up, w_down, layer, nblk)
    return moe_combine(x2, ysh, ys, dest, gw.T, gate, final_g, seq, final_norm)


def _qkv_weight(w_qkv):
    dq = N_Q_HEADS * HEAD_DIM
    dkv = N_KV_HEADS * HEAD_DIM
    d = w_qkv.shape[0]
    wq = w_qkv[:, :dq]
    wk = w_qkv[:, dq:dq + dkv].reshape(d, N_KV_HEADS, 1, HEAD_DIM)
    wv = w_qkv[:, dq + dkv:].reshape(d, N_KV_HEADS, 1, HEAD_DIM)
    wk2 = jnp.broadcast_to(wk, (d, N_KV_HEADS, 2, HEAD_DIM)).reshape(d, QK_COLS)
    wv2 = jnp.broadcast_to(wv, (d, N_KV_HEADS, 2, HEAD_DIM)).reshape(d, QK_COLS)
    return jnp.concatenate([wq, wk2, wv2], axis=1).astype(bf16)


def kernel(x, c, ada_w, ada_b, norm_mix_g, norm_ffn_g, hyb_w_in, conv_w, sgu_ln_g, sgu_ln_b, sgu_w, sgu_b,
           hyb_w_out, attn_w_qkv, attn_sinks, attn_w_o, rel_bias, router_w, router_bias, exp_w_gate,
           exp_w_up, exp_w_down, sh_w_gate, sh_w_up, sh_w_down, final_g):
    batch, seq, d = x.shape
    t = batch * seq
    x2 = x.reshape(t, d)
    mod = ada_mod(c, ada_w, ada_b)
    for l in range(DEPTH):
        parts = [mod[l, :, k * d:(k + 1) * d].reshape(batch, 1, d) for k in range(6)]
        sh_m, sc_m, g_m, sh_f, sc_f, g_f = parts
        i = l // 2
        if l % 2 == 0:
            ycat = hyb_in(x2, norm_mix_g[l], sh_m, sc_m, hyb_w_in[i].astype(bf16), conv_w[i], sgu_ln_g[i],
                          sgu_ln_b[i], sgu_w[i], sgu_b[i], seq)
            x2 = proj_residual(ycat, hyb_w_out[i].astype(bf16), x2, g_m, seq)
        else:
            qkv = qkv_proj(x2, norm_mix_g[l], sh_m, sc_m, _qkv_weight(attn_w_qkv[i]), seq, 1024)
            o = swa_attn(qkv, _attn_tables(rel_bias), attn_sinks[i], batch, seq)
            x2 = proj_residual(o, attn_w_o[i].astype(bf16), x2, g_m, seq)
        x2 = moe_layer(x2, norm_ffn_g[l], sh_f, sc_f, g_f, router_w[l], router_bias[l], exp_w_gate,
                       exp_w_up, exp_w_down, l, sh_w_gate[l], sh_w_up[l], sh_w_down[l], final_g, seq,
                       final_norm=(l == DEPTH - 1))
    return x2.reshape(batch, seq, d)
```

```python
import functools

import numpy as np
import jax
import jax.numpy as jnp
from jax import lax
from jax.experimental import pallas as pl
from jax.experimental.pallas import tpu as pltpu

f32 = jnp.float32
bf16 = jnp.bfloat16
i32 = jnp.int32
u32 = jnp.uint32

D_MODEL = 2048
DEPTH = 2
RMS_EPS = 1e-6
LN_EPS = 1e-5
D_CONV = 1024
CONV_WIDTH = 3
D_SGU = 1024
SGU_GROUPS = 8
SGU_HD = D_SGU // SGU_GROUPS
CHUNK = 128
HYB_IN = 3 * D_CONV + 2 * D_SGU
HEAD_DIM = 64
N_Q_HEADS = D_MODEL // HEAD_DIM
N_KV_HEADS = N_Q_HEADS // 8
GQA_GROUP = N_Q_HEADS // N_KV_HEADS
WINDOW = 128
ATT_BLOCK = 128
REL_BUCKETS = 32
REL_MAX_DIST = 128
N_EXPERTS = 64
TOP_K = 8
N_GROUPS = 8
TOPK_GROUPS = 4
E_PER_G = N_EXPERTS // N_GROUPS
D_EXPERT = 512
D_SHARED = 512
ROUTED_SCALE = 2.5

LANES = 128
HALF = D_MODEL // 2
VMEM_LIMIT = 56 * 1024 * 1024
TM = 512
TN_IN = 1024
TR = 512
TD = 256
BM = 512
ADA_TN = 1024
DMA_PRIORITIES = 2
COMB_RB = 16


def _cparams(sem, vmem=VMEM_LIMIT):
    return pltpu.CompilerParams(dimension_semantics=sem, vmem_limit_bytes=vmem)


def _norm_mod(x, g, sh, sc):
    ms = jnp.mean(x * x, axis=-1, keepdims=True)
    y = x * lax.rsqrt(ms + RMS_EPS)
    return (y * g) * (1.0 + sc) + sh


def _pack_halves(y):
    h = y.shape[1] // 2
    lo = lax.bitcast_convert_type(y[:, :h].astype(bf16).astype(f32), u32)
    hi = lax.bitcast_convert_type(y[:, h:].astype(bf16).astype(f32), u32)
    return (lo >> 16) | (hi & jnp.uint32(0xFFFF0000))


def _unpack_halves(w):
    lo = lax.bitcast_convert_type(w << 16, f32)
    hi = lax.bitcast_convert_type(w & jnp.uint32(0xFFFF0000), f32)
    return lo, hi


ROW_SUB = HALF // LANES


def _store_row_tiles(ref, packed):
    n = packed.shape[0]
    for c in range(ROW_SUB):
        ref[pl.ds(c, n, stride=ROW_SUB), :] = packed[:, c * LANES:(c + 1) * LANES]


def _load_row_tiles(ref, n):
    los, his = [], []
    for c in range(ROW_SUB):
        lo, hi = _unpack_halves(ref[pl.ds(c, n, stride=ROW_SUB), :])
        los.append(lo)
        his.append(hi)
    return los, his


def _gelu(x):
    return 0.5 * x * (1.0 + lax.erf(x * np.float32(np.sqrt(0.5))))


def _dot(a, b):
    return jnp.dot(a, b, preferred_element_type=f32)


def _dot_nt(a, b):
    return lax.dot_general(a, b, (((1,), (1,)), ((), ())), preferred_element_type=f32)


def _ada_kernel(c_ref, w_ref, b_ref, o_ref):
    ca = jax.nn.silu(c_ref[...]).astype(bf16)
    o_ref[...] = _dot(ca, w_ref[...].astype(bf16)) + b_ref[...]


def ada_mod(c, ada_w, ada_b):
    depth, d, n = ada_w.shape
    b = c.shape[0]
    return pl.pallas_call(
        _ada_kernel,
        grid=(depth, n // ADA_TN),
        in_specs=[
            pl.BlockSpec((b, d), lambda l, j: (0, 0)),
            pl.BlockSpec((None, d, ADA_TN), lambda l, j: (l, 0, j)),
            pl.BlockSpec((None, 1, ADA_TN), lambda l, j: (l, 0, j)),
        ],
        out_specs=pl.BlockSpec((None, b, ADA_TN), lambda l, j: (l, 0, j)),
        out_shape=jax.ShapeDtypeStruct((depth, b, n), f32),
        compiler_params=_cparams(("arbitrary", "arbitrary")),
        name="ada_mod",
    )(c, ada_w, ada_b.reshape(depth, 1, n))


def _hyb_in_kernel(x_ref, g_ref, sh_ref, sc_ref, w_ref, cw_ref, lng_ref, lnb_ref, sw_ref, sb_ref,
                   o_ref, xn_scr, a_scr, b_scr, carry_scr, *, tiles_per_seq):
    i = pl.program_id(0)
    j = pl.program_id(1)
    tm = x_ref.shape[0]

    @pl.when(j == 0)
    def _():
        xn_scr[...] = _norm_mod(x_ref[...], g_ref[...], sh_ref[...], sc_ref[...]).astype(bf16)

    @pl.when(jnp.logical_and(i == 0, j == 0))
    def _():
        carry_scr[...] = jnp.zeros_like(carry_scr)

    p = _dot(xn_scr[...], w_ref[...])

    @pl.when(j == 0)
    def _():
        a_scr[...] = p

    @pl.when(j == 1)
    def _():
        b_scr[...] = p

    @pl.when(j == 2)
    def _():
        z = b_scr[...] * p
        row = lax.broadcasted_iota(i32, z.shape, 0)
        first = (i % tiles_per_seq) == 0
        prev = jnp.where(first, 0.0, carry_scr[...])
        p1 = prev[7:8, :]
        p2 = prev[6:7, :]
        z1 = jnp.where(row == 0, p1, pltpu.roll(z, 1, 0))
        z2 = jnp.where(row == 0, p2, jnp.where(row == 1, p1, pltpu.roll(z, 2, 0)))
        cw = cw_ref[...]
        conv = cw[0:1, :] * z2 + cw[1:2, :] * z1 + cw[2:3, :] * z
        carry_scr[...] = z[tm - 8:, :]
        o_ref[:, :D_CONV] = (a_scr[...] * conv).astype(o_ref.dtype)

    @pl.when(j == 3)
    def _():
        a_scr[...] = _gelu(p)

    @pl.when(j == 4)
    def _():
        v = _gelu(p)
        mu = jnp.mean(v, axis=-1, keepdims=True)
        vc = v - mu
        var = jnp.mean(vc * vc, axis=-1, keepdims=True)
        v = (vc * lax.rsqrt(var + LN_EPS)) * lng_ref[...] + lnb_ref[...]
        vb = v.astype(bf16)
        nch = tm // CHUNK
        r = lax.broadcasted_iota(i32, (CHUNK, CHUNK), 0)
        c = lax.broadcasted_iota(i32, (CHUNK, CHUNK), 1)
        tril = r >= c
        for g in range(SGU_GROUPS):
            ws = jnp.where(tril, sw_ref[g], 0.0).astype(bf16)
            vg = jnp.concatenate(
                [vb[n * CHUNK:(n + 1) * CHUNK, g * SGU_HD:(g + 1) * SGU_HD] for n in range(nch)], axis=1)
            sg = _dot(ws, vg)
            bb = sb_ref[g]
            for n in range(nch):
                s = sg[:, n * SGU_HD:(n + 1) * SGU_HD] + bb
                u = a_scr[n * CHUNK:(n + 1) * CHUNK, g * SGU_HD:(g + 1) * SGU_HD]
                o_ref[n * CHUNK:(n + 1) * CHUNK, D_CONV + g * SGU_HD:D_CONV + (g + 1) * SGU_HD] = (
                    (u * s).astype(o_ref.dtype))


def hyb_in(x2, g, sh, sc, w_in_bf, conv_w, ln_g, ln_b, sgu_w, sgu_b, seq):
    t, d = x2.shape
    tps = seq // TM
    nj = HYB_IN // TN_IN
    sbb = jnp.broadcast_to(sgu_b[:, :, None], (SGU_GROUPS, CHUNK, SGU_HD))
    kern = functools.partial(_hyb_in_kernel, tiles_per_seq=tps)
    return pl.pallas_call(
        kern,
        grid=(t // TM, nj),
        in_specs=[
            pl.BlockSpec((TM, d), lambda i, j: (i, 0)),
            pl.BlockSpec((1, d), lambda i, j: (0, 0)),
            pl.BlockSpec((None, 1, d), lambda i, j: (i // tps, 0, 0)),
            pl.BlockSpec((None, 1, d), lambda i, j: (i // tps, 0, 0)),
            pl.BlockSpec((d, TN_IN), lambda i, j: (0, j)),
            pl.BlockSpec((CONV_WIDTH, D_CONV), lambda i, j: (0, 0)),
            pl.BlockSpec((1, D_SGU), lambda i, j: (0, 0)),
            pl.BlockSpec((1, D_SGU), lambda i, j: (0, 0)),
            pl.BlockSpec((SGU_GROUPS, CHUNK, CHUNK), lambda i, j: (0, 0, 0)),
            pl.BlockSpec((SGU_GROUPS, CHUNK, SGU_HD), lambda i, j: (0, 0, 0)),
        ],
        out_specs=pl.BlockSpec((TM, D_CONV + D_SGU), lambda i, j: (i, 0)),
        out_shape=jax.ShapeDtypeStruct((t, D_CONV + D_SGU), bf16),
        scratch_shapes=[
            pltpu.VMEM((TM, d), bf16),
            pltpu.VMEM((TM, TN_IN), f32),
            pltpu.VMEM((TM, TN_IN), f32),
            pltpu.VMEM((8, D_CONV), f32),
        ],
        compiler_params=_cparams(("arbitrary", "arbitrary")),
        name="hyb_in",
    )(x2, g.reshape(1, d), sh, sc, w_in_bf, conv_w, ln_g.reshape(1, -1), ln_b.reshape(1, -1), sgu_w, sbb)


def _qkv_kernel(x_ref, g_ref, sh_ref, sc_ref, w_ref, o_ref, xn_scr):
    @pl.when(pl.program_id(1) == 0)
    def _():
        xn_scr[...] = _norm_mod(x_ref[...], g_ref[...], sh_ref[...], sc_ref[...]).astype(bf16)

    o_ref[...] = _dot(xn_scr[...], w_ref[...]).astype(o_ref.dtype)


def qkv_proj(x2, g, sh, sc, w_bf, seq, tn):
    t, d = x2.shape
    n = w_bf.shape[1]
    tps = seq // TM
    return pl.pallas_call(
        _qkv_kernel,
        grid=(t // TM, n // tn),
        in_specs=[
            pl.BlockSpec((TM, d), lambda i, j: (i, 0)),
            pl.BlockSpec((1, d), lambda i, j: (0, 0)),
            pl.BlockSpec((None, 1, d), lambda i, j: (i // tps, 0, 0)),
            pl.BlockSpec((None, 1, d), lambda i, j: (i // tps, 0, 0)),
            pl.BlockSpec((d, tn), lambda i, j: (0, j)),
        ],
        out_specs=pl.BlockSpec((TM, tn), lambda i, j: (i, j)),
        out_shape=jax.ShapeDtypeStruct((t, n), bf16),
        scratch_shapes=[pltpu.VMEM((TM, d), bf16)],
        compiler_params=_cparams(("arbitrary", "arbitrary")),
        name="qkv_proj",
    )(x2, g.reshape(1, d), sh, sc, w_bf)


QK_COLS = N_KV_HEADS * 2 * HEAD_DIM


def _attn_kernel(q_ref, kp_ref, kc_ref, vp_ref, vc_ref, bias_ref, sink_ref, o_ref):
    lane = lax.broadcasted_iota(i32, (2 * ATT_BLOCK, 2 * HEAD_DIM), 1)
    low = lane < HEAD_DIM
    zero = jnp.zeros((), bf16)
    scale = jnp.asarray(HEAD_DIM ** -0.5, bf16)
    ones = jnp.ones((2 * ATT_BLOCK, 2 * HEAD_DIM), bf16)
    for kh in range(N_KV_HEADS):
        cs = slice(kh * 2 * HEAD_DIM, (kh + 1) * 2 * HEAD_DIM)
        kk = jnp.concatenate([kp_ref[:, cs], kc_ref[:, cs]], axis=0)
        vv = jnp.concatenate([vp_ref[:, cs], vc_ref[:, cs]], axis=0)
        kz = (jnp.where(low, kk, zero), jnp.where(low, zero, kk))
        vz = (jnp.where(low, vv, zero), jnp.where(low, zero, vv))
        q0 = kh * GQA_GROUP * HEAD_DIM
        qs = jnp.concatenate(
            [q_ref[:, q0 + pr * 2 * HEAD_DIM:q0 + (pr + 1) * 2 * HEAD_DIM] for pr in range(ATT_PAIRS)],
            axis=0) * scale
        acc = None
        for par in range(2):
            s = _dot_nt(qs, kz[par]) + bias_ref[kh, par]
            sk = jnp.concatenate(
                [jnp.full((ATT_BLOCK, ATT_BLOCK), sink_ref[kh * GQA_GROUP + 2 * pr + par], f32)
                 for pr in range(ATT_PAIRS)], axis=0)
            rm = jnp.max(s, axis=-1, keepdims=True)
            mb = jnp.maximum(jnp.broadcast_to(rm, sk.shape), sk)
            p = jnp.concatenate([jnp.exp(s[:, :ATT_BLOCK] - mb), jnp.exp(s[:, ATT_BLOCK:] - mb)], axis=1)
            ov = _dot(p.astype(bf16), jnp.concatenate([vz[par], ones], axis=1))
            den = ov[:, 2 * HEAD_DIM:] + jnp.exp(sk - mb)
            o = ov[:, :2 * HEAD_DIM] / den
            acc = o if acc is None else acc + o
        for pr in range(ATT_PAIRS):
            o_ref[:, q0 + pr * 2 * HEAD_DIM:q0 + (pr + 1) * 2 * HEAD_DIM] = (
                acc[pr * ATT_BLOCK:(pr + 1) * ATT_BLOCK, :].astype(o_ref.dtype))


ATT_PAIRS = GQA_GROUP // 2


def swa_attn(qkv, bias_m, sinks, batch, seq):
    t = qkv.shape[0]
    nb = seq // ATT_BLOCK
    dq = N_Q_HEADS * HEAD_DIM
    kcol = dq // QK_COLS
    vcol = kcol + 1

    def prev(b, i):
        return b * nb + jnp.maximum(i - 1, 0)

    return pl.pallas_call(
        _attn_kernel,
        grid=(batch, nb),
        in_specs=[
            pl.BlockSpec((ATT_BLOCK, dq), lambda b, i: (b * nb + i, 0)),
            pl.BlockSpec((ATT_BLOCK, QK_COLS), lambda b, i: (prev(b, i), kcol)),
            pl.BlockSpec((ATT_BLOCK, QK_COLS), lambda b, i: (b * nb + i, kcol)),
            pl.BlockSpec((ATT_BLOCK, QK_COLS), lambda b, i: (prev(b, i), vcol)),
            pl.BlockSpec((ATT_BLOCK, QK_COLS), lambda b, i: (b * nb + i, vcol)),
            pl.BlockSpec((None, N_KV_HEADS, 2, ATT_PAIRS * ATT_BLOCK, 2 * ATT_BLOCK),
                         lambda b, i: (jnp.where(i == 0, 1, 0), 0, 0, 0, 0)),
            pl.BlockSpec(memory_space=pltpu.SMEM),
        ],
        out_specs=pl.BlockSpec((ATT_BLOCK, dq), lambda b, i: (b * nb + i, 0)),
        out_shape=jax.ShapeDtypeStruct((t, dq), bf16),
        compiler_params=_cparams(("arbitrary", "arbitrary")),
        name="swa_attn",
    )(qkv, qkv, qkv, qkv, qkv, bias_m, sinks)


def _t5_bucket(dist):
    n = np.maximum(dist, 0)
    max_exact = REL_BUCKETS // 2
    large = max_exact + (np.log(np.maximum(n, 1) / max_exact) / np.log(REL_MAX_DIST / max_exact)
                         * (REL_BUCKETS - max_exact)).astype(np.int32)
    large = np.minimum(large, REL_BUCKETS - 1)
    return np.where(n < max_exact, n, large).astype(np.int32)


def _by_kv_parity(a):
    rest = a.shape[2:]
    a = a.reshape((N_KV_HEADS, ATT_PAIRS, 2, ATT_BLOCK) + rest)
    a = jnp.moveaxis(a, 2, 1)
    return a.reshape((N_KV_HEADS, 2, ATT_PAIRS * ATT_BLOCK) + rest)


def _attn_tables(rel_bias):
    t_loc = np.arange(ATT_BLOCK)[:, None]
    j_loc = np.arange(2 * ATT_BLOCK)[None, :]
    dist = ATT_BLOCK + t_loc - j_loc
    band = (dist >= 0) & (dist < WINDOW)
    onehot = jnp.asarray(np.eye(REL_BUCKETS, dtype=np.float32)[_t5_bucket(dist)])
    bias = jnp.einsum('tjb,bh->htj', onehot, rel_bias.astype(f32), precision=lax.Precision.HIGHEST)
    first = band & (j_loc >= ATT_BLOCK)
    tabs = [_by_kv_parity(jnp.where(jnp.asarray(mask)[None], bias, -jnp.inf)) for mask in (band, first)]
    return jnp.stack(tabs)


def _route_kernel(a_ref, w_ref, gate_ref, x_ref, g_ref, sh_ref, sc_ref, rw_ref, rb_ref,
                  xnew_ref, xp_ref, eidx_ref, rk_ref, gw_ref, cnt_ref, sel_scr, grp_scr, carry_scr):
    tm = x_ref.shape[0]
    ne = N_EXPERTS

    @pl.when(pl.program_id(0) == 0)
    def _():
        carry_scr[...] = jnp.zeros_like(carry_scr)

    x = x_ref[...] + gate_ref[...] * _dot(a_ref[...], w_ref[...])
    xnew_ref[...] = x
    xn = _norm_mod(x, g_ref[...], sh_ref[...], sc_ref[...])
    _store_row_tiles(xp_ref, _pack_halves(xn))
    x_hi = xn.astype(bf16)
    x_lo = (xn - x_hi.astype(f32)).astype(bf16)
    rw = rw_ref[...]
    a = _dot_nt(rw, x_hi)
    b = _dot_nt(rw[:ne], x_lo)
    logits = a[:ne] + a[ne:] + b
    scores = jax.nn.sigmoid(logits)
    sel = scores + rb_ref[...]

    for g in range(N_GROUPS):
        tile = sel[g * E_PER_G:(g + 1) * E_PER_G, :]
        t1 = jnp.max(tile, axis=0, keepdims=True)
        dup = jnp.sum(jnp.where(tile == t1, 1.0, 0.0), axis=0, keepdims=True) >= 2.0
        t2 = jnp.max(jnp.where(tile < t1, tile, -jnp.inf), axis=0, keepdims=True)
        grp_scr[g:g + 1, :] = t1 + jnp.where(dup, t1, t2)
    gs = grp_scr[...]

    gi = lax.broadcasted_iota(i32, (N_GROUPS, tm), 0)
    grank = jnp.zeros((N_GROUPS, tm), i32)
    for g in range(N_GROUPS):
        r = grp_scr[g:g + 1, :]
        ge = jnp.where(r >= gs, 1, 0)
        gt = jnp.where(r > gs, 1, 0)
        grank = grank + jnp.where(gi > g, ge, gt)
    grp_scr[...] = jnp.where(grank < TOPK_GROUPS, 1.0, 0.0)

    masked = jnp.concatenate(
        [jnp.where(grp_scr[g:g + 1, :] > 0.5, sel[g * E_PER_G:(g + 1) * E_PER_G, :], -jnp.inf)
         for g in range(N_GROUPS)], axis=0)

    sel_scr[...] = masked
    ei = lax.broadcasted_iota(i32, (ne, tm), 0)
    rank = jnp.zeros((ne, tm), i32)
    for e in range(ne):
        r = sel_scr[e:e + 1, :]
        ge = jnp.where(r >= masked, 1, 0)
        gt = jnp.where(r > masked, 1, 0)
        rank = rank + jnp.where(ei > e, ge, gt)
    chosen = rank < TOP_K
    chf = jnp.where(chosen, 1.0, 0.0)

    wun = jnp.where(chosen, scores, 0.0)
    gwt = wun / jnp.sum(wun, axis=0, keepdims=True) * ROUTED_SCALE

    rr = lax.broadcasted_iota(i32, (tm, tm), 0)
    cc = lax.broadcasted_iota(i32, (tm, tm), 1)
    upper = jnp.where(rr < cc, 1.0, 0.0).astype(bf16)
    chb = chf.astype(bf16)
    pos = carry_scr[:, 0:1] + _dot(chb, upper)
    carry_scr[...] = carry_scr[...] + jnp.sum(chf, axis=1, keepdims=True)
    cnt_ref[...] = carry_scr[...].astype(i32)

    er = lax.broadcasted_iota(i32, (ne, ne), 0)
    ec = lax.broadcasted_iota(i32, (ne, ne), 1)
    lower = jnp.where(ec < er, 1.0, 0.0).astype(bf16)
    below = _dot(lower, chb)
    slot = jnp.where(chosen, below, -1.0)
    eif = ei.astype(f32)
    for k in range(TOP_K):
        mk = slot == float(k)
        eidx_ref[k:k + 1, :] = jnp.sum(jnp.where(mk, eif, 0.0), axis=0, keepdims=True).astype(i32)
        rk_ref[k:k + 1, :] = jnp.sum(jnp.where(mk, pos, 0.0), axis=0, keepdims=True).astype(i32)
        gw_ref[k:k + 1, :] = jnp.sum(jnp.where(mk, gwt, 0.0), axis=0, keepdims=True)


def proj_route(a, w_bf, x2, gate, g, sh, sc, rw_cat, rbias, seq):
    t, d = x2.shape
    k = a.shape[1]
    tps = seq // TR
    return pl.pallas_call(
        _route_kernel,
        grid=(t // TR,),
        in_specs=[
            pl.BlockSpec((TR, k), lambda i: (i, 0)),
            pl.BlockSpec((k, d), lambda i: (0, 0)),
            pl.BlockSpec((None, 1, d), lambda i: (i // tps, 0, 0)),
            pl.BlockSpec((TR, d), lambda i: (i, 0)),
            pl.BlockSpec((1, d), lambda i: (0, 0)),
            pl.BlockSpec((None, 1, d), lambda i: (i // tps, 0, 0)),
            pl.BlockSpec((None, 1, d), lambda i: (i // tps, 0, 0)),
            pl.BlockSpec((2 * N_EXPERTS, d), lambda i: (0, 0)),
            pl.BlockSpec((N_EXPERTS, 1), lambda i: (0, 0)),
        ],
        out_specs=[
            pl.BlockSpec((TR, d), lambda i: (i, 0)),
            pl.BlockSpec((TR * ROW_SUB, LANES), lambda i: (i, 0)),
            pl.BlockSpec((TOP_K, TR), lambda i: (0, i)),
            pl.BlockSpec((TOP_K, TR), lambda i: (0, i)),
            pl.BlockSpec((TOP_K, TR), lambda i: (0, i)),
            pl.BlockSpec((N_EXPERTS, LANES), lambda i: (0, 0)),
        ],
        out_shape=[
            jax.ShapeDtypeStruct((t, d), f32),
            jax.ShapeDtypeStruct((t * ROW_SUB, LANES), u32),
            jax.ShapeDtypeStruct((TOP_K, t), i32),
            jax.ShapeDtypeStruct((TOP_K, t), i32),
            jax.ShapeDtypeStruct((TOP_K, t), f32),
            jax.ShapeDtypeStruct((N_EXPERTS, LANES), i32),
        ],
        scratch_shapes=[
            pltpu.VMEM((N_EXPERTS, TR), f32),
            pltpu.VMEM((N_GROUPS, TR), f32),
            pltpu.VMEM((N_EXPERTS, LANES), f32),
        ],
        compiler_params=_cparams(("arbitrary",)),
        name="proj_route",
    )(a, w_bf, gate, x2, g.reshape(1, d), sh, sc, rw_cat, rbias.reshape(N_EXPERTS, 1))


def _tile_rows(ref, r, n=1):
    return ref.at[pl.ds(pl.multiple_of(r * ROW_SUB, ROW_SUB), n * ROW_SUB), :]


def _row_copy(src, s, dst, d, sem):
    return pltpu.make_async_copy(_tile_rows(src, s), _tile_rows(dst, d), sem)


def _dispatch_kernel(ztail_ref, dest_ref, xp_ref, sg_ref, su_ref, sd_ref, xs_ref, ysh_ref, zeros_scr, sem, zsem):
    td = xp_ref.shape[0] // ROW_SUB

    @pl.when(pl.program_id(0) == 0)
    def _():
        zeros_scr[...] = jnp.zeros_like(zeros_scr)

        def zcopy(e):
            return pltpu.make_async_copy(zeros_scr, _tile_rows(xs_ref, ztail_ref[e], BM), zsem)

        def zstart(e, c):
            zcopy(e).start()
            return c

        def zwait(e, c):
            zcopy(e).wait()
            return c

        lax.fori_loop(0, N_EXPERTS, zstart, 0)
        lax.fori_loop(0, N_EXPERTS, zwait, 0)

    for t in range(td):
        for k in range(TOP_K):
            _row_copy(xp_ref, t, xs_ref, dest_ref[k, t], sem).start(priority=k % DMA_PRIORITIES)
    ysh_ref[...] = _ffn_packed(xp_ref, td, sg_ref, su_ref, sd_ref)
    for t in range(td):
        for k in range(TOP_K):
            _row_copy(xp_ref, t, xs_ref, dest_ref[k, t], sem).wait()


def moe_dispatch(xp, dest, ztail, nrows, sg_bf, su_bf, sd_bf):
    t = xp.shape[0] // ROW_SUB
    d = sg_bf.shape[0]
    grid_spec = pltpu.PrefetchScalarGridSpec(
        num_scalar_prefetch=1,
        grid=(t // TD,),
        in_specs=[
            pl.BlockSpec((TOP_K, TD), lambda i, z: (0, i), memory_space=pltpu.SMEM),
            pl.BlockSpec((TD * ROW_SUB, LANES), lambda i, z: (i, 0)),
            pl.BlockSpec((d, D_SHARED), lambda i, z: (0, 0)),
            pl.BlockSpec((d, D_SHARED), lambda i, z: (0, 0)),
            pl.BlockSpec((D_SHARED, d), lambda i, z: (0, 0)),
        ],
        out_specs=[pl.BlockSpec(memory_space=pl.ANY), pl.BlockSpec((TD, d), lambda i, z: (i, 0))],
        scratch_shapes=[
            pltpu.VMEM((BM * ROW_SUB, LANES), u32),
            pltpu.SemaphoreType.DMA(()),
            pltpu.SemaphoreType.DMA(()),
        ],
    )
    return pl.pallas_call(
        _dispatch_kernel,
        grid_spec=grid_spec,
        out_shape=[jax.ShapeDtypeStruct((nrows * ROW_SUB, LANES), u32), jax.ShapeDtypeStruct((t, d), f32)],
        compiler_params=_cparams(("arbitrary",)),
        name="moe_dispatch",
    )(ztail, dest, xp, sg_bf, su_bf, sd_bf)


def _ffn_packed(x_ref, m, wg, wu, wd):
    los, his = _load_row_tiles(x_ref, m)
    xa = jnp.concatenate([p.astype(bf16) for p in los], axis=1)
    xb = jnp.concatenate([p.astype(bf16) for p in his], axis=1)
    hg = _dot(xa, wg[:HALF, :]) + _dot(xb, wg[HALF:, :])
    hu = _dot(xa, wu[:HALF, :]) + _dot(xb, wu[HALF:, :])
    a = (jax.nn.silu(hg) * hu).astype(bf16)
    return _dot(a, wd[...])


def _gmm_kernel(be_ref, bf_ref, nu_ref, xs_ref, wg_ref, wu_ref, wd_ref, ys_ref, wg_s, wu_s, wd_s):
    i = pl.program_id(0)

    @pl.when(bf_ref[i] == 1)
    def _():
        wg_s[...] = wg_ref[...].astype(bf16)
        wu_s[...] = wu_ref[...].astype(bf16)
        wd_s[...] = wd_ref[...].astype(bf16)

    @pl.when(i < nu_ref[0])
    def _():
        _store_row_tiles(ys_ref, _pack_halves(_ffn_packed(xs_ref, BM, wg_s, wu_s, wd_s)))

    @pl.when(i >= nu_ref[0])
    def _():
        ys_ref[...] = jnp.zeros_like(ys_ref)


def moe_gmm(xs, blk_e, blk_first, n_used, w_gate, w_up, w_down, layer, nblk):
    d, de = w_gate.shape[2], w_gate.shape[3]
    grid_spec = pltpu.PrefetchScalarGridSpec(
        num_scalar_prefetch=3,
        grid=(nblk,),
        in_specs=[
            pl.BlockSpec((BM * ROW_SUB, LANES), lambda i, be, bf, nu: (jnp.minimum(i, nu[0] - 1), 0)),
            pl.BlockSpec((None, None, d, de), lambda i, be, bf, nu: (layer, be[i], 0, 0)),
            pl.BlockSpec((None, None, d, de), lambda i, be, bf, nu: (layer, be[i], 0, 0)),
            pl.BlockSpec((None, None, de, d), lambda i, be, bf, nu: (layer, be[i], 0, 0)),
        ],
        out_specs=pl.BlockSpec((BM * ROW_SUB, LANES), lambda i, be, bf, nu: (i, 0)),
        scratch_shapes=[
            pltpu.VMEM((d, de), bf16),
            pltpu.VMEM((d, de), bf16),
            pltpu.VMEM((de, d), bf16),
        ],
    )
    return pl.pallas_call(
        _gmm_kernel,
        grid_spec=grid_spec,
        out_shape=jax.ShapeDtypeStruct((nblk * BM * ROW_SUB, LANES), u32),
        compiler_params=_cparams(("arbitrary",)),
        name="moe_gmm",
    )(blk_e, blk_first, n_used, xs, w_gate, w_up, w_down)


def _combine_kernel(dest_ref, dnext_ref, x_ref, ysh_ref, gw_ref, gate_ref, *rest, tail):
    if tail == 'qkv':
        ng_ref, nsh_ref, nsc_ref, wq_ref, ys_ref, o_ref, q_ref, gath_a, gath_b, sem, xn_scr = rest
    elif tail == 'final':
        ng_ref, ys_ref, o_ref, gath_a, gath_b, sem = rest
    else:
        ys_ref, o_ref, gath_a, gath_b, sem = rest
    td = x_ref.shape[0]
    i = pl.program_id(0)

    def copy(dref, buf, s, t, k):
        return _row_copy(ys_ref, dref[k, t], buf.at[k], t, sem.at[s])

    def wait_tile(dref, buf, s):
        for t in range(td):
            for k in range(TOP_K):
                copy(dref, buf, s, t, k).wait()

    @pl.when(i == 0)
    def _():
        def body(t, c):
            for k in range(TOP_K):
                copy(dest_ref, gath_a, 0, t, k).start(priority=k % DMA_PRIORITIES)
            return c
        lax.fori_loop(0, td, body, 0)

    def step(cur_buf, cur_s, nxt_buf, nxt_s):
        wait_tile(dest_ref, cur_buf, cur_s)
        for t in range(td):
            for k in range(TOP_K):
                copy(dnext_ref, nxt_buf, nxt_s, t, k).start(priority=k % DMA_PRIORITIES)
        for rb in range(td // COMB_RB):
            rows = slice(rb * COMB_RB, (rb + 1) * COMB_RB)
            ysh = ysh_ref[rows, :]
            acc_lo = [ysh[:, c * LANES:(c + 1) * LANES] for c in range(ROW_SUB)]
            acc_hi = [ysh[:, HALF + c * LANES:HALF + (c + 1) * LANES] for c in range(ROW_SUB)]
            gw = gw_ref[rows, :]
            for k in range(TOP_K):
                wk = jnp.broadcast_to(gw[:, k:k + 1], (COMB_RB, LANES))
                for c in range(ROW_SUB):
                    word = cur_buf[k, pl.ds(rb * COMB_RB * ROW_SUB + c, COMB_RB, stride=ROW_SUB), :]
                    lo, hi = _unpack_halves(word)
                    acc_lo[c] = acc_lo[c] + wk * lo
                    acc_hi[c] = acc_hi[c] + wk * hi
            y = jnp.concatenate(acc_lo + acc_hi, axis=1)
            out = x_ref[rows, :] + gate_ref[...] * y
            if tail == 'final':
                ms = jnp.mean(out * out, axis=-1, keepdims=True)
                out = (out * lax.rsqrt(ms + RMS_EPS)) * ng_ref[...]
            o_ref[rows, :] = out
            if tail == 'qkv':
                xn_scr[rows, :] = _norm_mod(out, ng_ref[...], nsh_ref[...], nsc_ref[...]).astype(bf16)
        if tail == 'qkv':
            q_ref[...] = _dot(xn_scr[...], wq_ref[...]).astype(q_ref.dtype)

        @pl.when(i == pl.num_programs(0) - 1)
        def _():
            wait_tile(dnext_ref, nxt_buf, nxt_s)

    @pl.when(i % 2 == 0)
    def _():
        step(gath_a, 0, gath_b, 1)

    @pl.when(i % 2 == 1)
    def _():
        step(gath_b, 1, gath_a, 0)


def moe_combine(x2, ysh, ys, dest, gw_t, gate, seq, final_g=None, next_qkv=None):
    t, d = x2.shape
    tps = seq // TD
    last = t // TD - 1
    row = lambda i: (i, 0)
    per_batch = pl.BlockSpec((None, 1, d), lambda i: (i // tps, 0, 0))
    in_specs = [
        pl.BlockSpec((TOP_K, TD), lambda i: (0, i), memory_space=pltpu.SMEM),
        pl.BlockSpec((TOP_K, TD), lambda i: (0, jnp.minimum(i + 1, last)), memory_space=pltpu.SMEM),
        pl.BlockSpec((TD, d), row),
        pl.BlockSpec((TD, d), row),
        pl.BlockSpec((TD, TOP_K), row),
        per_batch,
    ]
    args = [dest, dest, x2, ysh, gw_t, gate]
    out_specs = [pl.BlockSpec((TD, d), row)]
    out_shape = [jax.ShapeDtypeStruct((t, d), f32)]
    scratch = [
        pltpu.VMEM((TOP_K, TD * ROW_SUB, LANES), u32),
        pltpu.VMEM((TOP_K, TD * ROW_SUB, LANES), u32),
        pltpu.SemaphoreType.DMA((2,)),
    ]
    tail = None
    if final_g is not None:
        tail = 'final'
        in_specs.append(pl.BlockSpec((1, d), lambda i: (0, 0)))
        args.append(final_g.reshape(1, d))
    elif next_qkv is not None:
        tail = 'qkv'
        g, sh, sc, w_bf = next_qkv
        nq = w_bf.shape[1]
        in_specs += [pl.BlockSpec((1, d), lambda i: (0, 0)), per_batch, per_batch,
                     pl.BlockSpec((d, nq), lambda i: (0, 0))]
        args += [g.reshape(1, d), sh, sc, w_bf]
        out_specs.append(pl.BlockSpec((TD, nq), row))
        out_shape.append(jax.ShapeDtypeStruct((t, nq), bf16))
        scratch.append(pltpu.VMEM((TD, d), bf16))
    in_specs.append(pl.BlockSpec(memory_space=pl.ANY))
    args.append(ys)
    outs = pl.pallas_call(
        functools.partial(_combine_kernel, tail=tail),
        grid=(t // TD,),
        in_specs=in_specs,
        out_specs=out_specs,
        out_shape=out_shape,
        scratch_shapes=scratch,
        compiler_params=_cparams(("arbitrary",)),
        name="moe_combine",
    )(*args)
    return outs if tail == 'qkv' else outs[0]


SLOT_TN = 4096


def _slots_kernel(ps_ref, eidx_ref, rk_ref, o_ref):
    e = eidx_ref[...]
    start = jnp.zeros_like(e)
    for j in range(N_EXPERTS):
        start = jnp.where(e == j, ps_ref[j], start)
    o_ref[...] = start + rk_ref[...]


def moe_slots(pad_start, eidx, rk):
    t = eidx.shape[1]
    tn = min(t, SLOT_TN)
    spec = pl.BlockSpec((TOP_K, tn), lambda i, ps: (0, i))
    grid_spec = pltpu.PrefetchScalarGridSpec(
        num_scalar_prefetch=1, grid=(t // tn,), in_specs=[spec, spec], out_specs=spec)
    return pl.pallas_call(
        _slots_kernel,
        grid_spec=grid_spec,
        out_shape=jax.ShapeDtypeStruct((TOP_K, t), i32),
        compiler_params=_cparams(("arbitrary",)),
        name="moe_slots",
    )(pad_start, eidx, rk)


def moe_layer(mix, w_out_bf, x2, gate_m, g, sh, sc, gate, router_w, router_bias, w_gate, w_up, w_down, layer,
              sh_gate, sh_up, sh_down, seq, final_g=None, next_qkv=None):
    t = x2.shape[0]
    a = t * TOP_K
    nblk = a // BM + N_EXPERTS
    rw_t = router_w.T
    rw_hi = rw_t.astype(bf16)
    rw_lo = (rw_t - rw_hi.astype(f32)).astype(bf16)
    rw_cat = jnp.concatenate([rw_hi, rw_lo], axis=0)

    x2, xp, eidx, rk, gw, cnt = proj_route(mix, w_out_bf, x2, gate_m, g, sh, sc, rw_cat, router_bias, seq)

    counts = cnt[:, 0]
    padded = ((counts + BM - 1) // BM) * BM
    cum_pad = jnp.cumsum(padded)
    pad_start = cum_pad - padded
    n_used = (cum_pad[-1] // BM).astype(i32).reshape(1)
    blk_row = jnp.arange(nblk, dtype=i32) * BM
    blk_e = jnp.minimum(jnp.sum((cum_pad[None, :] <= blk_row[:, None]).astype(i32), axis=1), N_EXPERTS - 1)
    blk_first = jnp.concatenate([jnp.ones((1,), i32), (blk_e[1:] != blk_e[:-1]).astype(i32)])
    dest = moe_slots(pad_start.astype(i32), eidx, rk)
    ztail = jnp.maximum(cum_pad - BM, 0).astype(i32)

    xs, ysh = moe_dispatch(xp, dest, ztail, nblk * BM, sh_gate.astype(bf16), sh_up.astype(bf16),
                           sh_down.astype(bf16))
    ys = moe_gmm(xs, blk_e, blk_first, n_used, w_gate, w_up, w_down, layer, nblk)
    return moe_combine(x2, ysh, ys, dest, gw.T, gate, seq, final_g=final_g, next_qkv=next_qkv)


def _qkv_weight(w_qkv):
    dq = N_Q_HEADS * HEAD_DIM
    dkv = N_KV_HEADS * HEAD_DIM
    d = w_qkv.shape[0]
    wq = w_qkv[:, :dq]
    wk = w_qkv[:, dq:dq + dkv].reshape(d, N_KV_HEADS, 1, HEAD_DIM)
    wv = w_qkv[:, dq + dkv:].reshape(d, N_KV_HEADS, 1, HEAD_DIM)
    wk2 = jnp.broadcast_to(wk, (d, N_KV_HEADS, 2, HEAD_DIM)).reshape(d, QK_COLS)
    wv2 = jnp.broadcast_to(wv, (d, N_KV_HEADS, 2, HEAD_DIM)).reshape(d, QK_COLS)
    return jnp.concatenate([wq, wk2, wv2], axis=1).astype(bf16)


def kernel(x, c, ada_w, ada_b, norm_mix_g, norm_ffn_g, hyb_w_in, conv_w, sgu_ln_g, sgu_ln_b, sgu_w, sgu_b,
           hyb_w_out, attn_w_qkv, attn_sinks, attn_w_o, rel_bias, router_w, router_bias, exp_w_gate,
           exp_w_up, exp_w_down, sh_w_gate, sh_w_up, sh_w_down, final_g):
    batch, seq, d = x.shape
    t = batch * seq
    x2 = x.reshape(t, d)
    mod = ada_mod(c, ada_w, ada_b)
    mods = [[mod[l, :, k * d:(k + 1) * d].reshape(batch, 1, d) for k in range(6)] for l in range(DEPTH)]
    qkv = None
    for l in range(DEPTH):
        sh_m, sc_m, g_m, sh_f, sc_f, g_f = mods[l]
        i = l // 2
        if l % 2 == 0:
            mix = hyb_in(x2, norm_mix_g[l], sh_m, sc_m, hyb_w_in[i].astype(bf16), conv_w[i], sgu_ln_g[i],
                         sgu_ln_b[i], sgu_w[i], sgu_b[i], seq)
            w_out = hyb_w_out[i].astype(bf16)
        else:
            if qkv is None:
                qkv = qkv_proj(x2, norm_mix_g[l], sh_m, sc_m, _qkv_weight(attn_w_qkv[i]), seq, 1024)
            mix = swa_attn(qkv, _attn_tables(rel_bias), attn_sinks[i], batch, seq)
            w_out = attn_w_o[i].astype(bf16)
        tail = {}
        if l == DEPTH - 1:
            tail = dict(final_g=final_g)
        elif (l + 1) % 2 == 1:
            tail = dict(next_qkv=(norm_mix_g[l + 1], mods[l + 1][0], mods[l + 1][1],
                                  _qkv_weight(attn_w_qkv[(l + 1) // 2])))
        res = moe_layer(mix, w_out, x2, g_m, norm_ffn_g[l], sh_f, sc_f, g_f, router_w[l], router_bias[l],
                        exp_w_gate, exp_w_up, exp_w_down, l, sh_w_gate[l], sh_w_up[l], sh_w_down[l], seq, **tail)
        x2, qkv = res if 'next_qkv' in tail else (res, None)
    return x2.reshape(batch, seq, d)
```

```python
import functools

import numpy as np
import jax
import jax.numpy as jnp
from jax import lax
from jax.experimental import pallas as pl
from jax.experimental.pallas import tpu as pltpu

f32 = jnp.float32
bf16 = jnp.bfloat16
i32 = jnp.int32
u32 = jnp.uint32

D_MODEL = 2048
DEPTH = 2
RMS_EPS = 1e-6
LN_EPS = 1e-5
D_CONV = 1024
CONV_WIDTH = 3
D_SGU = 1024
SGU_GROUPS = 8
SGU_HD = D_SGU // SGU_GROUPS
CHUNK = 128
HYB_IN = 3 * D_CONV + 2 * D_SGU
HEAD_DIM = 64
N_Q_HEADS = D_MODEL // HEAD_DIM
N_KV_HEADS = N_Q_HEADS // 8
GQA_GROUP = N_Q_HEADS // N_KV_HEADS
WINDOW = 128
ATT_BLOCK = 128
REL_BUCKETS = 32
REL_MAX_DIST = 128
N_EXPERTS = 64
TOP_K = 8
N_GROUPS = 8
TOPK_GROUPS = 4
E_PER_G = N_EXPERTS // N_GROUPS
D_EXPERT = 512
D_SHARED = 512
ROUTED_SCALE = 2.5

LANES = 128
HALF = D_MODEL // 2
VMEM_LIMIT = 56 * 1024 * 1024
TM = 512
TN_IN = 1024
TR = 512
TD = 256
BM = 512
ADA_TN = 1024
DMA_PRIORITIES = 2
COMB_RB = 16


def _cparams(sem, vmem=VMEM_LIMIT):
    return pltpu.CompilerParams(dimension_semantics=sem, vmem_limit_bytes=vmem)


def _norm_mod(x, g, sh, sc):
    ms = jnp.mean(x * x, axis=-1, keepdims=True)
    y = x * lax.rsqrt(ms + RMS_EPS)
    return (y * g) * (1.0 + sc) + sh


def _pack_halves(y):
    h = y.shape[1] // 2
    lo = lax.bitcast_convert_type(y[:, :h].astype(bf16).astype(f32), u32)
    hi = lax.bitcast_convert_type(y[:, h:].astype(bf16).astype(f32), u32)
    return (lo >> 16) | (hi & jnp.uint32(0xFFFF0000))


def _unpack_halves(w):
    lo = lax.bitcast_convert_type(w << 16, f32)
    hi = lax.bitcast_convert_type(w & jnp.uint32(0xFFFF0000), f32)
    return lo, hi


ROW_SUB = HALF // LANES


def _store_row_tiles(ref, packed):
    n = packed.shape[0]
    for c in range(ROW_SUB):
        ref[pl.ds(c, n, stride=ROW_SUB), :] = packed[:, c * LANES:(c + 1) * LANES]


def _load_row_tiles(ref, n):
    los, his = [], []
    for c in range(ROW_SUB):
        lo, hi = _unpack_halves(ref[pl.ds(c, n, stride=ROW_SUB), :])
        los.append(lo)
        his.append(hi)
    return los, his


def _gelu(x):
    return 0.5 * x * (1.0 + lax.erf(x * np.float32(np.sqrt(0.5))))


def _dot(a, b):
    return jnp.dot(a, b, preferred_element_type=f32)


def _dot_nt(a, b):
    return lax.dot_general(a, b, (((1,), (1,)), ((), ())), preferred_element_type=f32)


def _ada_kernel(c_ref, w_ref, b_ref, o_ref):
    ca = jax.nn.silu(c_ref[...]).astype(bf16)
    o_ref[...] = _dot(ca, w_ref[...].astype(bf16)) + b_ref[...]


def ada_mod(c, ada_w, ada_b):
    depth, d, n = ada_w.shape
    b = c.shape[0]
    return pl.pallas_call(
        _ada_kernel,
        grid=(depth, n // ADA_TN),
        in_specs=[
            pl.BlockSpec((b, d), lambda l, j: (0, 0)),
            pl.BlockSpec((None, d, ADA_TN), lambda l, j: (l, 0, j)),
            pl.BlockSpec((None, 1, ADA_TN), lambda l, j: (l, 0, j)),
        ],
        out_specs=pl.BlockSpec((None, b, ADA_TN), lambda l, j: (l, 0, j)),
        out_shape=jax.ShapeDtypeStruct((depth, b, n), f32),
        compiler_params=_cparams(("arbitrary", "arbitrary")),
        name="ada_mod",
    )(c, ada_w, ada_b.reshape(depth, 1, n))


def _hyb_in_kernel(x_ref, g_ref, sh_ref, sc_ref, w_ref, cw_ref, lng_ref, lnb_ref, sw_ref, sb_ref,
                   o_ref, xn_scr, a_scr, b_scr, carry_scr, *, tiles_per_seq):
    i = pl.program_id(0)
    j = pl.program_id(1)
    tm = x_ref.shape[0]

    @pl.when(j == 0)
    def _():
        xn_scr[...] = _norm_mod(x_ref[...], g_ref[...], sh_ref[...], sc_ref[...]).astype(bf16)

    @pl.when(jnp.logical_and(i == 0, j == 0))
    def _():
        carry_scr[...] = jnp.zeros_like(carry_scr)

    p = _dot(xn_scr[...], w_ref[...])

    @pl.when(j == 0)
    def _():
        a_scr[...] = p

    @pl.when(j == 1)
    def _():
        b_scr[...] = p

    @pl.when(j == 2)
    def _():
        z = b_scr[...] * p
        row = lax.broadcasted_iota(i32, z.shape, 0)
        first = (i % tiles_per_seq) == 0
        prev = jnp.where(first, 0.0, carry_scr[...])
        p1 = prev[7:8, :]
        p2 = prev[6:7, :]
        z1 = jnp.where(row == 0, p1, pltpu.roll(z, 1, 0))
        z2 = jnp.where(row == 0, p2, jnp.where(row == 1, p1, pltpu.roll(z, 2, 0)))
        cw = cw_ref[...]
        conv = cw[0:1, :] * z2 + cw[1:2, :] * z1 + cw[2:3, :] * z
        carry_scr[...] = z[tm - 8:, :]
        o_ref[:, :D_CONV] = (a_scr[...] * conv).astype(o_ref.dtype)

    @pl.when(j == 3)
    def _():
        a_scr[...] = _gelu(p)

    @pl.when(j == 4)
    def _():
        v = _gelu(p)
        mu = jnp.mean(v, axis=-1, keepdims=True)
        vc = v - mu
        var = jnp.mean(vc * vc, axis=-1, keepdims=True)
        v = (vc * lax.rsqrt(var + LN_EPS)) * lng_ref[...] + lnb_ref[...]
        vb = v.astype(bf16)
        nch = tm // CHUNK
        r = lax.broadcasted_iota(i32, (CHUNK, CHUNK), 0)
        c = lax.broadcasted_iota(i32, (CHUNK, CHUNK), 1)
        tril = r >= c
        for g in range(SGU_GROUPS):
            ws = jnp.where(tril, sw_ref[g], 0.0).astype(bf16)
            vg = jnp.concatenate(
                [vb[n * CHUNK:(n + 1) * CHUNK, g * SGU_HD:(g + 1) * SGU_HD] for n in range(nch)], axis=1)
            sg = _dot(ws, vg)
            bb = sb_ref[g]
            for n in range(nch):
                s = sg[:, n * SGU_HD:(n + 1) * SGU_HD] + bb
                u = a_scr[n * CHUNK:(n + 1) * CHUNK, g * SGU_HD:(g + 1) * SGU_HD]
                o_ref[n * CHUNK:(n + 1) * CHUNK, D_CONV + g * SGU_HD:D_CONV + (g + 1) * SGU_HD] = (
                    (u * s).astype(o_ref.dtype))


def hyb_in(x2, g, sh, sc, w_in_bf, conv_w, ln_g, ln_b, sgu_w, sgu_b, seq):
    t, d = x2.shape
    tps = seq // TM
    nj = HYB_IN // TN_IN
    sbb = jnp.broadcast_to(sgu_b[:, :, None], (SGU_GROUPS, CHUNK, SGU_HD))
    kern = functools.partial(_hyb_in_kernel, tiles_per_seq=tps)
    return pl.pallas_call(
        kern,
        grid=(t // TM, nj),
        in_specs=[
            pl.BlockSpec((TM, d), lambda i, j: (i, 0)),
            pl.BlockSpec((1, d), lambda i, j: (0, 0)),
            pl.BlockSpec((None, 1, d), lambda i, j: (i // tps, 0, 0)),
            pl.BlockSpec((None, 1, d), lambda i, j: (i // tps, 0, 0)),
            pl.BlockSpec((d, TN_IN), lambda i, j: (0, j)),
            pl.BlockSpec((CONV_WIDTH, D_CONV), lambda i, j: (0, 0)),
            pl.BlockSpec((1, D_SGU), lambda i, j: (0, 0)),
            pl.BlockSpec((1, D_SGU), lambda i, j: (0, 0)),
            pl.BlockSpec((SGU_GROUPS, CHUNK, CHUNK), lambda i, j: (0, 0, 0)),
            pl.BlockSpec((SGU_GROUPS, CHUNK, SGU_HD), lambda i, j: (0, 0, 0)),
        ],
        out_specs=pl.BlockSpec((TM, D_CONV + D_SGU), lambda i, j: (i, 0)),
        out_shape=jax.ShapeDtypeStruct((t, D_CONV + D_SGU), bf16),
        scratch_shapes=[
            pltpu.VMEM((TM, d), bf16),
            pltpu.VMEM((TM, TN_IN), f32),
            pltpu.VMEM((TM, TN_IN), f32),
            pltpu.VMEM((8, D_CONV), f32),
        ],
        compiler_params=_cparams(("arbitrary", "arbitrary")),
        name="hyb_in",
    )(x2, g.reshape(1, d), sh, sc, w_in_bf, conv_w, ln_g.reshape(1, -1), ln_b.reshape(1, -1), sgu_w, sbb)


def _qkv_kernel(x_ref, g_ref, sh_ref, sc_ref, w_ref, o_ref, xn_scr):
    @pl.when(pl.program_id(1) == 0)
    def _():
        xn_scr[...] = _norm_mod(x_ref[...], g_ref[...], sh_ref[...], sc_ref[...]).astype(bf16)

    o_ref[...] = _dot(xn_scr[...], w_ref[...]).astype(o_ref.dtype)


def qkv_proj(x2, g, sh, sc, w_bf, seq, tn):
    t, d = x2.shape
    n = w_bf.shape[1]
    tps = seq // TM
    return pl.pallas_call(
        _qkv_kernel,
        grid=(t // TM, n // tn),
        in_specs=[
            pl.BlockSpec((TM, d), lambda i, j: (i, 0)),
            pl.BlockSpec((1, d), lambda i, j: (0, 0)),
            pl.BlockSpec((None, 1, d), lambda i, j: (i // tps, 0, 0)),
            pl.BlockSpec((None, 1, d), lambda i, j: (i // tps, 0, 0)),
            pl.BlockSpec((d, tn), lambda i, j: (0, j)),
        ],
        out_specs=pl.BlockSpec((TM, tn), lambda i, j: (i, j)),
        out_shape=jax.ShapeDtypeStruct((t, n), bf16),
        scratch_shapes=[pltpu.VMEM((TM, d), bf16)],
        compiler_params=_cparams(("arbitrary", "arbitrary")),
        name="qkv_proj",
    )(x2, g.reshape(1, d), sh, sc, w_bf)


QK_COLS = N_KV_HEADS * 2 * HEAD_DIM


def _attn_kernel(q_ref, kp_ref, kc_ref, vp_ref, vc_ref, bias_ref, sink_ref, o_ref):
    lane = lax.broadcasted_iota(i32, (2 * ATT_BLOCK, 2 * HEAD_DIM), 1)
    low = lane < HEAD_DIM
    zero = jnp.zeros((), bf16)
    scale = jnp.asarray(HEAD_DIM ** -0.5, bf16)
    ones = jnp.ones((2 * ATT_BLOCK, 2 * HEAD_DIM), bf16)
    for kh in range(N_KV_HEADS):
        cs = slice(kh * 2 * HEAD_DIM, (kh + 1) * 2 * HEAD_DIM)
        kk = jnp.concatenate([kp_ref[:, cs], kc_ref[:, cs]], axis=0)
        vv = jnp.concatenate([vp_ref[:, cs], vc_ref[:, cs]], axis=0)
        kz = (jnp.where(low, kk, zero), jnp.where(low, zero, kk))
        vz = (jnp.where(low, vv, zero), jnp.where(low, zero, vv))
        q0 = kh * GQA_GROUP * HEAD_DIM
        qs = jnp.concatenate(
            [q_ref[:, q0 + pr * 2 * HEAD_DIM:q0 + (pr + 1) * 2 * HEAD_DIM] for pr in range(ATT_PAIRS)],
            axis=0) * scale
        acc = None
        for par in range(2):
            s = _dot_nt(qs, kz[par]) + bias_ref[kh, par]
            sk = jnp.concatenate(
                [jnp.full((ATT_BLOCK, ATT_BLOCK), sink_ref[kh * GQA_GROUP + 2 * pr + par], f32)
                 for pr in range(ATT_PAIRS)], axis=0)
            rm = jnp.max(s, axis=-1, keepdims=True)
            mb = jnp.maximum(jnp.broadcast_to(rm, sk.shape), sk)
            p = jnp.concatenate([jnp.exp(s[:, :ATT_BLOCK] - mb), jnp.exp(s[:, ATT_BLOCK:] - mb)], axis=1)
            ov = _dot(p.astype(bf16), jnp.concatenate([vz[par], ones], axis=1))
            den = ov[:, 2 * HEAD_DIM:] + jnp.exp(sk - mb)
            o = ov[:, :2 * HEAD_DIM] / den
            acc = o if acc is None else acc + o
        for pr in range(ATT_PAIRS):
            o_ref[:, q0 + pr * 2 * HEAD_DIM:q0 + (pr + 1) * 2 * HEAD_DIM] = (
                acc[pr * ATT_BLOCK:(pr + 1) * ATT_BLOCK, :].astype(o_ref.dtype))


ATT_PAIRS = GQA_GROUP // 2


def swa_attn(qkv, bias_m, sinks, batch, seq):
    t = qkv.shape[0]
    nb = seq // ATT_BLOCK
    dq = N_Q_HEADS * HEAD_DIM
    kcol = dq // QK_COLS
    vcol = kcol + 1

    def prev(b, i):
        return b * nb + jnp.maximum(i - 1, 0)

    return pl.pallas_call(
        _attn_kernel,
        grid=(batch, nb),
        in_specs=[
            pl.BlockSpec((ATT_BLOCK, dq), lambda b, i: (b * nb + i, 0)),
            pl.BlockSpec((ATT_BLOCK, QK_COLS), lambda b, i: (prev(b, i), kcol)),
            pl.BlockSpec((ATT_BLOCK, QK_COLS), lambda b, i: (b * nb + i, kcol)),
            pl.BlockSpec((ATT_BLOCK, QK_COLS), lambda b, i: (prev(b, i), vcol)),
            pl.BlockSpec((ATT_BLOCK, QK_COLS), lambda b, i: (b * nb + i, vcol)),
            pl.BlockSpec((None, N_KV_HEADS, 2, ATT_PAIRS * ATT_BLOCK, 2 * ATT_BLOCK),
                         lambda b, i: (jnp.where(i == 0, 1, 0), 0, 0, 0, 0)),
            pl.BlockSpec(memory_space=pltpu.SMEM),
        ],
        out_specs=pl.BlockSpec((ATT_BLOCK, dq), lambda b, i: (b * nb + i, 0)),
        out_shape=jax.ShapeDtypeStruct((t, dq), bf16),
        compiler_params=_cparams(("arbitrary", "arbitrary")),
        name="swa_attn",
    )(qkv, qkv, qkv, qkv, qkv, bias_m, sinks)


def _t5_bucket(dist):
    n = np.maximum(dist, 0)
    max_exact = REL_BUCKETS // 2
    large = max_exact + (np.log(np.maximum(n, 1) / max_exact) / np.log(REL_MAX_DIST / max_exact)
                         * (REL_BUCKETS - max_exact)).astype(np.int32)
    large = np.minimum(large, REL_BUCKETS - 1)
    return np.where(n < max_exact, n, large).astype(np.int32)


def _by_kv_parity(a):
    rest = a.shape[2:]
    a = a.reshape((N_KV_HEADS, ATT_PAIRS, 2, ATT_BLOCK) + rest)
    a = jnp.moveaxis(a, 2, 1)
    return a.reshape((N_KV_HEADS, 2, ATT_PAIRS * ATT_BLOCK) + rest)


def _attn_tables(rel_bias):
    t_loc = np.arange(ATT_BLOCK)[:, None]
    j_loc = np.arange(2 * ATT_BLOCK)[None, :]
    dist = ATT_BLOCK + t_loc - j_loc
    band = (dist >= 0) & (dist < WINDOW)
    onehot = jnp.asarray(np.eye(REL_BUCKETS, dtype=np.float32)[_t5_bucket(dist)])
    bias = jnp.einsum('tjb,bh->htj', onehot, rel_bias.astype(f32), precision=lax.Precision.HIGHEST)
    first = band & (j_loc >= ATT_BLOCK)
    tabs = [_by_kv_parity(jnp.where(jnp.asarray(mask)[None], bias, -jnp.inf)) for mask in (band, first)]
    return jnp.stack(tabs)


def _route_kernel(a_ref, w_ref, gate_ref, x_ref, g_ref, sh_ref, sc_ref, rw_ref, rb_ref,
                  xnew_ref, xp_ref, eidx_ref, rk_ref, gw_ref, cnt_ref, grp_scr, carry_scr):
    tm = x_ref.shape[0]
    ne = N_EXPERTS

    @pl.when(pl.program_id(0) == 0)
    def _():
        carry_scr[...] = jnp.zeros_like(carry_scr)

    x = x_ref[...] + gate_ref[...] * _dot(a_ref[...], w_ref[...])
    xnew_ref[...] = x
    xn = _norm_mod(x, g_ref[...], sh_ref[...], sc_ref[...])
    _store_row_tiles(xp_ref, _pack_halves(xn))
    x_hi = xn.astype(bf16)
    x_lo = (xn - x_hi.astype(f32)).astype(bf16)
    rw = rw_ref[...]
    a = _dot_nt(rw, x_hi)
    b = _dot_nt(rw[:ne], x_lo)
    logits = a[:ne] + a[ne:] + b
    scores = jax.nn.sigmoid(logits)
    sel = scores + rb_ref[...]

    for g in range(N_GROUPS):
        tile = sel[g * E_PER_G:(g + 1) * E_PER_G, :]
        t1 = jnp.max(tile, axis=0, keepdims=True)
        dup = jnp.sum(jnp.where(tile == t1, 1.0, 0.0), axis=0, keepdims=True) >= 2.0
        t2 = jnp.max(jnp.where(tile < t1, tile, -jnp.inf), axis=0, keepdims=True)
        grp_scr[g:g + 1, :] = t1 + jnp.where(dup, t1, t2)
    gs = grp_scr[...]

    gi = lax.broadcasted_iota(i32, (N_GROUPS, tm), 0)
    grank = jnp.zeros((N_GROUPS, tm), i32)
    for g in range(N_GROUPS):
        r = grp_scr[g:g + 1, :]
        ge = jnp.where(r >= gs, 1, 0)
        gt = jnp.where(r > gs, 1, 0)
        grank = grank + jnp.where(gi > g, ge, gt)
    grp_scr[...] = jnp.where(grank < TOPK_GROUPS, 1.0, 0.0)

    masked = jnp.concatenate(
        [jnp.where(grp_scr[g:g + 1, :] > 0.5, sel[g * E_PER_G:(g + 1) * E_PER_G, :], -jnp.inf)
         for g in range(N_GROUPS)], axis=0)

    ei = lax.broadcasted_iota(i32, (ne, tm), 0)
    cur = masked
    chf = jnp.zeros((ne, tm), f32)
    for _ in range(TOP_K):
        top = jnp.max(cur, axis=0, keepdims=True)
        first = jnp.min(jnp.where(cur == top, ei, ne), axis=0, keepdims=True)
        pick = ei == first
        chf = jnp.where(pick, 1.0, chf)
        cur = jnp.where(pick, -jnp.inf, cur)
    chosen = chf > 0.5

    wun = jnp.where(chosen, scores, 0.0)
    gwt = wun / jnp.sum(wun, axis=0, keepdims=True) * ROUTED_SCALE

    rr = lax.broadcasted_iota(i32, (tm, tm), 0)
    cc = lax.broadcasted_iota(i32, (tm, tm), 1)
    upper = jnp.where(rr < cc, 1.0, 0.0).astype(bf16)
    chb = chf.astype(bf16)
    pos = carry_scr[:, 0:1] + _dot(chb, upper)
    carry_scr[...] = carry_scr[...] + jnp.sum(chf, axis=1, keepdims=True)
    cnt_ref[...] = carry_scr[...].astype(i32)

    er = lax.broadcasted_iota(i32, (ne, ne), 0)
    ec = lax.broadcasted_iota(i32, (ne, ne), 1)
    lower = jnp.where(ec < er, 1.0, 0.0).astype(bf16)
    below = _dot(lower, chb)
    slot = jnp.where(chosen, below, -1.0)
    eif = ei.astype(f32)
    for k in range(TOP_K):
        mk = slot == float(k)
        eidx_ref[k:k + 1, :] = jnp.sum(jnp.where(mk, eif, 0.0), axis=0, keepdims=True).astype(i32)
        rk_ref[k:k + 1, :] = jnp.sum(jnp.where(mk, pos, 0.0), axis=0, keepdims=True).astype(i32)
        gw_ref[k:k + 1, :] = jnp.sum(jnp.where(mk, gwt, 0.0), axis=0, keepdims=True)


def proj_route(a, w_bf, x2, gate, g, sh, sc, rw_cat, rbias, seq):
    t, d = x2.shape
    k = a.shape[1]
    tps = seq // TR
    return pl.pallas_call(
        _route_kernel,
        grid=(t // TR,),
        in_specs=[
            pl.BlockSpec((TR, k), lambda i: (i, 0)),
            pl.BlockSpec((k, d), lambda i: (0, 0)),
            pl.BlockSpec((None, 1, d), lambda i: (i // tps, 0, 0)),
            pl.BlockSpec((TR, d), lambda i: (i, 0)),
            pl.BlockSpec((1, d), lambda i: (0, 0)),
            pl.BlockSpec((None, 1, d), lambda i: (i // tps, 0, 0)),
            pl.BlockSpec((None, 1, d), lambda i: (i // tps, 0, 0)),
            pl.BlockSpec((2 * N_EXPERTS, d), lambda i: (0, 0)),
            pl.BlockSpec((N_EXPERTS, 1), lambda i: (0, 0)),
        ],
        out_specs=[
            pl.BlockSpec((TR, d), lambda i: (i, 0)),
            pl.BlockSpec((TR * ROW_SUB, LANES), lambda i: (i, 0)),
            pl.BlockSpec((TOP_K, TR), lambda i: (0, i)),
            pl.BlockSpec((TOP_K, TR), lambda i: (0, i)),
            pl.BlockSpec((TOP_K, TR), lambda i: (0, i)),
            pl.BlockSpec((N_EXPERTS, LANES), lambda i: (0, 0)),
        ],
        out_shape=[
            jax.ShapeDtypeStruct((t, d), f32),
            jax.ShapeDtypeStruct((t * ROW_SUB, LANES), u32),
            jax.ShapeDtypeStruct((TOP_K, t), i32),
            jax.ShapeDtypeStruct((TOP_K, t), i32),
            jax.ShapeDtypeStruct((TOP_K, t), f32),
            jax.ShapeDtypeStruct((N_EXPERTS, LANES), i32),
        ],
        scratch_shapes=[
            pltpu.VMEM((N_GROUPS, TR), f32),
            pltpu.VMEM((N_EXPERTS, LANES), f32),
        ],
        compiler_params=_cparams(("arbitrary",)),
        name="proj_route",
    )(a, w_bf, gate, x2, g.reshape(1, d), sh, sc, rw_cat, rbias.reshape(N_EXPERTS, 1))


def _tile_rows(ref, r, n=1):
    return ref.at[pl.ds(pl.multiple_of(r * ROW_SUB, ROW_SUB), n * ROW_SUB), :]


def _row_copy(src, s, dst, d, sem):
    return pltpu.make_async_copy(_tile_rows(src, s), _tile_rows(dst, d), sem)


def _dispatch_kernel(ztail_ref, dest_ref, xp_ref, sg_ref, su_ref, sd_ref, xs_ref, ysh_ref, zeros_scr, sem, zsem):
    td = xp_ref.shape[0] // ROW_SUB

    @pl.when(pl.program_id(0) == 0)
    def _():
        zeros_scr[...] = jnp.zeros_like(zeros_scr)

        def zcopy(e):
            return pltpu.make_async_copy(zeros_scr, _tile_rows(xs_ref, ztail_ref[e], BM), zsem)

        def zstart(e, c):
            zcopy(e).start()
            return c

        def zwait(e, c):
            zcopy(e).wait()
            return c

        lax.fori_loop(0, N_EXPERTS, zstart, 0)
        lax.fori_loop(0, N_EXPERTS, zwait, 0)

    for t in range(td):
        for k in range(TOP_K):
            _row_copy(xp_ref, t, xs_ref, dest_ref[k, t], sem).start(priority=k % DMA_PRIORITIES)
    ysh_ref[...] = _ffn_packed(xp_ref, td, sg_ref, su_ref, sd_ref)
    for t in range(td):
        for k in range(TOP_K):
            _row_copy(xp_ref, t, xs_ref, dest_ref[k, t], sem).wait()


def moe_dispatch(xp, dest, ztail, nrows, sg_bf, su_bf, sd_bf):
    t = xp.shape[0] // ROW_SUB
    d = sg_bf.shape[0]
    grid_spec = pltpu.PrefetchScalarGridSpec(
        num_scalar_prefetch=1,
        grid=(t // TD,),
        in_specs=[
            pl.BlockSpec((TOP_K, TD), lambda i, z: (0, i), memory_space=pltpu.SMEM),
            pl.BlockSpec((TD * ROW_SUB, LANES), lambda i, z: (i, 0)),
            pl.BlockSpec((d, D_SHARED), lambda i, z: (0, 0)),
            pl.BlockSpec((d, D_SHARED), lambda i, z: (0, 0)),
            pl.BlockSpec((D_SHARED, d), lambda i, z: (0, 0)),
        ],
        out_specs=[pl.BlockSpec(memory_space=pl.ANY), pl.BlockSpec((TD, d), lambda i, z: (i, 0))],
        scratch_shapes=[
            pltpu.VMEM((BM * ROW_SUB, LANES), u32),
            pltpu.SemaphoreType.DMA(()),
            pltpu.SemaphoreType.DMA(()),
        ],
    )
    return pl.pallas_call(
        _dispatch_kernel,
        grid_spec=grid_spec,
        out_shape=[jax.ShapeDtypeStruct((nrows * ROW_SUB, LANES), u32), jax.ShapeDtypeStruct((t, d), f32)],
        compiler_params=_cparams(("arbitrary",)),
        name="moe_dispatch",
    )(ztail, dest, xp, sg_bf, su_bf, sd_bf)


def _ffn_packed(x_ref, m, wg, wu, wd):
    los, his = _load_row_tiles(x_ref, m)
    x = jnp.concatenate([p.astype(bf16) for p in los + his], axis=1)
    hg = _dot(x, wg[...])
    hu = _dot(x, wu[...])
    a = (jax.nn.silu(hg) * hu).astype(bf16)
    return _dot(a, wd[...])


def _gmm_kernel(be_ref, bf_ref, nu_ref, xs_ref, wg_ref, wu_ref, wd_ref, ys_ref, wg_s, wu_s, wd_s):
    i = pl.program_id(0)

    @pl.when(bf_ref[i] == 1)
    def _():
        wg_s[...] = wg_ref[...].astype(bf16)
        wu_s[...] = wu_ref[...].astype(bf16)
        wd_s[...] = wd_ref[...].astype(bf16)

    @pl.when(i < nu_ref[0])
    def _():
        _store_row_tiles(ys_ref, _pack_halves(_ffn_packed(xs_ref, BM, wg_s, wu_s, wd_s)))

    @pl.when(i >= nu_ref[0])
    def _():
        ys_ref[...] = jnp.zeros_like(ys_ref)


def moe_gmm(xs, blk_e, blk_first, n_used, w_gate, w_up, w_down, layer, nblk):
    d, de = w_gate.shape[2], w_gate.shape[3]
    grid_spec = pltpu.PrefetchScalarGridSpec(
        num_scalar_prefetch=3,
        grid=(nblk,),
        in_specs=[
            pl.BlockSpec((BM * ROW_SUB, LANES), lambda i, be, bf, nu: (jnp.minimum(i, nu[0] - 1), 0)),
            pl.BlockSpec((None, None, d, de), lambda i, be, bf, nu: (layer, be[i], 0, 0)),
            pl.BlockSpec((None, None, d, de), lambda i, be, bf, nu: (layer, be[i], 0, 0)),
            pl.BlockSpec((None, None, de, d), lambda i, be, bf, nu: (layer, be[i], 0, 0)),
        ],
        out_specs=pl.BlockSpec((BM * ROW_SUB, LANES), lambda i, be, bf, nu: (i, 0)),
        scratch_shapes=[
            pltpu.VMEM((d, de), bf16),
            pltpu.VMEM((d, de), bf16),
            pltpu.VMEM((de, d), bf16),
        ],
    )
    return pl.pallas_call(
        _gmm_kernel,
        grid_spec=grid_spec,
        out_shape=jax.ShapeDtypeStruct((nblk * BM * ROW_SUB, LANES), u32),
        compiler_params=_cparams(("arbitrary",)),
        name="moe_gmm",
    )(blk_e, blk_first, n_used, xs, w_gate, w_up, w_down)


def _combine_kernel(dest_ref, dnext_ref, x_ref, ysh_ref, gw_ref, gate_ref, *rest, tail):
    if tail == 'qkv':
        ng_ref, nsh_ref, nsc_ref, wq_ref, ys_ref, o_ref, q_ref, gath_a, gath_b, sem, xn_scr = rest
    elif tail == 'final':
        ng_ref, ys_ref, o_ref, gath_a, gath_b, sem = rest
    else:
        ys_ref, o_ref, gath_a, gath_b, sem = rest
    td = x_ref.shape[0]
    i = pl.program_id(0)

    def copy(dref, buf, s, t, k):
        return _row_copy(ys_ref, dref[k, t], buf.at[k], t, sem.at[s])

    def wait_tile(dref, buf, s):
        for t in range(td):
            for k in range(TOP_K):
                copy(dref, buf, s, t, k).wait()

    @pl.when(i == 0)
    def _():
        def body(t, c):
            for k in range(TOP_K):
                copy(dest_ref, gath_a, 0, t, k).start(priority=k % DMA_PRIORITIES)
            return c
        lax.fori_loop(0, td, body, 0)

    def step(cur_buf, cur_s, nxt_buf, nxt_s):
        wait_tile(dest_ref, cur_buf, cur_s)
        for t in range(td):
            for k in range(TOP_K):
                copy(dnext_ref, nxt_buf, nxt_s, t, k).start(priority=k % DMA_PRIORITIES)
        for rb in range(td // COMB_RB):
            rows = slice(rb * COMB_RB, (rb + 1) * COMB_RB)
            ysh = ysh_ref[rows, :]
            acc_lo = [ysh[:, c * LANES:(c + 1) * LANES] for c in range(ROW_SUB)]
            acc_hi = [ysh[:, HALF + c * LANES:HALF + (c + 1) * LANES] for c in range(ROW_SUB)]
            gw = gw_ref[rows, :]
            for k in range(TOP_K):
                wk = jnp.broadcast_to(gw[:, k:k + 1], (COMB_RB, LANES))
                for c in range(ROW_SUB):
                    word = cur_buf[k, pl.ds(rb * COMB_RB * ROW_SUB + c, COMB_RB, stride=ROW_SUB), :]
                    lo, hi = _unpack_halves(word)
                    acc_lo[c] = acc_lo[c] + wk * lo
                    acc_hi[c] = acc_hi[c] + wk * hi
            y = jnp.concatenate(acc_lo + acc_hi, axis=1)
            out = x_ref[rows, :] + gate_ref[...] * y
            if tail == 'final':
                ms = jnp.mean(out * out, axis=-1, keepdims=True)
                out = (out * lax.rsqrt(ms + RMS_EPS)) * ng_ref[...]
            o_ref[rows, :] = out
            if tail == 'qkv':
                xn_scr[rows, :] = _norm_mod(out, ng_ref[...], nsh_ref[...], nsc_ref[...]).astype(bf16)
        if tail == 'qkv':
            q_ref[...] = _dot(xn_scr[...], wq_ref[...]).astype(q_ref.dtype)

        @pl.when(i == pl.num_programs(0) - 1)
        def _():
            wait_tile(dnext_ref, nxt_buf, nxt_s)

    @pl.when(i % 2 == 0)
    def _():
        step(gath_a, 0, gath_b, 1)

    @pl.when(i % 2 == 1)
    def _():
        step(gath_b, 1, gath_a, 0)


def moe_combine(x2, ysh, ys, dest, gw_t, gate, seq, final_g=None, next_qkv=None):
    t, d = x2.shape
    tps = seq // TD
    last = t // TD - 1
    row = lambda i: (i, 0)
    per_batch = pl.BlockSpec((None, 1, d), lambda i: (i // tps, 0, 0))
    in_specs = [
        pl.BlockSpec((TOP_K, TD), lambda i: (0, i), memory_space=pltpu.SMEM),
        pl.BlockSpec((TOP_K, TD), lambda i: (0, jnp.minimum(i + 1, last)), memory_space=pltpu.SMEM),
        pl.BlockSpec((TD, d), row),
        pl.BlockSpec((TD, d), row),
        pl.BlockSpec((TD, TOP_K), row),
        per_batch,
    ]
    args = [dest, dest, x2, ysh, gw_t, gate]
    out_specs = [pl.BlockSpec((TD, d), row)]
    out_shape = [jax.ShapeDtypeStruct((t, d), f32)]
    scratch = [
        pltpu.VMEM((TOP_K, TD * ROW_SUB, LANES), u32),
        pltpu.VMEM((TOP_K, TD * ROW_SUB, LANES), u32),
        pltpu.SemaphoreType.DMA((2,)),
    ]
    tail = None
    if final_g is not None:
        tail = 'final'
        in_specs.append(pl.BlockSpec((1, d), lambda i: (0, 0)))
        args.append(final_g.reshape(1, d))
    elif next_qkv is not None:
        tail = 'qkv'
        g, sh, sc, w_bf = next_qkv
        nq = w_bf.shape[1]
        in_specs += [pl.BlockSpec((1, d), lambda i: (0, 0)), per_batch, per_batch,
                     pl.BlockSpec((d, nq), lambda i: (0, 0))]
        args += [g.reshape(1, d), sh, sc, w_bf]
        out_specs.append(pl.BlockSpec((TD, nq), row))
        out_shape.append(jax.ShapeDtypeStruct((t, nq), bf16))
        scratch.append(pltpu.VMEM((TD, d), bf16))
    in_specs.append(pl.BlockSpec(memory_space=pl.ANY))
    args.append(ys)
    outs = pl.pallas_call(
        functools.partial(_combine_kernel, tail=tail),
        grid=(t // TD,),
        in_specs=in_specs,
        out_specs=out_specs,
        out_shape=out_shape,
        scratch_shapes=scratch,
        compiler_params=_cparams(("arbitrary",)),
        name="moe_combine",
    )(*args)
    return outs if tail == 'qkv' else outs[0]


SLOT_TN = 4096


def _slots_kernel(ps_ref, eidx_ref, rk_ref, o_ref):
    e = eidx_ref[...]
    start = jnp.zeros_like(e)
    for j in range(N_EXPERTS):
        start = jnp.where(e == j, ps_ref[j], start)
    o_ref[...] = start + rk_ref[...]


def moe_slots(pad_start, eidx, rk):
    t = eidx.shape[1]
    tn = min(t, SLOT_TN)
    spec = pl.BlockSpec((TOP_K, tn), lambda i, ps: (0, i))
    grid_spec = pltpu.PrefetchScalarGridSpec(
        num_scalar_prefetch=1, grid=(t // tn,), in_specs=[spec, spec], out_specs=spec)
    return pl.pallas_call(
        _slots_kernel,
        grid_spec=grid_spec,
        out_shape=jax.ShapeDtypeStruct((TOP_K, t), i32),
        compiler_params=_cparams(("arbitrary",)),
        name="moe_slots",
    )(pad_start, eidx, rk)


def moe_layer(mix, w_out_bf, x2, gate_m, g, sh, sc, gate, router_w, router_bias, w_gate, w_up, w_down, layer,
              sh_gate, sh_up, sh_down, seq, final_g=None, next_qkv=None):
    t = x2.shape[0]
    a = t * TOP_K
    nblk = a // BM + N_EXPERTS
    rw_t = router_w.T
    rw_hi = rw_t.astype(bf16)
    rw_lo = (rw_t - rw_hi.astype(f32)).astype(bf16)
    rw_cat = jnp.concatenate([rw_hi, rw_lo], axis=0)

    x2, xp, eidx, rk, gw, cnt = proj_route(mix, w_out_bf, x2, gate_m, g, sh, sc, rw_cat, router_bias, seq)

    counts = cnt[:, 0]
    padded = ((counts + BM - 1) // BM) * BM
    cum_pad = jnp.cumsum(padded)
    pad_start = cum_pad - padded
    n_used = (cum_pad[-1] // BM).astype(i32).reshape(1)
    blk_row = jnp.arange(nblk, dtype=i32) * BM
    blk_e = jnp.minimum(jnp.sum((cum_pad[None, :] <= blk_row[:, None]).astype(i32), axis=1), N_EXPERTS - 1)
    blk_first = jnp.concatenate([jnp.ones((1,), i32), (blk_e[1:] != blk_e[:-1]).astype(i32)])
    dest = moe_slots(pad_start.astype(i32), eidx, rk)
    ztail = jnp.maximum(cum_pad - BM, 0).astype(i32)

    xs, ysh = moe_dispatch(xp, dest, ztail, nblk * BM, sh_gate.astype(bf16), sh_up.astype(bf16),
                           sh_down.astype(bf16))
    ys = moe_gmm(xs, blk_e, blk_first, n_used, w_gate, w_up, w_down, layer, nblk)
    return moe_combine(x2, ysh, ys, dest, gw.T, gate, seq, final_g=final_g, next_qkv=next_qkv)


def _qkv_weight(w_qkv):
    dq = N_Q_HEADS * HEAD_DIM
    dkv = N_KV_HEADS * HEAD_DIM
    d = w_qkv.shape[0]
    wq = w_qkv[:, :dq]
    wk = w_qkv[:, dq:dq + dkv].reshape(d, N_KV_HEADS, 1, HEAD_DIM)
    wv = w_qkv[:, dq + dkv:].reshape(d, N_KV_HEADS, 1, HEAD_DIM)
    wk2 = jnp.broadcast_to(wk, (d, N_KV_HEADS, 2, HEAD_DIM)).reshape(d, QK_COLS)
    wv2 = jnp.broadcast_to(wv, (d, N_KV_HEADS, 2, HEAD_DIM)).reshape(d, QK_COLS)
    return jnp.concatenate([wq, wk2, wv2], axis=1).astype(bf16)


def kernel(x, c, ada_w, ada_b, norm_mix_g, norm_ffn_g, hyb_w_in, conv_w, sgu_ln_g, sgu_ln_b, sgu_w, sgu_b,
           hyb_w_out, attn_w_qkv, attn_sinks, attn_w_o, rel_bias, router_w, router_bias, exp_w_gate,
           exp_w_up, exp_w_down, sh_w_gate, sh_w_up, sh_w_down, final_g):
    batch, seq, d = x.shape
    t = batch * seq
    x2 = x.reshape(t, d)
    mod = ada_mod(c, ada_w, ada_b)
    mods = [[mod[l, :, k * d:(k + 1) * d].reshape(batch, 1, d) for k in range(6)] for l in range(DEPTH)]
    qkv = None
    for l in range(DEPTH):
        sh_m, sc_m, g_m, sh_f, sc_f, g_f = mods[l]
        i = l // 2
        if l % 2 == 0:
            mix = hyb_in(x2, norm_mix_g[l], sh_m, sc_m, hyb_w_in[i].astype(bf16), conv_w[i], sgu_ln_g[i],
                         sgu_ln_b[i], sgu_w[i], sgu_b[i], seq)
            w_out = hyb_w_out[i].astype(bf16)
        else:
            if qkv is None:
                qkv = qkv_proj(x2, norm_mix_g[l], sh_m, sc_m, _qkv_weight(attn_w_qkv[i]), seq, 1024)
            mix = swa_attn(qkv, _attn_tables(rel_bias), attn_sinks[i], batch, seq)
            w_out = attn_w_o[i].astype(bf16)
        tail = {}
        if l == DEPTH - 1:
            tail = dict(final_g=final_g)
        elif (l + 1) % 2 == 1:
            tail = dict(next_qkv=(norm_mix_g[l + 1], mods[l + 1][0], mods[l + 1][1],
                                  _qkv_weight(attn_w_qkv[(l + 1) // 2])))
        res = moe_layer(mix, w_out, x2, g_m, norm_ffn_g[l], sh_f, sc_f, g_f, router_w[l], router_bias[l],
                        exp_w_gate, exp_w_up, exp_w_down, l, sh_w_gate[l], sh_w_up[l], sh_w_down[l], seq, **tail)
        x2, qkv = res if 'next_qkv' in tail else (res, None)
    return x2.reshape(batch, seq, d)
```

```python
import functools

import numpy as np
import jax
import jax.numpy as jnp
from jax import lax
from jax.experimental import pallas as pl
from jax.experimental.pallas import tpu as pltpu
from jax.experimental.pallas import tpu_sc as plsc

f32 = jnp.float32
bf16 = jnp.bfloat16
i32 = jnp.int32
u32 = jnp.uint32

D_MODEL = 2048
DEPTH = 2
RMS_EPS = 1e-6
LN_EPS = 1e-5
D_CONV = 1024
CONV_WIDTH = 3
D_SGU = 1024
SGU_GROUPS = 8
SGU_HD = D_SGU // SGU_GROUPS
CHUNK = 128
HYB_IN = 3 * D_CONV + 2 * D_SGU
HEAD_DIM = 64
N_Q_HEADS = D_MODEL // HEAD_DIM
N_KV_HEADS = N_Q_HEADS // 8
GQA_GROUP = N_Q_HEADS // N_KV_HEADS
WINDOW = 128
ATT_BLOCK = 128
REL_BUCKETS = 32
REL_MAX_DIST = 128
N_EXPERTS = 64
TOP_K = 8
N_GROUPS = 8
TOPK_GROUPS = 4
E_PER_G = N_EXPERTS // N_GROUPS
D_EXPERT = 512
D_SHARED = 512
ROUTED_SCALE = 2.5

LANES = 128
HALF = D_MODEL // 2
VMEM_LIMIT = 56 * 1024 * 1024
TM = 512
TN_IN = 1024
TR = 512
TD = 256
BM = 512
ADA_TN = 1024
DMA_PRIORITIES = 2
COMB_RB = 16


def _cparams(sem, vmem=VMEM_LIMIT):
    return pltpu.CompilerParams(dimension_semantics=sem, vmem_limit_bytes=vmem)


def _norm_mod(x, g, sh, sc):
    ms = jnp.mean(x * x, axis=-1, keepdims=True)
    y = x * lax.rsqrt(ms + RMS_EPS)
    return (y * g) * (1.0 + sc) + sh


def _pack_halves(y):
    h = y.shape[1] // 2
    lo = lax.bitcast_convert_type(y[:, :h].astype(bf16).astype(f32), u32)
    hi = lax.bitcast_convert_type(y[:, h:].astype(bf16).astype(f32), u32)
    return (lo >> 16) | (hi & jnp.uint32(0xFFFF0000))


def _unpack_halves(w):
    lo = lax.bitcast_convert_type(w << 16, f32)
    hi = lax.bitcast_convert_type(w & jnp.uint32(0xFFFF0000), f32)
    return lo, hi


ROW_SUB = HALF // LANES


def _store_row_tiles(ref, packed):
    n = packed.shape[0]
    for c in range(ROW_SUB):
        ref[pl.ds(c, n, stride=ROW_SUB), :] = packed[:, c * LANES:(c + 1) * LANES]


def _load_row_tiles(ref, n):
    los, his = [], []
    for c in range(ROW_SUB):
        lo, hi = _unpack_halves(ref[pl.ds(c, n, stride=ROW_SUB), :])
        los.append(lo)
        his.append(hi)
    return los, his


def _gelu(x):
    return 0.5 * x * (1.0 + lax.erf(x * np.float32(np.sqrt(0.5))))


def _dot(a, b):
    return jnp.dot(a, b, preferred_element_type=f32)


def _dot_nt(a, b):
    return lax.dot_general(a, b, (((1,), (1,)), ((), ())), preferred_element_type=f32)


def _ada_kernel(c_ref, w_ref, b_ref, o_ref):
    ca = jax.nn.silu(c_ref[...]).astype(bf16)
    o_ref[...] = _dot(ca, w_ref[...].astype(bf16)) + b_ref[...]


def ada_mod(c, ada_w, ada_b):
    depth, d, n = ada_w.shape
    b = c.shape[0]
    return pl.pallas_call(
        _ada_kernel,
        grid=(depth, n // ADA_TN),
        in_specs=[
            pl.BlockSpec((b, d), lambda l, j: (0, 0)),
            pl.BlockSpec((None, d, ADA_TN), lambda l, j: (l, 0, j)),
            pl.BlockSpec((None, 1, ADA_TN), lambda l, j: (l, 0, j)),
        ],
        out_specs=pl.BlockSpec((None, b, ADA_TN), lambda l, j: (l, 0, j)),
        out_shape=jax.ShapeDtypeStruct((depth, b, n), f32),
        compiler_params=_cparams(("arbitrary", "arbitrary")),
        name="ada_mod",
    )(c, ada_w, ada_b.reshape(depth, 1, n))


def _hyb_in_kernel(x_ref, g_ref, sh_ref, sc_ref, w_ref, cw_ref, lng_ref, lnb_ref, sw_ref, sb_ref,
                   o_ref, xn_scr, a_scr, b_scr, carry_scr, *, tiles_per_seq):
    i = pl.program_id(0)
    j = pl.program_id(1)
    tm = x_ref.shape[0]

    @pl.when(j == 0)
    def _():
        xn_scr[...] = _norm_mod(x_ref[...], g_ref[...], sh_ref[...], sc_ref[...]).astype(bf16)

    @pl.when(jnp.logical_and(i == 0, j == 0))
    def _():
        carry_scr[...] = jnp.zeros_like(carry_scr)

    p = _dot(xn_scr[...], w_ref[...])

    @pl.when(j == 0)
    def _():
        a_scr[...] = p

    @pl.when(j == 1)
    def _():
        b_scr[...] = p

    @pl.when(j == 2)
    def _():
        z = b_scr[...] * p
        row = lax.broadcasted_iota(i32, z.shape, 0)
        first = (i % tiles_per_seq) == 0
        prev = jnp.where(first, 0.0, carry_scr[...])
        p1 = prev[7:8, :]
        p2 = prev[6:7, :]
        z1 = jnp.where(row == 0, p1, pltpu.roll(z, 1, 0))
        z2 = jnp.where(row == 0, p2, jnp.where(row == 1, p1, pltpu.roll(z, 2, 0)))
        cw = cw_ref[...]
        conv = cw[0:1, :] * z2 + cw[1:2, :] * z1 + cw[2:3, :] * z
        carry_scr[...] = z[tm - 8:, :]
        o_ref[:, :D_CONV] = (a_scr[...] * conv).astype(o_ref.dtype)

    @pl.when(j == 3)
    def _():
        a_scr[...] = _gelu(p)

    @pl.when(j == 4)
    def _():
        v = _gelu(p)
        mu = jnp.mean(v, axis=-1, keepdims=True)
        vc = v - mu
        var = jnp.mean(vc * vc, axis=-1, keepdims=True)
        v = (vc * lax.rsqrt(var + LN_EPS)) * lng_ref[...] + lnb_ref[...]
        vb = v.astype(bf16)
        nch = tm // CHUNK
        r = lax.broadcasted_iota(i32, (CHUNK, CHUNK), 0)
        c = lax.broadcasted_iota(i32, (CHUNK, CHUNK), 1)
        tril = r >= c
        for g in range(SGU_GROUPS):
            ws = jnp.where(tril, sw_ref[g], 0.0).astype(bf16)
            vg = jnp.concatenate(
                [vb[n * CHUNK:(n + 1) * CHUNK, g * SGU_HD:(g + 1) * SGU_HD] for n in range(nch)], axis=1)
            sg = _dot(ws, vg)
            bb = sb_ref[g]
            for n in range(nch):
                s = sg[:, n * SGU_HD:(n + 1) * SGU_HD] + bb
                u = a_scr[n * CHUNK:(n + 1) * CHUNK, g * SGU_HD:(g + 1) * SGU_HD]
                o_ref[n * CHUNK:(n + 1) * CHUNK, D_CONV + g * SGU_HD:D_CONV + (g + 1) * SGU_HD] = (
                    (u * s).astype(o_ref.dtype))


def hyb_in(x2, g, sh, sc, w_in_bf, conv_w, ln_g, ln_b, sgu_w, sgu_b, seq):
    t, d = x2.shape
    tps = seq // TM
    nj = HYB_IN // TN_IN
    sbb = jnp.broadcast_to(sgu_b[:, :, None], (SGU_GROUPS, CHUNK, SGU_HD))
    kern = functools.partial(_hyb_in_kernel, tiles_per_seq=tps)
    return pl.pallas_call(
        kern,
        grid=(t // TM, nj),
        in_specs=[
            pl.BlockSpec((TM, d), lambda i, j: (i, 0)),
            pl.BlockSpec((1, d), lambda i, j: (0, 0)),
            pl.BlockSpec((None, 1, d), lambda i, j: (i // tps, 0, 0)),
            pl.BlockSpec((None, 1, d), lambda i, j: (i // tps, 0, 0)),
            pl.BlockSpec((d, TN_IN), lambda i, j: (0, j)),
            pl.BlockSpec((CONV_WIDTH, D_CONV), lambda i, j: (0, 0)),
            pl.BlockSpec((1, D_SGU), lambda i, j: (0, 0)),
            pl.BlockSpec((1, D_SGU), lambda i, j: (0, 0)),
            pl.BlockSpec((SGU_GROUPS, CHUNK, CHUNK), lambda i, j: (0, 0, 0)),
            pl.BlockSpec((SGU_GROUPS, CHUNK, SGU_HD), lambda i, j: (0, 0, 0)),
        ],
        out_specs=pl.BlockSpec((TM, D_CONV + D_SGU), lambda i, j: (i, 0)),
        out_shape=jax.ShapeDtypeStruct((t, D_CONV + D_SGU), bf16),
        scratch_shapes=[
            pltpu.VMEM((TM, d), bf16),
            pltpu.VMEM((TM, TN_IN), f32),
            pltpu.VMEM((TM, TN_IN), f32),
            pltpu.VMEM((8, D_CONV), f32),
        ],
        compiler_params=_cparams(("arbitrary", "arbitrary")),
        name="hyb_in",
    )(x2, g.reshape(1, d), sh, sc, w_in_bf, conv_w, ln_g.reshape(1, -1), ln_b.reshape(1, -1), sgu_w, sbb)


def _qkv_kernel(x_ref, g_ref, sh_ref, sc_ref, w_ref, o_ref, xn_scr):
    @pl.when(pl.program_id(1) == 0)
    def _():
        xn_scr[...] = _norm_mod(x_ref[...], g_ref[...], sh_ref[...], sc_ref[...]).astype(bf16)

    o_ref[...] = _dot(xn_scr[...], w_ref[...]).astype(o_ref.dtype)


def qkv_proj(x2, g, sh, sc, w_bf, seq, tn):
    t, d = x2.shape
    n = w_bf.shape[1]
    tps = seq // TM
    return pl.pallas_call(
        _qkv_kernel,
        grid=(t // TM, n // tn),
        in_specs=[
            pl.BlockSpec((TM, d), lambda i, j: (i, 0)),
            pl.BlockSpec((1, d), lambda i, j: (0, 0)),
            pl.BlockSpec((None, 1, d), lambda i, j: (i // tps, 0, 0)),
            pl.BlockSpec((None, 1, d), lambda i, j: (i // tps, 0, 0)),
            pl.BlockSpec((d, tn), lambda i, j: (0, j)),
        ],
        out_specs=pl.BlockSpec((TM, tn), lambda i, j: (i, j)),
        out_shape=jax.ShapeDtypeStruct((t, n), bf16),
        scratch_shapes=[pltpu.VMEM((TM, d), bf16)],
        compiler_params=_cparams(("arbitrary", "arbitrary")),
        name="qkv_proj",
    )(x2, g.reshape(1, d), sh, sc, w_bf)


QK_COLS = N_KV_HEADS * 2 * HEAD_DIM


def _attn_kernel(q_ref, kp_ref, kc_ref, vp_ref, vc_ref, bias_ref, sink_ref, o_ref):
    lane = lax.broadcasted_iota(i32, (2 * ATT_BLOCK, 2 * HEAD_DIM), 1)
    low = lane < HEAD_DIM
    zero = jnp.zeros((), bf16)
    scale = jnp.asarray(HEAD_DIM ** -0.5, bf16)
    ones = jnp.ones((2 * ATT_BLOCK, 2 * HEAD_DIM), bf16)
    for kh in range(N_KV_HEADS):
        cs = slice(kh * 2 * HEAD_DIM, (kh + 1) * 2 * HEAD_DIM)
        kk = jnp.concatenate([kp_ref[:, cs], kc_ref[:, cs]], axis=0)
        vv = jnp.concatenate([vp_ref[:, cs], vc_ref[:, cs]], axis=0)
        kz = (jnp.where(low, kk, zero), jnp.where(low, zero, kk))
        vz = (jnp.where(low, vv, zero), jnp.where(low, zero, vv))
        q0 = kh * GQA_GROUP * HEAD_DIM
        qs = jnp.concatenate(
            [q_ref[:, q0 + pr * 2 * HEAD_DIM:q0 + (pr + 1) * 2 * HEAD_DIM] for pr in range(ATT_PAIRS)],
            axis=0) * scale
        acc = None
        for par in range(2):
            s = _dot_nt(qs, kz[par]) + bias_ref[kh, par]
            sk = jnp.concatenate(
                [jnp.full((ATT_BLOCK, ATT_BLOCK), sink_ref[kh * GQA_GROUP + 2 * pr + par], f32)
                 for pr in range(ATT_PAIRS)], axis=0)
            rm = jnp.max(s, axis=-1, keepdims=True)
            mb = jnp.maximum(jnp.broadcast_to(rm, sk.shape), sk)
            p = jnp.concatenate([jnp.exp(s[:, :ATT_BLOCK] - mb), jnp.exp(s[:, ATT_BLOCK:] - mb)], axis=1)
            ov = _dot(p.astype(bf16), jnp.concatenate([vz[par], ones], axis=1))
            den = ov[:, 2 * HEAD_DIM:] + jnp.exp(sk - mb)
            o = ov[:, :2 * HEAD_DIM] / den
            acc = o if acc is None else acc + o
        for pr in range(ATT_PAIRS):
            o_ref[:, q0 + pr * 2 * HEAD_DIM:q0 + (pr + 1) * 2 * HEAD_DIM] = (
                acc[pr * ATT_BLOCK:(pr + 1) * ATT_BLOCK, :].astype(o_ref.dtype))


ATT_PAIRS = GQA_GROUP // 2


def swa_attn(qkv, bias_m, sinks, batch, seq):
    t = qkv.shape[0]
    nb = seq // ATT_BLOCK
    dq = N_Q_HEADS * HEAD_DIM
    kcol = dq // QK_COLS
    vcol = kcol + 1

    def prev(b, i):
        return b * nb + jnp.maximum(i - 1, 0)

    return pl.pallas_call(
        _attn_kernel,
        grid=(batch, nb),
        in_specs=[
            pl.BlockSpec((ATT_BLOCK, dq), lambda b, i: (b * nb + i, 0)),
            pl.BlockSpec((ATT_BLOCK, QK_COLS), lambda b, i: (prev(b, i), kcol)),
            pl.BlockSpec((ATT_BLOCK, QK_COLS), lambda b, i: (b * nb + i, kcol)),
            pl.BlockSpec((ATT_BLOCK, QK_COLS), lambda b, i: (prev(b, i), vcol)),
            pl.BlockSpec((ATT_BLOCK, QK_COLS), lambda b, i: (b * nb + i, vcol)),
            pl.BlockSpec((None, N_KV_HEADS, 2, ATT_PAIRS * ATT_BLOCK, 2 * ATT_BLOCK),
                         lambda b, i: (jnp.where(i == 0, 1, 0), 0, 0, 0, 0)),
            pl.BlockSpec(memory_space=pltpu.SMEM),
        ],
        out_specs=pl.BlockSpec((ATT_BLOCK, dq), lambda b, i: (b * nb + i, 0)),
        out_shape=jax.ShapeDtypeStruct((t, dq), bf16),
        compiler_params=_cparams(("arbitrary", "arbitrary")),
        name="swa_attn",
    )(qkv, qkv, qkv, qkv, qkv, bias_m, sinks)


def _t5_bucket(dist):
    n = np.maximum(dist, 0)
    max_exact = REL_BUCKETS // 2
    large = max_exact + (np.log(np.maximum(n, 1) / max_exact) / np.log(REL_MAX_DIST / max_exact)
                         * (REL_BUCKETS - max_exact)).astype(np.int32)
    large = np.minimum(large, REL_BUCKETS - 1)
    return np.where(n < max_exact, n, large).astype(np.int32)


def _by_kv_parity(a):
    rest = a.shape[2:]
    a = a.reshape((N_KV_HEADS, ATT_PAIRS, 2, ATT_BLOCK) + rest)
    a = jnp.moveaxis(a, 2, 1)
    return a.reshape((N_KV_HEADS, 2, ATT_PAIRS * ATT_BLOCK) + rest)


def _attn_tables(rel_bias):
    t_loc = np.arange(ATT_BLOCK)[:, None]
    j_loc = np.arange(2 * ATT_BLOCK)[None, :]
    dist = ATT_BLOCK + t_loc - j_loc
    band = (dist >= 0) & (dist < WINDOW)
    onehot = jnp.asarray(np.eye(REL_BUCKETS, dtype=np.float32)[_t5_bucket(dist)])
    bias = jnp.einsum('tjb,bh->htj', onehot, rel_bias.astype(f32), precision=lax.Precision.HIGHEST)
    first = band & (j_loc >= ATT_BLOCK)
    tabs = [_by_kv_parity(jnp.where(jnp.asarray(mask)[None], bias, -jnp.inf)) for mask in (band, first)]
    return jnp.stack(tabs)


def _route_kernel(a_ref, w_ref, gate_ref, x_ref, g_ref, sh_ref, sc_ref, rw_ref, rb_ref,
                  xnew_ref, xp_ref, eidx_ref, rk_ref, gw_ref, cnt_ref, grp_scr, carry_scr):
    tm = x_ref.shape[0]
    ne = N_EXPERTS

    @pl.when(pl.program_id(0) == 0)
    def _():
        carry_scr[...] = jnp.zeros_like(carry_scr)

    x = x_ref[...] + gate_ref[...] * _dot(a_ref[...], w_ref[...])
    xnew_ref[...] = x
    xn = _norm_mod(x, g_ref[...], sh_ref[...], sc_ref[...])
    _store_row_tiles(xp_ref, _pack_halves(xn))
    x_hi = xn.astype(bf16)
    x_lo = (xn - x_hi.astype(f32)).astype(bf16)
    rw = rw_ref[...]
    a = _dot_nt(rw, x_hi)
    b = _dot_nt(rw[:ne], x_lo)
    logits = a[:ne] + a[ne:] + b
    scores = jax.nn.sigmoid(logits)
    sel = scores + rb_ref[...]

    for g in range(N_GROUPS):
        tile = sel[g * E_PER_G:(g + 1) * E_PER_G, :]
        t1 = jnp.max(tile, axis=0, keepdims=True)
        dup = jnp.sum(jnp.where(tile == t1, 1.0, 0.0), axis=0, keepdims=True) >= 2.0
        t2 = jnp.max(jnp.where(tile < t1, tile, -jnp.inf), axis=0, keepdims=True)
        grp_scr[g:g + 1, :] = t1 + jnp.where(dup, t1, t2)
    gs = grp_scr[...]

    gi = lax.broadcasted_iota(i32, (N_GROUPS, tm), 0)
    grank = jnp.zeros((N_GROUPS, tm), i32)
    for g in range(N_GROUPS):
        r = grp_scr[g:g + 1, :]
        ge = jnp.where(r >= gs, 1, 0)
        gt = jnp.where(r > gs, 1, 0)
        grank = grank + jnp.where(gi > g, ge, gt)
    grp_scr[...] = jnp.where(grank < TOPK_GROUPS, 1.0, 0.0)

    masked = jnp.concatenate(
        [jnp.where(grp_scr[g:g + 1, :] > 0.5, sel[g * E_PER_G:(g + 1) * E_PER_G, :], -jnp.inf)
         for g in range(N_GROUPS)], axis=0)

    ei = lax.broadcasted_iota(i32, (ne, tm), 0)
    cur = masked
    chf = jnp.zeros((ne, tm), f32)
    for _ in range(TOP_K):
        top = jnp.max(cur, axis=0, keepdims=True)
        first = jnp.min(jnp.where(cur == top, ei, ne), axis=0, keepdims=True)
        pick = ei == first
        chf = jnp.where(pick, 1.0, chf)
        cur = jnp.where(pick, -jnp.inf, cur)
    chosen = chf > 0.5

    wun = jnp.where(chosen, scores, 0.0)
    gwt = wun / jnp.sum(wun, axis=0, keepdims=True) * ROUTED_SCALE

    rr = lax.broadcasted_iota(i32, (tm, tm), 0)
    cc = lax.broadcasted_iota(i32, (tm, tm), 1)
    upper = jnp.where(rr < cc, 1.0, 0.0).astype(bf16)
    chb = chf.astype(bf16)
    pos = carry_scr[:, 0:1] + _dot(chb, upper)
    carry_scr[...] = carry_scr[...] + jnp.sum(chf, axis=1, keepdims=True)
    cnt_ref[...] = carry_scr[...].astype(i32)

    er = lax.broadcasted_iota(i32, (ne, ne), 0)
    ec = lax.broadcasted_iota(i32, (ne, ne), 1)
    lower = jnp.where(ec < er, 1.0, 0.0).astype(bf16)
    below = _dot(lower, chb)
    slot = jnp.where(chosen, below, -1.0)
    eif = ei.astype(f32)
    for k in range(TOP_K):
        mk = slot == float(k)
        eidx_ref[k:k + 1, :] = jnp.sum(jnp.where(mk, eif, 0.0), axis=0, keepdims=True).astype(i32)
        rk_ref[k:k + 1, :] = jnp.sum(jnp.where(mk, pos, 0.0), axis=0, keepdims=True).astype(i32)
        gw_ref[k:k + 1, :] = jnp.sum(jnp.where(mk, gwt, 0.0), axis=0, keepdims=True)


def proj_route(a, w_bf, x2, gate, g, sh, sc, rw_cat, rbias, seq):
    t, d = x2.shape
    k = a.shape[1]
    tps = seq // TR
    return pl.pallas_call(
        _route_kernel,
        grid=(t // TR,),
        in_specs=[
            pl.BlockSpec((TR, k), lambda i: (i, 0)),
            pl.BlockSpec((k, d), lambda i: (0, 0)),
            pl.BlockSpec((None, 1, d), lambda i: (i // tps, 0, 0)),
            pl.BlockSpec((TR, d), lambda i: (i, 0)),
            pl.BlockSpec((1, d), lambda i: (0, 0)),
            pl.BlockSpec((None, 1, d), lambda i: (i // tps, 0, 0)),
            pl.BlockSpec((None, 1, d), lambda i: (i // tps, 0, 0)),
            pl.BlockSpec((2 * N_EXPERTS, d), lambda i: (0, 0)),
            pl.BlockSpec((N_EXPERTS, 1), lambda i: (0, 0)),
        ],
        out_specs=[
            pl.BlockSpec((TR, d), lambda i: (i, 0)),
            pl.BlockSpec((TR * ROW_SUB, LANES), lambda i: (i, 0)),
            pl.BlockSpec((TOP_K, TR), lambda i: (0, i)),
            pl.BlockSpec((TOP_K, TR), lambda i: (0, i)),
            pl.BlockSpec((TOP_K, TR), lambda i: (0, i)),
            pl.BlockSpec((N_EXPERTS, LANES), lambda i: (0, 0)),
        ],
        out_shape=[
            jax.ShapeDtypeStruct((t, d), f32),
            jax.ShapeDtypeStruct((t * ROW_SUB, LANES), u32),
            jax.ShapeDtypeStruct((TOP_K, t), i32),
            jax.ShapeDtypeStruct((TOP_K, t), i32),
            jax.ShapeDtypeStruct((TOP_K, t), f32),
            jax.ShapeDtypeStruct((N_EXPERTS, LANES), i32),
        ],
        scratch_shapes=[
            pltpu.VMEM((N_GROUPS, TR), f32),
            pltpu.VMEM((N_EXPERTS, LANES), f32),
        ],
        compiler_params=_cparams(("arbitrary",)),
        name="proj_route",
    )(a, w_bf, gate, x2, g.reshape(1, d), sh, sc, rw_cat, rbias.reshape(N_EXPERTS, 1))


def _tile_rows(ref, r, n=1):
    return ref.at[pl.ds(pl.multiple_of(r * ROW_SUB, ROW_SUB), n * ROW_SUB), :]


def _row_copy(src, s, dst, d, sem):
    return pltpu.make_async_copy(_tile_rows(src, s), _tile_rows(dst, d), sem)


def _shared_kernel(xp_ref, sg_ref, su_ref, sd_ref, o_ref):
    o_ref[...] = _ffn_packed(xp_ref, o_ref.shape[0], sg_ref, su_ref, sd_ref)


def moe_shared(xp, sg_bf, su_bf, sd_bf):
    t = xp.shape[0] // ROW_SUB
    d = sg_bf.shape[0]
    return pl.pallas_call(
        _shared_kernel,
        grid=(t // TM,),
        in_specs=[
            pl.BlockSpec((TM * ROW_SUB, LANES), lambda i: (i, 0)),
            pl.BlockSpec((d, D_SHARED), lambda i: (0, 0)),
            pl.BlockSpec((d, D_SHARED), lambda i: (0, 0)),
            pl.BlockSpec((D_SHARED, d), lambda i: (0, 0)),
        ],
        out_specs=pl.BlockSpec((TM, d), lambda i: (i, 0)),
        out_shape=jax.ShapeDtypeStruct((t, d), f32),
        compiler_params=_cparams(("arbitrary",)),
        name="moe_shared",
    )(xp, sg_bf, su_bf, sd_bf)


SC_WINDOW = 128


def moe_invert(dest, nrows):
    k, t = dest.shape
    a = k * t
    tok = jnp.broadcast_to((jnp.arange(a, dtype=i32) % t)[:, None], (a, LANES))
    mesh = plsc.VectorSubcoreMesh(core_axis_name="core", subcore_axis_name="subcore")

    @pl.kernel(out_type=jax.ShapeDtypeStruct((nrows, LANES), i32), mesh=mesh, scratch_types=[])
    def scatter(tok_hbm, idx_hbm, inv_hbm):
        def body(tok_vmem, idx_vmem):
            pltpu.sync_copy(tok_vmem, inv_hbm.at[idx_vmem.at[0]])

        pltpu.emit_pipeline(
            body,
            grid=(a // SC_WINDOW,),
            in_specs=[pl.BlockSpec((SC_WINDOW, LANES), index_map=lambda i: (i, 0)),
                      pl.BlockSpec((1, SC_WINDOW), index_map=lambda i: (0, i))],
            out_specs=[],
            core_axis_name="subcore",
            dimension_semantics=(pltpu.PARALLEL,),
        )(tok_hbm, idx_hbm)

    return scatter(tok, dest.reshape(1, a))[:, 0]


def _ffn_packed(x_ref, m, wg, wu, wd):
    los, his = _load_row_tiles(x_ref, m)
    x = jnp.concatenate([p.astype(bf16) for p in los + his], axis=1)
    hg = _dot(x, wg[...])
    hu = _dot(x, wu[...])
    a = (jax.nn.silu(hg) * hu).astype(bf16)
    return _dot(a, wd[...])


def _gmm_kernel(be_ref, bf_ref, nu_ref, nv_ref, tok_ref, tnext_ref, xp_ref, wg_ref, wu_ref, wd_ref,
                ys_ref, wg_s, wu_s, wd_s, xa, xb, sem):
    i = pl.program_id(0)
    nu = nu_ref[0]

    def copy(tref, nvalid, buf, s, r):
        tok = jnp.where(r < nvalid, tref[0, 0, r], 0)
        return _row_copy(xp_ref, tok, buf, r, sem.at[s])

    def wait_block(buf, s):
        for r in range(BM):
            copy(tok_ref, 0, buf, s, r).wait()

    @pl.when(bf_ref[i] == 1)
    def _():
        wg_s[...] = wg_ref[...].astype(bf16)
        wu_s[...] = wu_ref[...].astype(bf16)
        wd_s[...] = wd_ref[...].astype(bf16)

    @pl.when(i == 0)
    def _():
        def body(r, c):
            copy(tok_ref, nv_ref[0], xa, 0, r).start(priority=0)
            return c
        lax.fori_loop(0, BM, body, 0)

    def step(cur, cs, nxt, ns):
        wait_block(cur, cs)
        nvn = nv_ref[jnp.minimum(i + 1, nu - 1)]
        for r in range(BM):
            copy(tnext_ref, nvn, nxt, ns, r).start(priority=r % DMA_PRIORITIES)
        _store_row_tiles(ys_ref, _pack_halves(_ffn_packed(cur, BM, wg_s, wu_s, wd_s)))

        @pl.when(i == nu - 1)
        def _():
            wait_block(nxt, ns)

    @pl.when(jnp.logical_and(i < nu, i % 2 == 0))
    def _():
        step(xa, 0, xb, 1)

    @pl.when(jnp.logical_and(i < nu, i % 2 == 1))
    def _():
        step(xb, 1, xa, 0)

    @pl.when(i >= nu)
    def _():
        ys_ref[...] = jnp.zeros_like(ys_ref)


def moe_gmm(xp, inv, blk_e, blk_first, n_used, n_valid, w_gate, w_up, w_down, layer, nblk):
    d, de = w_gate.shape[2], w_gate.shape[3]
    grid_spec = pltpu.PrefetchScalarGridSpec(
        num_scalar_prefetch=4,
        grid=(nblk,),
        in_specs=[
            pl.BlockSpec((1, 1, BM), lambda i, be, bf, nu, nv: (jnp.minimum(i, nu[0] - 1), 0, 0),
                         memory_space=pltpu.SMEM),
            pl.BlockSpec((1, 1, BM), lambda i, be, bf, nu, nv: (jnp.minimum(i + 1, nu[0] - 1), 0, 0),
                         memory_space=pltpu.SMEM),
            pl.BlockSpec(memory_space=pl.ANY),
            pl.BlockSpec((None, None, d, de), lambda i, be, bf, nu, nv: (layer, be[i], 0, 0)),
            pl.BlockSpec((None, None, d, de), lambda i, be, bf, nu, nv: (layer, be[i], 0, 0)),
            pl.BlockSpec((None, None, de, d), lambda i, be, bf, nu, nv: (layer, be[i], 0, 0)),
        ],
        out_specs=pl.BlockSpec((BM * ROW_SUB, LANES), lambda i, be, bf, nu, nv: (i, 0)),
        scratch_shapes=[
            pltpu.VMEM((d, de), bf16),
            pltpu.VMEM((d, de), bf16),
            pltpu.VMEM((de, d), bf16),
            pltpu.VMEM((BM * ROW_SUB, LANES), u32),
            pltpu.VMEM((BM * ROW_SUB, LANES), u32),
            pltpu.SemaphoreType.DMA((2,)),
        ],
    )
    return pl.pallas_call(
        _gmm_kernel,
        grid_spec=grid_spec,
        out_shape=jax.ShapeDtypeStruct((nblk * BM * ROW_SUB, LANES), u32),
        compiler_params=_cparams(("arbitrary",)),
        name="moe_gmm",
    )(blk_e, blk_first, n_used, n_valid, inv, inv, xp, w_gate, w_up, w_down)


def _combine_kernel(dest_ref, dnext_ref, x_ref, ysh_ref, gw_ref, gate_ref, *rest, tail):
    if tail == 'qkv':
        ng_ref, nsh_ref, nsc_ref, wq_ref, ys_ref, o_ref, q_ref, gath_a, gath_b, sem, xn_scr = rest
    elif tail == 'final':
        ng_ref, ys_ref, o_ref, gath_a, gath_b, sem = rest
    else:
        ys_ref, o_ref, gath_a, gath_b, sem = rest
    td = x_ref.shape[0]
    i = pl.program_id(0)

    def copy(dref, buf, s, t, k):
        return _row_copy(ys_ref, dref[k, t], buf.at[k], t, sem.at[s])

    def wait_tile(dref, buf, s):
        for t in range(td):
            for k in range(TOP_K):
                copy(dref, buf, s, t, k).wait()

    @pl.when(i == 0)
    def _():
        def body(t, c):
            for k in range(TOP_K):
                copy(dest_ref, gath_a, 0, t, k).start(priority=k % DMA_PRIORITIES)
            return c
        lax.fori_loop(0, td, body, 0)

    def step(cur_buf, cur_s, nxt_buf, nxt_s):
        wait_tile(dest_ref, cur_buf, cur_s)
        for t in range(td):
            for k in range(TOP_K):
                copy(dnext_ref, nxt_buf, nxt_s, t, k).start(priority=k % DMA_PRIORITIES)
        for rb in range(td // COMB_RB):
            rows = slice(rb * COMB_RB, (rb + 1) * COMB_RB)
            ysh = ysh_ref[rows, :]
            acc_lo = [ysh[:, c * LANES:(c + 1) * LANES] for c in range(ROW_SUB)]
            acc_hi = [ysh[:, HALF + c * LANES:HALF + (c + 1) * LANES] for c in range(ROW_SUB)]
            gw = gw_ref[rows, :]
            for k in range(TOP_K):
                wk = jnp.broadcast_to(gw[:, k:k + 1], (COMB_RB, LANES))
                for c in range(ROW_SUB):
                    word = cur_buf[k, pl.ds(rb * COMB_RB * ROW_SUB + c, COMB_RB, stride=ROW_SUB), :]
                    lo, hi = _unpack_halves(word)
                    acc_lo[c] = acc_lo[c] + wk * lo
                    acc_hi[c] = acc_hi[c] + wk * hi
            y = jnp.concatenate(acc_lo + acc_hi, axis=1)
            out = x_ref[rows, :] + gate_ref[...] * y
            if tail == 'final':
                ms = jnp.mean(out * out, axis=-1, keepdims=True)
                out = (out * lax.rsqrt(ms + RMS_EPS)) * ng_ref[...]
            o_ref[rows, :] = out
            if tail == 'qkv':
                xn_scr[rows, :] = _norm_mod(out, ng_ref[...], nsh_ref[...], nsc_ref[...]).astype(bf16)
        if tail == 'qkv':
            q_ref[...] = _dot(xn_scr[...], wq_ref[...]).astype(q_ref.dtype)

        @pl.when(i == pl.num_programs(0) - 1)
        def _():
            wait_tile(dnext_ref, nxt_buf, nxt_s)

    @pl.when(i % 2 == 0)
    def _():
        step(gath_a, 0, gath_b, 1)

    @pl.when(i % 2 == 1)
    def _():
        step(gath_b, 1, gath_a, 0)


def moe_combine(x2, ysh, ys, dest, gw_t, gate, seq, final_g=None, next_qkv=None):
    t, d = x2.shape
    tps = seq // TD
    last = t // TD - 1
    row = lambda i: (i, 0)
    per_batch = pl.BlockSpec((None, 1, d), lambda i: (i // tps, 0, 0))
    in_specs = [
        pl.BlockSpec((TOP_K, TD), lambda i: (0, i), memory_space=pltpu.SMEM),
        pl.BlockSpec((TOP_K, TD), lambda i: (0, jnp.minimum(i + 1, last)), memory_space=pltpu.SMEM),
        pl.BlockSpec((TD, d), row),
        pl.BlockSpec((TD, d), row),
        pl.BlockSpec((TD, TOP_K), row),
        per_batch,
    ]
    args = [dest, dest, x2, ysh, gw_t, gate]
    out_specs = [pl.BlockSpec((TD, d), row)]
    out_shape = [jax.ShapeDtypeStruct((t, d), f32)]
    scratch = [
        pltpu.VMEM((TOP_K, TD * ROW_SUB, LANES), u32),
        pltpu.VMEM((TOP_K, TD * ROW_SUB, LANES), u32),
        pltpu.SemaphoreType.DMA((2,)),
    ]
    tail = None
    if final_g is not None:
        tail = 'final'
        in_specs.append(pl.BlockSpec((1, d), lambda i: (0, 0)))
        args.append(final_g.reshape(1, d))
    elif next_qkv is not None:
        tail = 'qkv'
        g, sh, sc, w_bf = next_qkv
        nq = w_bf.shape[1]
        in_specs += [pl.BlockSpec((1, d), lambda i: (0, 0)), per_batch, per_batch,
                     pl.BlockSpec((d, nq), lambda i: (0, 0))]
        args += [g.reshape(1, d), sh, sc, w_bf]
        out_specs.append(pl.BlockSpec((TD, nq), row))
        out_shape.append(jax.ShapeDtypeStruct((t, nq), bf16))
        scratch.append(pltpu.VMEM((TD, d), bf16))
    in_specs.append(pl.BlockSpec(memory_space=pl.ANY))
    args.append(ys)
    outs = pl.pallas_call(
        functools.partial(_combine_kernel, tail=tail),
        grid=(t // TD,),
        in_specs=in_specs,
        out_specs=out_specs,
        out_shape=out_shape,
        scratch_shapes=scratch,
        compiler_params=_cparams(("arbitrary",)),
        name="moe_combine",
    )(*args)
    return outs if tail == 'qkv' else outs[0]


SLOT_TN = 4096


def _slots_kernel(ps_ref, eidx_ref, rk_ref, o_ref):
    e = eidx_ref[...]
    start = jnp.zeros_like(e)
    for j in range(N_EXPERTS):
        start = jnp.where(e == j, ps_ref[j], start)
    o_ref[...] = start + rk_ref[...]


def moe_slots(pad_start, eidx, rk):
    t = eidx.shape[1]
    tn = min(t, SLOT_TN)
    spec = pl.BlockSpec((TOP_K, tn), lambda i, ps: (0, i))
    grid_spec = pltpu.PrefetchScalarGridSpec(
        num_scalar_prefetch=1, grid=(t // tn,), in_specs=[spec, spec], out_specs=spec)
    return pl.pallas_call(
        _slots_kernel,
        grid_spec=grid_spec,
        out_shape=jax.ShapeDtypeStruct((TOP_K, t), i32),
        compiler_params=_cparams(("arbitrary",)),
        name="moe_slots",
    )(pad_start, eidx, rk)


def moe_layer(mix, w_out_bf, x2, gate_m, g, sh, sc, gate, router_w, router_bias, w_gate, w_up, w_down, layer,
              sh_gate, sh_up, sh_down, seq, final_g=None, next_qkv=None):
    t = x2.shape[0]
    a = t * TOP_K
    nblk = a // BM + N_EXPERTS
    rw_t = router_w.T
    rw_hi = rw_t.astype(bf16)
    rw_lo = (rw_t - rw_hi.astype(f32)).astype(bf16)
    rw_cat = jnp.concatenate([rw_hi, rw_lo], axis=0)

    x2, xp, eidx, rk, gw, cnt = proj_route(mix, w_out_bf, x2, gate_m, g, sh, sc, rw_cat, router_bias, seq)

    counts = cnt[:, 0]
    padded = ((counts + BM - 1) // BM) * BM
    cum_pad = jnp.cumsum(padded)
    pad_start = cum_pad - padded
    n_used = (cum_pad[-1] // BM).astype(i32).reshape(1)
    blk_row = jnp.arange(nblk, dtype=i32) * BM
    blk_e = jnp.minimum(jnp.sum((cum_pad[None, :] <= blk_row[:, None]).astype(i32), axis=1), N_EXPERTS - 1)
    blk_first = jnp.concatenate([jnp.ones((1,), i32), (blk_e[1:] != blk_e[:-1]).astype(i32)])
    dest = moe_slots(pad_start.astype(i32), eidx, rk)
    n_valid = jnp.clip(jnp.take(pad_start + counts, blk_e) - blk_row, 0, BM).astype(i32)

    inv = moe_invert(dest, nblk * BM).reshape(nblk, 1, BM)
    ysh = moe_shared(xp, sh_gate.astype(bf16), sh_up.astype(bf16), sh_down.astype(bf16))
    ys = moe_gmm(xp, inv, blk_e, blk_first, n_used, n_valid, w_gate, w_up, w_down, layer, nblk)
    return moe_combine(x2, ysh, ys, dest, gw.T, gate, seq, final_g=final_g, next_qkv=next_qkv)


def _qkv_weight(w_qkv):
    dq = N_Q_HEADS * HEAD_DIM
    dkv = N_KV_HEADS * HEAD_DIM
    d = w_qkv.shape[0]
    wq = w_qkv[:, :dq]
    wk = w_qkv[:, dq:dq + dkv].reshape(d, N_KV_HEADS, 1, HEAD_DIM)
    wv = w_qkv[:, dq + dkv:].reshape(d, N_KV_HEADS, 1, HEAD_DIM)
    wk2 = jnp.broadcast_to(wk, (d, N_KV_HEADS, 2, HEAD_DIM)).reshape(d, QK_COLS)
    wv2 = jnp.broadcast_to(wv, (d, N_KV_HEADS, 2, HEAD_DIM)).reshape(d, QK_COLS)
    return jnp.concatenate([wq, wk2, wv2], axis=1).astype(bf16)


def kernel(x, c, ada_w, ada_b, norm_mix_g, norm_ffn_g, hyb_w_in, conv_w, sgu_ln_g, sgu_ln_b, sgu_w, sgu_b,
           hyb_w_out, attn_w_qkv, attn_sinks, attn_w_o, rel_bias, router_w, router_bias, exp_w_gate,
           exp_w_up, exp_w_down, sh_w_gate, sh_w_up, sh_w_down, final_g):
    batch, seq, d = x.shape
    t = batch * seq
    x2 = x.reshape(t, d)
    mod = ada_mod(c, ada_w, ada_b)
    mods = [[mod[l, :, k * d:(k + 1) * d].reshape(batch, 1, d) for k in range(6)] for l in range(DEPTH)]
    qkv = None
    for l in range(DEPTH):
        sh_m, sc_m, g_m, sh_f, sc_f, g_f = mods[l]
        i = l // 2
        if l % 2 == 0:
            mix = hyb_in(x2, norm_mix_g[l], sh_m, sc_m, hyb_w_in[i].astype(bf16), conv_w[i], sgu_ln_g[i],
                         sgu_ln_b[i], sgu_w[i], sgu_b[i], seq)
            w_out = hyb_w_out[i].astype(bf16)
        else:
            if qkv is None:
                qkv = qkv_proj(x2, norm_mix_g[l], sh_m, sc_m, _qkv_weight(attn_w_qkv[i]), seq, 1024)
            mix = swa_attn(qkv, _attn_tables(rel_bias), attn_sinks[i], batch, seq)
            w_out = attn_w_o[i].astype(bf16)
        tail = {}
        if l == DEPTH - 1:
            tail = dict(final_g=final_g)
        elif (l + 1) % 2 == 1:
            tail = dict(next_qkv=(norm_mix_g[l + 1], mods[l + 1][0], mods[l + 1][1],
                                  _qkv_weight(attn_w_qkv[(l + 1) // 2])))
        res = moe_layer(mix, w_out, x2, g_m, norm_ffn_g[l], sh_f, sc_f, g_f, router_w[l], router_bias[l],
                        exp_w_gate, exp_w_up, exp_w_down, l, sh_w_gate[l], sh_w_up[l], sh_w_down[l], seq, **tail)
        x2, qkv = res if 'next_qkv' in tail else (res, None)
    return x2.reshape(batch, seq, d)
```

```python
import functools

import numpy as np
import jax
import jax.numpy as jnp
from jax import lax
from jax.experimental import pallas as pl
from jax.experimental.pallas import tpu as pltpu

f32 = jnp.float32
bf16 = jnp.bfloat16
i32 = jnp.int32
u32 = jnp.uint32

D_MODEL = 2048
DEPTH = 2
RMS_EPS = 1e-6
LN_EPS = 1e-5
D_CONV = 1024
CONV_WIDTH = 3
D_SGU = 1024
SGU_GROUPS = 8
SGU_HD = D_SGU // SGU_GROUPS
CHUNK = 128
HYB_IN = 3 * D_CONV + 2 * D_SGU
HEAD_DIM = 64
N_Q_HEADS = D_MODEL // HEAD_DIM
N_KV_HEADS = N_Q_HEADS // 8
GQA_GROUP = N_Q_HEADS // N_KV_HEADS
WINDOW = 128
ATT_BLOCK = 128
REL_BUCKETS = 32
REL_MAX_DIST = 128
N_EXPERTS = 64
TOP_K = 8
N_GROUPS = 8
TOPK_GROUPS = 4
E_PER_G = N_EXPERTS // N_GROUPS
D_EXPERT = 512
D_SHARED = 512
ROUTED_SCALE = 2.5

LANES = 128
HALF = D_MODEL // 2
VMEM_LIMIT = 56 * 1024 * 1024
TM = 512
TN_IN = 1024
TR = 512
TD = 256
BM = 512
ADA_TN = 1024
DMA_PRIORITIES = 2
COMB_RB = 16


def _cparams(sem, vmem=VMEM_LIMIT):
    return pltpu.CompilerParams(dimension_semantics=sem, vmem_limit_bytes=vmem)


def _norm_mod(x, g, sh, sc):
    ms = jnp.mean(x * x, axis=-1, keepdims=True)
    y = x * lax.rsqrt(ms + RMS_EPS)
    return (y * g) * (1.0 + sc) + sh


def _pack_halves(y):
    h = y.shape[1] // 2
    lo = lax.bitcast_convert_type(y[:, :h].astype(bf16).astype(f32), u32)
    hi = lax.bitcast_convert_type(y[:, h:].astype(bf16).astype(f32), u32)
    return (lo >> 16) | (hi & jnp.uint32(0xFFFF0000))


def _unpack_halves(w):
    lo = lax.bitcast_convert_type(w << 16, f32)
    hi = lax.bitcast_convert_type(w & jnp.uint32(0xFFFF0000), f32)
    return lo, hi


ROW_SUB = HALF // LANES


def _store_row_tiles(ref, packed):
    n = packed.shape[0]
    for c in range(ROW_SUB):
        ref[pl.ds(c, n, stride=ROW_SUB), :] = packed[:, c * LANES:(c + 1) * LANES]


def _load_row_tiles(ref, n):
    los, his = [], []
    for c in range(ROW_SUB):
        lo, hi = _unpack_halves(ref[pl.ds(c, n, stride=ROW_SUB), :])
        los.append(lo)
        his.append(hi)
    return los, his


def _gelu(x):
    return 0.5 * x * (1.0 + lax.erf(x * np.float32(np.sqrt(0.5))))


def _dot(a, b):
    return jnp.dot(a, b, preferred_element_type=f32)


def _dot_nt(a, b):
    return lax.dot_general(a, b, (((1,), (1,)), ((), ())), preferred_element_type=f32)


def _ada_kernel(c_ref, w_ref, b_ref, o_ref):
    ca = jax.nn.silu(c_ref[...]).astype(bf16)
    o_ref[...] = _dot(ca, w_ref[...].astype(bf16)) + b_ref[...]


def ada_mod(c, ada_w, ada_b):
    depth, d, n = ada_w.shape
    b = c.shape[0]
    return pl.pallas_call(
        _ada_kernel,
        grid=(depth, n // ADA_TN),
        in_specs=[
            pl.BlockSpec((b, d), lambda l, j: (0, 0)),
            pl.BlockSpec((None, d, ADA_TN), lambda l, j: (l, 0, j)),
            pl.BlockSpec((None, 1, ADA_TN), lambda l, j: (l, 0, j)),
        ],
        out_specs=pl.BlockSpec((None, b, ADA_TN), lambda l, j: (l, 0, j)),
        out_shape=jax.ShapeDtypeStruct((depth, b, n), f32),
        compiler_params=_cparams(("arbitrary", "arbitrary")),
        name="ada_mod",
    )(c, ada_w, ada_b.reshape(depth, 1, n))


def _hyb_in_kernel(x_ref, g_ref, sh_ref, sc_ref, w_ref, cw_ref, lng_ref, lnb_ref, sw_ref, sb_ref,
                   o_ref, xn_scr, a_scr, b_scr, carry_scr, *, tiles_per_seq):
    i = pl.program_id(0)
    j = pl.program_id(1)
    tm = x_ref.shape[0]

    @pl.when(j == 0)
    def _():
        xn_scr[...] = _norm_mod(x_ref[...], g_ref[...], sh_ref[...], sc_ref[...]).astype(bf16)

    @pl.when(jnp.logical_and(i == 0, j == 0))
    def _():
        carry_scr[...] = jnp.zeros_like(carry_scr)

    p = _dot(xn_scr[...], w_ref[...])

    @pl.when(j == 0)
    def _():
        a_scr[...] = p

    @pl.when(j == 1)
    def _():
        b_scr[...] = p

    @pl.when(j == 2)
    def _():
        z = b_scr[...] * p
        row = lax.broadcasted_iota(i32, z.shape, 0)
        first = (i % tiles_per_seq) == 0
        prev = jnp.where(first, 0.0, carry_scr[...])
        p1 = prev[7:8, :]
        p2 = prev[6:7, :]
        z1 = jnp.where(row == 0, p1, pltpu.roll(z, 1, 0))
        z2 = jnp.where(row == 0, p2, jnp.where(row == 1, p1, pltpu.roll(z, 2, 0)))
        cw = cw_ref[...]
        conv = cw[0:1, :] * z2 + cw[1:2, :] * z1 + cw[2:3, :] * z
        carry_scr[...] = z[tm - 8:, :]
        o_ref[:, :D_CONV] = (a_scr[...] * conv).astype(o_ref.dtype)

    @pl.when(j == 3)
    def _():
        a_scr[...] = _gelu(p)

    @pl.when(j == 4)
    def _():
        v = _gelu(p)
        mu = jnp.mean(v, axis=-1, keepdims=True)
        vc = v - mu
        var = jnp.mean(vc * vc, axis=-1, keepdims=True)
        v = (vc * lax.rsqrt(var + LN_EPS)) * lng_ref[...] + lnb_ref[...]
        vb = v.astype(bf16)
        nch = tm // CHUNK
        r = lax.broadcasted_iota(i32, (CHUNK, CHUNK), 0)
        c = lax.broadcasted_iota(i32, (CHUNK, CHUNK), 1)
        tril = r >= c
        for g in range(SGU_GROUPS):
            ws = jnp.where(tril, sw_ref[g], 0.0).astype(bf16)
            vg = jnp.concatenate(
                [vb[n * CHUNK:(n + 1) * CHUNK, g * SGU_HD:(g + 1) * SGU_HD] for n in range(nch)], axis=1)
            sg = _dot(ws, vg)
            bb = sb_ref[g]
            for n in range(nch):
                s = sg[:, n * SGU_HD:(n + 1) * SGU_HD] + bb
                u = a_scr[n * CHUNK:(n + 1) * CHUNK, g * SGU_HD:(g + 1) * SGU_HD]
                o_ref[n * CHUNK:(n + 1) * CHUNK, D_CONV + g * SGU_HD:D_CONV + (g + 1) * SGU_HD] = (
                    (u * s).astype(o_ref.dtype))


def hyb_in(x2, g, sh, sc, w_in_bf, conv_w, ln_g, ln_b, sgu_w, sgu_b, seq):
    t, d = x2.shape
    tps = seq // TM
    nj = HYB_IN // TN_IN
    sbb = jnp.broadcast_to(sgu_b[:, :, None], (SGU_GROUPS, CHUNK, SGU_HD))
    kern = functools.partial(_hyb_in_kernel, tiles_per_seq=tps)
    return pl.pallas_call(
        kern,
        grid=(t // TM, nj),
        in_specs=[
            pl.BlockSpec((TM, d), lambda i, j: (i, 0)),
            pl.BlockSpec((1, d), lambda i, j: (0, 0)),
            pl.BlockSpec((None, 1, d), lambda i, j: (i // tps, 0, 0)),
            pl.BlockSpec((None, 1, d), lambda i, j: (i // tps, 0, 0)),
            pl.BlockSpec((d, TN_IN), lambda i, j: (0, j)),
            pl.BlockSpec((CONV_WIDTH, D_CONV), lambda i, j: (0, 0)),
            pl.BlockSpec((1, D_SGU), lambda i, j: (0, 0)),
            pl.BlockSpec((1, D_SGU), lambda i, j: (0, 0)),
            pl.BlockSpec((SGU_GROUPS, CHUNK, CHUNK), lambda i, j: (0, 0, 0)),
            pl.BlockSpec((SGU_GROUPS, CHUNK, SGU_HD), lambda i, j: (0, 0, 0)),
        ],
        out_specs=pl.BlockSpec((TM, D_CONV + D_SGU), lambda i, j: (i, 0)),
        out_shape=jax.ShapeDtypeStruct((t, D_CONV + D_SGU), bf16),
        scratch_shapes=[
            pltpu.VMEM((TM, d), bf16),
            pltpu.VMEM((TM, TN_IN), f32),
            pltpu.VMEM((TM, TN_IN), f32),
            pltpu.VMEM((8, D_CONV), f32),
        ],
        compiler_params=_cparams(("arbitrary", "arbitrary")),
        name="hyb_in",
    )(x2, g.reshape(1, d), sh, sc, w_in_bf, conv_w, ln_g.reshape(1, -1), ln_b.reshape(1, -1), sgu_w, sbb)


QK_COLS = N_KV_HEADS * 2 * HEAD_DIM


def _attn_kernel(q_ref, kp_ref, kc_ref, vp_ref, vc_ref, bias_ref, sink_ref, o_ref):
    lane = lax.broadcasted_iota(i32, (2 * ATT_BLOCK, 2 * HEAD_DIM), 1)
    low = lane < HEAD_DIM
    zero = jnp.zeros((), bf16)
    scale = jnp.asarray(HEAD_DIM ** -0.5, bf16)
    ones = jnp.ones((2 * ATT_BLOCK, 2 * HEAD_DIM), bf16)
    for kh in range(N_KV_HEADS):
        cs = slice(kh * 2 * HEAD_DIM, (kh + 1) * 2 * HEAD_DIM)
        kk = jnp.concatenate([kp_ref[:, cs], kc_ref[:, cs]], axis=0)
        vv = jnp.concatenate([vp_ref[:, cs], vc_ref[:, cs]], axis=0)
        kz = (jnp.where(low, kk, zero), jnp.where(low, zero, kk))
        vz = (jnp.where(low, vv, zero), jnp.where(low, zero, vv))
        q0 = kh * GQA_GROUP * HEAD_DIM
        qs = jnp.concatenate(
            [q_ref[:, q0 + pr * 2 * HEAD_DIM:q0 + (pr + 1) * 2 * HEAD_DIM] for pr in range(ATT_PAIRS)],
            axis=0) * scale
        acc = None
        for par in range(2):
            s = _dot_nt(qs, kz[par]) + bias_ref[kh, par]
            sk = jnp.concatenate(
                [jnp.full((ATT_BLOCK, ATT_BLOCK), sink_ref[kh * GQA_GROUP + 2 * pr + par], f32)
                 for pr in range(ATT_PAIRS)], axis=0)
            rm = jnp.max(s, axis=-1, keepdims=True)
            mb = jnp.maximum(jnp.broadcast_to(rm, sk.shape), sk)
            p = jnp.concatenate([jnp.exp(s[:, :ATT_BLOCK] - mb), jnp.exp(s[:, ATT_BLOCK:] - mb)], axis=1)
            ov = _dot(p.astype(bf16), jnp.concatenate([vz[par], ones], axis=1))
            den = ov[:, 2 * HEAD_DIM:] + jnp.exp(sk - mb)
            o = ov[:, :2 * HEAD_DIM] / den
            acc = o if acc is None else acc + o
        for pr in range(ATT_PAIRS):
            o_ref[:, q0 + pr * 2 * HEAD_DIM:q0 + (pr + 1) * 2 * HEAD_DIM] = (
                acc[pr * ATT_BLOCK:(pr + 1) * ATT_BLOCK, :].astype(o_ref.dtype))


ATT_PAIRS = GQA_GROUP // 2


def swa_attn(qkv, bias_m, sinks, batch, seq):
    t = qkv.shape[0]
    nb = seq // ATT_BLOCK
    dq = N_Q_HEADS * HEAD_DIM
    kcol = dq // QK_COLS
    vcol = kcol + 1

    def prev(b, i):
        return b * nb + jnp.maximum(i - 1, 0)

    return pl.pallas_call(
        _attn_kernel,
        grid=(batch, nb),
        in_specs=[
            pl.BlockSpec((ATT_BLOCK, dq), lambda b, i: (b * nb + i, 0)),
            pl.BlockSpec((ATT_BLOCK, QK_COLS), lambda b, i: (prev(b, i), kcol)),
            pl.BlockSpec((ATT_BLOCK, QK_COLS), lambda b, i: (b * nb + i, kcol)),
            pl.BlockSpec((ATT_BLOCK, QK_COLS), lambda b, i: (prev(b, i), vcol)),
            pl.BlockSpec((ATT_BLOCK, QK_COLS), lambda b, i: (b * nb + i, vcol)),
            pl.BlockSpec((None, N_KV_HEADS, 2, ATT_PAIRS * ATT_BLOCK, 2 * ATT_BLOCK),
                         lambda b, i: (jnp.where(i == 0, 1, 0), 0, 0, 0, 0)),
            pl.BlockSpec(memory_space=pltpu.SMEM),
        ],
        out_specs=pl.BlockSpec((ATT_BLOCK, dq), lambda b, i: (b * nb + i, 0)),
        out_shape=jax.ShapeDtypeStruct((t, dq), bf16),
        compiler_params=_cparams(("arbitrary", "arbitrary")),
        name="swa_attn",
    )(qkv, qkv, qkv, qkv, qkv, bias_m, sinks)


def _t5_bucket(dist):
    n = np.maximum(dist, 0)
    max_exact = REL_BUCKETS // 2
    large = max_exact + (np.log(np.maximum(n, 1) / max_exact) / np.log(REL_MAX_DIST / max_exact)
                         * (REL_BUCKETS - max_exact)).astype(np.int32)
    large = np.minimum(large, REL_BUCKETS - 1)
    return np.where(n < max_exact, n, large).astype(np.int32)


def _by_kv_parity(a):
    rest = a.shape[2:]
    a = a.reshape((N_KV_HEADS, ATT_PAIRS, 2, ATT_BLOCK) + rest)
    a = jnp.moveaxis(a, 2, 1)
    return a.reshape((N_KV_HEADS, 2, ATT_PAIRS * ATT_BLOCK) + rest)


def _attn_tables(rel_bias):
    t_loc = np.arange(ATT_BLOCK)[:, None]
    j_loc = np.arange(2 * ATT_BLOCK)[None, :]
    dist = ATT_BLOCK + t_loc - j_loc
    band = (dist >= 0) & (dist < WINDOW)
    onehot = jnp.asarray(np.eye(REL_BUCKETS, dtype=np.float32)[_t5_bucket(dist)])
    bias = jnp.einsum('tjb,bh->htj', onehot, rel_bias.astype(f32), precision=lax.Precision.HIGHEST)
    first = band & (j_loc >= ATT_BLOCK)
    tabs = [_by_kv_parity(jnp.where(jnp.asarray(mask)[None], bias, -jnp.inf)) for mask in (band, first)]
    return jnp.stack(tabs)


def _route_kernel(a_ref, w_ref, gate_ref, x_ref, g_ref, sh_ref, sc_ref, rw_ref, rb_ref,
                  xnew_ref, xp_ref, eidx_ref, rk_ref, gw_ref, cnt_ref, grp_scr, carry_scr):
    tm = x_ref.shape[0]
    ne = N_EXPERTS

    @pl.when(pl.program_id(0) == 0)
    def _():
        carry_scr[...] = jnp.zeros_like(carry_scr)

    x = x_ref[...] + gate_ref[...] * _dot(a_ref[...], w_ref[...])
    xnew_ref[...] = x
    xn = _norm_mod(x, g_ref[...], sh_ref[...], sc_ref[...])
    _store_row_tiles(xp_ref, _pack_halves(xn))
    x_hi = xn.astype(bf16)
    x_lo = (xn - x_hi.astype(f32)).astype(bf16)
    rw = rw_ref[...]
    a = _dot_nt(rw, x_hi)
    b = _dot_nt(rw[:ne], x_lo)
    logits = a[:ne] + a[ne:] + b
    scores = jax.nn.sigmoid(logits)
    sel = scores + rb_ref[...]

    for g in range(N_GROUPS):
        tile = sel[g * E_PER_G:(g + 1) * E_PER_G, :]
        t1 = jnp.max(tile, axis=0, keepdims=True)
        dup = jnp.sum(jnp.where(tile == t1, 1.0, 0.0), axis=0, keepdims=True) >= 2.0
        t2 = jnp.max(jnp.where(tile < t1, tile, -jnp.inf), axis=0, keepdims=True)
        grp_scr[g:g + 1, :] = t1 + jnp.where(dup, t1, t2)
    gs = grp_scr[...]

    gi = lax.broadcasted_iota(i32, (N_GROUPS, tm), 0)
    grank = jnp.zeros((N_GROUPS, tm), i32)
    for g in range(N_GROUPS):
        r = grp_scr[g:g + 1, :]
        ge = jnp.where(r >= gs, 1, 0)
        gt = jnp.where(r > gs, 1, 0)
        grank = grank + jnp.where(gi > g, ge, gt)
    grp_scr[...] = jnp.where(grank < TOPK_GROUPS, 1.0, 0.0)

    masked = jnp.concatenate(
        [jnp.where(grp_scr[g:g + 1, :] > 0.5, sel[g * E_PER_G:(g + 1) * E_PER_G, :], -jnp.inf)
         for g in range(N_GROUPS)], axis=0)

    ei = lax.broadcasted_iota(i32, (ne, tm), 0)
    cur = masked
    chf = jnp.zeros((ne, tm), f32)
    for _ in range(TOP_K):
        top = jnp.max(cur, axis=0, keepdims=True)
        first = jnp.min(jnp.where(cur == top, ei, ne), axis=0, keepdims=True)
        pick = ei == first
        chf = jnp.where(pick, 1.0, chf)
        cur = jnp.where(pick, -jnp.inf, cur)
    chosen = chf > 0.5

    wun = jnp.where(chosen, scores, 0.0)
    gwt = wun / jnp.sum(wun, axis=0, keepdims=True) * ROUTED_SCALE

    rr = lax.broadcasted_iota(i32, (tm, tm), 0)
    cc = lax.broadcasted_iota(i32, (tm, tm), 1)
    upper = jnp.where(rr < cc, 1.0, 0.0).astype(bf16)
    chb = chf.astype(bf16)
    pos = carry_scr[:, 0:1] + _dot(chb, upper)
    carry_scr[...] = carry_scr[...] + jnp.sum(chf, axis=1, keepdims=True)
    cnt_ref[...] = carry_scr[...].astype(i32)

    er = lax.broadcasted_iota(i32, (ne, ne), 0)
    ec = lax.broadcasted_iota(i32, (ne, ne), 1)
    lower = jnp.where(ec < er, 1.0, 0.0).astype(bf16)
    below = _dot(lower, chb)
    slot = jnp.where(chosen, below, -1.0)
    eif = ei.astype(f32)
    for k in range(TOP_K):
        mk = slot == float(k)
        eidx_ref[k:k + 1, :] = jnp.sum(jnp.where(mk, eif, 0.0), axis=0, keepdims=True).astype(i32)
        rk_ref[k:k + 1, :] = jnp.sum(jnp.where(mk, pos, 0.0), axis=0, keepdims=True).astype(i32)
        gw_ref[k:k + 1, :] = jnp.sum(jnp.where(mk, gwt, 0.0), axis=0, keepdims=True)


def proj_route(a, w_bf, x2, gate, g, sh, sc, rw_cat, rbias, seq):
    t, d = x2.shape
    k = a.shape[1]
    tps = seq // TR
    return pl.pallas_call(
        _route_kernel,
        grid=(t // TR,),
        in_specs=[
            pl.BlockSpec((TR, k), lambda i: (i, 0)),
            pl.BlockSpec((k, d), lambda i: (0, 0)),
            pl.BlockSpec((None, 1, d), lambda i: (i // tps, 0, 0)),
            pl.BlockSpec((TR, d), lambda i: (i, 0)),
            pl.BlockSpec((1, d), lambda i: (0, 0)),
            pl.BlockSpec((None, 1, d), lambda i: (i // tps, 0, 0)),
            pl.BlockSpec((None, 1, d), lambda i: (i // tps, 0, 0)),
            pl.BlockSpec((2 * N_EXPERTS, d), lambda i: (0, 0)),
            pl.BlockSpec((N_EXPERTS, 1), lambda i: (0, 0)),
        ],
        out_specs=[
            pl.BlockSpec((TR, d), lambda i: (i, 0)),
            pl.BlockSpec((TR * ROW_SUB, LANES), lambda i: (i, 0)),
            pl.BlockSpec((TOP_K, TR), lambda i: (0, i)),
            pl.BlockSpec((TOP_K, TR), lambda i: (0, i)),
            pl.BlockSpec((TOP_K, TR), lambda i: (0, i)),
            pl.BlockSpec((N_EXPERTS, LANES), lambda i: (0, 0)),
        ],
        out_shape=[
            jax.ShapeDtypeStruct((t, d), f32),
            jax.ShapeDtypeStruct((t * ROW_SUB, LANES), u32),
            jax.ShapeDtypeStruct((TOP_K, t), i32),
            jax.ShapeDtypeStruct((TOP_K, t), i32),
            jax.ShapeDtypeStruct((TOP_K, t), f32),
            jax.ShapeDtypeStruct((N_EXPERTS, LANES), i32),
        ],
        scratch_shapes=[
            pltpu.VMEM((N_GROUPS, TR), f32),
            pltpu.VMEM((N_EXPERTS, LANES), f32),
        ],
        compiler_params=_cparams(("arbitrary",)),
        name="proj_route",
    )(a, w_bf, gate, x2, g.reshape(1, d), sh, sc, rw_cat, rbias.reshape(N_EXPERTS, 1))


def _tile_rows(ref, r, n=1):
    return ref.at[pl.ds(pl.multiple_of(r * ROW_SUB, ROW_SUB), n * ROW_SUB), :]


def _row_copy(src, s, dst, d, sem):
    return pltpu.make_async_copy(_tile_rows(src, s), _tile_rows(dst, d), sem)


def _dispatch_kernel(ztail_ref, dest_ref, xp_ref, sg_ref, su_ref, sd_ref, xs_ref, ysh_ref, zeros_scr, sem, zsem):
    td = xp_ref.shape[0] // ROW_SUB

    @pl.when(pl.program_id(0) == 0)
    def _():
        zeros_scr[...] = jnp.zeros_like(zeros_scr)

        def zcopy(e):
            return pltpu.make_async_copy(zeros_scr, _tile_rows(xs_ref, ztail_ref[e], BM), zsem)

        def zstart(e, c):
            zcopy(e).start()
            return c

        def zwait(e, c):
            zcopy(e).wait()
            return c

        lax.fori_loop(0, N_EXPERTS, zstart, 0)
        lax.fori_loop(0, N_EXPERTS, zwait, 0)

    for t in range(td):
        for k in range(TOP_K):
            _row_copy(xp_ref, t, xs_ref, dest_ref[k, t], sem).start(priority=k % DMA_PRIORITIES)
    ysh_ref[...] = _ffn_packed(xp_ref, td, sg_ref, su_ref, sd_ref)
    for t in range(td):
        for k in range(TOP_K):
            _row_copy(xp_ref, t, xs_ref, dest_ref[k, t], sem).wait()


def moe_dispatch(xp, dest, ztail, nrows, sg_bf, su_bf, sd_bf):
    t = xp.shape[0] // ROW_SUB
    d = sg_bf.shape[0]
    grid_spec = pltpu.PrefetchScalarGridSpec(
        num_scalar_prefetch=1,
        grid=(t // TD,),
        in_specs=[
            pl.BlockSpec((TOP_K, TD), lambda i, z: (0, i), memory_space=pltpu.SMEM),
            pl.BlockSpec((TD * ROW_SUB, LANES), lambda i, z: (i, 0)),
            pl.BlockSpec((d, D_SHARED), lambda i, z: (0, 0)),
            pl.BlockSpec((d, D_SHARED), lambda i, z: (0, 0)),
            pl.BlockSpec((D_SHARED, d), lambda i, z: (0, 0)),
        ],
        out_specs=[pl.BlockSpec(memory_space=pl.ANY), pl.BlockSpec((TD, d), lambda i, z: (i, 0))],
        scratch_shapes=[
            pltpu.VMEM((BM * ROW_SUB, LANES), u32),
            pltpu.SemaphoreType.DMA(()),
            pltpu.SemaphoreType.DMA(()),
        ],
    )
    return pl.pallas_call(
        _dispatch_kernel,
        grid_spec=grid_spec,
        out_shape=[jax.ShapeDtypeStruct((nrows * ROW_SUB, LANES), u32), jax.ShapeDtypeStruct((t, d), f32)],
        compiler_params=_cparams(("arbitrary",)),
        name="moe_dispatch",
    )(ztail, dest, xp, sg_bf, su_bf, sd_bf)


def _ffn_packed(x_ref, m, wg, wu, wd):
    los, his = _load_row_tiles(x_ref, m)
    x = jnp.concatenate([p.astype(bf16) for p in los + his], axis=1)
    hg = _dot(x, wg[...])
    hu = _dot(x, wu[...])
    a = (jax.nn.silu(hg) * hu).astype(bf16)
    return _dot(a, wd[...])


def _gmm_kernel(be_ref, bf_ref, nu_ref, nv_ref, xs_ref, wg_ref, wu_ref, wd_ref, ys_ref, wg_s, wu_s, wd_s):
    i = pl.program_id(0)
    used = i < nu_ref[0]
    half = BM // 2

    @pl.when(bf_ref[i] == 1)
    def _():
        wg_s[...] = wg_ref[...].astype(bf16)
        wu_s[...] = wu_ref[...].astype(bf16)
        wd_s[...] = wd_ref[...].astype(bf16)

    @pl.when(jnp.logical_and(used, nv_ref[i] > half))
    def _():
        _store_row_tiles(ys_ref, _pack_halves(_ffn_packed(xs_ref, BM, wg_s, wu_s, wd_s)))

    @pl.when(jnp.logical_and(used, nv_ref[i] <= half))
    def _():
        _store_row_tiles(ys_ref, _pack_halves(_ffn_packed(xs_ref, half, wg_s, wu_s, wd_s)))
        ys_ref[half * ROW_SUB:, :] = jnp.zeros((half * ROW_SUB, LANES), u32)

    @pl.when(jnp.logical_not(used))
    def _():
        ys_ref[...] = jnp.zeros_like(ys_ref)


def moe_gmm(xs, blk_e, blk_first, n_used, n_valid, w_gate, w_up, w_down, layer, nblk):
    d, de = w_gate.shape[2], w_gate.shape[3]
    grid_spec = pltpu.PrefetchScalarGridSpec(
        num_scalar_prefetch=4,
        grid=(nblk,),
        in_specs=[
            pl.BlockSpec((BM * ROW_SUB, LANES), lambda i, be, bf, nu, nv: (jnp.minimum(i, nu[0] - 1), 0)),
            pl.BlockSpec((None, None, d, de), lambda i, be, bf, nu, nv: (layer, be[i], 0, 0)),
            pl.BlockSpec((None, None, d, de), lambda i, be, bf, nu, nv: (layer, be[i], 0, 0)),
            pl.BlockSpec((None, None, de, d), lambda i, be, bf, nu, nv: (layer, be[i], 0, 0)),
        ],
        out_specs=pl.BlockSpec((BM * ROW_SUB, LANES), lambda i, be, bf, nu, nv: (i, 0)),
        scratch_shapes=[
            pltpu.VMEM((d, de), bf16),
            pltpu.VMEM((d, de), bf16),
            pltpu.VMEM((de, d), bf16),
        ],
    )
    return pl.pallas_call(
        _gmm_kernel,
        grid_spec=grid_spec,
        out_shape=jax.ShapeDtypeStruct((nblk * BM * ROW_SUB, LANES), u32),
        compiler_params=_cparams(("arbitrary",)),
        name="moe_gmm",
    )(blk_e, blk_first, n_used, n_valid, xs, w_gate, w_up, w_down)


def _combine_kernel(dest_ref, dnext_ref, x_ref, ysh_ref, gw_ref, gate_ref, *rest, tail):
    if tail == 'qkv':
        ng_ref, nsh_ref, nsc_ref, wq_ref, ys_ref, o_ref, q_ref, gath_a, gath_b, sem, xn_scr = rest
    elif tail == 'final':
        ng_ref, ys_ref, o_ref, gath_a, gath_b, sem = rest
    else:
        ys_ref, o_ref, gath_a, gath_b, sem = rest
    td = x_ref.shape[0]
    i = pl.program_id(0)

    def copy(dref, buf, s, t, k):
        return _row_copy(ys_ref, dref[k, t], buf.at[k], t, sem.at[s])

    def wait_tile(dref, buf, s):
        for t in range(td):
            for k in range(TOP_K):
                copy(dref, buf, s, t, k).wait()

    @pl.when(i == 0)
    def _():
        def body(t, c):
            for k in range(TOP_K):
                copy(dest_ref, gath_a, 0, t, k).start(priority=k % DMA_PRIORITIES)
            return c
        lax.fori_loop(0, td, body, 0)

    def step(cur_buf, cur_s, nxt_buf, nxt_s):
        wait_tile(dest_ref, cur_buf, cur_s)
        for t in range(td):
            for k in range(TOP_K):
                copy(dnext_ref, nxt_buf, nxt_s, t, k).start(priority=k % DMA_PRIORITIES)
        for rb in range(td // COMB_RB):
            rows = slice(rb * COMB_RB, (rb + 1) * COMB_RB)
            ysh = ysh_ref[rows, :]
            acc_lo = [ysh[:, c * LANES:(c + 1) * LANES] for c in range(ROW_SUB)]
            acc_hi = [ysh[:, HALF + c * LANES:HALF + (c + 1) * LANES] for c in range(ROW_SUB)]
            gw = gw_ref[rows, :]
            for k in range(TOP_K):
                wk = jnp.broadcast_to(gw[:, k:k + 1], (COMB_RB, LANES))
                for c in range(ROW_SUB):
                    word = cur_buf[k, pl.ds(rb * COMB_RB * ROW_SUB + c, COMB_RB, stride=ROW_SUB), :]
                    lo, hi = _unpack_halves(word)
                    acc_lo[c] = acc_lo[c] + wk * lo
                    acc_hi[c] = acc_hi[c] + wk * hi
            y = jnp.concatenate(acc_lo + acc_hi, axis=1)
            out = x_ref[rows, :] + gate_ref[...] * y
            if tail == 'final':
                ms = jnp.mean(out * out, axis=-1, keepdims=True)
                out = (out * lax.rsqrt(ms + RMS_EPS)) * ng_ref[...]
            o_ref[rows, :] = out
            if tail == 'qkv':
                xn_scr[rows, :] = _norm_mod(out, ng_ref[...], nsh_ref[...], nsc_ref[...]).astype(bf16)
        if tail == 'qkv':
            q_ref[...] = _dot(xn_scr[...], wq_ref[...]).astype(q_ref.dtype)

        @pl.when(i == pl.num_programs(0) - 1)
        def _():
            wait_tile(dnext_ref, nxt_buf, nxt_s)

    @pl.when(i % 2 == 0)
    def _():
        step(gath_a, 0, gath_b, 1)

    @pl.when(i % 2 == 1)
    def _():
        step(gath_b, 1, gath_a, 0)


def moe_combine(x2, ysh, ys, dest, gw_t, gate, seq, final_g=None, next_qkv=None):
    t, d = x2.shape
    tps = seq // TD
    last = t // TD - 1
    row = lambda i: (i, 0)
    per_batch = pl.BlockSpec((None, 1, d), lambda i: (i // tps, 0, 0))
    in_specs = [
        pl.BlockSpec((TOP_K, TD), lambda i: (0, i), memory_space=pltpu.SMEM),
        pl.BlockSpec((TOP_K, TD), lambda i: (0, jnp.minimum(i + 1, last)), memory_space=pltpu.SMEM),
        pl.BlockSpec((TD, d), row),
        pl.BlockSpec((TD, d), row),
        pl.BlockSpec((TD, TOP_K), row),
        per_batch,
    ]
    args = [dest, dest, x2, ysh, gw_t, gate]
    out_specs = [pl.BlockSpec((TD, d), row)]
    out_shape = [jax.ShapeDtypeStruct((t, d), f32)]
    scratch = [
        pltpu.VMEM((TOP_K, TD * ROW_SUB, LANES), u32),
        pltpu.VMEM((TOP_K, TD * ROW_SUB, LANES), u32),
        pltpu.SemaphoreType.DMA((2,)),
    ]
    tail = None
    if final_g is not None:
        tail = 'final'
        in_specs.append(pl.BlockSpec((1, d), lambda i: (0, 0)))
        args.append(final_g.reshape(1, d))
    elif next_qkv is not None:
        tail = 'qkv'
        g, sh, sc, w_bf = next_qkv
        nq = w_bf.shape[1]
        in_specs += [pl.BlockSpec((1, d), lambda i: (0, 0)), per_batch, per_batch,
                     pl.BlockSpec((d, nq), lambda i: (0, 0))]
        args += [g.reshape(1, d), sh, sc, w_bf]
        out_specs.append(pl.BlockSpec((TD, nq), row))
        out_shape.append(jax.ShapeDtypeStruct((t, nq), bf16))
        scratch.append(pltpu.VMEM((TD, d), bf16))
    in_specs.append(pl.BlockSpec(memory_space=pl.ANY))
    args.append(ys)
    outs = pl.pallas_call(
        functools.partial(_combine_kernel, tail=tail),
        grid=(t // TD,),
        in_specs=in_specs,
        out_specs=out_specs,
        out_shape=out_shape,
        scratch_shapes=scratch,
        compiler_params=_cparams(("arbitrary",)),
        name="moe_combine",
    )(*args)
    return outs if tail == 'qkv' else outs[0]


SLOT_TN = 4096


def _slots_kernel(ps_ref, eidx_ref, rk_ref, o_ref):
    e = eidx_ref[...]
    start = jnp.zeros_like(e)
    for j in range(N_EXPERTS):
        start = jnp.where(e == j, ps_ref[j], start)
    o_ref[...] = start + rk_ref[...]


def moe_slots(pad_start, eidx, rk):
    t = eidx.shape[1]
    tn = min(t, SLOT_TN)
    spec = pl.BlockSpec((TOP_K, tn), lambda i, ps: (0, i))
    grid_spec = pltpu.PrefetchScalarGridSpec(
        num_scalar_prefetch=1, grid=(t // tn,), in_specs=[spec, spec], out_specs=spec)
    return pl.pallas_call(
        _slots_kernel,
        grid_spec=grid_spec,
        out_shape=jax.ShapeDtypeStruct((TOP_K, t), i32),
        compiler_params=_cparams(("arbitrary",)),
        name="moe_slots",
    )(pad_start, eidx, rk)


def moe_layer(mix, w_out_bf, x2, gate_m, g, sh, sc, gate, router_w, router_bias, w_gate, w_up, w_down, layer,
              sh_gate, sh_up, sh_down, seq, final_g=None, next_qkv=None):
    t = x2.shape[0]
    a = t * TOP_K
    nblk = a // BM + N_EXPERTS
    rw_t = router_w.T
    rw_hi = rw_t.astype(bf16)
    rw_lo = (rw_t - rw_hi.astype(f32)).astype(bf16)
    rw_cat = jnp.concatenate([rw_hi, rw_lo], axis=0)

    x2, xp, eidx, rk, gw, cnt = proj_route(mix, w_out_bf, x2, gate_m, g, sh, sc, rw_cat, router_bias, seq)

    counts = cnt[:, 0]
    padded = ((counts + BM - 1) // BM) * BM
    cum_pad = jnp.cumsum(padded)
    pad_start = cum_pad - padded
    n_used = (cum_pad[-1] // BM).astype(i32).reshape(1)
    blk_row = jnp.arange(nblk, dtype=i32) * BM
    blk_e = jnp.minimum(jnp.sum((cum_pad[None, :] <= blk_row[:, None]).astype(i32), axis=1), N_EXPERTS - 1)
    blk_first = jnp.concatenate([jnp.ones((1,), i32), (blk_e[1:] != blk_e[:-1]).astype(i32)])
    dest = moe_slots(pad_start.astype(i32), eidx, rk)
    ztail = jnp.maximum(cum_pad - BM, 0).astype(i32)
    n_valid = jnp.clip(jnp.take(pad_start + counts, blk_e) - blk_row, 0, BM).astype(i32)

    xs, ysh = moe_dispatch(xp, dest, ztail, nblk * BM, sh_gate.astype(bf16), sh_up.astype(bf16),
                           sh_down.astype(bf16))
    ys = moe_gmm(xs, blk_e, blk_first, n_used, n_valid, w_gate, w_up, w_down, layer, nblk)
    return moe_combine(x2, ysh, ys, dest, gw.T, gate, seq, final_g=final_g, next_qkv=next_qkv)


def _qkv_weight(w_qkv):
    dq = N_Q_HEADS * HEAD_DIM
    dkv = N_KV_HEADS * HEAD_DIM
    d = w_qkv.shape[0]
    wq = w_qkv[:, :dq]
    wk = w_qkv[:, dq:dq + dkv].reshape(d, N_KV_HEADS, 1, HEAD_DIM)
    wv = w_qkv[:, dq + dkv:].reshape(d, N_KV_HEADS, 1, HEAD_DIM)
    wk2 = jnp.broadcast_to(wk, (d, N_KV_HEADS, 2, HEAD_DIM)).reshape(d, QK_COLS)
    wv2 = jnp.broadcast_to(wv, (d, N_KV_HEADS, 2, HEAD_DIM)).reshape(d, QK_COLS)
    return jnp.concatenate([wq, wk2, wv2], axis=1).astype(bf16)


def kernel(x, c, ada_w, ada_b, norm_mix_g, norm_ffn_g, hyb_w_in, conv_w, sgu_ln_g, sgu_ln_b, sgu_w, sgu_b,
           hyb_w_out, attn_w_qkv, attn_sinks, attn_w_o, rel_bias, router_w, router_bias, exp_w_gate,
           exp_w_up, exp_w_down, sh_w_gate, sh_w_up, sh_w_down, final_g):
    batch, seq, d = x.shape
    t = batch * seq
    x2 = x.reshape(t, d)
    mod = ada_mod(c, ada_w, ada_b)
    mods = [[mod[l, :, k * d:(k + 1) * d].reshape(batch, 1, d) for k in range(6)] for l in range(DEPTH)]
    qkv = None
    for l in range(DEPTH):
        sh_m, sc_m, g_m, sh_f, sc_f, g_f = mods[l]
        i = l // 2
        if l % 2 == 0:
            mix = hyb_in(x2, norm_mix_g[l], sh_m, sc_m, hyb_w_in[i].astype(bf16), conv_w[i], sgu_ln_g[i],
                         sgu_ln_b[i], sgu_w[i], sgu_b[i], seq)
            w_out = hyb_w_out[i].astype(bf16)
        else:
            mix = swa_attn(qkv, _attn_tables(rel_bias), attn_sinks[i], batch, seq)
            w_out = attn_w_o[i].astype(bf16)
        tail = {}
        if l == DEPTH - 1:
            tail = dict(final_g=final_g)
        else:
            tail = dict(next_qkv=(norm_mix_g[l + 1], mods[l + 1][0], mods[l + 1][1],
                                  _qkv_weight(attn_w_qkv[(l + 1) // 2])))
        res = moe_layer(mix, w_out, x2, g_m, norm_ffn_g[l], sh_f, sc_f, g_f, router_w[l], router_bias[l],
                        exp_w_gate, exp_w_up, exp_w_down, l, sh_w_gate[l], sh_w_up[l], sh_w_down[l], seq, **tail)
        x2, qkv = res if 'next_qkv' in tail else (res, None)
    return x2.reshape(batch, seq, d)
```

```python
import functools

import numpy as np
import jax
import jax.numpy as jnp
from jax import lax
from jax.experimental import pallas as pl
from jax.experimental.pallas import tpu as pltpu

f32 = jnp.float32
bf16 = jnp.bfloat16
i32 = jnp.int32
u32 = jnp.uint32

D_MODEL = 2048
DEPTH = 2
RMS_EPS = 1e-6
LN_EPS = 1e-5
D_CONV = 1024
CONV_WIDTH = 3
D_SGU = 1024
SGU_GROUPS = 8
SGU_HD = D_SGU // SGU_GROUPS
CHUNK = 128
HYB_IN = 3 * D_CONV + 2 * D_SGU
HEAD_DIM = 64
N_Q_HEADS = D_MODEL // HEAD_DIM
N_KV_HEADS = N_Q_HEADS // 8
GQA_GROUP = N_Q_HEADS // N_KV_HEADS
WINDOW = 128
ATT_BLOCK = 128
REL_BUCKETS = 32
REL_MAX_DIST = 128
N_EXPERTS = 64
TOP_K = 8
N_GROUPS = 8
TOPK_GROUPS = 4
E_PER_G = N_EXPERTS // N_GROUPS
D_EXPERT = 512
D_SHARED = 512
ROUTED_SCALE = 2.5

LANES = 128
HALF = D_MODEL // 2
VMEM_LIMIT = 56 * 1024 * 1024
TM = 512
TN_IN = 1024
TR = 512
TD = 256
BM = 512
ADA_TN = 1024
DMA_PRIORITIES = 2
COMB_RB = 16


def _cparams(sem, vmem=VMEM_LIMIT):
    return pltpu.CompilerParams(dimension_semantics=sem, vmem_limit_bytes=vmem)


def _norm_mod(x, g, sh, sc):
    ms = jnp.mean(x * x, axis=-1, keepdims=True)
    y = x * lax.rsqrt(ms + RMS_EPS)
    return (y * g) * (1.0 + sc) + sh


def _pack_halves(y):
    h = y.shape[1] // 2
    lo = lax.bitcast_convert_type(y[:, :h].astype(bf16).astype(f32), u32)
    hi = lax.bitcast_convert_type(y[:, h:].astype(bf16).astype(f32), u32)
    return (lo >> 16) | (hi & jnp.uint32(0xFFFF0000))


def _unpack_halves(w):
    lo = lax.bitcast_convert_type(w << 16, f32)
    hi = lax.bitcast_convert_type(w & jnp.uint32(0xFFFF0000), f32)
    return lo, hi


ROW_SUB = HALF // LANES


def _store_row_tiles(ref, packed):
    n = packed.shape[0]
    for c in range(ROW_SUB):
        ref[pl.ds(c, n, stride=ROW_SUB), :] = packed[:, c * LANES:(c + 1) * LANES]


def _load_row_tiles(ref, n):
    los, his = [], []
    for c in range(ROW_SUB):
        lo, hi = _unpack_halves(ref[pl.ds(c, n, stride=ROW_SUB), :])
        los.append(lo)
        his.append(hi)
    return los, his


def _gelu(x):
    return 0.5 * x * (1.0 + lax.erf(x * np.float32(np.sqrt(0.5))))


def _dot(a, b):
    return jnp.dot(a, b, preferred_element_type=f32)


def _dot_nt(a, b):
    return lax.dot_general(a, b, (((1,), (1,)), ((), ())), preferred_element_type=f32)


def _ada_kernel(c_ref, w_ref, b_ref, o_ref):
    ca = jax.nn.silu(c_ref[...]).astype(bf16)
    o_ref[...] = _dot(ca, w_ref[...].astype(bf16)) + b_ref[...]


def ada_mod(c, ada_w, ada_b):
    depth, d, n = ada_w.shape
    b = c.shape[0]
    return pl.pallas_call(
        _ada_kernel,
        grid=(depth, n // ADA_TN),
        in_specs=[
            pl.BlockSpec((b, d), lambda l, j: (0, 0)),
            pl.BlockSpec((None, d, ADA_TN), lambda l, j: (l, 0, j)),
            pl.BlockSpec((None, 1, ADA_TN), lambda l, j: (l, 0, j)),
        ],
        out_specs=pl.BlockSpec((None, b, ADA_TN), lambda l, j: (l, 0, j)),
        out_shape=jax.ShapeDtypeStruct((depth, b, n), f32),
        compiler_params=_cparams(("arbitrary", "arbitrary")),
        name="ada_mod",
    )(c, ada_w, ada_b.reshape(depth, 1, n))


def _hyb_in_kernel(x_ref, g_ref, sh_ref, sc_ref, w_ref, cw_ref, lng_ref, lnb_ref, sw_ref, sb_ref,
                   o_ref, xn_scr, a_scr, b_scr, carry_scr, *, tiles_per_seq):
    i = pl.program_id(0)
    j = pl.program_id(1)
    tm = x_ref.shape[0]

    @pl.when(j == 0)
    def _():
        xn_scr[...] = _norm_mod(x_ref[...], g_ref[...], sh_ref[...], sc_ref[...]).astype(bf16)

    @pl.when(jnp.logical_and(i == 0, j == 0))
    def _():
        carry_scr[...] = jnp.zeros_like(carry_scr)

    p = _dot(xn_scr[...], w_ref[...])

    @pl.when(j == 0)
    def _():
        a_scr[...] = p

    @pl.when(j == 1)
    def _():
        b_scr[...] = p

    @pl.when(j == 2)
    def _():
        z = b_scr[...] * p
        row = lax.broadcasted_iota(i32, z.shape, 0)
        first = (i % tiles_per_seq) == 0
        prev = jnp.where(first, 0.0, carry_scr[...])
        p1 = prev[7:8, :]
        p2 = prev[6:7, :]
        z1 = jnp.where(row == 0, p1, pltpu.roll(z, 1, 0))
        z2 = jnp.where(row == 0, p2, jnp.where(row == 1, p1, pltpu.roll(z, 2, 0)))
        cw = cw_ref[...]
        conv = cw[0:1, :] * z2 + cw[1:2, :] * z1 + cw[2:3, :] * z
        carry_scr[...] = z[tm - 8:, :]
        o_ref[:, :D_CONV] = (a_scr[...] * conv).astype(o_ref.dtype)

    @pl.when(j == 3)
    def _():
        a_scr[...] = _gelu(p)

    @pl.when(j == 4)
    def _():
        v = _gelu(p)
        mu = jnp.mean(v, axis=-1, keepdims=True)
        vc = v - mu
        var = jnp.mean(vc * vc, axis=-1, keepdims=True)
        v = (vc * lax.rsqrt(var + LN_EPS)) * lng_ref[...] + lnb_ref[...]
        vb = v.astype(bf16)
        nch = tm // CHUNK
        r = lax.broadcasted_iota(i32, (CHUNK, CHUNK), 0)
        c = lax.broadcasted_iota(i32, (CHUNK, CHUNK), 1)
        tril = r >= c
        for g in range(SGU_GROUPS):
            ws = jnp.where(tril, sw_ref[g], 0.0).astype(bf16)
            vg = jnp.concatenate(
                [vb[n * CHUNK:(n + 1) * CHUNK, g * SGU_HD:(g + 1) * SGU_HD] for n in range(nch)], axis=1)
            sg = _dot(ws, vg)
            bb = sb_ref[g]
            for n in range(nch):
                s = sg[:, n * SGU_HD:(n + 1) * SGU_HD] + bb
                u = a_scr[n * CHUNK:(n + 1) * CHUNK, g * SGU_HD:(g + 1) * SGU_HD]
                o_ref[n * CHUNK:(n + 1) * CHUNK, D_CONV + g * SGU_HD:D_CONV + (g + 1) * SGU_HD] = (
                    (u * s).astype(o_ref.dtype))


def hyb_in(x2, g, sh, sc, w_in_bf, conv_w, ln_g, ln_b, sgu_w, sgu_b, seq):
    t, d = x2.shape
    tps = seq // TM
    nj = HYB_IN // TN_IN
    sbb = jnp.broadcast_to(sgu_b[:, :, None], (SGU_GROUPS, CHUNK, SGU_HD))
    kern = functools.partial(_hyb_in_kernel, tiles_per_seq=tps)
    return pl.pallas_call(
        kern,
        grid=(t // TM, nj),
        in_specs=[
            pl.BlockSpec((TM, d), lambda i, j: (i, 0)),
            pl.BlockSpec((1, d), lambda i, j: (0, 0)),
            pl.BlockSpec((None, 1, d), lambda i, j: (i // tps, 0, 0)),
            pl.BlockSpec((None, 1, d), lambda i, j: (i // tps, 0, 0)),
            pl.BlockSpec((d, TN_IN), lambda i, j: (0, j)),
            pl.BlockSpec((CONV_WIDTH, D_CONV), lambda i, j: (0, 0)),
            pl.BlockSpec((1, D_SGU), lambda i, j: (0, 0)),
            pl.BlockSpec((1, D_SGU), lambda i, j: (0, 0)),
            pl.BlockSpec((SGU_GROUPS, CHUNK, CHUNK), lambda i, j: (0, 0, 0)),
            pl.BlockSpec((SGU_GROUPS, CHUNK, SGU_HD), lambda i, j: (0, 0, 0)),
        ],
        out_specs=pl.BlockSpec((TM, D_CONV + D_SGU), lambda i, j: (i, 0)),
        out_shape=jax.ShapeDtypeStruct((t, D_CONV + D_SGU), bf16),
        scratch_shapes=[
            pltpu.VMEM((TM, d), bf16),
            pltpu.VMEM((TM, TN_IN), f32),
            pltpu.VMEM((TM, TN_IN), f32),
            pltpu.VMEM((8, D_CONV), f32),
        ],
        compiler_params=_cparams(("arbitrary", "arbitrary")),
        name="hyb_in",
    )(x2, g.reshape(1, d), sh, sc, w_in_bf, conv_w, ln_g.reshape(1, -1), ln_b.reshape(1, -1), sgu_w, sbb)


QK_COLS = N_KV_HEADS * 2 * HEAD_DIM


def _attn_kernel(q_ref, kp_ref, kc_ref, vp_ref, vc_ref, bias_ref, sink_ref, o_ref):
    lane = lax.broadcasted_iota(i32, (2 * ATT_BLOCK, 2 * HEAD_DIM), 1)
    low = lane < HEAD_DIM
    zero = jnp.zeros((), bf16)
    scale = jnp.asarray(HEAD_DIM ** -0.5, bf16)
    ones = jnp.ones((2 * ATT_BLOCK, 2 * HEAD_DIM), bf16)
    for kh in range(N_KV_HEADS):
        cs = slice(kh * 2 * HEAD_DIM, (kh + 1) * 2 * HEAD_DIM)
        kk = jnp.concatenate([kp_ref[:, cs], kc_ref[:, cs]], axis=0)
        vv = jnp.concatenate([vp_ref[:, cs], vc_ref[:, cs]], axis=0)
        kz = (jnp.where(low, kk, zero), jnp.where(low, zero, kk))
        vz = (jnp.where(low, vv, zero), jnp.where(low, zero, vv))
        q0 = kh * GQA_GROUP * HEAD_DIM
        qs = jnp.concatenate(
            [q_ref[:, q0 + pr * 2 * HEAD_DIM:q0 + (pr + 1) * 2 * HEAD_DIM] for pr in range(ATT_PAIRS)],
            axis=0) * scale
        acc = None
        for par in range(2):
            s = _dot_nt(qs, kz[par]) + bias_ref[kh, par]
            sk = jnp.concatenate(
                [jnp.full((ATT_BLOCK, ATT_BLOCK), sink_ref[kh * GQA_GROUP + 2 * pr + par], f32)
                 for pr in range(ATT_PAIRS)], axis=0)
            rm = jnp.max(s, axis=-1, keepdims=True)
            mb = jnp.maximum(jnp.broadcast_to(rm, sk.shape), sk)
            p = jnp.concatenate([jnp.exp(s[:, :ATT_BLOCK] - mb), jnp.exp(s[:, ATT_BLOCK:] - mb)], axis=1)
            ov = _dot(p.astype(bf16), jnp.concatenate([vz[par], ones], axis=1))
            den = ov[:, 2 * HEAD_DIM:] + jnp.exp(sk - mb)
            o = ov[:, :2 * HEAD_DIM] / den
            acc = o if acc is None else acc + o
        for pr in range(ATT_PAIRS):
            o_ref[:, q0 + pr * 2 * HEAD_DIM:q0 + (pr + 1) * 2 * HEAD_DIM] = (
                acc[pr * ATT_BLOCK:(pr + 1) * ATT_BLOCK, :].astype(o_ref.dtype))


ATT_PAIRS = GQA_GROUP // 2


def swa_attn(qkv, bias_m, sinks, batch, seq):
    t = qkv.shape[0]
    nb = seq // ATT_BLOCK
    dq = N_Q_HEADS * HEAD_DIM
    kcol = dq // QK_COLS
    vcol = kcol + 1

    def prev(b, i):
        return b * nb + jnp.maximum(i - 1, 0)

    return pl.pallas_call(
        _attn_kernel,
        grid=(batch, nb),
        in_specs=[
            pl.BlockSpec((ATT_BLOCK, dq), lambda b, i: (b * nb + i, 0)),
            pl.BlockSpec((ATT_BLOCK, QK_COLS), lambda b, i: (prev(b, i), kcol)),
            pl.BlockSpec((ATT_BLOCK, QK_COLS), lambda b, i: (b * nb + i, kcol)),
            pl.BlockSpec((ATT_BLOCK, QK_COLS), lambda b, i: (prev(b, i), vcol)),
            pl.BlockSpec((ATT_BLOCK, QK_COLS), lambda b, i: (b * nb + i, vcol)),
            pl.BlockSpec((None, N_KV_HEADS, 2, ATT_PAIRS * ATT_BLOCK, 2 * ATT_BLOCK),
                         lambda b, i: (jnp.where(i == 0, 1, 0), 0, 0, 0, 0)),
            pl.BlockSpec(memory_space=pltpu.SMEM),
        ],
        out_specs=pl.BlockSpec((ATT_BLOCK, dq), lambda b, i: (b * nb + i, 0)),
        out_shape=jax.ShapeDtypeStruct((t, dq), bf16),
        compiler_params=_cparams(("arbitrary", "arbitrary")),
        name="swa_attn",
    )(qkv, qkv, qkv, qkv, qkv, bias_m, sinks)


def _t5_bucket(dist):
    n = np.maximum(dist, 0)
    max_exact = REL_BUCKETS // 2
    large = max_exact + (np.log(np.maximum(n, 1) / max_exact) / np.log(REL_MAX_DIST / max_exact)
                         * (REL_BUCKETS - max_exact)).astype(np.int32)
    large = np.minimum(large, REL_BUCKETS - 1)
    return np.where(n < max_exact, n, large).astype(np.int32)


def _by_kv_parity(a):
    rest = a.shape[2:]
    a = a.reshape((N_KV_HEADS, ATT_PAIRS, 2, ATT_BLOCK) + rest)
    a = jnp.moveaxis(a, 2, 1)
    return a.reshape((N_KV_HEADS, 2, ATT_PAIRS * ATT_BLOCK) + rest)


def _attn_tables(rel_bias):
    t_loc = np.arange(ATT_BLOCK)[:, None]
    j_loc = np.arange(2 * ATT_BLOCK)[None, :]
    dist = ATT_BLOCK + t_loc - j_loc
    band = (dist >= 0) & (dist < WINDOW)
    onehot = jnp.asarray(np.eye(REL_BUCKETS, dtype=np.float32)[_t5_bucket(dist)])
    bias = jnp.einsum('tjb,bh->htj', onehot, rel_bias.astype(f32), precision=lax.Precision.HIGHEST)
    first = band & (j_loc >= ATT_BLOCK)
    tabs = [_by_kv_parity(jnp.where(jnp.asarray(mask)[None], bias, -jnp.inf)) for mask in (band, first)]
    return jnp.stack(tabs)


def _route_kernel(a_ref, w_ref, gate_ref, x_ref, g_ref, sh_ref, sc_ref, rw_ref, rb_ref,
                  xnew_ref, xp_ref, eidx_ref, rk_ref, gw_ref, cnt_ref, grp_scr, carry_scr):
    tm = x_ref.shape[0]
    ne = N_EXPERTS

    @pl.when(pl.program_id(0) == 0)
    def _():
        carry_scr[...] = jnp.zeros_like(carry_scr)

    x = x_ref[...] + gate_ref[...] * _dot(a_ref[...], w_ref[...])
    xnew_ref[...] = x
    xn = _norm_mod(x, g_ref[...], sh_ref[...], sc_ref[...])
    _store_row_tiles(xp_ref, _pack_halves(xn))
    x_hi = xn.astype(bf16)
    x_lo = (xn - x_hi.astype(f32)).astype(bf16)
    rw = rw_ref[...]
    a = _dot_nt(rw, x_hi)
    b = _dot_nt(rw[:ne], x_lo)
    logits = a[:ne] + a[ne:] + b
    scores = jax.nn.sigmoid(logits)
    sel = scores + rb_ref[...]

    for g in range(N_GROUPS):
        tile = sel[g * E_PER_G:(g + 1) * E_PER_G, :]
        t1 = jnp.max(tile, axis=0, keepdims=True)
        dup = jnp.sum(jnp.where(tile == t1, 1.0, 0.0), axis=0, keepdims=True) >= 2.0
        t2 = jnp.max(jnp.where(tile < t1, tile, -jnp.inf), axis=0, keepdims=True)
        grp_scr[g:g + 1, :] = t1 + jnp.where(dup, t1, t2)
    gs = grp_scr[...]

    gi = lax.broadcasted_iota(i32, (N_GROUPS, tm), 0)
    grank = jnp.zeros((N_GROUPS, tm), i32)
    for g in range(N_GROUPS):
        r = grp_scr[g:g + 1, :]
        ge = jnp.where(r >= gs, 1, 0)
        gt = jnp.where(r > gs, 1, 0)
        grank = grank + jnp.where(gi > g, ge, gt)
    grp_scr[...] = jnp.where(grank < TOPK_GROUPS, 1.0, 0.0)

    masked = jnp.concatenate(
        [jnp.where(grp_scr[g:g + 1, :] > 0.5, sel[g * E_PER_G:(g + 1) * E_PER_G, :], -jnp.inf)
         for g in range(N_GROUPS)], axis=0)

    ei = lax.broadcasted_iota(i32, (ne, tm), 0)
    cur = masked
    chf = jnp.zeros((ne, tm), f32)
    for _ in range(TOP_K):
        top = jnp.max(cur, axis=0, keepdims=True)
        first = jnp.min(jnp.where(cur == top, ei, ne), axis=0, keepdims=True)
        pick = ei == first
        chf = jnp.where(pick, 1.0, chf)
        cur = jnp.where(pick, -jnp.inf, cur)
    chosen = chf > 0.5

    wun = jnp.where(chosen, scores, 0.0)
    gwt = wun / jnp.sum(wun, axis=0, keepdims=True) * ROUTED_SCALE

    rr = lax.broadcasted_iota(i32, (tm, tm), 0)
    cc = lax.broadcasted_iota(i32, (tm, tm), 1)
    upper = jnp.where(rr < cc, 1.0, 0.0).astype(bf16)
    chb = chf.astype(bf16)
    pos = carry_scr[:, 0:1] + _dot(chb, upper)
    carry_scr[...] = carry_scr[...] + jnp.sum(chf, axis=1, keepdims=True)
    cnt_ref[...] = carry_scr[...].astype(i32)

    er = lax.broadcasted_iota(i32, (ne, ne), 0)
    ec = lax.broadcasted_iota(i32, (ne, ne), 1)
    lower = jnp.where(ec < er, 1.0, 0.0).astype(bf16)
    below = _dot(lower, chb)
    slot = jnp.where(chosen, below, -1.0)
    eif = ei.astype(f32)
    for k in range(TOP_K):
        mk = slot == float(k)
        eidx_ref[k:k + 1, :] = jnp.sum(jnp.where(mk, eif, 0.0), axis=0, keepdims=True).astype(i32)
        rk_ref[k:k + 1, :] = jnp.sum(jnp.where(mk, pos, 0.0), axis=0, keepdims=True).astype(i32)
        gw_ref[k:k + 1, :] = jnp.sum(jnp.where(mk, gwt, 0.0), axis=0, keepdims=True)


def proj_route(a, w_bf, x2, gate, g, sh, sc, rw_cat, rbias, seq):
    t, d = x2.shape
    k = a.shape[1]
    tps = seq // TR
    return pl.pallas_call(
        _route_kernel,
        grid=(t // TR,),
        in_specs=[
            pl.BlockSpec((TR, k), lambda i: (i, 0)),
            pl.BlockSpec((k, d), lambda i: (0, 0)),
            pl.BlockSpec((None, 1, d), lambda i: (i // tps, 0, 0)),
            pl.BlockSpec((TR, d), lambda i: (i, 0)),
            pl.BlockSpec((1, d), lambda i: (0, 0)),
            pl.BlockSpec((None, 1, d), lambda i: (i // tps, 0, 0)),
            pl.BlockSpec((None, 1, d), lambda i: (i // tps, 0, 0)),
            pl.BlockSpec((2 * N_EXPERTS, d), lambda i: (0, 0)),
            pl.BlockSpec((N_EXPERTS, 1), lambda i: (0, 0)),
        ],
        out_specs=[
            pl.BlockSpec((TR, d), lambda i: (i, 0)),
            pl.BlockSpec((TR * ROW_SUB, LANES), lambda i: (i, 0)),
            pl.BlockSpec((TOP_K, TR), lambda i: (0, i)),
            pl.BlockSpec((TOP_K, TR), lambda i: (0, i)),
            pl.BlockSpec((TOP_K, TR), lambda i: (0, i)),
            pl.BlockSpec((N_EXPERTS, LANES), lambda i: (0, 0)),
        ],
        out_shape=[
            jax.ShapeDtypeStruct((t, d), f32),
            jax.ShapeDtypeStruct((t * ROW_SUB, LANES), u32),
            jax.ShapeDtypeStruct((TOP_K, t), i32),
            jax.ShapeDtypeStruct((TOP_K, t), i32),
            jax.ShapeDtypeStruct((TOP_K, t), f32),
            jax.ShapeDtypeStruct((N_EXPERTS, LANES), i32),
        ],
        scratch_shapes=[
            pltpu.VMEM((N_GROUPS, TR), f32),
            pltpu.VMEM((N_EXPERTS, LANES), f32),
        ],
        compiler_params=_cparams(("arbitrary",)),
        name="proj_route",
    )(a, w_bf, gate, x2, g.reshape(1, d), sh, sc, rw_cat, rbias.reshape(N_EXPERTS, 1))


def _tile_rows(ref, r, n=1):
    return ref.at[pl.ds(pl.multiple_of(r * ROW_SUB, ROW_SUB), n * ROW_SUB), :]


def _row_copy(src, s, dst, d, sem):
    return pltpu.make_async_copy(_tile_rows(src, s), _tile_rows(dst, d), sem)


def _dispatch_kernel(ztail_ref, dest_ref, xp_ref, sg_ref, su_ref, sd_ref, xs_ref, ysh_ref, zeros_scr, sem, zsem):
    td = xp_ref.shape[0] // ROW_SUB

    @pl.when(pl.program_id(0) == 0)
    def _():
        zeros_scr[...] = jnp.zeros_like(zeros_scr)

        def zcopy(e):
            return pltpu.make_async_copy(zeros_scr, _tile_rows(xs_ref, ztail_ref[e], BM), zsem)

        def zstart(e, c):
            zcopy(e).start()
            return c

        def zwait(e, c):
            zcopy(e).wait()
            return c

        lax.fori_loop(0, N_EXPERTS, zstart, 0)
        lax.fori_loop(0, N_EXPERTS, zwait, 0)

    for t in range(td):
        for k in range(TOP_K):
            _row_copy(xp_ref, t, xs_ref, dest_ref[k, t], sem).start(priority=k % DMA_PRIORITIES)
    ysh_ref[...] = _ffn_packed(xp_ref, td, sg_ref, su_ref, sd_ref)
    for t in range(td):
        for k in range(TOP_K):
            _row_copy(xp_ref, t, xs_ref, dest_ref[k, t], sem).wait()


def moe_dispatch(xp, dest, ztail, nrows, sg_bf, su_bf, sd_bf):
    t = xp.shape[0] // ROW_SUB
    d = sg_bf.shape[0]
    grid_spec = pltpu.PrefetchScalarGridSpec(
        num_scalar_prefetch=1,
        grid=(t // TD,),
        in_specs=[
            pl.BlockSpec((TOP_K, TD), lambda i, z: (0, i), memory_space=pltpu.SMEM),
            pl.BlockSpec((TD * ROW_SUB, LANES), lambda i, z: (i, 0)),
            pl.BlockSpec((d, D_SHARED), lambda i, z: (0, 0)),
            pl.BlockSpec((d, D_SHARED), lambda i, z: (0, 0)),
            pl.BlockSpec((D_SHARED, d), lambda i, z: (0, 0)),
        ],
        out_specs=[pl.BlockSpec(memory_space=pl.ANY), pl.BlockSpec((TD, d), lambda i, z: (i, 0))],
        scratch_shapes=[
            pltpu.VMEM((BM * ROW_SUB, LANES), u32),
            pltpu.SemaphoreType.DMA(()),
            pltpu.SemaphoreType.DMA(()),
        ],
    )
    return pl.pallas_call(
        _dispatch_kernel,
        grid_spec=grid_spec,
        out_shape=[jax.ShapeDtypeStruct((nrows * ROW_SUB, LANES), u32), jax.ShapeDtypeStruct((t, d), f32)],
        compiler_params=_cparams(("arbitrary",)),
        name="moe_dispatch",
    )(ztail, dest, xp, sg_bf, su_bf, sd_bf)


def _ffn_packed(x_ref, m, wg, wu, wd):
    los, his = _load_row_tiles(x_ref, m)
    x = jnp.concatenate([p.astype(bf16) for p in los + his], axis=1)
    hg = _dot(x, wg[...])
    hu = _dot(x, wu[...])
    a = (jax.nn.silu(hg) * hu).astype(bf16)
    return _dot(a, wd[...])


def _gmm_kernel(be_ref, bf_ref, nu_ref, xs_ref, wg_ref, wu_ref, wd_ref, ys_ref, wg_s, wu_s, wd_s):
    i = pl.program_id(0)

    @pl.when(bf_ref[i] == 1)
    def _():
        wg_s[...] = wg_ref[...].astype(bf16)
        wu_s[...] = wu_ref[...].astype(bf16)
        wd_s[...] = wd_ref[...].astype(bf16)

    @pl.when(i < nu_ref[0])
    def _():
        _store_row_tiles(ys_ref, _pack_halves(_ffn_packed(xs_ref, BM, wg_s, wu_s, wd_s)))

    @pl.when(i >= nu_ref[0])
    def _():
        ys_ref[...] = jnp.zeros_like(ys_ref)


def moe_gmm(xs, blk_e, blk_first, n_used, w_gate, w_up, w_down, layer, nblk):
    d, de = w_gate.shape[2], w_gate.shape[3]
    grid_spec = pltpu.PrefetchScalarGridSpec(
        num_scalar_prefetch=3,
        grid=(nblk,),
        in_specs=[
            pl.BlockSpec((BM * ROW_SUB, LANES), lambda i, be, bf, nu: (jnp.minimum(i, nu[0] - 1), 0)),
            pl.BlockSpec((None, None, d, de), lambda i, be, bf, nu: (layer, be[i], 0, 0)),
            pl.BlockSpec((None, None, d, de), lambda i, be, bf, nu: (layer, be[i], 0, 0)),
            pl.BlockSpec((None, None, de, d), lambda i, be, bf, nu: (layer, be[i], 0, 0)),
        ],
        out_specs=pl.BlockSpec((BM * ROW_SUB, LANES), lambda i, be, bf, nu: (i, 0)),
        scratch_shapes=[
            pltpu.VMEM((d, de), bf16),
            pltpu.VMEM((d, de), bf16),
            pltpu.VMEM((de, d), bf16),
        ],
    )
    return pl.pallas_call(
        _gmm_kernel,
        grid_spec=grid_spec,
        out_shape=jax.ShapeDtypeStruct((nblk * BM * ROW_SUB, LANES), u32),
        compiler_params=_cparams(("arbitrary",)),
        name="moe_gmm",
    )(blk_e, blk_first, n_used, xs, w_gate, w_up, w_down)


def _combine_kernel(dest_ref, dnext_ref, x_ref, ysh_ref, gw_ref, gate_ref, *rest, tail):
    if tail == 'qkv':
        ng_ref, nsh_ref, nsc_ref, wq_ref, ys_ref, o_ref, q_ref, gath_a, gath_b, sem, xn_scr = rest
    elif tail == 'final':
        ng_ref, ys_ref, o_ref, gath_a, gath_b, sem = rest
    else:
        ys_ref, o_ref, gath_a, gath_b, sem = rest
    td = x_ref.shape[0]
    i = pl.program_id(0)

    def copy(dref, buf, s, t, k):
        return _row_copy(ys_ref, dref[k, t], buf.at[k], t, sem.at[s])

    def wait_tile(dref, buf, s):
        for t in range(td):
            for k in range(TOP_K):
                copy(dref, buf, s, t, k).wait()

    @pl.when(i == 0)
    def _():
        def body(t, c):
            for k in range(TOP_K):
                copy(dest_ref, gath_a, 0, t, k).start(priority=k % DMA_PRIORITIES)
            return c
        lax.fori_loop(0, td, body, 0)

    def step(cur_buf, cur_s, nxt_buf, nxt_s):
        wait_tile(dest_ref, cur_buf, cur_s)
        for t in range(td):
            for k in range(TOP_K):
                copy(dnext_ref, nxt_buf, nxt_s, t, k).start(priority=k % DMA_PRIORITIES)
        for rb in range(td // COMB_RB):
            rows = slice(rb * COMB_RB, (rb + 1) * COMB_RB)
            ysh = ysh_ref[rows, :]
            acc_lo = [ysh[:, c * LANES:(c + 1) * LANES] for c in range(ROW_SUB)]
            acc_hi = [ysh[:, HALF + c * LANES:HALF + (c + 1) * LANES] for c in range(ROW_SUB)]
            gw = gw_ref[rows, :]
            for k in range(TOP_K):
                wk = jnp.broadcast_to(gw[:, k:k + 1], (COMB_RB, LANES))
                for c in range(ROW_SUB):
                    word = cur_buf[k, pl.ds(rb * COMB_RB * ROW_SUB + c, COMB_RB, stride=ROW_SUB), :]
                    lo, hi = _unpack_halves(word)
                    acc_lo[c] = acc_lo[c] + wk * lo
                    acc_hi[c] = acc_hi[c] + wk * hi
            y = jnp.concatenate(acc_lo + acc_hi, axis=1)
            out = x_ref[rows, :] + gate_ref[...] * y
            if tail == 'final':
                ms = jnp.mean(out * out, axis=-1, keepdims=True)
                out = (out * lax.rsqrt(ms + RMS_EPS)) * ng_ref[...]
            o_ref[rows, :] = out
            if tail == 'qkv':
                xn_scr[rows, :] = _norm_mod(out, ng_ref[...], nsh_ref[...], nsc_ref[...]).astype(bf16)
        if tail == 'qkv':
            q_ref[...] = _dot(xn_scr[...], wq_ref[...]).astype(q_ref.dtype)

        @pl.when(i == pl.num_programs(0) - 1)
        def _():
            wait_tile(dnext_ref, nxt_buf, nxt_s)

    @pl.when(i % 2 == 0)
    def _():
        step(gath_a, 0, gath_b, 1)

    @pl.when(i % 2 == 1)
    def _():
        step(gath_b, 1, gath_a, 0)


def moe_combine(x2, ysh, ys, dest, gw_t, gate, seq, final_g=None, next_qkv=None):
    t, d = x2.shape
    tps = seq // TD
    last = t // TD - 1
    row = lambda i: (i, 0)
    per_batch = pl.BlockSpec((None, 1, d), lambda i: (i // tps, 0, 0))
    in_specs = [
        pl.BlockSpec((TOP_K, TD), lambda i: (0, i), memory_space=pltpu.SMEM),
        pl.BlockSpec((TOP_K, TD), lambda i: (0, jnp.minimum(i + 1, last)), memory_space=pltpu.SMEM),
        pl.BlockSpec((TD, d), row),
        pl.BlockSpec((TD, d), row),
        pl.BlockSpec((TD, TOP_K), row),
        per_batch,
    ]
    args = [dest, dest, x2, ysh, gw_t, gate]
    out_specs = [pl.BlockSpec((TD, d), row)]
    out_shape = [jax.ShapeDtypeStruct((t, d), f32)]
    scratch = [
        pltpu.VMEM((TOP_K, TD * ROW_SUB, LANES), u32),
        pltpu.VMEM((TOP_K, TD * ROW_SUB, LANES), u32),
        pltpu.SemaphoreType.DMA((2,)),
    ]
    tail = None
    if final_g is not None:
        tail = 'final'
        in_specs.append(pl.BlockSpec((1, d), lambda i: (0, 0)))
        args.append(final_g.reshape(1, d))
    elif next_qkv is not None:
        tail = 'qkv'
        g, sh, sc, w_bf = next_qkv
        nq = w_bf.shape[1]
        in_specs += [pl.BlockSpec((1, d), lambda i: (0, 0)), per_batch, per_batch,
                     pl.BlockSpec((d, nq), lambda i: (0, 0))]
        args += [g.reshape(1, d), sh, sc, w_bf]
        out_specs.append(pl.BlockSpec((TD, nq), row))
        out_shape.append(jax.ShapeDtypeStruct((t, nq), bf16))
        scratch.append(pltpu.VMEM((TD, d), bf16))
    in_specs.append(pl.BlockSpec(memory_space=pl.ANY))
    args.append(ys)
    outs = pl.pallas_call(
        functools.partial(_combine_kernel, tail=tail),
        grid=(t // TD,),
        in_specs=in_specs,
        out_specs=out_specs,
        out_shape=out_shape,
        scratch_shapes=scratch,
        compiler_params=_cparams(("arbitrary",)),
        name="moe_combine",
    )(*args)
    return outs if tail == 'qkv' else outs[0]


SLOT_TN = 4096


def _slots_kernel(ps_ref, eidx_ref, rk_ref, o_ref):
    e = eidx_ref[...]
    start = jnp.zeros_like(e)
    for j in range(N_EXPERTS):
        start = jnp.where(e == j, ps_ref[j], start)
    o_ref[...] = start + rk_ref[...]


def moe_slots(pad_start, eidx, rk):
    t = eidx.shape[1]
    tn = min(t, SLOT_TN)
    spec = pl.BlockSpec((TOP_K, tn), lambda i, ps: (0, i))
    grid_spec = pltpu.PrefetchScalarGridSpec(
        num_scalar_prefetch=1, grid=(t // tn,), in_specs=[spec, spec], out_specs=spec)
    return pl.pallas_call(
        _slots_kernel,
        grid_spec=grid_spec,
        out_shape=jax.ShapeDtypeStruct((TOP_K, t), i32),
        compiler_params=_cparams(("arbitrary",)),
        name="moe_slots",
    )(pad_start, eidx, rk)


def moe_layer(mix, w_out_bf, x2, gate_m, g, sh, sc, gate, router_w, router_bias, w_gate, w_up, w_down, layer,
              sh_gate, sh_up, sh_down, seq, final_g=None, next_qkv=None):
    t = x2.shape[0]
    a = t * TOP_K
    nblk = a // BM + N_EXPERTS
    rw_t = router_w.T
    rw_hi = rw_t.astype(bf16)
    rw_lo = (rw_t - rw_hi.astype(f32)).astype(bf16)
    rw_cat = jnp.concatenate([rw_hi, rw_lo], axis=0)

    x2, xp, eidx, rk, gw, cnt = proj_route(mix, w_out_bf, x2, gate_m, g, sh, sc, rw_cat, router_bias, seq)

    counts = cnt[:, 0]
    padded = ((counts + BM - 1) // BM) * BM
    cum_pad = jnp.cumsum(padded)
    pad_start = cum_pad - padded
    n_used = (cum_pad[-1] // BM).astype(i32).reshape(1)
    blk_row = jnp.arange(nblk, dtype=i32) * BM
    blk_e = jnp.minimum(jnp.sum((cum_pad[None, :] <= blk_row[:, None]).astype(i32), axis=1), N_EXPERTS - 1)
    blk_first = jnp.concatenate([jnp.ones((1,), i32), (blk_e[1:] != blk_e[:-1]).astype(i32)])
    dest = moe_slots(pad_start.astype(i32), eidx, rk)
    ztail = jnp.maximum(cum_pad - BM, 0).astype(i32)

    xs, ysh = moe_dispatch(xp, dest, ztail, nblk * BM, sh_gate.astype(bf16), sh_up.astype(bf16),
                           sh_down.astype(bf16))
    ys = moe_gmm(xs, blk_e, blk_first, n_used, w_gate, w_up, w_down, layer, nblk)
    return moe_combine(x2, ysh, ys, dest, gw.T, gate, seq, final_g=final_g, next_qkv=next_qkv)


def _qkv_weight(w_qkv):
    dq = N_Q_HEADS * HEAD_DIM
    dkv = N_KV_HEADS * HEAD_DIM
    d = w_qkv.shape[0]
    wq = w_qkv[:, :dq]
    wk = w_qkv[:, dq:dq + dkv].reshape(d, N_KV_HEADS, 1, HEAD_DIM)
    wv = w_qkv[:, dq + dkv:].reshape(d, N_KV_HEADS, 1, HEAD_DIM)
    wk2 = jnp.broadcast_to(wk, (d, N_KV_HEADS, 2, HEAD_DIM)).reshape(d, QK_COLS)
    wv2 = jnp.broadcast_to(wv, (d, N_KV_HEADS, 2, HEAD_DIM)).reshape(d, QK_COLS)
    return jnp.concatenate([wq, wk2, wv2], axis=1).astype(bf16)


def kernel(x, c, ada_w, ada_b, norm_mix_g, norm_ffn_g, hyb_w_in, conv_w, sgu_ln_g, sgu_ln_b, sgu_w, sgu_b,
           hyb_w_out, attn_w_qkv, attn_sinks, attn_w_o, rel_bias, router_w, router_bias, exp_w_gate,
           exp_w_up, exp_w_down, sh_w_gate, sh_w_up, sh_w_down, final_g):
    batch, seq, d = x.shape
    t = batch * seq
    x2 = x.reshape(t, d)
    mod = ada_mod(c, ada_w, ada_b)
    mods = [[mod[l, :, k * d:(k + 1) * d].reshape(batch, 1, d) for k in range(6)] for l in range(DEPTH)]
    qkv = None
    for l in range(DEPTH):
        sh_m, sc_m, g_m, sh_f, sc_f, g_f = mods[l]
        i = l // 2
        if l % 2 == 0:
            mix = hyb_in(x2, norm_mix_g[l], sh_m, sc_m, hyb_w_in[i].astype(bf16), conv_w[i], sgu_ln_g[i],
                         sgu_ln_b[i], sgu_w[i], sgu_b[i], seq)
            w_out = hyb_w_out[i].astype(bf16)
        else:
            mix = swa_attn(qkv, _attn_tables(rel_bias), attn_sinks[i], batch, seq)
            w_out = attn_w_o[i].astype(bf16)
        tail = {}
        if l == DEPTH - 1:
            tail = dict(final_g=final_g)
        else:
            tail = dict(next_qkv=(norm_mix_g[l + 1], mods[l + 1][0], mods[l + 1][1],
                                  _qkv_weight(attn_w_qkv[(l + 1) // 2])))
        res = moe_layer(mix, w_out, x2, g_m, norm_ffn_g[l], sh_f, sc_f, g_f, router_w[l], router_bias[l],
                        exp_w_gate, exp_w_up, exp_w_down, l, sh_w_gate[l], sh_w_up[l], sh_w_down[l], seq, **tail)
        x2, qkv = res if 'next_qkv' in tail else (res, None)
    return x2.reshape(batch, seq, d)
```

```python
import functools

import numpy as np
import jax
import jax.numpy as jnp
from jax import lax
from jax.experimental import pallas as pl
from jax.experimental.pallas import tpu as pltpu

f32 = jnp.float32
bf16 = jnp.bfloat16
i32 = jnp.int32
u32 = jnp.uint32

D_MODEL = 2048
DEPTH = 2
RMS_EPS = 1e-6
LN_EPS = 1e-5
D_CONV = 1024
CONV_WIDTH = 3
D_SGU = 1024
SGU_GROUPS = 8
SGU_HD = D_SGU // SGU_GROUPS
CHUNK = 128
HYB_IN = 3 * D_CONV + 2 * D_SGU
HEAD_DIM = 64
N_Q_HEADS = D_MODEL // HEAD_DIM
N_KV_HEADS = N_Q_HEADS // 8
GQA_GROUP = N_Q_HEADS // N_KV_HEADS
WINDOW = 128
ATT_BLOCK = 128
REL_BUCKETS = 32
REL_MAX_DIST = 128
N_EXPERTS = 64
TOP_K = 8
N_GROUPS = 8
TOPK_GROUPS = 4
E_PER_G = N_EXPERTS // N_GROUPS
D_EXPERT = 512
D_SHARED = 512
ROUTED_SCALE = 2.5

LANES = 128
HALF = D_MODEL // 2
VMEM_LIMIT = 56 * 1024 * 1024
TM = 512
TN_IN = 1024
TR = 512
TD = 256
BM = 512
ADA_TN = 1024
DMA_PRIORITIES = 2
COMB_RB = 16


def _cparams(sem, vmem=VMEM_LIMIT):
    return pltpu.CompilerParams(dimension_semantics=sem, vmem_limit_bytes=vmem)


def _norm_mod(x, g, sh, sc):
    ms = jnp.mean(x * x, axis=-1, keepdims=True)
    y = x * lax.rsqrt(ms + RMS_EPS)
    return (y * g) * (1.0 + sc) + sh


def _pack_halves(y):
    h = y.shape[1] // 2
    lo = lax.bitcast_convert_type(y[:, :h].astype(bf16).astype(f32), u32)
    hi = lax.bitcast_convert_type(y[:, h:].astype(bf16).astype(f32), u32)
    return (lo >> 16) | (hi & jnp.uint32(0xFFFF0000))


def _unpack_halves(w):
    lo = lax.bitcast_convert_type(w << 16, f32)
    hi = lax.bitcast_convert_type(w & jnp.uint32(0xFFFF0000), f32)
    return lo, hi


ROW_SUB = HALF // LANES


def _store_row_tiles(ref, packed):
    n = packed.shape[0]
    for c in range(ROW_SUB):
        ref[pl.ds(c, n, stride=ROW_SUB), :] = packed[:, c * LANES:(c + 1) * LANES]


def _load_row_tiles(ref, n):
    los, his = [], []
    for c in range(ROW_SUB):
        lo, hi = _unpack_halves(ref[pl.ds(c, n, stride=ROW_SUB), :])
        los.append(lo)
        his.append(hi)
    return los, his


def _gelu(x):
    return 0.5 * x * (1.0 + lax.erf(x * np.float32(np.sqrt(0.5))))


def _dot(a, b):
    return jnp.dot(a, b, preferred_element_type=f32)


def _dot_nt(a, b):
    return lax.dot_general(a, b, (((1,), (1,)), ((), ())), preferred_element_type=f32)


def _ada_kernel(c_ref, w_ref, b_ref, o_ref):
    ca = jax.nn.silu(c_ref[...]).astype(bf16)
    o_ref[...] = _dot(ca, w_ref[...].astype(bf16)) + b_ref[...]


def ada_mod(c, ada_w, ada_b):
    depth, d, n = ada_w.shape
    b = c.shape[0]
    return pl.pallas_call(
        _ada_kernel,
        grid=(depth, n // ADA_TN),
        in_specs=[
            pl.BlockSpec((b, d), lambda l, j: (0, 0)),
            pl.BlockSpec((None, d, ADA_TN), lambda l, j: (l, 0, j)),
            pl.BlockSpec((None, 1, ADA_TN), lambda l, j: (l, 0, j)),
        ],
        out_specs=pl.BlockSpec((None, b, ADA_TN), lambda l, j: (l, 0, j)),
        out_shape=jax.ShapeDtypeStruct((depth, b, n), f32),
        compiler_params=_cparams(("arbitrary", "arbitrary")),
        name="ada_mod",
    )(c, ada_w, ada_b.reshape(depth, 1, n))


def _hyb_in_kernel(x_ref, g_ref, sh_ref, sc_ref, w_ref, cw_ref, lng_ref, lnb_ref, sw_ref, sb_ref,
                   o_ref, xn_scr, a_scr, b_scr, carry_scr, *, tiles_per_seq):
    i = pl.program_id(0)
    j = pl.program_id(1)
    tm = x_ref.shape[0]

    @pl.when(j == 0)
    def _():
        xn_scr[...] = _norm_mod(x_ref[...], g_ref[...], sh_ref[...], sc_ref[...]).astype(bf16)

    @pl.when(jnp.logical_and(i == 0, j == 0))
    def _():
        carry_scr[...] = jnp.zeros_like(carry_scr)

    p = _dot(xn_scr[...], w_ref[...])

    @pl.when(j == 0)
    def _():
        a_scr[...] = p

    @pl.when(j == 1)
    def _():
        b_scr[...] = p

    @pl.when(j == 2)
    def _():
        z = b_scr[...] * p
        row = lax.broadcasted_iota(i32, z.shape, 0)
        first = (i % tiles_per_seq) == 0
        prev = jnp.where(first, 0.0, carry_scr[...])
        p1 = prev[7:8, :]
        p2 = prev[6:7, :]
        z1 = jnp.where(row == 0, p1, pltpu.roll(z, 1, 0))
        z2 = jnp.where(row == 0, p2, jnp.where(row == 1, p1, pltpu.roll(z, 2, 0)))
        cw = cw_ref[...]
        conv = cw[0:1, :] * z2 + cw[1:2, :] * z1 + cw[2:3, :] * z
        carry_scr[...] = z[tm - 8:, :]
        o_ref[:, :D_CONV] = (a_scr[...] * conv).astype(o_ref.dtype)

    @pl.when(j == 3)
    def _():
        a_scr[...] = _gelu(p)

    @pl.when(j == 4)
    def _():
        v = _gelu(p)
        mu = jnp.mean(v, axis=-1, keepdims=True)
        vc = v - mu
        var = jnp.mean(vc * vc, axis=-1, keepdims=True)
        v = (vc * lax.rsqrt(var + LN_EPS)) * lng_ref[...] + lnb_ref[...]
        vb = v.astype(bf16)
        nch = tm // CHUNK
        r = lax.broadcasted_iota(i32, (CHUNK, CHUNK), 0)
        c = lax.broadcasted_iota(i32, (CHUNK, CHUNK), 1)
        tril = r >= c
        for g in range(SGU_GROUPS):
            ws = jnp.where(tril, sw_ref[g], 0.0).astype(bf16)
            vg = jnp.concatenate(
                [vb[n * CHUNK:(n + 1) * CHUNK, g * SGU_HD:(g + 1) * SGU_HD] for n in range(nch)], axis=1)
            sg = _dot(ws, vg)
            bb = sb_ref[g]
            for n in range(nch):
                s = sg[:, n * SGU_HD:(n + 1) * SGU_HD] + bb
                u = a_scr[n * CHUNK:(n + 1) * CHUNK, g * SGU_HD:(g + 1) * SGU_HD]
                o_ref[n * CHUNK:(n + 1) * CHUNK, D_CONV + g * SGU_HD:D_CONV + (g + 1) * SGU_HD] = (
                    (u * s).astype(o_ref.dtype))


def hyb_in(x2, g, sh, sc, w_in_bf, conv_w, ln_g, ln_b, sgu_w, sgu_b, seq):
    t, d = x2.shape
    tps = seq // TM
    nj = HYB_IN // TN_IN
    sbb = jnp.broadcast_to(sgu_b[:, :, None], (SGU_GROUPS, CHUNK, SGU_HD))
    kern = functools.partial(_hyb_in_kernel, tiles_per_seq=tps)
    return pl.pallas_call(
        kern,
        grid=(t // TM, nj),
        in_specs=[
            pl.BlockSpec((TM, d), lambda i, j: (i, 0)),
            pl.BlockSpec((1, d), lambda i, j: (0, 0)),
            pl.BlockSpec((None, 1, d), lambda i, j: (i // tps, 0, 0)),
            pl.BlockSpec((None, 1, d), lambda i, j: (i // tps, 0, 0)),
            pl.BlockSpec((d, TN_IN), lambda i, j: (0, j)),
            pl.BlockSpec((CONV_WIDTH, D_CONV), lambda i, j: (0, 0)),
            pl.BlockSpec((1, D_SGU), lambda i, j: (0, 0)),
            pl.BlockSpec((1, D_SGU), lambda i, j: (0, 0)),
            pl.BlockSpec((SGU_GROUPS, CHUNK, CHUNK), lambda i, j: (0, 0, 0)),
            pl.BlockSpec((SGU_GROUPS, CHUNK, SGU_HD), lambda i, j: (0, 0, 0)),
        ],
        out_specs=pl.BlockSpec((TM, D_CONV + D_SGU), lambda i, j: (i, 0)),
        out_shape=jax.ShapeDtypeStruct((t, D_CONV + D_SGU), bf16),
        scratch_shapes=[
            pltpu.VMEM((TM, d), bf16),
            pltpu.VMEM((TM, TN_IN), f32),
            pltpu.VMEM((TM, TN_IN), f32),
            pltpu.VMEM((8, D_CONV), f32),
        ],
        compiler_params=_cparams(("arbitrary", "arbitrary")),
        name="hyb_in",
    )(x2, g.reshape(1, d), sh, sc, w_in_bf, conv_w, ln_g.reshape(1, -1), ln_b.reshape(1, -1), sgu_w, sbb)


QK_COLS = N_KV_HEADS * 2 * HEAD_DIM


def _attn_kernel(q_ref, kp_ref, kc_ref, vp_ref, vc_ref, bias_ref, sink_ref, o_ref):
    lane = lax.broadcasted_iota(i32, (2 * ATT_BLOCK, 2 * HEAD_DIM), 1)
    low = lane < HEAD_DIM
    zero = jnp.zeros((), bf16)
    scale = jnp.asarray(HEAD_DIM ** -0.5, bf16)
    ones = jnp.ones((2 * ATT_BLOCK, 2 * HEAD_DIM), bf16)
    for kh in range(N_KV_HEADS):
        cs = slice(kh * 2 * HEAD_DIM, (kh + 1) * 2 * HEAD_DIM)
        kk = jnp.concatenate([kp_ref[:, cs], kc_ref[:, cs]], axis=0)
        vv = jnp.concatenate([vp_ref[:, cs], vc_ref[:, cs]], axis=0)
        kz = (jnp.where(low, kk, zero), jnp.where(low, zero, kk))
        vz = (jnp.where(low, vv, zero), jnp.where(low, zero, vv))
        q0 = kh * GQA_GROUP * HEAD_DIM
        qs = jnp.concatenate(
            [q_ref[:, q0 + pr * 2 * HEAD_DIM:q0 + (pr + 1) * 2 * HEAD_DIM] for pr in range(ATT_PAIRS)],
            axis=0) * scale
        acc = None
        for par in range(2):
            s = _dot_nt(qs, kz[par]) + bias_ref[kh, par]
            sk = jnp.concatenate(
                [jnp.full((ATT_BLOCK, ATT_BLOCK), sink_ref[kh * GQA_GROUP + 2 * pr + par], f32)
                 for pr in range(ATT_PAIRS)], axis=0)
            rm = jnp.max(s, axis=-1, keepdims=True)
            mb = jnp.maximum(jnp.broadcast_to(rm, sk.shape), sk)
            p = jnp.concatenate([jnp.exp(s[:, :ATT_BLOCK] - mb), jnp.exp(s[:, ATT_BLOCK:] - mb)], axis=1)
            ov = _dot(p.astype(bf16), jnp.concatenate([vz[par], ones], axis=1))
            den = ov[:, 2 * HEAD_DIM:] + jnp.exp(sk - mb)
            o = ov[:, :2 * HEAD_DIM] / den
            acc = o if acc is None else acc + o
        for pr in range(ATT_PAIRS):
            o_ref[:, q0 + pr * 2 * HEAD_DIM:q0 + (pr + 1) * 2 * HEAD_DIM] = (
                acc[pr * ATT_BLOCK:(pr + 1) * ATT_BLOCK, :].astype(o_ref.dtype))


ATT_PAIRS = GQA_GROUP // 2


def swa_attn(qkv, bias_m, sinks, batch, seq):
    t = qkv.shape[0]
    nb = seq // ATT_BLOCK
    dq = N_Q_HEADS * HEAD_DIM
    kcol = dq // QK_COLS
    vcol = kcol + 1

    def prev(b, i):
        return b * nb + jnp.maximum(i - 1, 0)

    return pl.pallas_call(
        _attn_kernel,
        grid=(batch, nb),
        in_specs=[
            pl.BlockSpec((ATT_BLOCK, dq), lambda b, i: (b * nb + i, 0)),
            pl.BlockSpec((ATT_BLOCK, QK_COLS), lambda b, i: (prev(b, i), kcol)),
            pl.BlockSpec((ATT_BLOCK, QK_COLS), lambda b, i: (b * nb + i, kcol)),
            pl.BlockSpec((ATT_BLOCK, QK_COLS), lambda b, i: (prev(b, i), vcol)),
            pl.BlockSpec((ATT_BLOCK, QK_COLS), lambda b, i: (b * nb + i, vcol)),
            pl.BlockSpec((None, N_KV_HEADS, 2, ATT_PAIRS * ATT_BLOCK, 2 * ATT_BLOCK),
                         lambda b, i: (jnp.where(i == 0, 1, 0), 0, 0, 0, 0)),
            pl.BlockSpec(memory_space=pltpu.SMEM),
        ],
        out_specs=pl.BlockSpec((ATT_BLOCK, dq), lambda b, i: (b * nb + i, 0)),
        out_shape=jax.ShapeDtypeStruct((t, dq), bf16),
        compiler_params=_cparams(("arbitrary", "arbitrary")),
        name="swa_attn",
    )(qkv, qkv, qkv, qkv, qkv, bias_m, sinks)


def _t5_bucket(dist):
    n = np.maximum(dist, 0)
    max_exact = REL_BUCKETS // 2
    large = max_exact + (np.log(np.maximum(n, 1) / max_exact) / np.log(REL_MAX_DIST / max_exact)
                         * (REL_BUCKETS - max_exact)).astype(np.int32)
    large = np.minimum(large, REL_BUCKETS - 1)
    return np.where(n < max_exact, n, large).astype(np.int32)


def _by_kv_parity(a):
    rest = a.shape[2:]
    a = a.reshape((N_KV_HEADS, ATT_PAIRS, 2, ATT_BLOCK) + rest)
    a = jnp.moveaxis(a, 2, 1)
    return a.reshape((N_KV_HEADS, 2, ATT_PAIRS * ATT_BLOCK) + rest)


def _attn_tables(rel_bias):
    t_loc = np.arange(ATT_BLOCK)[:, None]
    j_loc = np.arange(2 * ATT_BLOCK)[None, :]
    dist = ATT_BLOCK + t_loc - j_loc
    band = (dist >= 0) & (dist < WINDOW)
    onehot = jnp.asarray(np.eye(REL_BUCKETS, dtype=np.float32)[_t5_bucket(dist)])
    bias = jnp.einsum('tjb,bh->htj', onehot, rel_bias.astype(f32), precision=lax.Precision.HIGHEST)
    first = band & (j_loc >= ATT_BLOCK)
    tabs = [_by_kv_parity(jnp.where(jnp.asarray(mask)[None], bias, -jnp.inf)) for mask in (band, first)]
    return jnp.stack(tabs)


def _route_kernel(a_ref, w_ref, gate_ref, x_ref, g_ref, sh_ref, sc_ref, rw_ref, rb_ref,
                  xnew_ref, xp_ref, eidx_ref, rk_ref, gw_ref, cnt_ref, grp_scr, carry_scr):
    tm = x_ref.shape[0]
    ne = N_EXPERTS

    @pl.when(pl.program_id(0) == 0)
    def _():
        carry_scr[...] = jnp.zeros_like(carry_scr)

    x = x_ref[...] + gate_ref[...] * _dot(a_ref[...], w_ref[...])
    xnew_ref[...] = x
    xn = _norm_mod(x, g_ref[...], sh_ref[...], sc_ref[...])
    _store_row_tiles(xp_ref, _pack_halves(xn))
    x_hi = xn.astype(bf16)
    x_lo = (xn - x_hi.astype(f32)).astype(bf16)
    rw = rw_ref[...]
    a = _dot_nt(rw, x_hi)
    b = _dot_nt(rw[:ne], x_lo)
    logits = a[:ne] + a[ne:] + b
    scores = jax.nn.sigmoid(logits)
    sel = scores + rb_ref[...]

    for g in range(N_GROUPS):
        tile = sel[g * E_PER_G:(g + 1) * E_PER_G, :]
        t1 = jnp.max(tile, axis=0, keepdims=True)
        dup = jnp.sum(jnp.where(tile == t1, 1.0, 0.0), axis=0, keepdims=True) >= 2.0
        t2 = jnp.max(jnp.where(tile < t1, tile, -jnp.inf), axis=0, keepdims=True)
        grp_scr[g:g + 1, :] = t1 + jnp.where(dup, t1, t2)
    gs = grp_scr[...]

    gi = lax.broadcasted_iota(i32, (N_GROUPS, tm), 0)
    grank = jnp.zeros((N_GROUPS, tm), i32)
    for g in range(N_GROUPS):
        r = grp_scr[g:g + 1, :]
        ge = jnp.where(r >= gs, 1, 0)
        gt = jnp.where(r > gs, 1, 0)
        grank = grank + jnp.where(gi > g, ge, gt)
    grp_scr[...] = jnp.where(grank < TOPK_GROUPS, 1.0, 0.0)

    masked = jnp.concatenate(
        [jnp.where(grp_scr[g:g + 1, :] > 0.5, sel[g * E_PER_G:(g + 1) * E_PER_G, :], -jnp.inf)
         for g in range(N_GROUPS)], axis=0)

    ei = lax.broadcasted_iota(i32, (ne, tm), 0)
    cur = masked
    chf = jnp.zeros((ne, tm), f32)
    for _ in range(TOP_K):
        top = jnp.max(cur, axis=0, keepdims=True)
        first = jnp.min(jnp.where(cur == top, ei, ne), axis=0, keepdims=True)
        pick = ei == first
        chf = jnp.where(pick, 1.0, chf)
        cur = jnp.where(pick, -jnp.inf, cur)
    chosen = chf > 0.5

    wun = jnp.where(chosen, scores, 0.0)
    gwt = wun / jnp.sum(wun, axis=0, keepdims=True) * ROUTED_SCALE

    rr = lax.broadcasted_iota(i32, (tm, tm), 0)
    cc = lax.broadcasted_iota(i32, (tm, tm), 1)
    upper = jnp.where(rr < cc, 1.0, 0.0).astype(bf16)
    chb = chf.astype(bf16)
    pos = carry_scr[:, 0:1] + _dot(chb, upper)
    carry_scr[...] = carry_scr[...] + jnp.sum(chf, axis=1, keepdims=True)
    cnt_ref[...] = carry_scr[...].astype(i32)

    er = lax.broadcasted_iota(i32, (ne, ne), 0)
    ec = lax.broadcasted_iota(i32, (ne, ne), 1)
    lower = jnp.where(ec < er, 1.0, 0.0).astype(bf16)
    below = _dot(lower, chb)
    slot = jnp.where(chosen, below, -1.0)
    eif = ei.astype(f32)
    for k in range(TOP_K):
        mk = slot == float(k)
        eidx_ref[k:k + 1, :] = jnp.sum(jnp.where(mk, eif, 0.0), axis=0, keepdims=True).astype(i32)
        rk_ref[k:k + 1, :] = jnp.sum(jnp.where(mk, pos, 0.0), axis=0, keepdims=True).astype(i32)
        gw_ref[k:k + 1, :] = jnp.sum(jnp.where(mk, gwt, 0.0), axis=0, keepdims=True)


def proj_route(a, w_bf, x2, gate, g, sh, sc, rw_cat, rbias, seq):
    t, d = x2.shape
    k = a.shape[1]
    tps = seq // TR
    return pl.pallas_call(
        _route_kernel,
        grid=(t // TR,),
        in_specs=[
            pl.BlockSpec((TR, k), lambda i: (i, 0)),
            pl.BlockSpec((k, d), lambda i: (0, 0)),
            pl.BlockSpec((None, 1, d), lambda i: (i // tps, 0, 0)),
            pl.BlockSpec((TR, d), lambda i: (i, 0)),
            pl.BlockSpec((1, d), lambda i: (0, 0)),
            pl.BlockSpec((None, 1, d), lambda i: (i // tps, 0, 0)),
            pl.BlockSpec((None, 1, d), lambda i: (i // tps, 0, 0)),
            pl.BlockSpec((2 * N_EXPERTS, d), lambda i: (0, 0)),
            pl.BlockSpec((N_EXPERTS, 1), lambda i: (0, 0)),
        ],
        out_specs=[
            pl.BlockSpec((TR, d), lambda i: (i, 0)),
            pl.BlockSpec((TR * ROW_SUB, LANES), lambda i: (i, 0)),
            pl.BlockSpec((TOP_K, TR), lambda i: (0, i)),
            pl.BlockSpec((TOP_K, TR), lambda i: (0, i)),
            pl.BlockSpec((TOP_K, TR), lambda i: (0, i)),
            pl.BlockSpec((N_EXPERTS, LANES), lambda i: (0, 0)),
        ],
        out_shape=[
            jax.ShapeDtypeStruct((t, d), f32),
            jax.ShapeDtypeStruct((t * ROW_SUB, LANES), u32),
            jax.ShapeDtypeStruct((TOP_K, t), i32),
            jax.ShapeDtypeStruct((TOP_K, t), i32),
            jax.ShapeDtypeStruct((TOP_K, t), f32),
            jax.ShapeDtypeStruct((N_EXPERTS, LANES), i32),
        ],
        scratch_shapes=[
            pltpu.VMEM((N_GROUPS, TR), f32),
            pltpu.VMEM((N_EXPERTS, LANES), f32),
        ],
        compiler_params=_cparams(("arbitrary",)),
        name="proj_route",
    )(a, w_bf, gate, x2, g.reshape(1, d), sh, sc, rw_cat, rbias.reshape(N_EXPERTS, 1))


def _tile_rows(ref, r, n=1):
    return ref.at[pl.ds(pl.multiple_of(r * ROW_SUB, ROW_SUB), n * ROW_SUB), :]


def _row_copy(src, s, dst, d, sem):
    return pltpu.make_async_copy(_tile_rows(src, s), _tile_rows(dst, d), sem)


def _dispatch_kernel(ztail_ref, dest_ref, xp_ref, sg_ref, su_ref, sd_ref, xs_ref, ysh_ref, zeros_scr, sem, zsem):
    td = xp_ref.shape[0] // ROW_SUB

    @pl.when(pl.program_id(0) == 0)
    def _():
        zeros_scr[...] = jnp.zeros_like(zeros_scr)

        def zcopy(e):
            return pltpu.make_async_copy(zeros_scr, _tile_rows(xs_ref, ztail_ref[e], BM), zsem)

        def zstart(e, c):
            zcopy(e).start()
            return c

        def zwait(e, c):
            zcopy(e).wait()
            return c

        lax.fori_loop(0, N_EXPERTS, zstart, 0)
        lax.fori_loop(0, N_EXPERTS, zwait, 0)

    for t in range(td):
        for k in range(TOP_K):
            _row_copy(xp_ref, t, xs_ref, dest_ref[k, t], sem).start(priority=k % DMA_PRIORITIES)
    ysh_ref[...] = _ffn_packed(xp_ref, td, sg_ref, su_ref, sd_ref)
    for t in range(td):
        for k in range(TOP_K):
            _row_copy(xp_ref, t, xs_ref, dest_ref[k, t], sem).wait()


def moe_dispatch(xp, dest, ztail, nrows, sg_bf, su_bf, sd_bf):
    t = xp.shape[0] // ROW_SUB
    d = sg_bf.shape[0]
    grid_spec = pltpu.PrefetchScalarGridSpec(
        num_scalar_prefetch=1,
        grid=(t // TD,),
        in_specs=[
            pl.BlockSpec((TOP_K, TD), lambda i, z: (0, i), memory_space=pltpu.SMEM),
            pl.BlockSpec((TD * ROW_SUB, LANES), lambda i, z: (i, 0)),
            pl.BlockSpec((d, D_SHARED), lambda i, z: (0, 0)),
            pl.BlockSpec((d, D_SHARED), lambda i, z: (0, 0)),
            pl.BlockSpec((D_SHARED, d), lambda i, z: (0, 0)),
        ],
        out_specs=[pl.BlockSpec(memory_space=pl.ANY), pl.BlockSpec((TD, d), lambda i, z: (i, 0))],
        scratch_shapes=[
            pltpu.VMEM((BM * ROW_SUB, LANES), u32),
            pltpu.SemaphoreType.DMA(()),
            pltpu.SemaphoreType.DMA(()),
        ],
    )
    return pl.pallas_call(
        _dispatch_kernel,
        grid_spec=grid_spec,
        out_shape=[jax.ShapeDtypeStruct((nrows * ROW_SUB, LANES), u32), jax.ShapeDtypeStruct((t, d), f32)],
        compiler_params=_cparams(("arbitrary",)),
        name="moe_dispatch",
    )(ztail, dest, xp, sg_bf, su_bf, sd_bf)


def _ffn_packed(x_ref, m, wg, wu, wd):
    los, his = _load_row_tiles(x_ref, m)
    x = jnp.concatenate([p.astype(bf16) for p in los + his], axis=1)
    hg = _dot(x, wg[...])
    hu = _dot(x, wu[...])
    a = (jax.nn.silu(hg) * hu).astype(bf16)
    return _dot(a, wd[...])


def _gmm_kernel(be_ref, bf_ref, nu_ref, nx_ref, sl_ref, xs_ref, wg_ref, wu_ref, wd_ref,
                ys_ref, wg_f, wu_f, wd_f, wg_s, wu_s, wd_s, sem, *, layer):
    i = pl.program_id(0)

    def copies(e, s):
        return (pltpu.make_async_copy(wg_ref.at[layer, e], wg_f.at[s], sem.at[s]),
                pltpu.make_async_copy(wu_ref.at[layer, e], wu_f.at[s], sem.at[s]),
                pltpu.make_async_copy(wd_ref.at[layer, e], wd_f.at[s], sem.at[s]))

    @pl.when(i == 0)
    def _():
        for cp in copies(be_ref[0], sl_ref[0]):
            cp.start()

    @pl.when(bf_ref[i] == 1)
    def _():
        s = sl_ref[i]
        for cp in copies(be_ref[i], s):
            cp.wait()

        @pl.when(nx_ref[i] >= 0)
        def _():
            for cp in copies(nx_ref[i], 1 - s):
                cp.start()

        wg_s[...] = wg_f[s].astype(bf16)
        wu_s[...] = wu_f[s].astype(bf16)
        wd_s[...] = wd_f[s].astype(bf16)

    @pl.when(i < nu_ref[0])
    def _():
        _store_row_tiles(ys_ref, _pack_halves(_ffn_packed(xs_ref, BM, wg_s, wu_s, wd_s)))

    @pl.when(i >= nu_ref[0])
    def _():
        ys_ref[...] = jnp.zeros_like(ys_ref)


def moe_gmm(xs, blk_e, blk_first, n_used, blk_next, blk_slot, w_gate, w_up, w_down, layer, nblk):
    d, de = w_gate.shape[2], w_gate.shape[3]
    grid_spec = pltpu.PrefetchScalarGridSpec(
        num_scalar_prefetch=5,
        grid=(nblk,),
        in_specs=[
            pl.BlockSpec((BM * ROW_SUB, LANES), lambda i, be, bf, nu, nx, sl: (jnp.minimum(i, nu[0] - 1), 0)),
            pl.BlockSpec(memory_space=pl.ANY),
            pl.BlockSpec(memory_space=pl.ANY),
            pl.BlockSpec(memory_space=pl.ANY),
        ],
        out_specs=pl.BlockSpec((BM * ROW_SUB, LANES), lambda i, be, bf, nu, nx, sl: (i, 0)),
        scratch_shapes=[
            pltpu.VMEM((2, d, de), f32),
            pltpu.VMEM((2, d, de), f32),
            pltpu.VMEM((2, de, d), f32),
            pltpu.VMEM((d, de), bf16),
            pltpu.VMEM((d, de), bf16),
            pltpu.VMEM((de, d), bf16),
            pltpu.SemaphoreType.DMA((2,)),
        ],
    )
    return pl.pallas_call(
        functools.partial(_gmm_kernel, layer=layer),
        grid_spec=grid_spec,
        out_shape=jax.ShapeDtypeStruct((nblk * BM * ROW_SUB, LANES), u32),
        compiler_params=_cparams(("arbitrary",)),
        name="moe_gmm",
    )(blk_e, blk_first, n_used, blk_next, blk_slot, xs, w_gate, w_up, w_down)


def _combine_kernel(dest_ref, dnext_ref, x_ref, ysh_ref, gw_ref, gate_ref, *rest, tail):
    if tail == 'qkv':
        ng_ref, nsh_ref, nsc_ref, wq_ref, ys_ref, o_ref, q_ref, gath_a, gath_b, sem, xn_scr = rest
    elif tail == 'final':
        ng_ref, ys_ref, o_ref, gath_a, gath_b, sem = rest
    else:
        ys_ref, o_ref, gath_a, gath_b, sem = rest
    td = x_ref.shape[0]
    i = pl.program_id(0)

    def copy(dref, buf, s, t, k):
        return _row_copy(ys_ref, dref[k, t], buf.at[k], t, sem.at[s])

    def wait_tile(dref, buf, s):
        for t in range(td):
            for k in range(TOP_K):
                copy(dref, buf, s, t, k).wait()

    @pl.when(i == 0)
    def _():
        def body(t, c):
            for k in range(TOP_K):
                copy(dest_ref, gath_a, 0, t, k).start(priority=k % DMA_PRIORITIES)
            return c
        lax.fori_loop(0, td, body, 0)

    def step(cur_buf, cur_s, nxt_buf, nxt_s):
        wait_tile(dest_ref, cur_buf, cur_s)
        for t in range(td):
            for k in range(TOP_K):
                copy(dnext_ref, nxt_buf, nxt_s, t, k).start(priority=k % DMA_PRIORITIES)
        for rb in range(td // COMB_RB):
            rows = slice(rb * COMB_RB, (rb + 1) * COMB_RB)
            ysh = ysh_ref[rows, :]
            acc_lo = [ysh[:, c * LANES:(c + 1) * LANES] for c in range(ROW_SUB)]
            acc_hi = [ysh[:, HALF + c * LANES:HALF + (c + 1) * LANES] for c in range(ROW_SUB)]
            gw = gw_ref[rows, :]
            for k in range(TOP_K):
                wk = jnp.broadcast_to(gw[:, k:k + 1], (COMB_RB, LANES))
                for c in range(ROW_SUB):
                    word = cur_buf[k, pl.ds(rb * COMB_RB * ROW_SUB + c, COMB_RB, stride=ROW_SUB), :]
                    lo, hi = _unpack_halves(word)
                    acc_lo[c] = acc_lo[c] + wk * lo
                    acc_hi[c] = acc_hi[c] + wk * hi
            y = jnp.concatenate(acc_lo + acc_hi, axis=1)
            out = x_ref[rows, :] + gate_ref[...] * y
            if tail == 'final':
                ms = jnp.mean(out * out, axis=-1, keepdims=True)
                out = (out * lax.rsqrt(ms + RMS_EPS)) * ng_ref[...]
            o_ref[rows, :] = out
            if tail == 'qkv':
                xn_scr[rows, :] = _norm_mod(out, ng_ref[...], nsh_ref[...], nsc_ref[...]).astype(bf16)
        if tail == 'qkv':
            q_ref[...] = _dot(xn_scr[...], wq_ref[...]).astype(q_ref.dtype)

        @pl.when(i == pl.num_programs(0) - 1)
        def _():
            wait_tile(dnext_ref, nxt_buf, nxt_s)

    @pl.when(i % 2 == 0)
    def _():
        step(gath_a, 0, gath_b, 1)

    @pl.when(i % 2 == 1)
    def _():
        step(gath_b, 1, gath_a, 0)


def moe_combine(x2, ysh, ys, dest, gw_t, gate, seq, final_g=None, next_qkv=None):
    t, d = x2.shape
    tps = seq // TD
    last = t // TD - 1
    row = lambda i: (i, 0)
    per_batch = pl.BlockSpec((None, 1, d), lambda i: (i // tps, 0, 0))
    in_specs = [
        pl.BlockSpec((TOP_K, TD), lambda i: (0, i), memory_space=pltpu.SMEM),
        pl.BlockSpec((TOP_K, TD), lambda i: (0, jnp.minimum(i + 1, last)), memory_space=pltpu.SMEM),
        pl.BlockSpec((TD, d), row),
        pl.BlockSpec((TD, d), row),
        pl.BlockSpec((TD, TOP_K), row),
        per_batch,
    ]
    args = [dest, dest, x2, ysh, gw_t, gate]
    out_specs = [pl.BlockSpec((TD, d), row)]
    out_shape = [jax.ShapeDtypeStruct((t, d), f32)]
    scratch = [
        pltpu.VMEM((TOP_K, TD * ROW_SUB, LANES), u32),
        pltpu.VMEM((TOP_K, TD * ROW_SUB, LANES), u32),
        pltpu.SemaphoreType.DMA((2,)),
    ]
    tail = None
    if final_g is not None:
        tail = 'final'
        in_specs.append(pl.BlockSpec((1, d), lambda i: (0, 0)))
        args.append(final_g.reshape(1, d))
    elif next_qkv is not None:
        tail = 'qkv'
        g, sh, sc, w_bf = next_qkv
        nq = w_bf.shape[1]
        in_specs += [pl.BlockSpec((1, d), lambda i: (0, 0)), per_batch, per_batch,
                     pl.BlockSpec((d, nq), lambda i: (0, 0))]
        args += [g.reshape(1, d), sh, sc, w_bf]
        out_specs.append(pl.BlockSpec((TD, nq), row))
        out_shape.append(jax.ShapeDtypeStruct((t, nq), bf16))
        scratch.append(pltpu.VMEM((TD, d), bf16))
    in_specs.append(pl.BlockSpec(memory_space=pl.ANY))
    args.append(ys)
    outs = pl.pallas_call(
        functools.partial(_combine_kernel, tail=tail),
        grid=(t // TD,),
        in_specs=in_specs,
        out_specs=out_specs,
        out_shape=out_shape,
        scratch_shapes=scratch,
        compiler_params=_cparams(("arbitrary",)),
        name="moe_combine",
    )(*args)
    return outs if tail == 'qkv' else outs[0]


SLOT_TN = 4096


def _slots_kernel(ps_ref, eidx_ref, rk_ref, o_ref):
    e = eidx_ref[...]
    start = jnp.zeros_like(e)
    for j in range(N_EXPERTS):
        start = jnp.where(e == j, ps_ref[j], start)
    o_ref[...] = start + rk_ref[...]


def moe_slots(pad_start, eidx, rk):
    t = eidx.shape[1]
    tn = min(t, SLOT_TN)
    spec = pl.BlockSpec((TOP_K, tn), lambda i, ps: (0, i))
    grid_spec = pltpu.PrefetchScalarGridSpec(
        num_scalar_prefetch=1, grid=(t // tn,), in_specs=[spec, spec], out_specs=spec)
    return pl.pallas_call(
        _slots_kernel,
        grid_spec=grid_spec,
        out_shape=jax.ShapeDtypeStruct((TOP_K, t), i32),
        compiler_params=_cparams(("arbitrary",)),
        name="moe_slots",
    )(pad_start, eidx, rk)


def moe_layer(mix, w_out_bf, x2, gate_m, g, sh, sc, gate, router_w, router_bias, w_gate, w_up, w_down, layer,
              sh_gate, sh_up, sh_down, seq, final_g=None, next_qkv=None):
    t = x2.shape[0]
    a = t * TOP_K
    nblk = a // BM + N_EXPERTS
    rw_t = router_w.T
    rw_hi = rw_t.astype(bf16)
    rw_lo = (rw_t - rw_hi.astype(f32)).astype(bf16)
    rw_cat = jnp.concatenate([rw_hi, rw_lo], axis=0)

    x2, xp, eidx, rk, gw, cnt = proj_route(mix, w_out_bf, x2, gate_m, g, sh, sc, rw_cat, router_bias, seq)

    counts = cnt[:, 0]
    padded = ((counts + BM - 1) // BM) * BM
    cum_pad = jnp.cumsum(padded)
    pad_start = cum_pad - padded
    n_used = (cum_pad[-1] // BM).astype(i32).reshape(1)
    blk_row = jnp.arange(nblk, dtype=i32) * BM
    blk_e = jnp.minimum(jnp.sum((cum_pad[None, :] <= blk_row[:, None]).astype(i32), axis=1), N_EXPERTS - 1)
    changed = jnp.concatenate([jnp.ones((1,), i32), (blk_e[1:] != blk_e[:-1]).astype(i32)])
    blk_first = changed * (blk_row < cum_pad[-1]).astype(i32)
    eids = jnp.arange(N_EXPERTS, dtype=i32)
    has = counts > 0
    later = jnp.flip(lax.cummin(jnp.flip(jnp.where(has, eids, N_EXPERTS))))
    nxt_e = jnp.concatenate([later[1:], jnp.full((1,), N_EXPERTS, i32)])
    nxt_e = jnp.where(nxt_e < N_EXPERTS, nxt_e, -1)
    slot_e = (jnp.cumsum(has.astype(i32)) - 1) % 2
    onehot = (blk_e[:, None] == eids[None, :]).astype(i32)
    blk_next = jnp.sum(onehot * nxt_e[None, :], axis=1).astype(i32)
    blk_slot = jnp.sum(onehot * slot_e[None, :], axis=1).astype(i32)
    dest = moe_slots(pad_start.astype(i32), eidx, rk)
    ztail = jnp.maximum(cum_pad - BM, 0).astype(i32)

    xs, ysh = moe_dispatch(xp, dest, ztail, nblk * BM, sh_gate.astype(bf16), sh_up.astype(bf16),
                           sh_down.astype(bf16))
    ys = moe_gmm(xs, blk_e, blk_first, n_used, blk_next, blk_slot, w_gate, w_up, w_down, layer, nblk)
    return moe_combine(x2, ysh, ys, dest, gw.T, gate, seq, final_g=final_g, next_qkv=next_qkv)


def _qkv_weight(w_qkv):
    dq = N_Q_HEADS * HEAD_DIM
    dkv = N_KV_HEADS * HEAD_DIM
    d = w_qkv.shape[0]
    wq = w_qkv[:, :dq]
    wk = w_qkv[:, dq:dq + dkv].reshape(d, N_KV_HEADS, 1, HEAD_DIM)
    wv = w_qkv[:, dq + dkv:].reshape(d, N_KV_HEADS, 1, HEAD_DIM)
    wk2 = jnp.broadcast_to(wk, (d, N_KV_HEADS, 2, HEAD_DIM)).reshape(d, QK_COLS)
    wv2 = jnp.broadcast_to(wv, (d, N_KV_HEADS, 2, HEAD_DIM)).reshape(d, QK_COLS)
    return jnp.concatenate([wq, wk2, wv2], axis=1).astype(bf16)


def kernel(x, c, ada_w, ada_b, norm_mix_g, norm_ffn_g, hyb_w_in, conv_w, sgu_ln_g, sgu_ln_b, sgu_w, sgu_b,
           hyb_w_out, attn_w_qkv, attn_sinks, attn_w_o, rel_bias, router_w, router_bias, exp_w_gate,
           exp_w_up, exp_w_down, sh_w_gate, sh_w_up, sh_w_down, final_g):
    batch, seq, d = x.shape
    t = batch * seq
    x2 = x.reshape(t, d)
    mod = ada_mod(c, ada_w, ada_b)
    mods = [[mod[l, :, k * d:(k + 1) * d].reshape(batch, 1, d) for k in range(6)] for l in range(DEPTH)]
    qkv = None
    for l in range(DEPTH):
        sh_m, sc_m, g_m, sh_f, sc_f, g_f = mods[l]
        i = l // 2
        if l % 2 == 0:
            mix = hyb_in(x2, norm_mix_g[l], sh_m, sc_m, hyb_w_in[i].astype(bf16), conv_w[i], sgu_ln_g[i],
                         sgu_ln_b[i], sgu_w[i], sgu_b[i], seq)
            w_out = hyb_w_out[i].astype(bf16)
        else:
            mix = swa_attn(qkv, _attn_tables(rel_bias), attn_sinks[i], batch, seq)
            w_out = attn_w_o[i].astype(bf16)
        tail = {}
        if l == DEPTH - 1:
            tail = dict(final_g=final_g)
        else:
            tail = dict(next_qkv=(norm_mix_g[l + 1], mods[l + 1][0], mods[l + 1][1],
                                  _qkv_weight(attn_w_qkv[(l + 1) // 2])))
        res = moe_layer(mix, w_out, x2, g_m, norm_ffn_g[l], sh_f, sc_f, g_f, router_w[l], router_bias[l],
                        exp_w_gate, exp_w_up, exp_w_down, l, sh_w_gate[l], sh_w_up[l], sh_w_down[l], seq, **tail)
        x2, qkv = res if 'next_qkv' in tail else (res, None)
    return x2.reshape(batch, seq, d)
```

```python
import functools

import numpy as np
import jax
import jax.numpy as jnp
from jax import lax
from jax.experimental import pallas as pl
from jax.experimental.pallas import tpu as pltpu

f32 = jnp.float32
bf16 = jnp.bfloat16
i32 = jnp.int32
u32 = jnp.uint32

D_MODEL = 2048
DEPTH = 2
RMS_EPS = 1e-6
LN_EPS = 1e-5
D_CONV = 1024
CONV_WIDTH = 3
D_SGU = 1024
SGU_GROUPS = 8
SGU_HD = D_SGU // SGU_GROUPS
CHUNK = 128
HYB_IN = 3 * D_CONV + 2 * D_SGU
HEAD_DIM = 64
N_Q_HEADS = D_MODEL // HEAD_DIM
N_KV_HEADS = N_Q_HEADS // 8
GQA_GROUP = N_Q_HEADS // N_KV_HEADS
WINDOW = 128
ATT_BLOCK = 128
REL_BUCKETS = 32
REL_MAX_DIST = 128
N_EXPERTS = 64
TOP_K = 8
N_GROUPS = 8
TOPK_GROUPS = 4
E_PER_G = N_EXPERTS // N_GROUPS
D_EXPERT = 512
D_SHARED = 512
ROUTED_SCALE = 2.5

LANES = 128
HALF = D_MODEL // 2
VMEM_LIMIT = 56 * 1024 * 1024
TM = 512
TN_IN = 1024
TR = 512
TD = 256
BM = 512
ADA_TN = 1024
DMA_PRIORITIES = 2
COMB_RB = 16


def _cparams(sem, vmem=VMEM_LIMIT):
    return pltpu.CompilerParams(dimension_semantics=sem, vmem_limit_bytes=vmem)


def _norm_mod(x, g, sh, sc):
    ms = jnp.mean(x * x, axis=-1, keepdims=True)
    y = x * lax.rsqrt(ms + RMS_EPS)
    return (y * g) * (1.0 + sc) + sh


def _pack_halves(y):
    h = y.shape[1] // 2
    lo = lax.bitcast_convert_type(y[:, :h].astype(bf16).astype(f32), u32)
    hi = lax.bitcast_convert_type(y[:, h:].astype(bf16).astype(f32), u32)
    return (lo >> 16) | (hi & jnp.uint32(0xFFFF0000))


def _unpack_halves(w):
    lo = lax.bitcast_convert_type(w << 16, f32)
    hi = lax.bitcast_convert_type(w & jnp.uint32(0xFFFF0000), f32)
    return lo, hi


ROW_SUB = HALF // LANES


def _store_row_tiles(ref, packed):
    n = packed.shape[0]
    for c in range(ROW_SUB):
        ref[pl.ds(c, n, stride=ROW_SUB), :] = packed[:, c * LANES:(c + 1) * LANES]


def _load_row_tiles(ref, n):
    los, his = [], []
    for c in range(ROW_SUB):
        lo, hi = _unpack_halves(ref[pl.ds(c, n, stride=ROW_SUB), :])
        los.append(lo)
        his.append(hi)
    return los, his


def _gelu(x):
    return 0.5 * x * (1.0 + lax.erf(x * np.float32(np.sqrt(0.5))))


def _dot(a, b):
    return jnp.dot(a, b, preferred_element_type=f32)


def _dot_nt(a, b):
    return lax.dot_general(a, b, (((1,), (1,)), ((), ())), preferred_element_type=f32)


def _ada_kernel(c_ref, w_ref, b_ref, o_ref):
    ca = jax.nn.silu(c_ref[...]).astype(bf16)
    o_ref[...] = _dot(ca, w_ref[...].astype(bf16)) + b_ref[...]


def ada_mod(c, ada_w, ada_b):
    depth, d, n = ada_w.shape
    b = c.shape[0]
    return pl.pallas_call(
        _ada_kernel,
        grid=(depth, n // ADA_TN),
        in_specs=[
            pl.BlockSpec((b, d), lambda l, j: (0, 0)),
            pl.BlockSpec((None, d, ADA_TN), lambda l, j: (l, 0, j)),
            pl.BlockSpec((None, 1, ADA_TN), lambda l, j: (l, 0, j)),
        ],
        out_specs=pl.BlockSpec((None, b, ADA_TN), lambda l, j: (l, 0, j)),
        out_shape=jax.ShapeDtypeStruct((depth, b, n), f32),
        compiler_params=_cparams(("arbitrary", "arbitrary")),
        name="ada_mod",
    )(c, ada_w, ada_b.reshape(depth, 1, n))


def _hyb_in_kernel(x_ref, g_ref, sh_ref, sc_ref, w_ref, cw_ref, lng_ref, lnb_ref, sw_ref, sb_ref,
                   o_ref, xn_scr, a_scr, b_scr, carry_scr, *, tiles_per_seq):
    i = pl.program_id(0)
    j = pl.program_id(1)
    tm = x_ref.shape[0]

    @pl.when(j == 0)
    def _():
        xn_scr[...] = _norm_mod(x_ref[...], g_ref[...], sh_ref[...], sc_ref[...]).astype(bf16)

    @pl.when(jnp.logical_and(i == 0, j == 0))
    def _():
        carry_scr[...] = jnp.zeros_like(carry_scr)

    p = _dot(xn_scr[...], w_ref[...])

    @pl.when(j == 0)
    def _():
        a_scr[...] = p

    @pl.when(j == 1)
    def _():
        b_scr[...] = p

    @pl.when(j == 2)
    def _():
        z = b_scr[...] * p
        row = lax.broadcasted_iota(i32, z.shape, 0)
        first = (i % tiles_per_seq) == 0
        prev = jnp.where(first, 0.0, carry_scr[...])
        p1 = prev[7:8, :]
        p2 = prev[6:7, :]
        z1 = jnp.where(row == 0, p1, pltpu.roll(z, 1, 0))
        z2 = jnp.where(row == 0, p2, jnp.where(row == 1, p1, pltpu.roll(z, 2, 0)))
        cw = cw_ref[...]
        conv = cw[0:1, :] * z2 + cw[1:2, :] * z1 + cw[2:3, :] * z
        carry_scr[...] = z[tm - 8:, :]
        o_ref[:, :D_CONV] = (a_scr[...] * conv).astype(o_ref.dtype)

    @pl.when(j == 3)
    def _():
        a_scr[...] = _gelu(p)

    @pl.when(j == 4)
    def _():
        v = _gelu(p)
        mu = jnp.mean(v, axis=-1, keepdims=True)
        vc = v - mu
        var = jnp.mean(vc * vc, axis=-1, keepdims=True)
        v = (vc * lax.rsqrt(var + LN_EPS)) * lng_ref[...] + lnb_ref[...]
        vb = v.astype(bf16)
        nch = tm // CHUNK
        r = lax.broadcasted_iota(i32, (CHUNK, CHUNK), 0)
        c = lax.broadcasted_iota(i32, (CHUNK, CHUNK), 1)
        tril = r >= c
        for g in range(SGU_GROUPS):
            ws = jnp.where(tril, sw_ref[g], 0.0).astype(bf16)
            vg = jnp.concatenate(
                [vb[n * CHUNK:(n + 1) * CHUNK, g * SGU_HD:(g + 1) * SGU_HD] for n in range(nch)], axis=1)
            sg = _dot(ws, vg)
            bb = sb_ref[g]
            for n in range(nch):
                s = sg[:, n * SGU_HD:(n + 1) * SGU_HD] + bb
                u = a_scr[n * CHUNK:(n + 1) * CHUNK, g * SGU_HD:(g + 1) * SGU_HD]
                o_ref[n * CHUNK:(n + 1) * CHUNK, D_CONV + g * SGU_HD:D_CONV + (g + 1) * SGU_HD] = (
                    (u * s).astype(o_ref.dtype))


def hyb_in(x2, g, sh, sc, w_in_bf, conv_w, ln_g, ln_b, sgu_w, sgu_b, seq):
    t, d = x2.shape
    tps = seq // TM
    nj = HYB_IN // TN_IN
    sbb = jnp.broadcast_to(sgu_b[:, :, None], (SGU_GROUPS, CHUNK, SGU_HD))
    kern = functools.partial(_hyb_in_kernel, tiles_per_seq=tps)
    return pl.pallas_call(
        kern,
        grid=(t // TM, nj),
        in_specs=[
            pl.BlockSpec((TM, d), lambda i, j: (i, 0)),
            pl.BlockSpec((1, d), lambda i, j: (0, 0)),
            pl.BlockSpec((None, 1, d), lambda i, j: (i // tps, 0, 0)),
            pl.BlockSpec((None, 1, d), lambda i, j: (i // tps, 0, 0)),
            pl.BlockSpec((d, TN_IN), lambda i, j: (0, j)),
            pl.BlockSpec((CONV_WIDTH, D_CONV), lambda i, j: (0, 0)),
            pl.BlockSpec((1, D_SGU), lambda i, j: (0, 0)),
            pl.BlockSpec((1, D_SGU), lambda i, j: (0, 0)),
            pl.BlockSpec((SGU_GROUPS, CHUNK, CHUNK), lambda i, j: (0, 0, 0)),
            pl.BlockSpec((SGU_GROUPS, CHUNK, SGU_HD), lambda i, j: (0, 0, 0)),
        ],
        out_specs=pl.BlockSpec((TM, D_CONV + D_SGU), lambda i, j: (i, 0)),
        out_shape=jax.ShapeDtypeStruct((t, D_CONV + D_SGU), bf16),
        scratch_shapes=[
            pltpu.VMEM((TM, d), bf16),
            pltpu.VMEM((TM, TN_IN), f32),
            pltpu.VMEM((TM, TN_IN), f32),
            pltpu.VMEM((8, D_CONV), f32),
        ],
        compiler_params=_cparams(("arbitrary", "arbitrary")),
        name="hyb_in",
    )(x2, g.reshape(1, d), sh, sc, w_in_bf, conv_w, ln_g.reshape(1, -1), ln_b.reshape(1, -1), sgu_w, sbb)


QK_COLS = N_KV_HEADS * 2 * HEAD_DIM


def _attn_kernel(q_ref, kp_ref, kc_ref, vp_ref, vc_ref, bias_ref, sink_ref, o_ref):
    lane = lax.broadcasted_iota(i32, (2 * ATT_BLOCK, 2 * HEAD_DIM), 1)
    low = lane < HEAD_DIM
    zero = jnp.zeros((), bf16)
    scale = jnp.asarray(HEAD_DIM ** -0.5, bf16)
    ones = jnp.ones((2 * ATT_BLOCK, 2 * HEAD_DIM), bf16)
    for kh in range(N_KV_HEADS):
        cs = slice(kh * 2 * HEAD_DIM, (kh + 1) * 2 * HEAD_DIM)
        kk = jnp.concatenate([kp_ref[:, cs], kc_ref[:, cs]], axis=0)
        vv = jnp.concatenate([vp_ref[:, cs], vc_ref[:, cs]], axis=0)
        kz = (jnp.where(low, kk, zero), jnp.where(low, zero, kk))
        vz = (jnp.where(low, vv, zero), jnp.where(low, zero, vv))
        q0 = kh * GQA_GROUP * HEAD_DIM
        qs = jnp.concatenate(
            [q_ref[:, q0 + pr * 2 * HEAD_DIM:q0 + (pr + 1) * 2 * HEAD_DIM] for pr in range(ATT_PAIRS)],
            axis=0) * scale
        acc = None
        for par in range(2):
            s = _dot_nt(qs, kz[par]) + bias_ref[kh, par]
            sk = jnp.concatenate(
                [jnp.full((ATT_BLOCK, ATT_BLOCK), sink_ref[kh * GQA_GROUP + 2 * pr + par], f32)
                 for pr in range(ATT_PAIRS)], axis=0)
            rm = jnp.max(s, axis=-1, keepdims=True)
            mb = jnp.maximum(jnp.broadcast_to(rm, sk.shape), sk)
            p = jnp.concatenate([jnp.exp(s[:, :ATT_BLOCK] - mb), jnp.exp(s[:, ATT_BLOCK:] - mb)], axis=1)
            ov = _dot(p.astype(bf16), jnp.concatenate([vz[par], ones], axis=1))
            den = ov[:, 2 * HEAD_DIM:] + jnp.exp(sk - mb)
            o = ov[:, :2 * HEAD_DIM] / den
            acc = o if acc is None else acc + o
        for pr in range(ATT_PAIRS):
            o_ref[:, q0 + pr * 2 * HEAD_DIM:q0 + (pr + 1) * 2 * HEAD_DIM] = (
                acc[pr * ATT_BLOCK:(pr + 1) * ATT_BLOCK, :].astype(o_ref.dtype))


ATT_PAIRS = GQA_GROUP // 2


def swa_attn(qkv, bias_m, sinks, batch, seq):
    t = qkv.shape[0]
    nb = seq // ATT_BLOCK
    dq = N_Q_HEADS * HEAD_DIM
    kcol = dq // QK_COLS
    vcol = kcol + 1

    def prev(b, i):
        return b * nb + jnp.maximum(i - 1, 0)

    return pl.pallas_call(
        _attn_kernel,
        grid=(batch, nb),
        in_specs=[
            pl.BlockSpec((ATT_BLOCK, dq), lambda b, i: (b * nb + i, 0)),
            pl.BlockSpec((ATT_BLOCK, QK_COLS), lambda b, i: (prev(b, i), kcol)),
            pl.BlockSpec((ATT_BLOCK, QK_COLS), lambda b, i: (b * nb + i, kcol)),
            pl.BlockSpec((ATT_BLOCK, QK_COLS), lambda b, i: (prev(b, i), vcol)),
            pl.BlockSpec((ATT_BLOCK, QK_COLS), lambda b, i: (b * nb + i, vcol)),
            pl.BlockSpec((None, N_KV_HEADS, 2, ATT_PAIRS * ATT_BLOCK, 2 * ATT_BLOCK),
                         lambda b, i: (jnp.where(i == 0, 1, 0), 0, 0, 0, 0)),
            pl.BlockSpec(memory_space=pltpu.SMEM),
        ],
        out_specs=pl.BlockSpec((ATT_BLOCK, dq), lambda b, i: (b * nb + i, 0)),
        out_shape=jax.ShapeDtypeStruct((t, dq), bf16),
        compiler_params=_cparams(("arbitrary", "arbitrary")),
        name="swa_attn",
    )(qkv, qkv, qkv, qkv, qkv, bias_m, sinks)


def _t5_bucket(dist):
    n = np.maximum(dist, 0)
    max_exact = REL_BUCKETS // 2
    large = max_exact + (np.log(np.maximum(n, 1) / max_exact) / np.log(REL_MAX_DIST / max_exact)
                         * (REL_BUCKETS - max_exact)).astype(np.int32)
    large = np.minimum(large, REL_BUCKETS - 1)
    return np.where(n < max_exact, n, large).astype(np.int32)


def _by_kv_parity(a):
    rest = a.shape[2:]
    a = a.reshape((N_KV_HEADS, ATT_PAIRS, 2, ATT_BLOCK) + rest)
    a = jnp.moveaxis(a, 2, 1)
    return a.reshape((N_KV_HEADS, 2, ATT_PAIRS * ATT_BLOCK) + rest)


def _attn_tables(rel_bias):
    t_loc = np.arange(ATT_BLOCK)[:, None]
    j_loc = np.arange(2 * ATT_BLOCK)[None, :]
    dist = ATT_BLOCK + t_loc - j_loc
    band = (dist >= 0) & (dist < WINDOW)
    onehot = jnp.asarray(np.eye(REL_BUCKETS, dtype=np.float32)[_t5_bucket(dist)])
    bias = jnp.einsum('tjb,bh->htj', onehot, rel_bias.astype(f32), precision=lax.Precision.HIGHEST)
    first = band & (j_loc >= ATT_BLOCK)
    tabs = [_by_kv_parity(jnp.where(jnp.asarray(mask)[None], bias, -jnp.inf)) for mask in (band, first)]
    return jnp.stack(tabs)


def _route_kernel(a_ref, w_ref, gate_ref, x_ref, g_ref, sh_ref, sc_ref, rw_ref, rb_ref, sg_ref, su_ref, sd_ref,
                  xnew_ref, ysh_ref, dest_ref, gw_ref, cnt_ref, xs_ref,
                  grp_scr, carry_scr, stash_a, stash_b, dvm_scr, dsm_a, dsm_b, sem, dsem, *, ntiles, tokens):
    tm = x_ref.shape[0]
    step = pl.program_id(0)

    def issue(stash, dsm, s):
        for t in range(tm):
            for k in range(TOP_K):
                _row_copy(stash, t, xs_ref, dsm[k, t], sem.at[s]).start(priority=k % DMA_PRIORITIES)

    def drain(stash, s):
        for t in range(tm):
            for k in range(TOP_K):
                _row_copy(stash, t, xs_ref, 0, sem.at[s]).wait()

    @pl.when(step == 0)
    def _():
        carry_scr[...] = jnp.zeros_like(carry_scr)

    @pl.when(jnp.logical_and(step >= 2, step % 2 == 0))
    def _():
        drain(stash_a, 0)

    @pl.when(jnp.logical_and(step >= 2, step % 2 == 1))
    def _():
        drain(stash_b, 1)

    @pl.when(jnp.logical_and(step >= 1, step % 2 == 1))
    def _():
        issue(stash_a, dsm_a, 0)

    @pl.when(jnp.logical_and(step >= 1, step % 2 == 0))
    def _():
        issue(stash_b, dsm_b, 1)

    @pl.when(step < ntiles)
    def _():
        packed = _route_tile(a_ref, w_ref, gate_ref, x_ref, g_ref, sh_ref, sc_ref, rw_ref, rb_ref, sg_ref, su_ref,
                             sd_ref, xnew_ref, ysh_ref, dest_ref, gw_ref, cnt_ref, grp_scr, carry_scr, dvm_scr,
                             tokens)

        @pl.when(step % 2 == 0)
        def _():
            _store_row_tiles(stash_a, packed)
            cp = pltpu.make_async_copy(dvm_scr, dsm_a, dsem)
            cp.start()
            cp.wait()

        @pl.when(step % 2 == 1)
        def _():
            _store_row_tiles(stash_b, packed)
            cp = pltpu.make_async_copy(dvm_scr, dsm_b, dsem)
            cp.start()
            cp.wait()

    @pl.when(jnp.logical_and(step == ntiles, (ntiles - 1) % 2 == 0))
    def _():
        drain(stash_a, 0)

    @pl.when(jnp.logical_and(step == ntiles, (ntiles - 1) % 2 == 1))
    def _():
        drain(stash_b, 1)


def _route_tile(a_ref, w_ref, gate_ref, x_ref, g_ref, sh_ref, sc_ref, rw_ref, rb_ref, sg_ref, su_ref, sd_ref,
                xnew_ref, ysh_ref, dest_ref, gw_ref, cnt_ref, grp_scr, carry_scr, dvm_scr, tokens):
    tm = x_ref.shape[0]
    ne = N_EXPERTS
    x = x_ref[...] + gate_ref[...] * _dot(a_ref[...], w_ref[...])
    xnew_ref[...] = x
    xn = _norm_mod(x, g_ref[...], sh_ref[...], sc_ref[...])
    x_hi = xn.astype(bf16)
    x_lo = (xn - x_hi.astype(f32)).astype(bf16)
    rw = rw_ref[...]
    a = _dot_nt(rw, x_hi)
    b = _dot_nt(rw[:ne], x_lo)
    logits = a[:ne] + a[ne:] + b
    scores = jax.nn.sigmoid(logits)
    sel = scores + rb_ref[...]

    for g in range(N_GROUPS):
        tile = sel[g * E_PER_G:(g + 1) * E_PER_G, :]
        t1 = jnp.max(tile, axis=0, keepdims=True)
        dup = jnp.sum(jnp.where(tile == t1, 1.0, 0.0), axis=0, keepdims=True) >= 2.0
        t2 = jnp.max(jnp.where(tile < t1, tile, -jnp.inf), axis=0, keepdims=True)
        grp_scr[g:g + 1, :] = t1 + jnp.where(dup, t1, t2)
    gs = grp_scr[...]

    gi = lax.broadcasted_iota(i32, (N_GROUPS, tm), 0)
    grank = jnp.zeros((N_GROUPS, tm), i32)
    for g in range(N_GROUPS):
        r = grp_scr[g:g + 1, :]
        ge = jnp.where(r >= gs, 1, 0)
        gt = jnp.where(r > gs, 1, 0)
        grank = grank + jnp.where(gi > g, ge, gt)
    grp_scr[...] = jnp.where(grank < TOPK_GROUPS, 1.0, 0.0)

    masked = jnp.concatenate(
        [jnp.where(grp_scr[g:g + 1, :] > 0.5, sel[g * E_PER_G:(g + 1) * E_PER_G, :], -jnp.inf)
         for g in range(N_GROUPS)], axis=0)

    ei = lax.broadcasted_iota(i32, (ne, tm), 0)
    cur = masked
    chf = jnp.zeros((ne, tm), f32)
    for _ in range(TOP_K):
        top = jnp.max(cur, axis=0, keepdims=True)
        first = jnp.min(jnp.where(cur == top, ei, ne), axis=0, keepdims=True)
        pick = ei == first
        chf = jnp.where(pick, 1.0, chf)
        cur = jnp.where(pick, -jnp.inf, cur)
    chosen = chf > 0.5

    wun = jnp.where(chosen, scores, 0.0)
    gwt = wun / jnp.sum(wun, axis=0, keepdims=True) * ROUTED_SCALE

    rr = lax.broadcasted_iota(i32, (tm, tm), 0)
    cc = lax.broadcasted_iota(i32, (tm, tm), 1)
    upper = jnp.where(rr < cc, 1.0, 0.0).astype(bf16)
    chb = chf.astype(bf16)
    pos = carry_scr[:, 0:1] + _dot(chb, upper)
    carry_scr[...] = carry_scr[...] + jnp.sum(chf, axis=1, keepdims=True)
    cnt_ref[...] = carry_scr[...].astype(i32)

    er = lax.broadcasted_iota(i32, (ne, ne), 0)
    ec = lax.broadcasted_iota(i32, (ne, ne), 1)
    lower = jnp.where(ec < er, 1.0, 0.0).astype(bf16)
    below = _dot(lower, chb)
    slot = jnp.where(chosen, below, -1.0)
    eif = ei.astype(f32)
    for k in range(TOP_K):
        mk = slot == float(k)
        slot_k = jnp.sum(jnp.where(mk, eif * float(tokens) + pos, 0.0), axis=0, keepdims=True).astype(i32)
        dest_ref[k:k + 1, :] = slot_k
        dvm_scr[k:k + 1, :] = slot_k
        gw_ref[k:k + 1, :] = jnp.sum(jnp.where(mk, gwt, 0.0), axis=0, keepdims=True)

    hg = _dot(x_hi, sg_ref[...])
    hu = _dot(x_hi, su_ref[...])
    ysh_ref[...] = _dot((jax.nn.silu(hg) * hu).astype(bf16), sd_ref[...])
    return _pack_halves(xn)


def proj_route(a, w_bf, x2, gate, g, sh, sc, rw_cat, rbias, sg_bf, su_bf, sd_bf, seq):
    t, d = x2.shape
    k = a.shape[1]
    tps = seq // TR
    n = t // TR
    tile = lambda i: jnp.minimum(i, n - 1)
    const = lambda i: (0, 0)
    once = dict(pipeline_mode=pl.Buffered(1))
    per_batch = pl.BlockSpec((None, 1, d), lambda i: (tile(i) // tps, 0, 0))
    kern = functools.partial(_route_kernel, ntiles=n, tokens=t)
    return pl.pallas_call(
        kern,
        grid=(n + 1,),
        in_specs=[
            pl.BlockSpec((TR, k), lambda i: (tile(i), 0)),
            pl.BlockSpec((k, d), const, **once),
            per_batch,
            pl.BlockSpec((TR, d), lambda i: (tile(i), 0)),
            pl.BlockSpec((1, d), const),
            per_batch,
            per_batch,
            pl.BlockSpec((2 * N_EXPERTS, d), const),
            pl.BlockSpec((N_EXPERTS, 1), const),
            pl.BlockSpec((d, D_SHARED), const, **once),
            pl.BlockSpec((d, D_SHARED), const, **once),
            pl.BlockSpec((D_SHARED, d), const, **once),
        ],
        out_specs=[
            pl.BlockSpec((TR, d), lambda i: (tile(i), 0)),
            pl.BlockSpec((TR, d), lambda i: (tile(i), 0)),
            pl.BlockSpec((TOP_K, TR), lambda i: (0, tile(i))),
            pl.BlockSpec((TOP_K, TR), lambda i: (0, tile(i))),
            pl.BlockSpec((N_EXPERTS, LANES), const),
            pl.BlockSpec(memory_space=pl.ANY),
        ],
        out_shape=[
            jax.ShapeDtypeStruct((t, d), f32),
            jax.ShapeDtypeStruct((t, d), f32),
            jax.ShapeDtypeStruct((TOP_K, t), i32),
            jax.ShapeDtypeStruct((TOP_K, t), f32),
            jax.ShapeDtypeStruct((N_EXPERTS, LANES), i32),
            jax.ShapeDtypeStruct((N_EXPERTS * t * ROW_SUB, LANES), u32),
        ],
        scratch_shapes=[
            pltpu.VMEM((N_GROUPS, TR), f32),
            pltpu.VMEM((N_EXPERTS, LANES), f32),
            pltpu.VMEM((TR * ROW_SUB, LANES), u32),
            pltpu.VMEM((TR * ROW_SUB, LANES), u32),
            pltpu.VMEM((TOP_K, TR), i32),
            pltpu.SMEM((TOP_K, TR), i32),
            pltpu.SMEM((TOP_K, TR), i32),
            pltpu.SemaphoreType.DMA((2,)),
            pltpu.SemaphoreType.DMA(()),
        ],
        compiler_params=_cparams(("arbitrary",)),
        name="proj_route",
    )(a, w_bf, gate, x2, g.reshape(1, d), sh, sc, rw_cat, rbias.reshape(N_EXPERTS, 1), sg_bf, su_bf, sd_bf)


def _tile_rows(ref, r, n=1):
    return ref.at[pl.ds(pl.multiple_of(r * ROW_SUB, ROW_SUB), n * ROW_SUB), :]


def _row_copy(src, s, dst, d, sem):
    return pltpu.make_async_copy(_tile_rows(src, s), _tile_rows(dst, d), sem)


def _ffn_packed(x_ref, m, wg, wu, wd):
    los, his = _load_row_tiles(x_ref, m)
    x = jnp.concatenate([p.astype(bf16) for p in los + his], axis=1)
    hg = _dot(x, wg[...])
    hu = _dot(x, wu[...])
    a = (jax.nn.silu(hg) * hu).astype(bf16)
    return _dot(a, wd[...])


def _gmm_kernel(be_ref, bf_ref, nu_ref, nx_ref, sl_ref, rb_ref, xs_ref, wg_ref, wu_ref, wd_ref,
                ys_ref, wg_f, wu_f, wd_f, wg_s, wu_s, wd_s, sem, *, layer):
    i = pl.program_id(0)

    def copies(e, s):
        return (pltpu.make_async_copy(wg_ref.at[layer, e], wg_f.at[s], sem.at[s]),
                pltpu.make_async_copy(wu_ref.at[layer, e], wu_f.at[s], sem.at[s]),
                pltpu.make_async_copy(wd_ref.at[layer, e], wd_f.at[s], sem.at[s]))

    @pl.when(i == 0)
    def _():
        for cp in copies(be_ref[0], sl_ref[0]):
            cp.start()

    @pl.when(bf_ref[i] == 1)
    def _():
        s = sl_ref[i]
        for cp in copies(be_ref[i], s):
            cp.wait()

        @pl.when(nx_ref[i] >= 0)
        def _():
            for cp in copies(nx_ref[i], 1 - s):
                cp.start()

        wg_s[...] = wg_f[s].astype(bf16)
        wu_s[...] = wu_f[s].astype(bf16)
        wd_s[...] = wd_f[s].astype(bf16)

    @pl.when(i < nu_ref[0])
    def _():
        _store_row_tiles(ys_ref, _pack_halves(_ffn_packed(xs_ref, BM, wg_s, wu_s, wd_s)))


def moe_gmm(xs, blk_e, blk_first, n_used, blk_next, blk_slot, blk_rb, w_gate, w_up, w_down, layer, nblk):
    d, de = w_gate.shape[2], w_gate.shape[3]
    row_block = lambda i, be, bf, nu, nx, sl, rb: (rb[jnp.minimum(i, nu[0] - 1)], 0)
    grid_spec = pltpu.PrefetchScalarGridSpec(
        num_scalar_prefetch=6,
        grid=(nblk,),
        in_specs=[
            pl.BlockSpec((BM * ROW_SUB, LANES), row_block),
            pl.BlockSpec(memory_space=pl.ANY),
            pl.BlockSpec(memory_space=pl.ANY),
            pl.BlockSpec(memory_space=pl.ANY),
        ],
        out_specs=pl.BlockSpec((BM * ROW_SUB, LANES), row_block),
        scratch_shapes=[
            pltpu.VMEM((2, d, de), f32),
            pltpu.VMEM((2, d, de), f32),
            pltpu.VMEM((2, de, d), f32),
            pltpu.VMEM((d, de), bf16),
            pltpu.VMEM((d, de), bf16),
            pltpu.VMEM((de, d), bf16),
            pltpu.SemaphoreType.DMA((2,)),
        ],
    )
    return pl.pallas_call(
        functools.partial(_gmm_kernel, layer=layer),
        grid_spec=grid_spec,
        out_shape=jax.ShapeDtypeStruct(xs.shape, u32),
        compiler_params=_cparams(("arbitrary",)),
        name="moe_gmm",
    )(blk_e, blk_first, n_used, blk_next, blk_slot, blk_rb, xs, w_gate, w_up, w_down)


def _combine_kernel(dest_ref, dnext_ref, x_ref, ysh_ref, gw_ref, gate_ref, *rest, tail):
    if tail == 'qkv':
        ng_ref, nsh_ref, nsc_ref, wq_ref, ys_ref, o_ref, q_ref, gath_a, gath_b, sem, xn_scr = rest
    elif tail == 'final':
        ng_ref, ys_ref, o_ref, gath_a, gath_b, sem = rest
    else:
        ys_ref, o_ref, gath_a, gath_b, sem = rest
    td = x_ref.shape[0]
    i = pl.program_id(0)

    def copy(dref, buf, s, t, k):
        return _row_copy(ys_ref, dref[k, t], buf.at[k], t, sem.at[s])

    def wait_tile(dref, buf, s):
        for t in range(td):
            for k in range(TOP_K):
                copy(dref, buf, s, t, k).wait()

    @pl.when(i == 0)
    def _():
        def body(t, c):
            for k in range(TOP_K):
                copy(dest_ref, gath_a, 0, t, k).start(priority=k % DMA_PRIORITIES)
            return c
        lax.fori_loop(0, td, body, 0)

    def step(cur_buf, cur_s, nxt_buf, nxt_s):
        wait_tile(dest_ref, cur_buf, cur_s)
        for t in range(td):
            for k in range(TOP_K):
                copy(dnext_ref, nxt_buf, nxt_s, t, k).start(priority=k % DMA_PRIORITIES)
        for rb in range(td // COMB_RB):
            rows = slice(rb * COMB_RB, (rb + 1) * COMB_RB)
            ysh = ysh_ref[rows, :]
            acc_lo = [ysh[:, c * LANES:(c + 1) * LANES] for c in range(ROW_SUB)]
            acc_hi = [ysh[:, HALF + c * LANES:HALF + (c + 1) * LANES] for c in range(ROW_SUB)]
            gw = gw_ref[rows, :]
            for k in range(TOP_K):
                wk = jnp.broadcast_to(gw[:, k:k + 1], (COMB_RB, LANES))
                for c in range(ROW_SUB):
                    word = cur_buf[k, pl.ds(rb * COMB_RB * ROW_SUB + c, COMB_RB, stride=ROW_SUB), :]
                    lo, hi = _unpack_halves(word)
                    acc_lo[c] = acc_lo[c] + wk * lo
                    acc_hi[c] = acc_hi[c] + wk * hi
            y = jnp.concatenate(acc_lo + acc_hi, axis=1)
            out = x_ref[rows, :] + gate_ref[...] * y
            if tail == 'final':
                ms = jnp.mean(out * out, axis=-1, keepdims=True)
                out = (out * lax.rsqrt(ms + RMS_EPS)) * ng_ref[...]
            o_ref[rows, :] = out
            if tail == 'qkv':
                xn_scr[rows, :] = _norm_mod(out, ng_ref[...], nsh_ref[...], nsc_ref[...]).astype(bf16)
        if tail == 'qkv':
            q_ref[...] = _dot(xn_scr[...], wq_ref[...]).astype(q_ref.dtype)

        @pl.when(i == pl.num_programs(0) - 1)
        def _():
            wait_tile(dnext_ref, nxt_buf, nxt_s)

    @pl.when(i % 2 == 0)
    def _():
        step(gath_a, 0, gath_b, 1)

    @pl.when(i % 2 == 1)
    def _():
        step(gath_b, 1, gath_a, 0)


def moe_combine(x2, ysh, ys, dest, gw_t, gate, seq, final_g=None, next_qkv=None):
    t, d = x2.shape
    tps = seq // TD
    last = t // TD - 1
    row = lambda i: (i, 0)
    per_batch = pl.BlockSpec((None, 1, d), lambda i: (i // tps, 0, 0))
    in_specs = [
        pl.BlockSpec((TOP_K, TD), lambda i: (0, i), memory_space=pltpu.SMEM),
        pl.BlockSpec((TOP_K, TD), lambda i: (0, jnp.minimum(i + 1, last)), memory_space=pltpu.SMEM),
        pl.BlockSpec((TD, d), row),
        pl.BlockSpec((TD, d), row),
        pl.BlockSpec((TD, TOP_K), row),
        per_batch,
    ]
    args = [dest, dest, x2, ysh, gw_t, gate]
    out_specs = [pl.BlockSpec((TD, d), row)]
    out_shape = [jax.ShapeDtypeStruct((t, d), f32)]
    scratch = [
        pltpu.VMEM((TOP_K, TD * ROW_SUB, LANES), u32),
        pltpu.VMEM((TOP_K, TD * ROW_SUB, LANES), u32),
        pltpu.SemaphoreType.DMA((2,)),
    ]
    tail = None
    if final_g is not None:
        tail = 'final'
        in_specs.append(pl.BlockSpec((1, d), lambda i: (0, 0)))
        args.append(final_g.reshape(1, d))
    elif next_qkv is not None:
        tail = 'qkv'
        g, sh, sc, w_bf = next_qkv
        nq = w_bf.shape[1]
        in_specs += [pl.BlockSpec((1, d), lambda i: (0, 0)), per_batch, per_batch,
                     pl.BlockSpec((d, nq), lambda i: (0, 0))]
        args += [g.reshape(1, d), sh, sc, w_bf]
        out_specs.append(pl.BlockSpec((TD, nq), row))
        out_shape.append(jax.ShapeDtypeStruct((t, nq), bf16))
        scratch.append(pltpu.VMEM((TD, d), bf16))
    in_specs.append(pl.BlockSpec(memory_space=pl.ANY))
    args.append(ys)
    outs = pl.pallas_call(
        functools.partial(_combine_kernel, tail=tail),
        grid=(t // TD,),
        in_specs=in_specs,
        out_specs=out_specs,
        out_shape=out_shape,
        scratch_shapes=scratch,
        compiler_params=_cparams(("arbitrary",)),
        name="moe_combine",
    )(*args)
    return outs if tail == 'qkv' else outs[0]


def moe_layer(mix, w_out_bf, x2, gate_m, g, sh, sc, gate, router_w, router_bias, w_gate, w_up, w_down, layer,
              sh_gate, sh_up, sh_down, seq, final_g=None, next_qkv=None):
    t = x2.shape[0]
    a = t * TOP_K
    nblk = a // BM + N_EXPERTS
    rw_t = router_w.T
    rw_hi = rw_t.astype(bf16)
    rw_lo = (rw_t - rw_hi.astype(f32)).astype(bf16)
    rw_cat = jnp.concatenate([rw_hi, rw_lo], axis=0)

    x2, ysh, dest, gw, cnt, xs = proj_route(mix, w_out_bf, x2, gate_m, g, sh, sc, rw_cat, router_bias,
                                            sh_gate.astype(bf16), sh_up.astype(bf16), sh_down.astype(bf16), seq)

    counts = cnt[:, 0]
    nb_e = (counts + BM - 1) // BM
    cum_nb = jnp.cumsum(nb_e)
    n_used = cum_nb[-1].astype(i32).reshape(1)
    blk_idx = jnp.arange(nblk, dtype=i32)
    blk_e = jnp.minimum(jnp.sum((cum_nb[None, :] <= blk_idx[:, None]).astype(i32), axis=1), N_EXPERTS - 1)
    changed = jnp.concatenate([jnp.ones((1,), i32), (blk_e[1:] != blk_e[:-1]).astype(i32)])
    blk_first = changed * (blk_idx < cum_nb[-1]).astype(i32)
    eids = jnp.arange(N_EXPERTS, dtype=i32)
    has = counts > 0
    later = jnp.flip(lax.cummin(jnp.flip(jnp.where(has, eids, N_EXPERTS))))
    nxt_e = jnp.concatenate([later[1:], jnp.full((1,), N_EXPERTS, i32)])
    nxt_e = jnp.where(nxt_e < N_EXPERTS, nxt_e, -1)
    slot_e = (jnp.cumsum(has.astype(i32)) - 1) % 2
    onehot = (blk_e[:, None] == eids[None, :]).astype(i32)
    blk_next = jnp.sum(onehot * nxt_e[None, :], axis=1).astype(i32)
    blk_slot = jnp.sum(onehot * slot_e[None, :], axis=1).astype(i32)
    first_blk_e = (cum_nb - nb_e).astype(i32)
    blk_rb = (blk_e * (t // BM) + blk_idx - jnp.sum(onehot * first_blk_e[None, :], axis=1)).astype(i32)

    ys = moe_gmm(xs, blk_e, blk_first, n_used, blk_next, blk_slot, blk_rb, w_gate, w_up, w_down, layer, nblk)
    return moe_combine(x2, ysh, ys, dest, gw.T, gate, seq, final_g=final_g, next_qkv=next_qkv)


def _qkv_weight(w_qkv):
    dq = N_Q_HEADS * HEAD_DIM
    dkv = N_KV_HEADS * HEAD_DIM
    d = w_qkv.shape[0]
    wq = w_qkv[:, :dq]
    wk = w_qkv[:, dq:dq + dkv].reshape(d, N_KV_HEADS, 1, HEAD_DIM)
    wv = w_qkv[:, dq + dkv:].reshape(d, N_KV_HEADS, 1, HEAD_DIM)
    wk2 = jnp.broadcast_to(wk, (d, N_KV_HEADS, 2, HEAD_DIM)).reshape(d, QK_COLS)
    wv2 = jnp.broadcast_to(wv, (d, N_KV_HEADS, 2, HEAD_DIM)).reshape(d, QK_COLS)
    return jnp.concatenate([wq, wk2, wv2], axis=1).astype(bf16)


def kernel(x, c, ada_w, ada_b, norm_mix_g, norm_ffn_g, hyb_w_in, conv_w, sgu_ln_g, sgu_ln_b, sgu_w, sgu_b,
           hyb_w_out, attn_w_qkv, attn_sinks, attn_w_o, rel_bias, router_w, router_bias, exp_w_gate,
           exp_w_up, exp_w_down, sh_w_gate, sh_w_up, sh_w_down, final_g):
    batch, seq, d = x.shape
    t = batch * seq
    x2 = x.reshape(t, d)
    mod = ada_mod(c, ada_w, ada_b)
    mods = [[mod[l, :, k * d:(k + 1) * d].reshape(batch, 1, d) for k in range(6)] for l in range(DEPTH)]
    qkv = None
    for l in range(DEPTH):
        sh_m, sc_m, g_m, sh_f, sc_f, g_f = mods[l]
        i = l // 2
        if l % 2 == 0:
            mix = hyb_in(x2, norm_mix_g[l], sh_m, sc_m, hyb_w_in[i].astype(bf16), conv_w[i], sgu_ln_g[i],
                         sgu_ln_b[i], sgu_w[i], sgu_b[i], seq)
            w_out = hyb_w_out[i].astype(bf16)
        else:
            mix = swa_attn(qkv, _attn_tables(rel_bias), attn_sinks[i], batch, seq)
            w_out = attn_w_o[i].astype(bf16)
        tail = {}
        if l == DEPTH - 1:
            tail = dict(final_g=final_g)
        else:
            tail = dict(next_qkv=(norm_mix_g[l + 1], mods[l + 1][0], mods[l + 1][1],
                                  _qkv_weight(attn_w_qkv[(l + 1) // 2])))
        res = moe_layer(mix, w_out, x2, g_m, norm_ffn_g[l], sh_f, sc_f, g_f, router_w[l], router_bias[l],
                        exp_w_gate, exp_w_up, exp_w_down, l, sh_w_gate[l], sh_w_up[l], sh_w_down[l], seq, **tail)
        x2, qkv = res if 'next_qkv' in tail else (res, None)
    return x2.reshape(batch, seq, d)
```

```python
import functools

import numpy as np
import jax
import jax.numpy as jnp
from jax import lax
from jax.experimental import pallas as pl
from jax.experimental.pallas import tpu as pltpu

f32 = jnp.float32
bf16 = jnp.bfloat16
i32 = jnp.int32
u32 = jnp.uint32

D_MODEL = 2048
DEPTH = 2
RMS_EPS = 1e-6
LN_EPS = 1e-5
D_CONV = 1024
CONV_WIDTH = 3
D_SGU = 1024
SGU_GROUPS = 8
SGU_HD = D_SGU // SGU_GROUPS
CHUNK = 128
HYB_IN = 3 * D_CONV + 2 * D_SGU
HEAD_DIM = 64
N_Q_HEADS = D_MODEL // HEAD_DIM
N_KV_HEADS = N_Q_HEADS // 8
GQA_GROUP = N_Q_HEADS // N_KV_HEADS
WINDOW = 128
ATT_BLOCK = 128
REL_BUCKETS = 32
REL_MAX_DIST = 128
N_EXPERTS = 64
TOP_K = 8
N_GROUPS = 8
TOPK_GROUPS = 4
E_PER_G = N_EXPERTS // N_GROUPS
D_EXPERT = 512
D_SHARED = 512
ROUTED_SCALE = 2.5

LANES = 128
HALF = D_MODEL // 2
VMEM_LIMIT = 56 * 1024 * 1024
TM = 512
TN_IN = 1024
TR = 512
TD = 256
BM = 512
ADA_TN = 1024
DMA_PRIORITIES = 2
COMB_RB = 16


def _cparams(sem, vmem=VMEM_LIMIT):
    return pltpu.CompilerParams(dimension_semantics=sem, vmem_limit_bytes=vmem)


def _norm_mod(x, g, sh, sc):
    ms = jnp.mean(x * x, axis=-1, keepdims=True)
    y = x * lax.rsqrt(ms + RMS_EPS)
    return (y * g) * (1.0 + sc) + sh


def _pack_halves(y):
    h = y.shape[1] // 2
    lo = lax.bitcast_convert_type(y[:, :h].astype(bf16).astype(f32), u32)
    hi = lax.bitcast_convert_type(y[:, h:].astype(bf16).astype(f32), u32)
    return (lo >> 16) | (hi & jnp.uint32(0xFFFF0000))


def _unpack_halves(w):
    lo = lax.bitcast_convert_type(w << 16, f32)
    hi = lax.bitcast_convert_type(w & jnp.uint32(0xFFFF0000), f32)
    return lo, hi


ROW_SUB = HALF // LANES


def _store_row_tiles(ref, packed):
    n = packed.shape[0]
    for c in range(ROW_SUB):
        ref[pl.ds(c, n, stride=ROW_SUB), :] = packed[:, c * LANES:(c + 1) * LANES]


def _load_row_tiles(ref, n):
    los, his = [], []
    for c in range(ROW_SUB):
        lo, hi = _unpack_halves(ref[pl.ds(c, n, stride=ROW_SUB), :])
        los.append(lo)
        his.append(hi)
    return los, his


def _gelu(x):
    return 0.5 * x * (1.0 + lax.erf(x * np.float32(np.sqrt(0.5))))


def _dot(a, b):
    return jnp.dot(a, b, preferred_element_type=f32)


def _dot_nt(a, b):
    return lax.dot_general(a, b, (((1,), (1,)), ((), ())), preferred_element_type=f32)


def _ada_kernel(c_ref, w_ref, b_ref, o_ref):
    ca = jax.nn.silu(c_ref[...]).astype(bf16)
    o_ref[...] = _dot(ca, w_ref[...].astype(bf16)) + b_ref[...]


def ada_mod(c, ada_w, ada_b):
    depth, d, n = ada_w.shape
    b = c.shape[0]
    return pl.pallas_call(
        _ada_kernel,
        grid=(depth, n // ADA_TN),
        in_specs=[
            pl.BlockSpec((b, d), lambda l, j: (0, 0)),
            pl.BlockSpec((None, d, ADA_TN), lambda l, j: (l, 0, j)),
            pl.BlockSpec((None, 1, ADA_TN), lambda l, j: (l, 0, j)),
        ],
        out_specs=pl.BlockSpec((None, b, ADA_TN), lambda l, j: (l, 0, j)),
        out_shape=jax.ShapeDtypeStruct((depth, b, n), f32),
        compiler_params=_cparams(("arbitrary", "arbitrary")),
        name="ada_mod",
    )(c, ada_w, ada_b.reshape(depth, 1, n))


def _hyb_in_kernel(x_ref, g_ref, sh_ref, sc_ref, w_ref, cw_ref, lng_ref, lnb_ref, sw_ref, sb_ref,
                   o_ref, xn_scr, a_scr, b_scr, carry_scr, *, tiles_per_seq):
    i = pl.program_id(0)
    j = pl.program_id(1)
    tm = x_ref.shape[0]

    @pl.when(j == 0)
    def _():
        xn_scr[...] = _norm_mod(x_ref[...], g_ref[...], sh_ref[...], sc_ref[...]).astype(bf16)

    @pl.when(jnp.logical_and(i == 0, j == 0))
    def _():
        carry_scr[...] = jnp.zeros_like(carry_scr)

    p = _dot(xn_scr[...], w_ref[...])

    @pl.when(j == 0)
    def _():
        a_scr[...] = p

    @pl.when(j == 1)
    def _():
        b_scr[...] = p

    @pl.when(j == 2)
    def _():
        z = b_scr[...] * p
        row = lax.broadcasted_iota(i32, z.shape, 0)
        first = (i % tiles_per_seq) == 0
        prev = jnp.where(first, 0.0, carry_scr[...])
        p1 = prev[7:8, :]
        p2 = prev[6:7, :]
        z1 = jnp.where(row == 0, p1, pltpu.roll(z, 1, 0))
        z2 = jnp.where(row == 0, p2, jnp.where(row == 1, p1, pltpu.roll(z, 2, 0)))
        cw = cw_ref[...]
        conv = cw[0:1, :] * z2 + cw[1:2, :] * z1 + cw[2:3, :] * z
        carry_scr[...] = z[tm - 8:, :]
        o_ref[:, :D_CONV] = (a_scr[...] * conv).astype(o_ref.dtype)

    @pl.when(j == 3)
    def _():
        a_scr[...] = _gelu(p)

    @pl.when(j == 4)
    def _():
        v = _gelu(p)
        mu = jnp.mean(v, axis=-1, keepdims=True)
        vc = v - mu
        var = jnp.mean(vc * vc, axis=-1, keepdims=True)
        v = (vc * lax.rsqrt(var + LN_EPS)) * lng_ref[...] + lnb_ref[...]
        vb = v.astype(bf16)
        nch = tm // CHUNK
        r = lax.broadcasted_iota(i32, (CHUNK, CHUNK), 0)
        c = lax.broadcasted_iota(i32, (CHUNK, CHUNK), 1)
        tril = r >= c
        for g in range(SGU_GROUPS):
            ws = jnp.where(tril, sw_ref[g], 0.0).astype(bf16)
            vg = jnp.concatenate(
                [vb[n * CHUNK:(n + 1) * CHUNK, g * SGU_HD:(g + 1) * SGU_HD] for n in range(nch)], axis=1)
            sg = _dot(ws, vg)
            bb = sb_ref[g]
            for n in range(nch):
                s = sg[:, n * SGU_HD:(n + 1) * SGU_HD] + bb
                u = a_scr[n * CHUNK:(n + 1) * CHUNK, g * SGU_HD:(g + 1) * SGU_HD]
                o_ref[n * CHUNK:(n + 1) * CHUNK, D_CONV + g * SGU_HD:D_CONV + (g + 1) * SGU_HD] = (
                    (u * s).astype(o_ref.dtype))


def hyb_in(x2, g, sh, sc, w_in_bf, conv_w, ln_g, ln_b, sgu_w, sgu_b, seq):
    t, d = x2.shape
    tps = seq // TM
    nj = HYB_IN // TN_IN
    sbb = jnp.broadcast_to(sgu_b[:, :, None], (SGU_GROUPS, CHUNK, SGU_HD))
    kern = functools.partial(_hyb_in_kernel, tiles_per_seq=tps)
    return pl.pallas_call(
        kern,
        grid=(t // TM, nj),
        in_specs=[
            pl.BlockSpec((TM, d), lambda i, j: (i, 0)),
            pl.BlockSpec((1, d), lambda i, j: (0, 0)),
            pl.BlockSpec((None, 1, d), lambda i, j: (i // tps, 0, 0)),
            pl.BlockSpec((None, 1, d), lambda i, j: (i // tps, 0, 0)),
            pl.BlockSpec((d, TN_IN), lambda i, j: (0, j)),
            pl.BlockSpec((CONV_WIDTH, D_CONV), lambda i, j: (0, 0)),
            pl.BlockSpec((1, D_SGU), lambda i, j: (0, 0)),
            pl.BlockSpec((1, D_SGU), lambda i, j: (0, 0)),
            pl.BlockSpec((SGU_GROUPS, CHUNK, CHUNK), lambda i, j: (0, 0, 0)),
            pl.BlockSpec((SGU_GROUPS, CHUNK, SGU_HD), lambda i, j: (0, 0, 0)),
        ],
        out_specs=pl.BlockSpec((TM, D_CONV + D_SGU), lambda i, j: (i, 0)),
        out_shape=jax.ShapeDtypeStruct((t, D_CONV + D_SGU), bf16),
        scratch_shapes=[
            pltpu.VMEM((TM, d), bf16),
            pltpu.VMEM((TM, TN_IN), f32),
            pltpu.VMEM((TM, TN_IN), f32),
            pltpu.VMEM((8, D_CONV), f32),
        ],
        compiler_params=_cparams(("arbitrary", "arbitrary")),
        name="hyb_in",
    )(x2, g.reshape(1, d), sh, sc, w_in_bf, conv_w, ln_g.reshape(1, -1), ln_b.reshape(1, -1), sgu_w, sbb)


QK_COLS = N_KV_HEADS * 2 * HEAD_DIM


def _attn_kernel(q_ref, kp_ref, kc_ref, vp_ref, vc_ref, bias_ref, sink_ref, o_ref):
    lane = lax.broadcasted_iota(i32, (2 * ATT_BLOCK, 2 * HEAD_DIM), 1)
    low = lane < HEAD_DIM
    zero = jnp.zeros((), bf16)
    scale = jnp.asarray(HEAD_DIM ** -0.5, bf16)
    ones = jnp.ones((2 * ATT_BLOCK, 2 * HEAD_DIM), bf16)
    for kh in range(N_KV_HEADS):
        cs = slice(kh * 2 * HEAD_DIM, (kh + 1) * 2 * HEAD_DIM)
        kk = jnp.concatenate([kp_ref[:, cs], kc_ref[:, cs]], axis=0)
        vv = jnp.concatenate([vp_ref[:, cs], vc_ref[:, cs]], axis=0)
        kz = (jnp.where(low, kk, zero), jnp.where(low, zero, kk))
        vz = (jnp.where(low, vv, zero), jnp.where(low, zero, vv))
        q0 = kh * GQA_GROUP * HEAD_DIM
        qs = jnp.concatenate(
            [q_ref[:, q0 + pr * 2 * HEAD_DIM:q0 + (pr + 1) * 2 * HEAD_DIM] for pr in range(ATT_PAIRS)],
            axis=0) * scale
        acc = None
        for par in range(2):
            s = _dot_nt(qs, kz[par]) + bias_ref[kh, par]
            sk = jnp.concatenate(
                [jnp.full((ATT_BLOCK, ATT_BLOCK), sink_ref[kh * GQA_GROUP + 2 * pr + par], f32)
                 for pr in range(ATT_PAIRS)], axis=0)
            rm = jnp.max(s, axis=-1, keepdims=True)
            mb = jnp.maximum(jnp.broadcast_to(rm, sk.shape), sk)
            p = jnp.concatenate([jnp.exp(s[:, :ATT_BLOCK] - mb), jnp.exp(s[:, ATT_BLOCK:] - mb)], axis=1)
            ov = _dot(p.astype(bf16), jnp.concatenate([vz[par], ones], axis=1))
            den = ov[:, 2 * HEAD_DIM:] + jnp.exp(sk - mb)
            o = ov[:, :2 * HEAD_DIM] / den
            acc = o if acc is None else acc + o
        for pr in range(ATT_PAIRS):
            o_ref[:, q0 + pr * 2 * HEAD_DIM:q0 + (pr + 1) * 2 * HEAD_DIM] = (
                acc[pr * ATT_BLOCK:(pr + 1) * ATT_BLOCK, :].astype(o_ref.dtype))


ATT_PAIRS = GQA_GROUP // 2


def swa_attn(qkv, bias_m, sinks, batch, seq):
    t = qkv.shape[0]
    nb = seq // ATT_BLOCK
    dq = N_Q_HEADS * HEAD_DIM
    kcol = dq // QK_COLS
    vcol = kcol + 1

    def prev(b, i):
        return b * nb + jnp.maximum(i - 1, 0)

    return pl.pallas_call(
        _attn_kernel,
        grid=(batch, nb),
        in_specs=[
            pl.BlockSpec((ATT_BLOCK, dq), lambda b, i: (b * nb + i, 0)),
            pl.BlockSpec((ATT_BLOCK, QK_COLS), lambda b, i: (prev(b, i), kcol)),
            pl.BlockSpec((ATT_BLOCK, QK_COLS), lambda b, i: (b * nb + i, kcol)),
            pl.BlockSpec((ATT_BLOCK, QK_COLS), lambda b, i: (prev(b, i), vcol)),
            pl.BlockSpec((ATT_BLOCK, QK_COLS), lambda b, i: (b * nb + i, vcol)),
            pl.BlockSpec((None, N_KV_HEADS, 2, ATT_PAIRS * ATT_BLOCK, 2 * ATT_BLOCK),
                         lambda b, i: (jnp.where(i == 0, 1, 0), 0, 0, 0, 0)),
            pl.BlockSpec(memory_space=pltpu.SMEM),
        ],
        out_specs=pl.BlockSpec((ATT_BLOCK, dq), lambda b, i: (b * nb + i, 0)),
        out_shape=jax.ShapeDtypeStruct((t, dq), bf16),
        compiler_params=_cparams(("arbitrary", "arbitrary")),
        name="swa_attn",
    )(qkv, qkv, qkv, qkv, qkv, bias_m, sinks)


def _t5_bucket(dist):
    n = np.maximum(dist, 0)
    max_exact = REL_BUCKETS // 2
    large = max_exact + (np.log(np.maximum(n, 1) / max_exact) / np.log(REL_MAX_DIST / max_exact)
                         * (REL_BUCKETS - max_exact)).astype(np.int32)
    large = np.minimum(large, REL_BUCKETS - 1)
    return np.where(n < max_exact, n, large).astype(np.int32)


def _by_kv_parity(a):
    rest = a.shape[2:]
    a = a.reshape((N_KV_HEADS, ATT_PAIRS, 2, ATT_BLOCK) + rest)
    a = jnp.moveaxis(a, 2, 1)
    return a.reshape((N_KV_HEADS, 2, ATT_PAIRS * ATT_BLOCK) + rest)


def _attn_tables(rel_bias):
    t_loc = np.arange(ATT_BLOCK)[:, None]
    j_loc = np.arange(2 * ATT_BLOCK)[None, :]
    dist = ATT_BLOCK + t_loc - j_loc
    band = (dist >= 0) & (dist < WINDOW)
    onehot = jnp.asarray(np.eye(REL_BUCKETS, dtype=np.float32)[_t5_bucket(dist)])
    bias = jnp.einsum('tjb,bh->htj', onehot, rel_bias.astype(f32), precision=lax.Precision.HIGHEST)
    first = band & (j_loc >= ATT_BLOCK)
    tabs = [_by_kv_parity(jnp.where(jnp.asarray(mask)[None], bias, -jnp.inf)) for mask in (band, first)]
    return jnp.stack(tabs)


def _route_kernel(a_ref, w_ref, gate_ref, x_ref, g_ref, sh_ref, sc_ref, rw_ref, rb_ref, sg_ref, su_ref, sd_ref,
                  xnew_ref, ysh_ref, dest_ref, gw_ref, cnt_ref, xs_ref,
                  grp_scr, carry_scr, stash_a, stash_b, dvm_scr, dsm_a, dsm_b, sem, dsem, *, ntiles, tokens):
    tm = x_ref.shape[0]
    step = pl.program_id(0)

    def issue(stash, dsm, s):
        for t in range(tm):
            for k in range(TOP_K):
                _row_copy(stash, t, xs_ref, dsm[k, t], sem.at[s]).start(priority=k % DMA_PRIORITIES)

    def drain(stash, s):
        for t in range(tm):
            for k in range(TOP_K):
                _row_copy(stash, t, xs_ref, 0, sem.at[s]).wait()

    @pl.when(step == 0)
    def _():
        carry_scr[...] = jnp.zeros_like(carry_scr)

    @pl.when(jnp.logical_and(step >= 2, step % 2 == 0))
    def _():
        drain(stash_a, 0)

    @pl.when(jnp.logical_and(step >= 2, step % 2 == 1))
    def _():
        drain(stash_b, 1)

    def tile_step(prev, cur):
        if prev is not None:
            issue(*prev)
        packed = _route_tile(a_ref, w_ref, gate_ref, x_ref, g_ref, sh_ref, sc_ref, rw_ref, rb_ref, sg_ref, su_ref,
                             sd_ref, xnew_ref, ysh_ref, dest_ref, gw_ref, cnt_ref, grp_scr, carry_scr, dvm_scr,
                             tokens)
        stash, dsm, _ = cur
        _store_row_tiles(stash, packed)
        cp = pltpu.make_async_copy(dvm_scr, dsm, dsem)
        cp.start()
        cp.wait()

    slot_a = (stash_a, dsm_a, 0)
    slot_b = (stash_b, dsm_b, 1)

    @pl.when(step == 0)
    def _():
        tile_step(None, slot_a)

    @pl.when(jnp.logical_and(jnp.logical_and(step >= 1, step < ntiles), step % 2 == 1))
    def _():
        tile_step(slot_a, slot_b)

    @pl.when(jnp.logical_and(jnp.logical_and(step >= 1, step < ntiles), step % 2 == 0))
    def _():
        tile_step(slot_b, slot_a)

    last = slot_a if (ntiles - 1) % 2 == 0 else slot_b

    @pl.when(step == ntiles)
    def _():
        issue(*last)
        drain(last[0], last[2])


def _route_tile(a_ref, w_ref, gate_ref, x_ref, g_ref, sh_ref, sc_ref, rw_ref, rb_ref, sg_ref, su_ref, sd_ref,
                xnew_ref, ysh_ref, dest_ref, gw_ref, cnt_ref, grp_scr, carry_scr, dvm_scr, tokens):
    tm = x_ref.shape[0]
    ne = N_EXPERTS
    x = x_ref[...] + gate_ref[...] * _dot(a_ref[...], w_ref[...])
    xnew_ref[...] = x
    xn = _norm_mod(x, g_ref[...], sh_ref[...], sc_ref[...])
    x_hi = xn.astype(bf16)
    x_lo = (xn - x_hi.astype(f32)).astype(bf16)
    rw = rw_ref[...]
    a = _dot_nt(rw, x_hi)
    b = _dot_nt(rw[:ne], x_lo)
    logits = a[:ne] + a[ne:] + b
    scores = jax.nn.sigmoid(logits)
    sel = scores + rb_ref[...]

    for g in range(N_GROUPS):
        tile = sel[g * E_PER_G:(g + 1) * E_PER_G, :]
        t1 = jnp.max(tile, axis=0, keepdims=True)
        dup = jnp.sum(jnp.where(tile == t1, 1.0, 0.0), axis=0, keepdims=True) >= 2.0
        t2 = jnp.max(jnp.where(tile < t1, tile, -jnp.inf), axis=0, keepdims=True)
        grp_scr[g:g + 1, :] = t1 + jnp.where(dup, t1, t2)
    gs = grp_scr[...]

    gi = lax.broadcasted_iota(i32, (N_GROUPS, tm), 0)
    grank = jnp.zeros((N_GROUPS, tm), i32)
    for g in range(N_GROUPS):
        r = grp_scr[g:g + 1, :]
        ge = jnp.where(r >= gs, 1, 0)
        gt = jnp.where(r > gs, 1, 0)
        grank = grank + jnp.where(gi > g, ge, gt)
    grp_scr[...] = jnp.where(grank < TOPK_GROUPS, 1.0, 0.0)

    masked = jnp.concatenate(
        [jnp.where(grp_scr[g:g + 1, :] > 0.5, sel[g * E_PER_G:(g + 1) * E_PER_G, :], -jnp.inf)
         for g in range(N_GROUPS)], axis=0)

    ei = lax.broadcasted_iota(i32, (ne, tm), 0)
    cur = masked
    chf = jnp.zeros((ne, tm), f32)
    for _ in range(TOP_K):
        top = jnp.max(cur, axis=0, keepdims=True)
        first = jnp.min(jnp.where(cur == top, ei, ne), axis=0, keepdims=True)
        pick = ei == first
        chf = jnp.where(pick, 1.0, chf)
        cur = jnp.where(pick, -jnp.inf, cur)
    chosen = chf > 0.5

    wun = jnp.where(chosen, scores, 0.0)
    gwt = wun / jnp.sum(wun, axis=0, keepdims=True) * ROUTED_SCALE

    rr = lax.broadcasted_iota(i32, (tm, tm), 0)
    cc = lax.broadcasted_iota(i32, (tm, tm), 1)
    upper = jnp.where(rr < cc, 1.0, 0.0).astype(bf16)
    chb = chf.astype(bf16)
    pos = carry_scr[:, 0:1] + _dot(chb, upper)
    carry_scr[...] = carry_scr[...] + jnp.sum(chf, axis=1, keepdims=True)
    cnt_ref[...] = carry_scr[...].astype(i32)

    er = lax.broadcasted_iota(i32, (ne, ne), 0)
    ec = lax.broadcasted_iota(i32, (ne, ne), 1)
    lower = jnp.where(ec < er, 1.0, 0.0).astype(bf16)
    below = _dot(lower, chb)
    slot = jnp.where(chosen, below, -1.0)
    eif = ei.astype(f32)
    for k in range(TOP_K):
        mk = slot == float(k)
        slot_k = jnp.sum(jnp.where(mk, eif * float(tokens) + pos, 0.0), axis=0, keepdims=True).astype(i32)
        dest_ref[k:k + 1, :] = slot_k
        dvm_scr[k:k + 1, :] = slot_k
        gw_ref[k:k + 1, :] = jnp.sum(jnp.where(mk, gwt, 0.0), axis=0, keepdims=True)

    hg = _dot(x_hi, sg_ref[...])
    hu = _dot(x_hi, su_ref[...])
    ysh_ref[...] = _dot((jax.nn.silu(hg) * hu).astype(bf16), sd_ref[...])
    return _pack_halves(xn)


def proj_route(a, w_bf, x2, gate, g, sh, sc, rw_cat, rbias, sg_bf, su_bf, sd_bf, seq):
    t, d = x2.shape
    k = a.shape[1]
    tps = seq // TR
    n = t // TR
    tile = lambda i: jnp.minimum(i, n - 1)
    const = lambda i: (0, 0)
    once = dict(pipeline_mode=pl.Buffered(1))
    per_batch = pl.BlockSpec((None, 1, d), lambda i: (tile(i) // tps, 0, 0))
    kern = functools.partial(_route_kernel, ntiles=n, tokens=t)
    return pl.pallas_call(
        kern,
        grid=(n + 1,),
        in_specs=[
            pl.BlockSpec((TR, k), lambda i: (tile(i), 0)),
            pl.BlockSpec((k, d), const, **once),
            per_batch,
            pl.BlockSpec((TR, d), lambda i: (tile(i), 0)),
            pl.BlockSpec((1, d), const),
            per_batch,
            per_batch,
            pl.BlockSpec((2 * N_EXPERTS, d), const),
            pl.BlockSpec((N_EXPERTS, 1), const),
            pl.BlockSpec((d, D_SHARED), const, **once),
            pl.BlockSpec((d, D_SHARED), const, **once),
            pl.BlockSpec((D_SHARED, d), const, **once),
        ],
        out_specs=[
            pl.BlockSpec((TR, d), lambda i: (tile(i), 0)),
            pl.BlockSpec((TR, d), lambda i: (tile(i), 0)),
            pl.BlockSpec((TOP_K, TR), lambda i: (0, tile(i))),
            pl.BlockSpec((TOP_K, TR), lambda i: (0, tile(i))),
            pl.BlockSpec((N_EXPERTS, LANES), const),
            pl.BlockSpec(memory_space=pl.ANY),
        ],
        out_shape=[
            jax.ShapeDtypeStruct((t, d), f32),
            jax.ShapeDtypeStruct((t, d), f32),
            jax.ShapeDtypeStruct((TOP_K, t), i32),
            jax.ShapeDtypeStruct((TOP_K, t), f32),
            jax.ShapeDtypeStruct((N_EXPERTS, LANES), i32),
            jax.ShapeDtypeStruct((N_EXPERTS * t * ROW_SUB, LANES), u32),
        ],
        scratch_shapes=[
            pltpu.VMEM((N_GROUPS, TR), f32),
            pltpu.VMEM((N_EXPERTS, LANES), f32),
            pltpu.VMEM((TR * ROW_SUB, LANES), u32),
            pltpu.VMEM((TR * ROW_SUB, LANES), u32),
            pltpu.VMEM((TOP_K, TR), i32),
            pltpu.SMEM((TOP_K, TR), i32),
            pltpu.SMEM((TOP_K, TR), i32),
            pltpu.SemaphoreType.DMA((2,)),
            pltpu.SemaphoreType.DMA(()),
        ],
        compiler_params=_cparams(("arbitrary",)),
        name="proj_route",
    )(a, w_bf, gate, x2, g.reshape(1, d), sh, sc, rw_cat, rbias.reshape(N_EXPERTS, 1), sg_bf, su_bf, sd_bf)


def _tile_rows(ref, r, n=1):
    return ref.at[pl.ds(pl.multiple_of(r * ROW_SUB, ROW_SUB), n * ROW_SUB), :]


def _row_copy(src, s, dst, d, sem):
    return pltpu.make_async_copy(_tile_rows(src, s), _tile_rows(dst, d), sem)


def _ffn_packed(x_ref, m, wg, wu, wd):
    los, his = _load_row_tiles(x_ref, m)
    x = jnp.concatenate([p.astype(bf16) for p in los + his], axis=1)
    hg = _dot(x, wg[...])
    hu = _dot(x, wu[...])
    a = (jax.nn.silu(hg) * hu).astype(bf16)
    return _dot(a, wd[...])


def _gmm_kernel(be_ref, bf_ref, nu_ref, nx_ref, sl_ref, rb_ref, xs_ref, wg_ref, wu_ref, wd_ref,
                ys_ref, wg_f, wu_f, wd_f, wg_s, wu_s, wd_s, sem, *, layer):
    i = pl.program_id(0)

    def copies(e, s):
        return (pltpu.make_async_copy(wg_ref.at[layer, e], wg_f.at[s], sem.at[s]),
                pltpu.make_async_copy(wu_ref.at[layer, e], wu_f.at[s], sem.at[s]),
                pltpu.make_async_copy(wd_ref.at[layer, e], wd_f.at[s], sem.at[s]))

    @pl.when(i == 0)
    def _():
        for cp in copies(be_ref[0], sl_ref[0]):
            cp.start()

    @pl.when(bf_ref[i] == 1)
    def _():
        s = sl_ref[i]
        for cp in copies(be_ref[i], s):
            cp.wait()

        @pl.when(nx_ref[i] >= 0)
        def _():
            for cp in copies(nx_ref[i], 1 - s):
                cp.start()

        wg_s[...] = wg_f[s].astype(bf16)
        wu_s[...] = wu_f[s].astype(bf16)
        wd_s[...] = wd_f[s].astype(bf16)

    @pl.when(i < nu_ref[0])
    def _():
        _store_row_tiles(ys_ref, _pack_halves(_ffn_packed(xs_ref, BM, wg_s, wu_s, wd_s)))


def moe_gmm(xs, blk_e, blk_first, n_used, blk_next, blk_slot, blk_rb, w_gate, w_up, w_down, layer, nblk):
    d, de = w_gate.shape[2], w_gate.shape[3]
    row_block = lambda i, be, bf, nu, nx, sl, rb: (rb[jnp.minimum(i, nu[0] - 1)], 0)
    grid_spec = pltpu.PrefetchScalarGridSpec(
        num_scalar_prefetch=6,
        grid=(nblk,),
        in_specs=[
            pl.BlockSpec((BM * ROW_SUB, LANES), row_block),
            pl.BlockSpec(memory_space=pl.ANY),
            pl.BlockSpec(memory_space=pl.ANY),
            pl.BlockSpec(memory_space=pl.ANY),
        ],
        out_specs=pl.BlockSpec((BM * ROW_SUB, LANES), row_block),
        scratch_shapes=[
            pltpu.VMEM((2, d, de), f32),
            pltpu.VMEM((2, d, de), f32),
            pltpu.VMEM((2, de, d), f32),
            pltpu.VMEM((d, de), bf16),
            pltpu.VMEM((d, de), bf16),
            pltpu.VMEM((de, d), bf16),
            pltpu.SemaphoreType.DMA((2,)),
        ],
    )
    return pl.pallas_call(
        functools.partial(_gmm_kernel, layer=layer),
        grid_spec=grid_spec,
        out_shape=jax.ShapeDtypeStruct(xs.shape, u32),
        compiler_params=_cparams(("arbitrary",)),
        name="moe_gmm",
    )(blk_e, blk_first, n_used, blk_next, blk_slot, blk_rb, xs, w_gate, w_up, w_down)


def _combine_kernel(dest_ref, dnext_ref, x_ref, ysh_ref, gw_ref, gate_ref, *rest, tail):
    if tail == 'qkv':
        ng_ref, nsh_ref, nsc_ref, wq_ref, ys_ref, o_ref, q_ref, gath_a, gath_b, sem, xn_scr = rest
    elif tail == 'final':
        ng_ref, ys_ref, o_ref, gath_a, gath_b, sem = rest
    else:
        ys_ref, o_ref, gath_a, gath_b, sem = rest
    td = x_ref.shape[0]
    i = pl.program_id(0)

    def copy(dref, buf, s, t, k):
        return _row_copy(ys_ref, dref[k, t], buf.at[k], t, sem.at[s])

    def wait_tile(dref, buf, s):
        for t in range(td):
            for k in range(TOP_K):
                copy(dref, buf, s, t, k).wait()

    @pl.when(i == 0)
    def _():
        def body(t, c):
            for k in range(TOP_K):
                copy(dest_ref, gath_a, 0, t, k).start(priority=k % DMA_PRIORITIES)
            return c
        lax.fori_loop(0, td, body, 0)

    def step(cur_buf, cur_s, nxt_buf, nxt_s):
        wait_tile(dest_ref, cur_buf, cur_s)
        for t in range(td):
            for k in range(TOP_K):
                copy(dnext_ref, nxt_buf, nxt_s, t, k).start(priority=k % DMA_PRIORITIES)
        for rb in range(td // COMB_RB):
            rows = slice(rb * COMB_RB, (rb + 1) * COMB_RB)
            ysh = ysh_ref[rows, :]
            acc_lo = [ysh[:, c * LANES:(c + 1) * LANES] for c in range(ROW_SUB)]
            acc_hi = [ysh[:, HALF + c * LANES:HALF + (c + 1) * LANES] for c in range(ROW_SUB)]
            gw = gw_ref[rows, :]
            for k in range(TOP_K):
                wk = jnp.broadcast_to(gw[:, k:k + 1], (COMB_RB, LANES))
                for c in range(ROW_SUB):
                    word = cur_buf[k, pl.ds(rb * COMB_RB * ROW_SUB + c, COMB_RB, stride=ROW_SUB), :]
                    lo, hi = _unpack_halves(word)
                    acc_lo[c] = acc_lo[c] + wk * lo
                    acc_hi[c] = acc_hi[c] + wk * hi
            y = jnp.concatenate(acc_lo + acc_hi, axis=1)
            out = x_ref[rows, :] + gate_ref[...] * y
            if tail == 'final':
                ms = jnp.mean(out * out, axis=-1, keepdims=True)
                out = (out * lax.rsqrt(ms + RMS_EPS)) * ng_ref[...]
            o_ref[rows, :] = out
            if tail == 'qkv':
                xn_scr[rows, :] = _norm_mod(out, ng_ref[...], nsh_ref[...], nsc_ref[...]).astype(bf16)
        if tail == 'qkv':
            q_ref[...] = _dot(xn_scr[...], wq_ref[...]).astype(q_ref.dtype)

        @pl.when(i == pl.num_programs(0) - 1)
        def _():
            wait_tile(dnext_ref, nxt_buf, nxt_s)

    @pl.when(i % 2 == 0)
    def _():
        step(gath_a, 0, gath_b, 1)

    @pl.when(i % 2 == 1)
    def _():
        step(gath_b, 1, gath_a, 0)


def moe_combine(x2, ysh, ys, dest, gw_t, gate, seq, final_g=None, next_qkv=None):
    t, d = x2.shape
    tps = seq // TD
    last = t // TD - 1
    row = lambda i: (i, 0)
    per_batch = pl.BlockSpec((None, 1, d), lambda i: (i // tps, 0, 0))
    in_specs = [
        pl.BlockSpec((TOP_K, TD), lambda i: (0, i), memory_space=pltpu.SMEM),
        pl.BlockSpec((TOP_K, TD), lambda i: (0, jnp.minimum(i + 1, last)), memory_space=pltpu.SMEM),
        pl.BlockSpec((TD, d), row),
        pl.BlockSpec((TD, d), row),
        pl.BlockSpec((TD, TOP_K), row),
        per_batch,
    ]
    args = [dest, dest, x2, ysh, gw_t, gate]
    out_specs = [pl.BlockSpec((TD, d), row)]
    out_shape = [jax.ShapeDtypeStruct((t, d), f32)]
    scratch = [
        pltpu.VMEM((TOP_K, TD * ROW_SUB, LANES), u32),
        pltpu.VMEM((TOP_K, TD * ROW_SUB, LANES), u32),
        pltpu.SemaphoreType.DMA((2,)),
    ]
    tail = None
    if final_g is not None:
        tail = 'final'
        in_specs.append(pl.BlockSpec((1, d), lambda i: (0, 0)))
        args.append(final_g.reshape(1, d))
    elif next_qkv is not None:
        tail = 'qkv'
        g, sh, sc, w_bf = next_qkv
        nq = w_bf.shape[1]
        in_specs += [pl.BlockSpec((1, d), lambda i: (0, 0)), per_batch, per_batch,
                     pl.BlockSpec((d, nq), lambda i: (0, 0))]
        args += [g.reshape(1, d), sh, sc, w_bf]
        out_specs.append(pl.BlockSpec((TD, nq), row))
        out_shape.append(jax.ShapeDtypeStruct((t, nq), bf16))
        scratch.append(pltpu.VMEM((TD, d), bf16))
    in_specs.append(pl.BlockSpec(memory_space=pl.ANY))
    args.append(ys)
    outs = pl.pallas_call(
        functools.partial(_combine_kernel, tail=tail),
        grid=(t // TD,),
        in_specs=in_specs,
        out_specs=out_specs,
        out_shape=out_shape,
        scratch_shapes=scratch,
        compiler_params=_cparams(("arbitrary",)),
        name="moe_combine",
    )(*args)
    return outs if tail == 'qkv' else outs[0]


def moe_layer(mix, w_out_bf, x2, gate_m, g, sh, sc, gate, router_w, router_bias, w_gate, w_up, w_down, layer,
              sh_gate, sh_up, sh_down, seq, final_g=None, next_qkv=None):
    t = x2.shape[0]
    a = t * TOP_K
    nblk = a // BM + N_EXPERTS
    rw_t = router_w.T
    rw_hi = rw_t.astype(bf16)
    rw_lo = (rw_t - rw_hi.astype(f32)).astype(bf16)
    rw_cat = jnp.concatenate([rw_hi, rw_lo], axis=0)

    x2, ysh, dest, gw, cnt, xs = proj_route(mix, w_out_bf, x2, gate_m, g, sh, sc, rw_cat, router_bias,
                                            sh_gate.astype(bf16), sh_up.astype(bf16), sh_down.astype(bf16), seq)

    counts = cnt[:, 0]
    nb_e = (counts + BM - 1) // BM
    cum_nb = jnp.cumsum(nb_e)
    n_used = cum_nb[-1].astype(i32).reshape(1)
    blk_idx = jnp.arange(nblk, dtype=i32)
    blk_e = jnp.minimum(jnp.sum((cum_nb[None, :] <= blk_idx[:, None]).astype(i32), axis=1), N_EXPERTS - 1)
    changed = jnp.concatenate([jnp.ones((1,), i32), (blk_e[1:] != blk_e[:-1]).astype(i32)])
    blk_first = changed * (blk_idx < cum_nb[-1]).astype(i32)
    eids = jnp.arange(N_EXPERTS, dtype=i32)
    has = counts > 0
    later = jnp.flip(lax.cummin(jnp.flip(jnp.where(has, eids, N_EXPERTS))))
    nxt_e = jnp.concatenate([later[1:], jnp.full((1,), N_EXPERTS, i32)])
    nxt_e = jnp.where(nxt_e < N_EXPERTS, nxt_e, -1)
    slot_e = (jnp.cumsum(has.astype(i32)) - 1) % 2
    onehot = (blk_e[:, None] == eids[None, :]).astype(i32)
    blk_next = jnp.sum(onehot * nxt_e[None, :], axis=1).astype(i32)
    blk_slot = jnp.sum(onehot * slot_e[None, :], axis=1).astype(i32)
    first_blk_e = (cum_nb - nb_e).astype(i32)
    blk_rb = (blk_e * (t // BM) + blk_idx - jnp.sum(onehot * first_blk_e[None, :], axis=1)).astype(i32)

    ys = moe_gmm(xs, blk_e, blk_first, n_used, blk_next, blk_slot, blk_rb, w_gate, w_up, w_down, layer, nblk)
    return moe_combine(x2, ysh, ys, dest, gw.T, gate, seq, final_g=final_g, next_qkv=next_qkv)


def _qkv_weight(w_qkv):
    dq = N_Q_HEADS * HEAD_DIM
    dkv = N_KV_HEADS * HEAD_DIM
    d = w_qkv.shape[0]
    wq = w_qkv[:, :dq]
    wk = w_qkv[:, dq:dq + dkv].reshape(d, N_KV_HEADS, 1, HEAD_DIM)
    wv = w_qkv[:, dq + dkv:].reshape(d, N_KV_HEADS, 1, HEAD_DIM)
    wk2 = jnp.broadcast_to(wk, (d, N_KV_HEADS, 2, HEAD_DIM)).reshape(d, QK_COLS)
    wv2 = jnp.broadcast_to(wv, (d, N_KV_HEADS, 2, HEAD_DIM)).reshape(d, QK_COLS)
    return jnp.concatenate([wq, wk2, wv2], axis=1).astype(bf16)


def kernel(x, c, ada_w, ada_b, norm_mix_g, norm_ffn_g, hyb_w_in, conv_w, sgu_ln_g, sgu_ln_b, sgu_w, sgu_b,
           hyb_w_out, attn_w_qkv, attn_sinks, attn_w_o, rel_bias, router_w, router_bias, exp_w_gate,
           exp_w_up, exp_w_down, sh_w_gate, sh_w_up, sh_w_down, final_g):
    batch, seq, d = x.shape
    t = batch * seq
    x2 = x.reshape(t, d)
    mod = ada_mod(c, ada_w, ada_b)
    mods = [[mod[l, :, k * d:(k + 1) * d].reshape(batch, 1, d) for k in range(6)] for l in range(DEPTH)]
    qkv = None
    for l in range(DEPTH):
        sh_m, sc_m, g_m, sh_f, sc_f, g_f = mods[l]
        i = l // 2
        if l % 2 == 0:
            mix = hyb_in(x2, norm_mix_g[l], sh_m, sc_m, hyb_w_in[i].astype(bf16), conv_w[i], sgu_ln_g[i],
                         sgu_ln_b[i], sgu_w[i], sgu_b[i], seq)
            w_out = hyb_w_out[i].astype(bf16)
        else:
            mix = swa_attn(qkv, _attn_tables(rel_bias), attn_sinks[i], batch, seq)
            w_out = attn_w_o[i].astype(bf16)
        tail = {}
        if l == DEPTH - 1:
            tail = dict(final_g=final_g)
        else:
            tail = dict(next_qkv=(norm_mix_g[l + 1], mods[l + 1][0], mods[l + 1][1],
                                  _qkv_weight(attn_w_qkv[(l + 1) // 2])))
        res = moe_layer(mix, w_out, x2, g_m, norm_ffn_g[l], sh_f, sc_f, g_f, router_w[l], router_bias[l],
                        exp_w_gate, exp_w_up, exp_w_down, l, sh_w_gate[l], sh_w_up[l], sh_w_down[l], seq, **tail)
        x2, qkv = res if 'next_qkv' in tail else (res, None)
    return x2.reshape(batch, seq, d)
```

```python
import functools

import numpy as np
import jax
import jax.numpy as jnp
from jax import lax
from jax.experimental import pallas as pl
from jax.experimental.pallas import tpu as pltpu

f32 = jnp.float32
bf16 = jnp.bfloat16
i32 = jnp.int32
u32 = jnp.uint32

D_MODEL = 2048
DEPTH = 2
RMS_EPS = 1e-6
LN_EPS = 1e-5
D_CONV = 1024
CONV_WIDTH = 3
D_SGU = 1024
SGU_GROUPS = 8
SGU_HD = D_SGU // SGU_GROUPS
CHUNK = 128
HYB_IN = 3 * D_CONV + 2 * D_SGU
HEAD_DIM = 64
N_Q_HEADS = D_MODEL // HEAD_DIM
N_KV_HEADS = N_Q_HEADS // 8
GQA_GROUP = N_Q_HEADS // N_KV_HEADS
WINDOW = 128
ATT_BLOCK = 128
REL_BUCKETS = 32
REL_MAX_DIST = 128
N_EXPERTS = 64
TOP_K = 8
N_GROUPS = 8
TOPK_GROUPS = 4
E_PER_G = N_EXPERTS // N_GROUPS
D_EXPERT = 512
D_SHARED = 512
ROUTED_SCALE = 2.5

LANES = 128
HALF = D_MODEL // 2
VMEM_LIMIT = 56 * 1024 * 1024
TM = 512
TN_IN = 1024
TR = 512
TD = 256
BM = 512
ADA_TN = 1024
DMA_PRIORITIES = 2
COMB_RB = 16


def _cparams(sem, vmem=VMEM_LIMIT):
    return pltpu.CompilerParams(dimension_semantics=sem, vmem_limit_bytes=vmem)


def _norm_mod(x, g, sh, sc):
    ms = jnp.mean(x * x, axis=-1, keepdims=True)
    y = x * lax.rsqrt(ms + RMS_EPS)
    return (y * g) * (1.0 + sc) + sh


def _pack_halves(y):
    h = y.shape[1] // 2
    lo = lax.bitcast_convert_type(y[:, :h].astype(bf16).astype(f32), u32)
    hi = lax.bitcast_convert_type(y[:, h:].astype(bf16).astype(f32), u32)
    return (lo >> 16) | (hi & jnp.uint32(0xFFFF0000))


def _unpack_halves(w):
    lo = lax.bitcast_convert_type(w << 16, f32)
    hi = lax.bitcast_convert_type(w & jnp.uint32(0xFFFF0000), f32)
    return lo, hi


ROW_SUB = HALF // LANES


def _store_row_tiles(ref, packed):
    n = packed.shape[0]
    for c in range(ROW_SUB):
        ref[pl.ds(c, n, stride=ROW_SUB), :] = packed[:, c * LANES:(c + 1) * LANES]


def _load_row_tiles(ref, n):
    los, his = [], []
    for c in range(ROW_SUB):
        lo, hi = _unpack_halves(ref[pl.ds(c, n, stride=ROW_SUB), :])
        los.append(lo)
        his.append(hi)
    return los, his


def _gelu(x):
    return 0.5 * x * (1.0 + lax.erf(x * np.float32(np.sqrt(0.5))))


def _dot(a, b):
    return jnp.dot(a, b, preferred_element_type=f32)


def _dot_nt(a, b):
    return lax.dot_general(a, b, (((1,), (1,)), ((), ())), preferred_element_type=f32)


def _ada_kernel(c_ref, w_ref, b_ref, o_ref):
    ca = jax.nn.silu(c_ref[...]).astype(bf16)
    o_ref[...] = _dot(ca, w_ref[...].astype(bf16)) + b_ref[...]


def ada_mod(c, ada_w, ada_b):
    depth, d, n = ada_w.shape
    b = c.shape[0]
    return pl.pallas_call(
        _ada_kernel,
        grid=(depth, n // ADA_TN),
        in_specs=[
            pl.BlockSpec((b, d), lambda l, j: (0, 0)),
            pl.BlockSpec((None, d, ADA_TN), lambda l, j: (l, 0, j)),
            pl.BlockSpec((None, 1, ADA_TN), lambda l, j: (l, 0, j)),
        ],
        out_specs=pl.BlockSpec((None, b, ADA_TN), lambda l, j: (l, 0, j)),
        out_shape=jax.ShapeDtypeStruct((depth, b, n), f32),
        compiler_params=_cparams(("arbitrary", "arbitrary")),
        name="ada_mod",
    )(c, ada_w, ada_b.reshape(depth, 1, n))


def _hyb_in_kernel(x_ref, g_ref, sh_ref, sc_ref, w_ref, cw_ref, lng_ref, lnb_ref, sw_ref, sb_ref,
                   o_ref, xn_scr, a_scr, b_scr, carry_scr, *, tiles_per_seq):
    i = pl.program_id(0)
    j = pl.program_id(1)
    tm = x_ref.shape[0]

    @pl.when(j == 0)
    def _():
        xn_scr[...] = _norm_mod(x_ref[...], g_ref[...], sh_ref[...], sc_ref[...]).astype(bf16)

    @pl.when(jnp.logical_and(i == 0, j == 0))
    def _():
        carry_scr[...] = jnp.zeros_like(carry_scr)

    p = _dot(xn_scr[...], w_ref[...])

    @pl.when(j == 0)
    def _():
        a_scr[...] = p

    @pl.when(j == 1)
    def _():
        b_scr[...] = p

    @pl.when(j == 2)
    def _():
        z = b_scr[...] * p
        row = lax.broadcasted_iota(i32, z.shape, 0)
        first = (i % tiles_per_seq) == 0
        prev = jnp.where(first, 0.0, carry_scr[...])
        p1 = prev[7:8, :]
        p2 = prev[6:7, :]
        z1 = jnp.where(row == 0, p1, pltpu.roll(z, 1, 0))
        z2 = jnp.where(row == 0, p2, jnp.where(row == 1, p1, pltpu.roll(z, 2, 0)))
        cw = cw_ref[...]
        conv = cw[0:1, :] * z2 + cw[1:2, :] * z1 + cw[2:3, :] * z
        carry_scr[...] = z[tm - 8:, :]
        o_ref[:, :D_CONV] = (a_scr[...] * conv).astype(o_ref.dtype)

    @pl.when(j == 3)
    def _():
        a_scr[...] = _gelu(p)

    @pl.when(j == 4)
    def _():
        v = _gelu(p)
        mu = jnp.mean(v, axis=-1, keepdims=True)
        vc = v - mu
        var = jnp.mean(vc * vc, axis=-1, keepdims=True)
        v = (vc * lax.rsqrt(var + LN_EPS)) * lng_ref[...] + lnb_ref[...]
        vb = v.astype(bf16)
        nch = tm // CHUNK
        r = lax.broadcasted_iota(i32, (CHUNK, CHUNK), 0)
        c = lax.broadcasted_iota(i32, (CHUNK, CHUNK), 1)
        tril = r >= c
        for g in range(SGU_GROUPS):
            ws = jnp.where(tril, sw_ref[g], 0.0).astype(bf16)
            vg = jnp.concatenate(
                [vb[n * CHUNK:(n + 1) * CHUNK, g * SGU_HD:(g + 1) * SGU_HD] for n in range(nch)], axis=1)
            sg = _dot(ws, vg)
            bb = sb_ref[g]
            for n in range(nch):
                s = sg[:, n * SGU_HD:(n + 1) * SGU_HD] + bb
                u = a_scr[n * CHUNK:(n + 1) * CHUNK, g * SGU_HD:(g + 1) * SGU_HD]
                o_ref[n * CHUNK:(n + 1) * CHUNK, D_CONV + g * SGU_HD:D_CONV + (g + 1) * SGU_HD] = (
                    (u * s).astype(o_ref.dtype))


def hyb_in(x2, g, sh, sc, w_in_bf, conv_w, ln_g, ln_b, sgu_w, sgu_b, seq):
    t, d = x2.shape
    tps = seq // TM
    nj = HYB_IN // TN_IN
    sbb = jnp.broadcast_to(sgu_b[:, :, None], (SGU_GROUPS, CHUNK, SGU_HD))
    kern = functools.partial(_hyb_in_kernel, tiles_per_seq=tps)
    return pl.pallas_call(
        kern,
        grid=(t // TM, nj),
        in_specs=[
            pl.BlockSpec((TM, d), lambda i, j: (i, 0)),
            pl.BlockSpec((1, d), lambda i, j: (0, 0)),
            pl.BlockSpec((None, 1, d), lambda i, j: (i // tps, 0, 0)),
            pl.BlockSpec((None, 1, d), lambda i, j: (i // tps, 0, 0)),
            pl.BlockSpec((d, TN_IN), lambda i, j: (0, j)),
            pl.BlockSpec((CONV_WIDTH, D_CONV), lambda i, j: (0, 0)),
            pl.BlockSpec((1, D_SGU), lambda i, j: (0, 0)),
            pl.BlockSpec((1, D_SGU), lambda i, j: (0, 0)),
            pl.BlockSpec((SGU_GROUPS, CHUNK, CHUNK), lambda i, j: (0, 0, 0)),
            pl.BlockSpec((SGU_GROUPS, CHUNK, SGU_HD), lambda i, j: (0, 0, 0)),
        ],
        out_specs=pl.BlockSpec((TM, D_CONV + D_SGU), lambda i, j: (i, 0)),
        out_shape=jax.ShapeDtypeStruct((t, D_CONV + D_SGU), bf16),
        scratch_shapes=[
            pltpu.VMEM((TM, d), bf16),
            pltpu.VMEM((TM, TN_IN), f32),
            pltpu.VMEM((TM, TN_IN), f32),
            pltpu.VMEM((8, D_CONV), f32),
        ],
        compiler_params=_cparams(("arbitrary", "arbitrary")),
        name="hyb_in",
    )(x2, g.reshape(1, d), sh, sc, w_in_bf, conv_w, ln_g.reshape(1, -1), ln_b.reshape(1, -1), sgu_w, sbb)


QK_COLS = N_KV_HEADS * 2 * HEAD_DIM


def _attn_kernel(q_ref, kp_ref, kc_ref, vp_ref, vc_ref, bias_ref, sink_ref, o_ref):
    lane = lax.broadcasted_iota(i32, (2 * ATT_BLOCK, 2 * HEAD_DIM), 1)
    low = lane < HEAD_DIM
    zero = jnp.zeros((), bf16)
    scale = jnp.asarray(HEAD_DIM ** -0.5, bf16)
    ones = jnp.ones((2 * ATT_BLOCK, 2 * HEAD_DIM), bf16)
    for kh in range(N_KV_HEADS):
        cs = slice(kh * 2 * HEAD_DIM, (kh + 1) * 2 * HEAD_DIM)
        kk = jnp.concatenate([kp_ref[:, cs], kc_ref[:, cs]], axis=0)
        vv = jnp.concatenate([vp_ref[:, cs], vc_ref[:, cs]], axis=0)
        kz = (jnp.where(low, kk, zero), jnp.where(low, zero, kk))
        vz = (jnp.where(low, vv, zero), jnp.where(low, zero, vv))
        q0 = kh * GQA_GROUP * HEAD_DIM
        qs = jnp.concatenate(
            [q_ref[:, q0 + pr * 2 * HEAD_DIM:q0 + (pr + 1) * 2 * HEAD_DIM] for pr in range(ATT_PAIRS)],
            axis=0) * scale
        acc = None
        for par in range(2):
            s = _dot_nt(qs, kz[par]) + bias_ref[kh, par]
            sk = jnp.concatenate(
                [jnp.full((ATT_BLOCK, ATT_BLOCK), sink_ref[kh * GQA_GROUP + 2 * pr + par], f32)
                 for pr in range(ATT_PAIRS)], axis=0)
            rm = jnp.max(s, axis=-1, keepdims=True)
            mb = jnp.maximum(jnp.broadcast_to(rm, sk.shape), sk)
            p = jnp.concatenate([jnp.exp(s[:, :ATT_BLOCK] - mb), jnp.exp(s[:, ATT_BLOCK:] - mb)], axis=1)
            ov = _dot(p.astype(bf16), jnp.concatenate([vz[par], ones], axis=1))
            den = ov[:, 2 * HEAD_DIM:] + jnp.exp(sk - mb)
            o = ov[:, :2 * HEAD_DIM] / den
            acc = o if acc is None else acc + o
        for pr in range(ATT_PAIRS):
            o_ref[:, q0 + pr * 2 * HEAD_DIM:q0 + (pr + 1) * 2 * HEAD_DIM] = (
                acc[pr * ATT_BLOCK:(pr + 1) * ATT_BLOCK, :].astype(o_ref.dtype))


ATT_PAIRS = GQA_GROUP // 2


def swa_attn(qkv, bias_m, sinks, batch, seq):
    t = qkv.shape[0]
    nb = seq // ATT_BLOCK
    dq = N_Q_HEADS * HEAD_DIM
    kcol = dq // QK_COLS
    vcol = kcol + 1

    def prev(b, i):
        return b * nb + jnp.maximum(i - 1, 0)

    return pl.pallas_call(
        _attn_kernel,
        grid=(batch, nb),
        in_specs=[
            pl.BlockSpec((ATT_BLOCK, dq), lambda b, i: (b * nb + i, 0)),
            pl.BlockSpec((ATT_BLOCK, QK_COLS), lambda b, i: (prev(b, i), kcol)),
            pl.BlockSpec((ATT_BLOCK, QK_COLS), lambda b, i: (b * nb + i, kcol)),
            pl.BlockSpec((ATT_BLOCK, QK_COLS), lambda b, i: (prev(b, i), vcol)),
            pl.BlockSpec((ATT_BLOCK, QK_COLS), lambda b, i: (b * nb + i, vcol)),
            pl.BlockSpec((None, N_KV_HEADS, 2, ATT_PAIRS * ATT_BLOCK, 2 * ATT_BLOCK),
                         lambda b, i: (jnp.where(i == 0, 1, 0), 0, 0, 0, 0)),
            pl.BlockSpec(memory_space=pltpu.SMEM),
        ],
        out_specs=pl.BlockSpec((ATT_BLOCK, dq), lambda b, i: (b * nb + i, 0)),
        out_shape=jax.ShapeDtypeStruct((t, dq), bf16),
        compiler_params=_cparams(("arbitrary", "arbitrary")),
        name="swa_attn",
    )(qkv, qkv, qkv, qkv, qkv, bias_m, sinks)


def _t5_bucket(dist):
    n = np.maximum(dist, 0)
    max_exact = REL_BUCKETS // 2
    large = max_exact + (np.log(np.maximum(n, 1) / max_exact) / np.log(REL_MAX_DIST / max_exact)
                         * (REL_BUCKETS - max_exact)).astype(np.int32)
    large = np.minimum(large, REL_BUCKETS - 1)
    return np.where(n < max_exact, n, large).astype(np.int32)


def _by_kv_parity(a):
    rest = a.shape[2:]
    a = a.reshape((N_KV_HEADS, ATT_PAIRS, 2, ATT_BLOCK) + rest)
    a = jnp.moveaxis(a, 2, 1)
    return a.reshape((N_KV_HEADS, 2, ATT_PAIRS * ATT_BLOCK) + rest)


def _attn_tables(rel_bias):
    t_loc = np.arange(ATT_BLOCK)[:, None]
    j_loc = np.arange(2 * ATT_BLOCK)[None, :]
    dist = ATT_BLOCK + t_loc - j_loc
    band = (dist >= 0) & (dist < WINDOW)
    onehot = jnp.asarray(np.eye(REL_BUCKETS, dtype=np.float32)[_t5_bucket(dist)])
    bias = jnp.einsum('tjb,bh->htj', onehot, rel_bias.astype(f32), precision=lax.Precision.HIGHEST)
    first = band & (j_loc >= ATT_BLOCK)
    tabs = [_by_kv_parity(jnp.where(jnp.asarray(mask)[None], bias, -jnp.inf)) for mask in (band, first)]
    return jnp.stack(tabs)


def _route_kernel(a_ref, w_ref, gate_ref, x_ref, g_ref, sh_ref, sc_ref, rw_ref, rb_ref,
                  xnew_ref, xp_ref, eidx_ref, rk_ref, gw_ref, cnt_ref, grp_scr, carry_scr):
    tm = x_ref.shape[0]
    ne = N_EXPERTS

    @pl.when(pl.program_id(0) == 0)
    def _():
        carry_scr[...] = jnp.zeros_like(carry_scr)

    x = x_ref[...] + gate_ref[...] * _dot(a_ref[...], w_ref[...])
    xnew_ref[...] = x
    xn = _norm_mod(x, g_ref[...], sh_ref[...], sc_ref[...])
    _store_row_tiles(xp_ref, _pack_halves(xn))
    x_hi = xn.astype(bf16)
    x_lo = (xn - x_hi.astype(f32)).astype(bf16)
    rw = rw_ref[...]
    a = _dot_nt(rw, x_hi)
    b = _dot_nt(rw[:ne], x_lo)
    logits = a[:ne] + a[ne:] + b
    scores = jax.nn.sigmoid(logits)
    sel = scores + rb_ref[...]

    for g in range(N_GROUPS):
        tile = sel[g * E_PER_G:(g + 1) * E_PER_G, :]
        t1 = jnp.max(tile, axis=0, keepdims=True)
        dup = jnp.sum(jnp.where(tile == t1, 1.0, 0.0), axis=0, keepdims=True) >= 2.0
        t2 = jnp.max(jnp.where(tile < t1, tile, -jnp.inf), axis=0, keepdims=True)
        grp_scr[g:g + 1, :] = t1 + jnp.where(dup, t1, t2)
    gs = grp_scr[...]

    gi = lax.broadcasted_iota(i32, (N_GROUPS, tm), 0)
    grank = jnp.zeros((N_GROUPS, tm), i32)
    for g in range(N_GROUPS):
        r = grp_scr[g:g + 1, :]
        ge = jnp.where(r >= gs, 1, 0)
        gt = jnp.where(r > gs, 1, 0)
        grank = grank + jnp.where(gi > g, ge, gt)
    grp_scr[...] = jnp.where(grank < TOPK_GROUPS, 1.0, 0.0)

    masked = jnp.concatenate(
        [jnp.where(grp_scr[g:g + 1, :] > 0.5, sel[g * E_PER_G:(g + 1) * E_PER_G, :], -jnp.inf)
         for g in range(N_GROUPS)], axis=0)

    ei = lax.broadcasted_iota(i32, (ne, tm), 0)
    cur = masked
    chf = jnp.zeros((ne, tm), f32)
    for _ in range(TOP_K):
        top = jnp.max(cur, axis=0, keepdims=True)
        first = jnp.min(jnp.where(cur == top, ei, ne), axis=0, keepdims=True)
        pick = ei == first
        chf = jnp.where(pick, 1.0, chf)
        cur = jnp.where(pick, -jnp.inf, cur)
    chosen = chf > 0.5

    wun = jnp.where(chosen, scores, 0.0)
    gwt = wun / jnp.sum(wun, axis=0, keepdims=True) * ROUTED_SCALE

    rr = lax.broadcasted_iota(i32, (tm, tm), 0)
    cc = lax.broadcasted_iota(i32, (tm, tm), 1)
    upper = jnp.where(rr < cc, 1.0, 0.0).astype(bf16)
    chb = chf.astype(bf16)
    pos = carry_scr[:, 0:1] + _dot(chb, upper)
    carry_scr[...] = carry_scr[...] + jnp.sum(chf, axis=1, keepdims=True)
    cnt_ref[...] = carry_scr[...].astype(i32)

    er = lax.broadcasted_iota(i32, (ne, ne), 0)
    ec = lax.broadcasted_iota(i32, (ne, ne), 1)
    lower = jnp.where(ec < er, 1.0, 0.0).astype(bf16)
    below = _dot(lower, chb)
    slot = jnp.where(chosen, below, -1.0)
    eif = ei.astype(f32)
    for k in range(TOP_K):
        mk = slot == float(k)
        eidx_ref[k:k + 1, :] = jnp.sum(jnp.where(mk, eif, 0.0), axis=0, keepdims=True).astype(i32)
        rk_ref[k:k + 1, :] = jnp.sum(jnp.where(mk, pos, 0.0), axis=0, keepdims=True).astype(i32)
        gw_ref[k:k + 1, :] = jnp.sum(jnp.where(mk, gwt, 0.0), axis=0, keepdims=True)


def proj_route(a, w_bf, x2, gate, g, sh, sc, rw_cat, rbias, seq):
    t, d = x2.shape
    k = a.shape[1]
    tps = seq // TR
    return pl.pallas_call(
        _route_kernel,
        grid=(t // TR,),
        in_specs=[
            pl.BlockSpec((TR, k), lambda i: (i, 0)),
            pl.BlockSpec((k, d), lambda i: (0, 0)),
            pl.BlockSpec((None, 1, d), lambda i: (i // tps, 0, 0)),
            pl.BlockSpec((TR, d), lambda i: (i, 0)),
            pl.BlockSpec((1, d), lambda i: (0, 0)),
            pl.BlockSpec((None, 1, d), lambda i: (i // tps, 0, 0)),
            pl.BlockSpec((None, 1, d), lambda i: (i // tps, 0, 0)),
            pl.BlockSpec((2 * N_EXPERTS, d), lambda i: (0, 0)),
            pl.BlockSpec((N_EXPERTS, 1), lambda i: (0, 0)),
        ],
        out_specs=[
            pl.BlockSpec((TR, d), lambda i: (i, 0)),
            pl.BlockSpec((TR * ROW_SUB, LANES), lambda i: (i, 0)),
            pl.BlockSpec((TOP_K, TR), lambda i: (0, i)),
            pl.BlockSpec((TOP_K, TR), lambda i: (0, i)),
            pl.BlockSpec((TOP_K, TR), lambda i: (0, i)),
            pl.BlockSpec((N_EXPERTS, LANES), lambda i: (0, 0)),
        ],
        out_shape=[
            jax.ShapeDtypeStruct((t, d), f32),
            jax.ShapeDtypeStruct((t * ROW_SUB, LANES), u32),
            jax.ShapeDtypeStruct((TOP_K, t), i32),
            jax.ShapeDtypeStruct((TOP_K, t), i32),
            jax.ShapeDtypeStruct((TOP_K, t), f32),
            jax.ShapeDtypeStruct((N_EXPERTS, LANES), i32),
        ],
        scratch_shapes=[
            pltpu.VMEM((N_GROUPS, TR), f32),
            pltpu.VMEM((N_EXPERTS, LANES), f32),
        ],
        compiler_params=_cparams(("arbitrary",)),
        name="proj_route",
    )(a, w_bf, gate, x2, g.reshape(1, d), sh, sc, rw_cat, rbias.reshape(N_EXPERTS, 1))


def _tile_rows(ref, r, n=1):
    return ref.at[pl.ds(pl.multiple_of(r * ROW_SUB, ROW_SUB), n * ROW_SUB), :]


def _row_copy(src, s, dst, d, sem):
    return pltpu.make_async_copy(_tile_rows(src, s), _tile_rows(dst, d), sem)


def _dispatch_kernel(ztail_ref, dest_ref, xp_ref, sg_ref, su_ref, sd_ref, xs_ref, ysh_ref, zeros_scr, sem, zsem):
    td = xp_ref.shape[0] // ROW_SUB

    @pl.when(pl.program_id(0) == 0)
    def _():
        zeros_scr[...] = jnp.zeros_like(zeros_scr)

        def zcopy(e):
            return pltpu.make_async_copy(zeros_scr, _tile_rows(xs_ref, ztail_ref[e], BM), zsem)

        def zstart(e, c):
            zcopy(e).start()
            return c

        def zwait(e, c):
            zcopy(e).wait()
            return c

        lax.fori_loop(0, N_EXPERTS, zstart, 0)
        lax.fori_loop(0, N_EXPERTS, zwait, 0)

    for t in range(td):
        for k in range(TOP_K):
            _row_copy(xp_ref, t, xs_ref, dest_ref[k, t], sem).start(priority=k % DMA_PRIORITIES)
    ysh_ref[...] = _ffn_packed(xp_ref, td, sg_ref, su_ref, sd_ref)
    for t in range(td):
        for k in range(TOP_K):
            _row_copy(xp_ref, t, xs_ref, dest_ref[k, t], sem).wait()


def moe_dispatch(xp, dest, ztail, nrows, sg_bf, su_bf, sd_bf):
    t = xp.shape[0] // ROW_SUB
    d = sg_bf.shape[0]
    grid_spec = pltpu.PrefetchScalarGridSpec(
        num_scalar_prefetch=1,
        grid=(t // TD,),
        in_specs=[
            pl.BlockSpec((TOP_K, TD), lambda i, z: (0, i), memory_space=pltpu.SMEM),
            pl.BlockSpec((TD * ROW_SUB, LANES), lambda i, z: (i, 0)),
            pl.BlockSpec((d, D_SHARED), lambda i, z: (0, 0)),
            pl.BlockSpec((d, D_SHARED), lambda i, z: (0, 0)),
            pl.BlockSpec((D_SHARED, d), lambda i, z: (0, 0)),
        ],
        out_specs=[pl.BlockSpec(memory_space=pl.ANY), pl.BlockSpec((TD, d), lambda i, z: (i, 0))],
        scratch_shapes=[
            pltpu.VMEM((BM * ROW_SUB, LANES), u32),
            pltpu.SemaphoreType.DMA(()),
            pltpu.SemaphoreType.DMA(()),
        ],
    )
    return pl.pallas_call(
        _dispatch_kernel,
        grid_spec=grid_spec,
        out_shape=[jax.ShapeDtypeStruct((nrows * ROW_SUB, LANES), u32), jax.ShapeDtypeStruct((t, d), f32)],
        compiler_params=_cparams(("arbitrary",)),
        name="moe_dispatch",
    )(ztail, dest, xp, sg_bf, su_bf, sd_bf)


def _ffn_packed(x_ref, m, wg, wu, wd):
    los, his = _load_row_tiles(x_ref, m)
    x = jnp.concatenate([p.astype(bf16) for p in los + his], axis=1)
    hg = _dot(x, wg[...])
    hu = _dot(x, wu[...])
    a = (jax.nn.silu(hg) * hu).astype(bf16)
    return _dot(a, wd[...])


def _gmm_kernel(be_ref, bf_ref, nu_ref, nx_ref, sl_ref, xs_ref, wg_ref, wu_ref, wd_ref,
                ys_ref, wg_f, wu_f, wd_f, wg_s, wu_s, wd_s, sem, *, layer):
    i = pl.program_id(0)

    def copies(e, s):
        return (pltpu.make_async_copy(wg_ref.at[layer, e], wg_f.at[s], sem.at[s]),
                pltpu.make_async_copy(wu_ref.at[layer, e], wu_f.at[s], sem.at[s]),
                pltpu.make_async_copy(wd_ref.at[layer, e], wd_f.at[s], sem.at[s]))

    @pl.when(i == 0)
    def _():
        for cp in copies(be_ref[0], sl_ref[0]):
            cp.start()

    @pl.when(bf_ref[i] == 1)
    def _():
        s = sl_ref[i]
        for cp in copies(be_ref[i], s):
            cp.wait()

        @pl.when(nx_ref[i] >= 0)
        def _():
            for cp in copies(nx_ref[i], 1 - s):
                cp.start()

        wg_s[...] = wg_f[s].astype(bf16)
        wu_s[...] = wu_f[s].astype(bf16)
        wd_s[...] = wd_f[s].astype(bf16)

    @pl.when(i < nu_ref[0])
    def _():
        _store_row_tiles(ys_ref, _pack_halves(_ffn_packed(xs_ref, BM, wg_s, wu_s, wd_s)))


def moe_gmm(xs, blk_e, blk_first, n_used, blk_next, blk_slot, w_gate, w_up, w_down, layer, nblk):
    d, de = w_gate.shape[2], w_gate.shape[3]
    grid_spec = pltpu.PrefetchScalarGridSpec(
        num_scalar_prefetch=5,
        grid=(nblk,),
        in_specs=[
            pl.BlockSpec((BM * ROW_SUB, LANES), lambda i, be, bf, nu, nx, sl: (jnp.minimum(i, nu[0] - 1), 0)),
            pl.BlockSpec(memory_space=pl.ANY),
            pl.BlockSpec(memory_space=pl.ANY),
            pl.BlockSpec(memory_space=pl.ANY),
        ],
        out_specs=pl.BlockSpec((BM * ROW_SUB, LANES), lambda i, be, bf, nu, nx, sl: (jnp.minimum(i, nu[0] - 1), 0)),
        scratch_shapes=[
            pltpu.VMEM((2, d, de), f32),
            pltpu.VMEM((2, d, de), f32),
            pltpu.VMEM((2, de, d), f32),
            pltpu.VMEM((d, de), bf16),
            pltpu.VMEM((d, de), bf16),
            pltpu.VMEM((de, d), bf16),
            pltpu.SemaphoreType.DMA((2,)),
        ],
    )
    return pl.pallas_call(
        functools.partial(_gmm_kernel, layer=layer),
        grid_spec=grid_spec,
        out_shape=jax.ShapeDtypeStruct((nblk * BM * ROW_SUB, LANES), u32),
        compiler_params=_cparams(("arbitrary",)),
        name="moe_gmm",
    )(blk_e, blk_first, n_used, blk_next, blk_slot, xs, w_gate, w_up, w_down)


def _combine_kernel(dest_ref, dnext_ref, x_ref, ysh_ref, gw_ref, gate_ref, *rest, tail):
    if tail == 'qkv':
        ng_ref, nsh_ref, nsc_ref, wq_ref, ys_ref, o_ref, q_ref, gath_a, gath_b, sem, xn_scr = rest
    elif tail == 'final':
        ng_ref, ys_ref, o_ref, gath_a, gath_b, sem = rest
    else:
        ys_ref, o_ref, gath_a, gath_b, sem = rest
    td = x_ref.shape[0]
    i = pl.program_id(0)

    def copy(dref, buf, s, t, k):
        return _row_copy(ys_ref, dref[k, t], buf.at[k], t, sem.at[s])

    def wait_tile(dref, buf, s):
        for t in range(td):
            for k in range(TOP_K):
                copy(dref, buf, s, t, k).wait()

    @pl.when(i == 0)
    def _():
        def body(t, c):
            for k in range(TOP_K):
                copy(dest_ref, gath_a, 0, t, k).start(priority=k % DMA_PRIORITIES)
            return c
        lax.fori_loop(0, td, body, 0)

    def step(cur_buf, cur_s, nxt_buf, nxt_s):
        wait_tile(dest_ref, cur_buf, cur_s)
        for t in range(td):
            for k in range(TOP_K):
                copy(dnext_ref, nxt_buf, nxt_s, t, k).start(priority=k % DMA_PRIORITIES)
        for rb in range(td // COMB_RB):
            rows = slice(rb * COMB_RB, (rb + 1) * COMB_RB)
            ysh = ysh_ref[rows, :]
            acc_lo = [ysh[:, c * LANES:(c + 1) * LANES] for c in range(ROW_SUB)]
            acc_hi = [ysh[:, HALF + c * LANES:HALF + (c + 1) * LANES] for c in range(ROW_SUB)]
            gw = gw_ref[rows, :]
            for k in range(TOP_K):
                wk = jnp.broadcast_to(gw[:, k:k + 1], (COMB_RB, LANES))
                for c in range(ROW_SUB):
                    word = cur_buf[k, pl.ds(rb * COMB_RB * ROW_SUB + c, COMB_RB, stride=ROW_SUB), :]
                    lo, hi = _unpack_halves(word)
                    acc_lo[c] = acc_lo[c] + wk * lo
                    acc_hi[c] = acc_hi[c] + wk * hi
            y = jnp.concatenate(acc_lo + acc_hi, axis=1)
            out = x_ref[rows, :] + gate_ref[...] * y
            if tail == 'final':
                ms = jnp.mean(out * out, axis=-1, keepdims=True)
                out = (out * lax.rsqrt(ms + RMS_EPS)) * ng_ref[...]
            o_ref[rows, :] = out
            if tail == 'qkv':
                xn_scr[rows, :] = _norm_mod(out, ng_ref[...], nsh_ref[...], nsc_ref[...]).astype(bf16)
        if tail == 'qkv':
            q_ref[...] = _dot(xn_scr[...], wq_ref[...]).astype(q_ref.dtype)

        @pl.when(i == pl.num_programs(0) - 1)
        def _():
            wait_tile(dnext_ref, nxt_buf, nxt_s)

    @pl.when(i % 2 == 0)
    def _():
        step(gath_a, 0, gath_b, 1)

    @pl.when(i % 2 == 1)
    def _():
        step(gath_b, 1, gath_a, 0)


def moe_combine(x2, ysh, ys, dest, gw_t, gate, seq, final_g=None, next_qkv=None):
    t, d = x2.shape
    tps = seq // TD
    last = t // TD - 1
    row = lambda i: (i, 0)
    per_batch = pl.BlockSpec((None, 1, d), lambda i: (i // tps, 0, 0))
    in_specs = [
        pl.BlockSpec((TOP_K, TD), lambda i: (0, i), memory_space=pltpu.SMEM),
        pl.BlockSpec((TOP_K, TD), lambda i: (0, jnp.minimum(i + 1, last)), memory_space=pltpu.SMEM),
        pl.BlockSpec((TD, d), row),
        pl.BlockSpec((TD, d), row),
        pl.BlockSpec((TD, TOP_K), row),
        per_batch,
    ]
    args = [dest, dest, x2, ysh, gw_t, gate]
    out_specs = [pl.BlockSpec((TD, d), row)]
    out_shape = [jax.ShapeDtypeStruct((t, d), f32)]
    scratch = [
        pltpu.VMEM((TOP_K, TD * ROW_SUB, LANES), u32),
        pltpu.VMEM((TOP_K, TD * ROW_SUB, LANES), u32),
        pltpu.SemaphoreType.DMA((2,)),
    ]
    tail = None
    if final_g is not None:
        tail = 'final'
        in_specs.append(pl.BlockSpec((1, d), lambda i: (0, 0)))
        args.append(final_g.reshape(1, d))
    elif next_qkv is not None:
        tail = 'qkv'
        g, sh, sc, w_bf = next_qkv
        nq = w_bf.shape[1]
        in_specs += [pl.BlockSpec((1, d), lambda i: (0, 0)), per_batch, per_batch,
                     pl.BlockSpec((d, nq), lambda i: (0, 0))]
        args += [g.reshape(1, d), sh, sc, w_bf]
        out_specs.append(pl.BlockSpec((TD, nq), row))
        out_shape.append(jax.ShapeDtypeStruct((t, nq), bf16))
        scratch.append(pltpu.VMEM((TD, d), bf16))
    in_specs.append(pl.BlockSpec(memory_space=pl.ANY))
    args.append(ys)
    outs = pl.pallas_call(
        functools.partial(_combine_kernel, tail=tail),
        grid=(t // TD,),
        in_specs=in_specs,
        out_specs=out_specs,
        out_shape=out_shape,
        scratch_shapes=scratch,
        compiler_params=_cparams(("arbitrary",)),
        name="moe_combine",
    )(*args)
    return outs if tail == 'qkv' else outs[0]


SLOT_TN = 4096


def _slots_kernel(ps_ref, eidx_ref, rk_ref, o_ref):
    e = eidx_ref[...]
    start = jnp.zeros_like(e)
    for j in range(N_EXPERTS):
        start = jnp.where(e == j, ps_ref[j], start)
    o_ref[...] = start + rk_ref[...]


def moe_slots(pad_start, eidx, rk):
    t = eidx.shape[1]
    tn = min(t, SLOT_TN)
    spec = pl.BlockSpec((TOP_K, tn), lambda i, ps: (0, i))
    grid_spec = pltpu.PrefetchScalarGridSpec(
        num_scalar_prefetch=1, grid=(t // tn,), in_specs=[spec, spec], out_specs=spec)
    return pl.pallas_call(
        _slots_kernel,
        grid_spec=grid_spec,
        out_shape=jax.ShapeDtypeStruct((TOP_K, t), i32),
        compiler_params=_cparams(("arbitrary",)),
        name="moe_slots",
    )(pad_start, eidx, rk)


def moe_layer(mix, w_out_bf, x2, gate_m, g, sh, sc, gate, router_w, router_bias, w_gate, w_up, w_down, layer,
              sh_gate, sh_up, sh_down, seq, final_g=None, next_qkv=None):
    t = x2.shape[0]
    a = t * TOP_K
    nblk = a // BM + N_EXPERTS
    rw_t = router_w.T
    rw_hi = rw_t.astype(bf16)
    rw_lo = (rw_t - rw_hi.astype(f32)).astype(bf16)
    rw_cat = jnp.concatenate([rw_hi, rw_lo], axis=0)

    x2, xp, eidx, rk, gw, cnt = proj_route(mix, w_out_bf, x2, gate_m, g, sh, sc, rw_cat, router_bias, seq)

    counts = cnt[:, 0]
    padded = ((counts + BM - 1) // BM) * BM
    cum_pad = jnp.cumsum(padded)
    pad_start = cum_pad - padded
    n_used = (cum_pad[-1] // BM).astype(i32).reshape(1)
    blk_row = jnp.arange(nblk, dtype=i32) * BM
    blk_e = jnp.minimum(jnp.sum((cum_pad[None, :] <= blk_row[:, None]).astype(i32), axis=1), N_EXPERTS - 1)
    changed = jnp.concatenate([jnp.ones((1,), i32), (blk_e[1:] != blk_e[:-1]).astype(i32)])
    blk_first = changed * (blk_row < cum_pad[-1]).astype(i32)
    eids = jnp.arange(N_EXPERTS, dtype=i32)
    has = counts > 0
    later = jnp.flip(lax.cummin(jnp.flip(jnp.where(has, eids, N_EXPERTS))))
    nxt_e = jnp.concatenate([later[1:], jnp.full((1,), N_EXPERTS, i32)])
    nxt_e = jnp.where(nxt_e < N_EXPERTS, nxt_e, -1)
    slot_e = (jnp.cumsum(has.astype(i32)) - 1) % 2
    onehot = (blk_e[:, None] == eids[None, :]).astype(i32)
    blk_next = jnp.sum(onehot * nxt_e[None, :], axis=1).astype(i32)
    blk_slot = jnp.sum(onehot * slot_e[None, :], axis=1).astype(i32)
    dest = moe_slots(pad_start.astype(i32), eidx, rk)
    ztail = jnp.maximum(cum_pad - BM, 0).astype(i32)

    xs, ysh = moe_dispatch(xp, dest, ztail, nblk * BM, sh_gate.astype(bf16), sh_up.astype(bf16),
                           sh_down.astype(bf16))
    ys = moe_gmm(xs, blk_e, blk_first, n_used, blk_next, blk_slot, w_gate, w_up, w_down, layer, nblk)
    return moe_combine(x2, ysh, ys, dest, gw.T, gate, seq, final_g=final_g, next_qkv=next_qkv)


def _qkv_weight(w_qkv):
    dq = N_Q_HEADS * HEAD_DIM
    dkv = N_KV_HEADS * HEAD_DIM
    d = w_qkv.shape[0]
    wq = w_qkv[:, :dq]
    wk = w_qkv[:, dq:dq + dkv].reshape(d, N_KV_HEADS, 1, HEAD_DIM)
    wv = w_qkv[:, dq + dkv:].reshape(d, N_KV_HEADS, 1, HEAD_DIM)
    wk2 = jnp.broadcast_to(wk, (d, N_KV_HEADS, 2, HEAD_DIM)).reshape(d, QK_COLS)
    wv2 = jnp.broadcast_to(wv, (d, N_KV_HEADS, 2, HEAD_DIM)).reshape(d, QK_COLS)
    return jnp.concatenate([wq, wk2, wv2], axis=1).astype(bf16)


def kernel(x, c, ada_w, ada_b, norm_mix_g, norm_ffn_g, hyb_w_in, conv_w, sgu_ln_g, sgu_ln_b, sgu_w, sgu_b,
           hyb_w_out, attn_w_qkv, attn_sinks, attn_w_o, rel_bias, router_w, router_bias, exp_w_gate,
           exp_w_up, exp_w_down, sh_w_gate, sh_w_up, sh_w_down, final_g):
    batch, seq, d = x.shape
    t = batch * seq
    x2 = x.reshape(t, d)
    mod = ada_mod(c, ada_w, ada_b)
    mods = [[mod[l, :, k * d:(k + 1) * d].reshape(batch, 1, d) for k in range(6)] for l in range(DEPTH)]
    qkv = None
    for l in range(DEPTH):
        sh_m, sc_m, g_m, sh_f, sc_f, g_f = mods[l]
        i = l // 2
        if l % 2 == 0:
            mix = hyb_in(x2, norm_mix_g[l], sh_m, sc_m, hyb_w_in[i].astype(bf16), conv_w[i], sgu_ln_g[i],
                         sgu_ln_b[i], sgu_w[i], sgu_b[i], seq)
            w_out = hyb_w_out[i].astype(bf16)
        else:
            mix = swa_attn(qkv, _attn_tables(rel_bias), attn_sinks[i], batch, seq)
            w_out = attn_w_o[i].astype(bf16)
        tail = {}
        if l == DEPTH - 1:
            tail = dict(final_g=final_g)
        else:
            tail = dict(next_qkv=(norm_mix_g[l + 1], mods[l + 1][0], mods[l + 1][1],
                                  _qkv_weight(attn_w_qkv[(l + 1) // 2])))
        res = moe_layer(mix, w_out, x2, g_m, norm_ffn_g[l], sh_f, sc_f, g_f, router_w[l], router_bias[l],
                        exp_w_gate, exp_w_up, exp_w_down, l, sh_w_gate[l], sh_w_up[l], sh_w_down[l], seq, **tail)
        x2, qkv = res if 'next_qkv' in tail else (res, None)
    return x2.reshape(batch, seq, d)
```

```python
import functools

import numpy as np
import jax
import jax.numpy as jnp
from jax import lax
from jax.experimental import pallas as pl
from jax.experimental.pallas import tpu as pltpu

f32 = jnp.float32
bf16 = jnp.bfloat16
i32 = jnp.int32
u32 = jnp.uint32

D_MODEL = 2048
DEPTH = 2
RMS_EPS = 1e-6
LN_EPS = 1e-5
D_CONV = 1024
CONV_WIDTH = 3
D_SGU = 1024
SGU_GROUPS = 8
SGU_HD = D_SGU // SGU_GROUPS
CHUNK = 128
HYB_IN = 3 * D_CONV + 2 * D_SGU
HEAD_DIM = 64
N_Q_HEADS = D_MODEL // HEAD_DIM
N_KV_HEADS = N_Q_HEADS // 8
GQA_GROUP = N_Q_HEADS // N_KV_HEADS
WINDOW = 128
ATT_BLOCK = 128
REL_BUCKETS = 32
REL_MAX_DIST = 128
N_EXPERTS = 64
TOP_K = 8
N_GROUPS = 8
TOPK_GROUPS = 4
E_PER_G = N_EXPERTS // N_GROUPS
D_EXPERT = 512
D_SHARED = 512
ROUTED_SCALE = 2.5

LANES = 128
HALF = D_MODEL // 2
VMEM_LIMIT = 56 * 1024 * 1024
TM = 512
TN_IN = 1024
TR = 512
TD = 256
BM = 512
ADA_TN = 1024
DMA_PRIORITIES = 2
COMB_RB = 16
HBM_COPIES = 1


def _cparams(sem, vmem=VMEM_LIMIT):
    return pltpu.CompilerParams(dimension_semantics=sem, vmem_limit_bytes=vmem)


def _norm_mod(x, g, sh, sc):
    ms = jnp.mean(x * x, axis=-1, keepdims=True)
    y = x * lax.rsqrt(ms + RMS_EPS)
    return (y * g) * (1.0 + sc) + sh


def _pack_halves(y):
    h = y.shape[1] // 2
    lo = lax.bitcast_convert_type(y[:, :h].astype(bf16).astype(f32), u32)
    hi = lax.bitcast_convert_type(y[:, h:].astype(bf16).astype(f32), u32)
    return (lo >> 16) | (hi & jnp.uint32(0xFFFF0000))


def _unpack_halves(w):
    lo = lax.bitcast_convert_type(w << 16, f32)
    hi = lax.bitcast_convert_type(w & jnp.uint32(0xFFFF0000), f32)
    return lo, hi


ROW_SUB = HALF // LANES


def _store_row_tiles(ref, packed):
    n = packed.shape[0]
    for c in range(ROW_SUB):
        ref[pl.ds(c, n, stride=ROW_SUB), :] = packed[:, c * LANES:(c + 1) * LANES]


def _load_row_tiles(ref, n):
    los, his = [], []
    for c in range(ROW_SUB):
        lo, hi = _unpack_halves(ref[pl.ds(c, n, stride=ROW_SUB), :])
        los.append(lo)
        his.append(hi)
    return los, his


def _gelu(x):
    return 0.5 * x * (1.0 + lax.erf(x * np.float32(np.sqrt(0.5))))


def _dot(a, b):
    return jnp.dot(a, b, preferred_element_type=f32)


def _dot_nt(a, b):
    return lax.dot_general(a, b, (((1,), (1,)), ((), ())), preferred_element_type=f32)


def _ada_kernel(c_ref, w_ref, b_ref, o_ref):
    ca = jax.nn.silu(c_ref[...]).astype(bf16)
    o_ref[...] = _dot(ca, w_ref[...].astype(bf16)) + b_ref[...]


def ada_mod(c, ada_w, ada_b):
    depth, d, n = ada_w.shape
    b = c.shape[0]
    return pl.pallas_call(
        _ada_kernel,
        grid=(depth, n // ADA_TN),
        in_specs=[
            pl.BlockSpec((b, d), lambda l, j: (0, 0)),
            pl.BlockSpec((None, d, ADA_TN), lambda l, j: (l, 0, j)),
            pl.BlockSpec((None, 1, ADA_TN), lambda l, j: (l, 0, j)),
        ],
        out_specs=pl.BlockSpec((None, b, ADA_TN), lambda l, j: (l, 0, j)),
        out_shape=jax.ShapeDtypeStruct((depth, b, n), f32),
        compiler_params=_cparams(("arbitrary", "arbitrary")),
        name="ada_mod",
    )(c, ada_w, ada_b.reshape(depth, 1, n))


def _hyb_in_kernel(x_ref, g_ref, sh_ref, sc_ref, w_ref, cw_ref, lng_ref, lnb_ref, sw_ref, sb_ref,
                   o_ref, xn_scr, a_scr, b_scr, carry_scr, *, tiles_per_seq):
    i = pl.program_id(0)
    j = pl.program_id(1)
    tm = x_ref.shape[0]

    @pl.when(j == 0)
    def _():
        xn_scr[...] = _norm_mod(x_ref[...], g_ref[...], sh_ref[...], sc_ref[...]).astype(bf16)

    @pl.when(jnp.logical_and(i == 0, j == 0))
    def _():
        carry_scr[...] = jnp.zeros_like(carry_scr)

    p = _dot(xn_scr[...], w_ref[...])

    @pl.when(j == 0)
    def _():
        a_scr[...] = p

    @pl.when(j == 1)
    def _():
        b_scr[...] = p

    @pl.when(j == 2)
    def _():
        z = b_scr[...] * p
        row = lax.broadcasted_iota(i32, z.shape, 0)
        first = (i % tiles_per_seq) == 0
        prev = jnp.where(first, 0.0, carry_scr[...])
        p1 = prev[7:8, :]
        p2 = prev[6:7, :]
        z1 = jnp.where(row == 0, p1, pltpu.roll(z, 1, 0))
        z2 = jnp.where(row == 0, p2, jnp.where(row == 1, p1, pltpu.roll(z, 2, 0)))
        cw = cw_ref[...]
        conv = cw[0:1, :] * z2 + cw[1:2, :] * z1 + cw[2:3, :] * z
        carry_scr[...] = z[tm - 8:, :]
        o_ref[:, :D_CONV] = (a_scr[...] * conv).astype(o_ref.dtype)

    @pl.when(j == 3)
    def _():
        a_scr[...] = _gelu(p)

    @pl.when(j == 4)
    def _():
        v = _gelu(p)
        mu = jnp.mean(v, axis=-1, keepdims=True)
        vc = v - mu
        var = jnp.mean(vc * vc, axis=-1, keepdims=True)
        v = (vc * lax.rsqrt(var + LN_EPS)) * lng_ref[...] + lnb_ref[...]
        vb = v.astype(bf16)
        nch = tm // CHUNK
        r = lax.broadcasted_iota(i32, (CHUNK, CHUNK), 0)
        c = lax.broadcasted_iota(i32, (CHUNK, CHUNK), 1)
        tril = r >= c
        for g in range(SGU_GROUPS):
            ws = jnp.where(tril, sw_ref[g], 0.0).astype(bf16)
            vg = jnp.concatenate(
                [vb[n * CHUNK:(n + 1) * CHUNK, g * SGU_HD:(g + 1) * SGU_HD] for n in range(nch)], axis=1)
            sg = _dot(ws, vg)
            bb = sb_ref[g]
            for n in range(nch):
                s = sg[:, n * SGU_HD:(n + 1) * SGU_HD] + bb
                u = a_scr[n * CHUNK:(n + 1) * CHUNK, g * SGU_HD:(g + 1) * SGU_HD]
                o_ref[n * CHUNK:(n + 1) * CHUNK, D_CONV + g * SGU_HD:D_CONV + (g + 1) * SGU_HD] = (
                    (u * s).astype(o_ref.dtype))


def hyb_in(x2, g, sh, sc, w_in_bf, conv_w, ln_g, ln_b, sgu_w, sgu_b, seq):
    t, d = x2.shape
    tps = seq // TM
    nj = HYB_IN // TN_IN
    sbb = jnp.broadcast_to(sgu_b[:, :, None], (SGU_GROUPS, CHUNK, SGU_HD))
    kern = functools.partial(_hyb_in_kernel, tiles_per_seq=tps)
    return pl.pallas_call(
        kern,
        grid=(t // TM, nj),
        in_specs=[
            pl.BlockSpec((TM, d), lambda i, j: (i, 0)),
            pl.BlockSpec((1, d), lambda i, j: (0, 0)),
            pl.BlockSpec((None, 1, d), lambda i, j: (i // tps, 0, 0)),
            pl.BlockSpec((None, 1, d), lambda i, j: (i // tps, 0, 0)),
            pl.BlockSpec((d, TN_IN), lambda i, j: (0, j)),
            pl.BlockSpec((CONV_WIDTH, D_CONV), lambda i, j: (0, 0)),
            pl.BlockSpec((1, D_SGU), lambda i, j: (0, 0)),
            pl.BlockSpec((1, D_SGU), lambda i, j: (0, 0)),
            pl.BlockSpec((SGU_GROUPS, CHUNK, CHUNK), lambda i, j: (0, 0, 0)),
            pl.BlockSpec((SGU_GROUPS, CHUNK, SGU_HD), lambda i, j: (0, 0, 0)),
        ],
        out_specs=pl.BlockSpec((TM, D_CONV + D_SGU), lambda i, j: (i, 0)),
        out_shape=jax.ShapeDtypeStruct((t, D_CONV + D_SGU), bf16),
        scratch_shapes=[
            pltpu.VMEM((TM, d), bf16),
            pltpu.VMEM((TM, TN_IN), f32),
            pltpu.VMEM((TM, TN_IN), f32),
            pltpu.VMEM((8, D_CONV), f32),
        ],
        compiler_params=_cparams(("arbitrary", "arbitrary")),
        name="hyb_in",
    )(x2, g.reshape(1, d), sh, sc, w_in_bf, conv_w, ln_g.reshape(1, -1), ln_b.reshape(1, -1), sgu_w, sbb)


QK_COLS = N_KV_HEADS * 2 * HEAD_DIM


def _attn_kernel(q_ref, kp_ref, kc_ref, vp_ref, vc_ref, bias_ref, sink_ref, o_ref):
    lane = lax.broadcasted_iota(i32, (2 * ATT_BLOCK, 2 * HEAD_DIM), 1)
    low = lane < HEAD_DIM
    zero = jnp.zeros((), bf16)
    scale = jnp.asarray(HEAD_DIM ** -0.5, bf16)
    ones = jnp.ones((2 * ATT_BLOCK, 2 * HEAD_DIM), bf16)
    for kh in range(N_KV_HEADS):
        cs = slice(kh * 2 * HEAD_DIM, (kh + 1) * 2 * HEAD_DIM)
        kk = jnp.concatenate([kp_ref[:, cs], kc_ref[:, cs]], axis=0)
        vv = jnp.concatenate([vp_ref[:, cs], vc_ref[:, cs]], axis=0)
        kz = (jnp.where(low, kk, zero), jnp.where(low, zero, kk))
        vz = (jnp.where(low, vv, zero), jnp.where(low, zero, vv))
        q0 = kh * GQA_GROUP * HEAD_DIM
        qs = jnp.concatenate(
            [q_ref[:, q0 + pr * 2 * HEAD_DIM:q0 + (pr + 1) * 2 * HEAD_DIM] for pr in range(ATT_PAIRS)],
            axis=0) * scale
        acc = None
        for par in range(2):
            s = _dot_nt(qs, kz[par]) + bias_ref[kh, par]
            sk = jnp.concatenate(
                [jnp.full((ATT_BLOCK, ATT_BLOCK), sink_ref[kh * GQA_GROUP + 2 * pr + par], f32)
                 for pr in range(ATT_PAIRS)], axis=0)
            rm = jnp.max(s, axis=-1, keepdims=True)
            mb = jnp.maximum(jnp.broadcast_to(rm, sk.shape), sk)
            p = jnp.concatenate([jnp.exp(s[:, :ATT_BLOCK] - mb), jnp.exp(s[:, ATT_BLOCK:] - mb)], axis=1)
            ov = _dot(p.astype(bf16), jnp.concatenate([vz[par], ones], axis=1))
            den = ov[:, 2 * HEAD_DIM:] + jnp.exp(sk - mb)
            o = ov[:, :2 * HEAD_DIM] / den
            acc = o if acc is None else acc + o
        for pr in range(ATT_PAIRS):
            o_ref[:, q0 + pr * 2 * HEAD_DIM:q0 + (pr + 1) * 2 * HEAD_DIM] = (
                acc[pr * ATT_BLOCK:(pr + 1) * ATT_BLOCK, :].astype(o_ref.dtype))


ATT_PAIRS = GQA_GROUP // 2


def swa_attn(qkv, bias_m, sinks, batch, seq):
    t = qkv.shape[0]
    nb = seq // ATT_BLOCK
    dq = N_Q_HEADS * HEAD_DIM
    kcol = dq // QK_COLS
    vcol = kcol + 1

    def prev(b, i):
        return b * nb + jnp.maximum(i - 1, 0)

    return pl.pallas_call(
        _attn_kernel,
        grid=(batch, nb),
        in_specs=[
            pl.BlockSpec((ATT_BLOCK, dq), lambda b, i: (b * nb + i, 0)),
            pl.BlockSpec((ATT_BLOCK, QK_COLS), lambda b, i: (prev(b, i), kcol)),
            pl.BlockSpec((ATT_BLOCK, QK_COLS), lambda b, i: (b * nb + i, kcol)),
            pl.BlockSpec((ATT_BLOCK, QK_COLS), lambda b, i: (prev(b, i), vcol)),
            pl.BlockSpec((ATT_BLOCK, QK_COLS), lambda b, i: (b * nb + i, vcol)),
            pl.BlockSpec((None, N_KV_HEADS, 2, ATT_PAIRS * ATT_BLOCK, 2 * ATT_BLOCK),
                         lambda b, i: (jnp.where(i == 0, 1, 0), 0, 0, 0, 0)),
            pl.BlockSpec(memory_space=pltpu.SMEM),
        ],
        out_specs=pl.BlockSpec((ATT_BLOCK, dq), lambda b, i: (b * nb + i, 0)),
        out_shape=jax.ShapeDtypeStruct((t, dq), bf16),
        compiler_params=_cparams(("arbitrary", "arbitrary")),
        name="swa_attn",
    )(qkv, qkv, qkv, qkv, qkv, bias_m, sinks)


def _t5_bucket(dist):
    n = np.maximum(dist, 0)
    max_exact = REL_BUCKETS // 2
    large = max_exact + (np.log(np.maximum(n, 1) / max_exact) / np.log(REL_MAX_DIST / max_exact)
                         * (REL_BUCKETS - max_exact)).astype(np.int32)
    large = np.minimum(large, REL_BUCKETS - 1)
    return np.where(n < max_exact, n, large).astype(np.int32)


def _by_kv_parity(a):
    rest = a.shape[2:]
    a = a.reshape((N_KV_HEADS, ATT_PAIRS, 2, ATT_BLOCK) + rest)
    a = jnp.moveaxis(a, 2, 1)
    return a.reshape((N_KV_HEADS, 2, ATT_PAIRS * ATT_BLOCK) + rest)


def _attn_tables(rel_bias):
    t_loc = np.arange(ATT_BLOCK)[:, None]
    j_loc = np.arange(2 * ATT_BLOCK)[None, :]
    dist = ATT_BLOCK + t_loc - j_loc
    band = (dist >= 0) & (dist < WINDOW)
    onehot = jnp.asarray(np.eye(REL_BUCKETS, dtype=np.float32)[_t5_bucket(dist)])
    bias = jnp.einsum('tjb,bh->htj', onehot, rel_bias.astype(f32), precision=lax.Precision.HIGHEST)
    first = band & (j_loc >= ATT_BLOCK)
    tabs = [_by_kv_parity(jnp.where(jnp.asarray(mask)[None], bias, -jnp.inf)) for mask in (band, first)]
    return jnp.stack(tabs)


def _route_kernel(a_ref, w_ref, gate_ref, x_ref, g_ref, sh_ref, sc_ref, rw_ref, rb_ref,
                  xnew_ref, xp_ref, eidx_ref, rk_ref, gw_ref, cnt_ref, grp_scr, carry_scr):
    tm = x_ref.shape[0]
    ne = N_EXPERTS

    @pl.when(pl.program_id(0) == 0)
    def _():
        carry_scr[...] = jnp.zeros_like(carry_scr)

    x = x_ref[...] + gate_ref[...] * _dot(a_ref[...], w_ref[...])
    xnew_ref[...] = x
    xn = _norm_mod(x, g_ref[...], sh_ref[...], sc_ref[...])
    _store_row_tiles(xp_ref, _pack_halves(xn))
    x_hi = xn.astype(bf16)
    x_lo = (xn - x_hi.astype(f32)).astype(bf16)
    rw = rw_ref[...]
    a = _dot_nt(rw, x_hi)
    b = _dot_nt(rw[:ne], x_lo)
    logits = a[:ne] + a[ne:] + b
    scores = jax.nn.sigmoid(logits)
    sel = scores + rb_ref[...]

    for g in range(N_GROUPS):
        tile = sel[g * E_PER_G:(g + 1) * E_PER_G, :]
        t1 = jnp.max(tile, axis=0, keepdims=True)
        dup = jnp.sum(jnp.where(tile == t1, 1.0, 0.0), axis=0, keepdims=True) >= 2.0
        t2 = jnp.max(jnp.where(tile < t1, tile, -jnp.inf), axis=0, keepdims=True)
        grp_scr[g:g + 1, :] = t1 + jnp.where(dup, t1, t2)
    gs = grp_scr[...]

    gi = lax.broadcasted_iota(i32, (N_GROUPS, tm), 0)
    grank = jnp.zeros((N_GROUPS, tm), i32)
    for g in range(N_GROUPS):
        r = grp_scr[g:g + 1, :]
        ge = jnp.where(r >= gs, 1, 0)
        gt = jnp.where(r > gs, 1, 0)
        grank = grank + jnp.where(gi > g, ge, gt)
    grp_scr[...] = jnp.where(grank < TOPK_GROUPS, 1.0, 0.0)

    masked = jnp.concatenate(
        [jnp.where(grp_scr[g:g + 1, :] > 0.5, sel[g * E_PER_G:(g + 1) * E_PER_G, :], -jnp.inf)
         for g in range(N_GROUPS)], axis=0)

    ei = lax.broadcasted_iota(i32, (ne, tm), 0)
    cur = masked
    chf = jnp.zeros((ne, tm), f32)
    for _ in range(TOP_K):
        top = jnp.max(cur, axis=0, keepdims=True)
        first = jnp.min(jnp.where(cur == top, ei, ne), axis=0, keepdims=True)
        pick = ei == first
        chf = jnp.where(pick, 1.0, chf)
        cur = jnp.where(pick, -jnp.inf, cur)
    chosen = chf > 0.5

    wun = jnp.where(chosen, scores, 0.0)
    gwt = wun / jnp.sum(wun, axis=0, keepdims=True) * ROUTED_SCALE

    rr = lax.broadcasted_iota(i32, (tm, tm), 0)
    cc = lax.broadcasted_iota(i32, (tm, tm), 1)
    upper = jnp.where(rr < cc, 1.0, 0.0).astype(bf16)
    chb = chf.astype(bf16)
    pos = carry_scr[:, 0:1] + _dot(chb, upper)
    carry_scr[...] = carry_scr[...] + jnp.sum(chf, axis=1, keepdims=True)
    cnt_ref[...] = carry_scr[...].astype(i32)

    er = lax.broadcasted_iota(i32, (ne, ne), 0)
    ec = lax.broadcasted_iota(i32, (ne, ne), 1)
    lower = jnp.where(ec < er, 1.0, 0.0).astype(bf16)
    below = _dot(lower, chb)
    slot = jnp.where(chosen, below, -1.0)
    eif = ei.astype(f32)
    for k in range(TOP_K):
        mk = slot == float(k)
        eidx_ref[k:k + 1, :] = jnp.sum(jnp.where(mk, eif, 0.0), axis=0, keepdims=True).astype(i32)
        rk_ref[k:k + 1, :] = jnp.sum(jnp.where(mk, pos, 0.0), axis=0, keepdims=True).astype(i32)
        gw_ref[k:k + 1, :] = jnp.sum(jnp.where(mk, gwt, 0.0), axis=0, keepdims=True)


def proj_route(a, w_bf, x2, gate, g, sh, sc, rw_cat, rbias, seq):
    t, d = x2.shape
    k = a.shape[1]
    tps = seq // TR
    return pl.pallas_call(
        _route_kernel,
        grid=(t // TR,),
        in_specs=[
            pl.BlockSpec((TR, k), lambda i: (i, 0)),
            pl.BlockSpec((k, d), lambda i: (0, 0)),
            pl.BlockSpec((None, 1, d), lambda i: (i // tps, 0, 0)),
            pl.BlockSpec((TR, d), lambda i: (i, 0)),
            pl.BlockSpec((1, d), lambda i: (0, 0)),
            pl.BlockSpec((None, 1, d), lambda i: (i // tps, 0, 0)),
            pl.BlockSpec((None, 1, d), lambda i: (i // tps, 0, 0)),
            pl.BlockSpec((2 * N_EXPERTS, d), lambda i: (0, 0)),
            pl.BlockSpec((N_EXPERTS, 1), lambda i: (0, 0)),
        ],
        out_specs=[
            pl.BlockSpec((TR, d), lambda i: (i, 0)),
            pl.BlockSpec((TR * ROW_SUB, LANES), lambda i: (i, 0)),
            pl.BlockSpec((TOP_K, TR), lambda i: (0, i)),
            pl.BlockSpec((TOP_K, TR), lambda i: (0, i)),
            pl.BlockSpec((TOP_K, TR), lambda i: (0, i)),
            pl.BlockSpec((N_EXPERTS, LANES), lambda i: (0, 0)),
        ],
        out_shape=[
            jax.ShapeDtypeStruct((t, d), f32),
            jax.ShapeDtypeStruct((t * ROW_SUB, LANES), u32),
            jax.ShapeDtypeStruct((TOP_K, t), i32),
            jax.ShapeDtypeStruct((TOP_K, t), i32),
            jax.ShapeDtypeStruct((TOP_K, t), f32),
            jax.ShapeDtypeStruct((N_EXPERTS, LANES), i32),
        ],
        scratch_shapes=[
            pltpu.VMEM((N_GROUPS, TR), f32),
            pltpu.VMEM((N_EXPERTS, LANES), f32),
        ],
        compiler_params=_cparams(("arbitrary",)),
        name="proj_route",
    )(a, w_bf, gate, x2, g.reshape(1, d), sh, sc, rw_cat, rbias.reshape(N_EXPERTS, 1))


def _tile_rows(ref, r, n=1):
    return ref.at[pl.ds(pl.multiple_of(r * ROW_SUB, ROW_SUB), n * ROW_SUB), :]


def _row_copy(src, s, dst, d, sem):
    return pltpu.make_async_copy(_tile_rows(src, s), _tile_rows(dst, d), sem)


def _dispatch_kernel(ztail_ref, dest_ref, xp_ref, sg_ref, su_ref, sd_ref, xph_ref, xs_ref, ysh_ref,
                     zeros_scr, sem, zsem, hsem):
    td = xp_ref.shape[0] // ROW_SUB
    tok0 = pl.program_id(0) * td

    def copy(t, k):
        if k >= TOP_K - HBM_COPIES:
            return _row_copy(xph_ref, tok0 + t, xs_ref, dest_ref[k, t], hsem)
        return _row_copy(xp_ref, t, xs_ref, dest_ref[k, t], sem)

    @pl.when(pl.program_id(0) == 0)
    def _():
        zeros_scr[...] = jnp.zeros_like(zeros_scr)

        def zcopy(e):
            return pltpu.make_async_copy(zeros_scr, _tile_rows(xs_ref, ztail_ref[e], BM), zsem)

        def zstart(e, c):
            zcopy(e).start()
            return c

        def zwait(e, c):
            zcopy(e).wait()
            return c

        lax.fori_loop(0, N_EXPERTS, zstart, 0)
        lax.fori_loop(0, N_EXPERTS, zwait, 0)

    for t in range(td):
        for k in range(TOP_K):
            copy(t, k).start(priority=k % DMA_PRIORITIES)
    ysh_ref[...] = _ffn_packed(xp_ref, td, sg_ref, su_ref, sd_ref)
    for t in range(td):
        for k in range(TOP_K):
            copy(t, k).wait()


def moe_dispatch(xp, dest, ztail, nrows, sg_bf, su_bf, sd_bf):
    t = xp.shape[0] // ROW_SUB
    d = sg_bf.shape[0]
    grid_spec = pltpu.PrefetchScalarGridSpec(
        num_scalar_prefetch=1,
        grid=(t // TD,),
        in_specs=[
            pl.BlockSpec((TOP_K, TD), lambda i, z: (0, i), memory_space=pltpu.SMEM),
            pl.BlockSpec((TD * ROW_SUB, LANES), lambda i, z: (i, 0)),
            pl.BlockSpec((d, D_SHARED), lambda i, z: (0, 0)),
            pl.BlockSpec((d, D_SHARED), lambda i, z: (0, 0)),
            pl.BlockSpec((D_SHARED, d), lambda i, z: (0, 0)),
            pl.BlockSpec(memory_space=pl.ANY),
        ],
        out_specs=[pl.BlockSpec(memory_space=pl.ANY), pl.BlockSpec((TD, d), lambda i, z: (i, 0))],
        scratch_shapes=[
            pltpu.VMEM((BM * ROW_SUB, LANES), u32),
            pltpu.SemaphoreType.DMA(()),
            pltpu.SemaphoreType.DMA(()),
            pltpu.SemaphoreType.DMA(()),
        ],
    )
    return pl.pallas_call(
        _dispatch_kernel,
        grid_spec=grid_spec,
        out_shape=[jax.ShapeDtypeStruct((nrows * ROW_SUB, LANES), u32), jax.ShapeDtypeStruct((t, d), f32)],
        compiler_params=_cparams(("arbitrary",)),
        name="moe_dispatch",
    )(ztail, dest, xp, sg_bf, su_bf, sd_bf, xp)


def _ffn_packed(x_ref, m, wg, wu, wd):
    los, his = _load_row_tiles(x_ref, m)
    x = jnp.concatenate([p.astype(bf16) for p in los + his], axis=1)
    hg = _dot(x, wg[...])
    hu = _dot(x, wu[...])
    a = (jax.nn.silu(hg) * hu).astype(bf16)
    return _dot(a, wd[...])


def _gmm_kernel(be_ref, bf_ref, nu_ref, nx_ref, sl_ref, xs_ref, wg_ref, wu_ref, wd_ref,
                ys_ref, wg_f, wu_f, wd_f, wg_s, wu_s, wd_s, sem, *, layer):
    i = pl.program_id(0)

    def copies(e, s):
        return (pltpu.make_async_copy(wg_ref.at[layer, e], wg_f.at[s], sem.at[s]),
                pltpu.make_async_copy(wu_ref.at[layer, e], wu_f.at[s], sem.at[s]),
                pltpu.make_async_copy(wd_ref.at[layer, e], wd_f.at[s], sem.at[s]))

    @pl.when(i == 0)
    def _():
        for cp in copies(be_ref[0], sl_ref[0]):
            cp.start()

    @pl.when(bf_ref[i] == 1)
    def _():
        s = sl_ref[i]
        for cp in copies(be_ref[i], s):
            cp.wait()

        @pl.when(nx_ref[i] >= 0)
        def _():
            for cp in copies(nx_ref[i], 1 - s):
                cp.start()

        wg_s[...] = wg_f[s].astype(bf16)
        wu_s[...] = wu_f[s].astype(bf16)
        wd_s[...] = wd_f[s].astype(bf16)

    @pl.when(i < nu_ref[0])
    def _():
        _store_row_tiles(ys_ref, _pack_halves(_ffn_packed(xs_ref, BM, wg_s, wu_s, wd_s)))


def moe_gmm(xs, blk_e, blk_first, n_used, blk_next, blk_slot, w_gate, w_up, w_down, layer, nblk):
    d, de = w_gate.shape[2], w_gate.shape[3]
    grid_spec = pltpu.PrefetchScalarGridSpec(
        num_scalar_prefetch=5,
        grid=(nblk,),
        in_specs=[
            pl.BlockSpec((BM * ROW_SUB, LANES), lambda i, be, bf, nu, nx, sl: (jnp.minimum(i, nu[0] - 1), 0)),
            pl.BlockSpec(memory_space=pl.ANY),
            pl.BlockSpec(memory_space=pl.ANY),
            pl.BlockSpec(memory_space=pl.ANY),
        ],
        out_specs=pl.BlockSpec((BM * ROW_SUB, LANES), lambda i, be, bf, nu, nx, sl: (jnp.minimum(i, nu[0] - 1), 0)),
        scratch_shapes=[
            pltpu.VMEM((2, d, de), f32),
            pltpu.VMEM((2, d, de), f32),
            pltpu.VMEM((2, de, d), f32),
            pltpu.VMEM((d, de), bf16),
            pltpu.VMEM((d, de), bf16),
            pltpu.VMEM((de, d), bf16),
            pltpu.SemaphoreType.DMA((2,)),
        ],
    )
    return pl.pallas_call(
        functools.partial(_gmm_kernel, layer=layer),
        grid_spec=grid_spec,
        out_shape=jax.ShapeDtypeStruct((nblk * BM * ROW_SUB, LANES), u32),
        compiler_params=_cparams(("arbitrary",)),
        name="moe_gmm",
    )(blk_e, blk_first, n_used, blk_next, blk_slot, xs, w_gate, w_up, w_down)


def _combine_kernel(dest_ref, dnext_ref, x_ref, ysh_ref, gw_ref, gate_ref, *rest, tail):
    if tail == 'qkv':
        ng_ref, nsh_ref, nsc_ref, wq_ref, ys_ref, o_ref, q_ref, gath_a, gath_b, sem, xn_scr = rest
    elif tail == 'final':
        ng_ref, ys_ref, o_ref, gath_a, gath_b, sem = rest
    else:
        ys_ref, o_ref, gath_a, gath_b, sem = rest
    td = x_ref.shape[0]
    i = pl.program_id(0)

    def copy(dref, buf, s, t, k):
        return _row_copy(ys_ref, dref[k, t], buf.at[k], t, sem.at[s])

    def wait_tile(dref, buf, s):
        for t in range(td):
            for k in range(TOP_K):
                copy(dref, buf, s, t, k).wait()

    @pl.when(i == 0)
    def _():
        def body(t, c):
            for k in range(TOP_K):
                copy(dest_ref, gath_a, 0, t, k).start(priority=k % DMA_PRIORITIES)
            return c
        lax.fori_loop(0, td, body, 0)

    def step(cur_buf, cur_s, nxt_buf, nxt_s):
        wait_tile(dest_ref, cur_buf, cur_s)
        for t in range(td):
            for k in range(TOP_K):
                copy(dnext_ref, nxt_buf, nxt_s, t, k).start(priority=k % DMA_PRIORITIES)
        for rb in range(td // COMB_RB):
            rows = slice(rb * COMB_RB, (rb + 1) * COMB_RB)
            ysh = ysh_ref[rows, :]
            acc_lo = [ysh[:, c * LANES:(c + 1) * LANES] for c in range(ROW_SUB)]
            acc_hi = [ysh[:, HALF + c * LANES:HALF + (c + 1) * LANES] for c in range(ROW_SUB)]
            gw = gw_ref[rows, :]
            for k in range(TOP_K):
                wk = jnp.broadcast_to(gw[:, k:k + 1], (COMB_RB, LANES))
                for c in range(ROW_SUB):
                    word = cur_buf[k, pl.ds(rb * COMB_RB * ROW_SUB + c, COMB_RB, stride=ROW_SUB), :]
                    lo, hi = _unpack_halves(word)
                    acc_lo[c] = acc_lo[c] + wk * lo
                    acc_hi[c] = acc_hi[c] + wk * hi
            y = jnp.concatenate(acc_lo + acc_hi, axis=1)
            out = x_ref[rows, :] + gate_ref[...] * y
            if tail == 'final':
                ms = jnp.mean(out * out, axis=-1, keepdims=True)
                out = (out * lax.rsqrt(ms + RMS_EPS)) * ng_ref[...]
            o_ref[rows, :] = out
            if tail == 'qkv':
                xn_scr[rows, :] = _norm_mod(out, ng_ref[...], nsh_ref[...], nsc_ref[...]).astype(bf16)
        if tail == 'qkv':
            q_ref[...] = _dot(xn_scr[...], wq_ref[...]).astype(q_ref.dtype)

        @pl.when(i == pl.num_programs(0) - 1)
        def _():
            wait_tile(dnext_ref, nxt_buf, nxt_s)

    @pl.when(i % 2 == 0)
    def _():
        step(gath_a, 0, gath_b, 1)

    @pl.when(i % 2 == 1)
    def _():
        step(gath_b, 1, gath_a, 0)


def moe_combine(x2, ysh, ys, dest, gw_t, gate, seq, final_g=None, next_qkv=None):
    t, d = x2.shape
    tps = seq // TD
    last = t // TD - 1
    row = lambda i: (i, 0)
    per_batch = pl.BlockSpec((None, 1, d), lambda i: (i // tps, 0, 0))
    in_specs = [
        pl.BlockSpec((TOP_K, TD), lambda i: (0, i), memory_space=pltpu.SMEM),
        pl.BlockSpec((TOP_K, TD), lambda i: (0, jnp.minimum(i + 1, last)), memory_space=pltpu.SMEM),
        pl.BlockSpec((TD, d), row),
        pl.BlockSpec((TD, d), row),
        pl.BlockSpec((TD, TOP_K), row),
        per_batch,
    ]
    args = [dest, dest, x2, ysh, gw_t, gate]
    out_specs = [pl.BlockSpec((TD, d), row)]
    out_shape = [jax.ShapeDtypeStruct((t, d), f32)]
    scratch = [
        pltpu.VMEM((TOP_K, TD * ROW_SUB, LANES), u32),
        pltpu.VMEM((TOP_K, TD * ROW_SUB, LANES), u32),
        pltpu.SemaphoreType.DMA((2,)),
    ]
    tail = None
    if final_g is not None:
        tail = 'final'
        in_specs.append(pl.BlockSpec((1, d), lambda i: (0, 0)))
        args.append(final_g.reshape(1, d))
    elif next_qkv is not None:
        tail = 'qkv'
        g, sh, sc, w_bf = next_qkv
        nq = w_bf.shape[1]
        in_specs += [pl.BlockSpec((1, d), lambda i: (0, 0)), per_batch, per_batch,
                     pl.BlockSpec((d, nq), lambda i: (0, 0))]
        args += [g.reshape(1, d), sh, sc, w_bf]
        out_specs.append(pl.BlockSpec((TD, nq), row))
        out_shape.append(jax.ShapeDtypeStruct((t, nq), bf16))
        scratch.append(pltpu.VMEM((TD, d), bf16))
    in_specs.append(pl.BlockSpec(memory_space=pl.ANY))
    args.append(ys)
    outs = pl.pallas_call(
        functools.partial(_combine_kernel, tail=tail),
        grid=(t // TD,),
        in_specs=in_specs,
        out_specs=out_specs,
        out_shape=out_shape,
        scratch_shapes=scratch,
        compiler_params=_cparams(("arbitrary",)),
        name="moe_combine",
    )(*args)
    return outs if tail == 'qkv' else outs[0]


SLOT_TN = 4096


def _slots_kernel(ps_ref, eidx_ref, rk_ref, o_ref):
    e = eidx_ref[...]
    start = jnp.zeros_like(e)
    for j in range(N_EXPERTS):
        start = jnp.where(e == j, ps_ref[j], start)
    o_ref[...] = start + rk_ref[...]


def moe_slots(pad_start, eidx, rk):
    t = eidx.shape[1]
    tn = min(t, SLOT_TN)
    spec = pl.BlockSpec((TOP_K, tn), lambda i, ps: (0, i))
    grid_spec = pltpu.PrefetchScalarGridSpec(
        num_scalar_prefetch=1, grid=(t // tn,), in_specs=[spec, spec], out_specs=spec)
    return pl.pallas_call(
        _slots_kernel,
        grid_spec=grid_spec,
        out_shape=jax.ShapeDtypeStruct((TOP_K, t), i32),
        compiler_params=_cparams(("arbitrary",)),
        name="moe_slots",
    )(pad_start, eidx, rk)


def moe_layer(mix, w_out_bf, x2, gate_m, g, sh, sc, gate, router_w, router_bias, w_gate, w_up, w_down, layer,
              sh_gate, sh_up, sh_down, seq, final_g=None, next_qkv=None):
    t = x2.shape[0]
    a = t * TOP_K
    nblk = a // BM + N_EXPERTS
    rw_t = router_w.T
    rw_hi = rw_t.astype(bf16)
    rw_lo = (rw_t - rw_hi.astype(f32)).astype(bf16)
    rw_cat = jnp.concatenate([rw_hi, rw_lo], axis=0)

    x2, xp, eidx, rk, gw, cnt = proj_route(mix, w_out_bf, x2, gate_m, g, sh, sc, rw_cat, router_bias, seq)

    counts = cnt[:, 0]
    padded = ((counts + BM - 1) // BM) * BM
    cum_pad = jnp.cumsum(padded)
    pad_start = cum_pad - padded
    n_used = (cum_pad[-1] // BM).astype(i32).reshape(1)
    blk_row = jnp.arange(nblk, dtype=i32) * BM
    blk_e = jnp.minimum(jnp.sum((cum_pad[None, :] <= blk_row[:, None]).astype(i32), axis=1), N_EXPERTS - 1)
    changed = jnp.concatenate([jnp.ones((1,), i32), (blk_e[1:] != blk_e[:-1]).astype(i32)])
    blk_first = changed * (blk_row < cum_pad[-1]).astype(i32)
    eids = jnp.arange(N_EXPERTS, dtype=i32)
    has = counts > 0
    later = jnp.flip(lax.cummin(jnp.flip(jnp.where(has, eids, N_EXPERTS))))
    nxt_e = jnp.concatenate([later[1:], jnp.full((1,), N_EXPERTS, i32)])
    nxt_e = jnp.where(nxt_e < N_EXPERTS, nxt_e, -1)
    slot_e = (jnp.cumsum(has.astype(i32)) - 1) % 2
    onehot = (blk_e[:, None] == eids[None, :]).astype(i32)
    blk_next = jnp.sum(onehot * nxt_e[None, :], axis=1).astype(i32)
    blk_slot = jnp.sum(onehot * slot_e[None, :], axis=1).astype(i32)
    dest = moe_slots(pad_start.astype(i32), eidx, rk)
    ztail = jnp.maximum(cum_pad - BM, 0).astype(i32)

    xs, ysh = moe_dispatch(xp, dest, ztail, nblk * BM, sh_gate.astype(bf16), sh_up.astype(bf16),
                           sh_down.astype(bf16))
    ys = moe_gmm(xs, blk_e, blk_first, n_used, blk_next, blk_slot, w_gate, w_up, w_down, layer, nblk)
    return moe_combine(x2, ysh, ys, dest, gw.T, gate, seq, final_g=final_g, next_qkv=next_qkv)


def _qkv_weight(w_qkv):
    dq = N_Q_HEADS * HEAD_DIM
    dkv = N_KV_HEADS * HEAD_DIM
    d = w_qkv.shape[0]
    wq = w_qkv[:, :dq]
    wk = w_qkv[:, dq:dq + dkv].reshape(d, N_KV_HEADS, 1, HEAD_DIM)
    wv = w_qkv[:, dq + dkv:].reshape(d, N_KV_HEADS, 1, HEAD_DIM)
    wk2 = jnp.broadcast_to(wk, (d, N_KV_HEADS, 2, HEAD_DIM)).reshape(d, QK_COLS)
    wv2 = jnp.broadcast_to(wv, (d, N_KV_HEADS, 2, HEAD_DIM)).reshape(d, QK_COLS)
    return jnp.concatenate([wq, wk2, wv2], axis=1).astype(bf16)


def kernel(x, c, ada_w, ada_b, norm_mix_g, norm_ffn_g, hyb_w_in, conv_w, sgu_ln_g, sgu_ln_b, sgu_w, sgu_b,
           hyb_w_out, attn_w_qkv, attn_sinks, attn_w_o, rel_bias, router_w, router_bias, exp_w_gate,
           exp_w_up, exp_w_down, sh_w_gate, sh_w_up, sh_w_down, final_g):
    batch, seq, d = x.shape
    t = batch * seq
    x2 = x.reshape(t, d)
    mod = ada_mod(c, ada_w, ada_b)
    mods = [[mod[l, :, k * d:(k + 1) * d].reshape(batch, 1, d) for k in range(6)] for l in range(DEPTH)]
    qkv = None
    for l in range(DEPTH):
        sh_m, sc_m, g_m, sh_f, sc_f, g_f = mods[l]
        i = l // 2
        if l % 2 == 0:
            mix = hyb_in(x2, norm_mix_g[l], sh_m, sc_m, hyb_w_in[i].astype(bf16), conv_w[i], sgu_ln_g[i],
                         sgu_ln_b[i], sgu_w[i], sgu_b[i], seq)
            w_out = hyb_w_out[i].astype(bf16)
        else:
            mix = swa_attn(qkv, _attn_tables(rel_bias), attn_sinks[i], batch, seq)
            w_out = attn_w_o[i].astype(bf16)
        tail = {}
        if l == DEPTH - 1:
            tail = dict(final_g=final_g)
        else:
            tail = dict(next_qkv=(norm_mix_g[l + 1], mods[l + 1][0], mods[l + 1][1],
                                  _qkv_weight(attn_w_qkv[(l + 1) // 2])))
        res = moe_layer(mix, w_out, x2, g_m, norm_ffn_g[l], sh_f, sc_f, g_f, router_w[l], router_bias[l],
                        exp_w_gate, exp_w_up, exp_w_down, l, sh_w_gate[l], sh_w_up[l], sh_w_down[l], seq, **tail)
        x2, qkv = res if 'next_qkv' in tail else (res, None)
    return x2.reshape(batch, seq, d)
```
